```python
import jax, jax.numpy as jnp
from jax import lax
import numpy as np

D_MODEL = 1024
BATCH = 8
SEQ = 4096
DEPTH = 4

GRID_W = 64
CTX_LEN = 256
N_MIXERS = 3
D_FF = 4 * D_MODEL
N_MOD = 6
LN_EPS = 1e-6
DEEPNORM_ALPHA = (2.0 * DEPTH) ** 0.25
DEEPNORM_BETA = (8.0 * DEPTH) ** -0.25

ATT_HEAD_DIM = 128
ATT_HEADS = D_MODEL // ATT_HEAD_DIM
ATT_KV_HEADS = 2
ATT_GROUPS = ATT_HEADS // ATT_KV_HEADS
ATT_QKV = (ATT_HEADS + 2 * ATT_KV_HEADS) * ATT_HEAD_DIM
ATT_BLOCK = 128
ROPE_BASE = 10000.0

NA_HEAD_DIM = 64
NA_HEADS = D_MODEL // NA_HEAD_DIM
NA_WIN_H = 8
NA_WIN_W = 16

RW_HEAD = 64
RW_HEADS = D_MODEL // RW_HEAD
RW_DECAY_LORA = 64
RW_AAA_LORA = 64
RW_GATE_LORA = 128
RW_GN_EPS = 64e-5
N_DIR = 2

N_LAYERS_A = (DEPTH + 2) // 3
N_LAYERS_B = (DEPTH + 1) // 3
N_LAYERS_C = DEPTH // 3

kernel_name = 'hybrid_dit_gqa_natten_rwkv7'


def _layernorm(x, g, b):
    xf = x.astype(jnp.float32)
    mu = jnp.mean(xf, -1, keepdims=True)
    var = jnp.mean(jnp.square(xf - mu), -1, keepdims=True)
    return ((xf - mu) * lax.rsqrt(var + LN_EPS) * g + b).astype(x.dtype)


def _rmsnorm(x, g):
    xf = x.astype(jnp.float32)
    return (xf * lax.rsqrt(jnp.mean(xf * xf, -1, keepdims=True) + LN_EPS) * g).astype(x.dtype)


def _modulation(cond, w, b):
    m = jax.nn.silu(cond) @ w + b
    return jnp.split(m[..., None, :], N_MOD, axis=-1)


def _sqrelu_mlp(h, w1, w2):
    return jnp.square(jax.nn.relu(h @ w1)) @ w2


def _axial_rope(n_tokens, head_dim):
    t = jnp.arange(n_tokens)
    row = (t // GRID_W).astype(jnp.float32)
    col = (t % GRID_W).astype(jnp.float32)
    half = head_dim // 2
    freqs = ROPE_BASE ** (-jnp.arange(0, half, 2, dtype=jnp.float32) / half)
    ang_r = row[:, None] * freqs[None, :]
    ang_c = col[:, None] * freqs[None, :]
    ang = jnp.concatenate([ang_r, ang_r, ang_c, ang_c], axis=-1)
    return jnp.cos(ang), jnp.sin(ang)


def _apply_rope(x, cos, sin):
    x1, x2, x3, x4 = jnp.split(x, 4, axis=-1)
    rot = jnp.concatenate([-x2, x1, -x4, x3], axis=-1)
    return (x * cos[:, None] + rot * sin[:, None]).astype(x.dtype)


def _gqa_attend(qb, k, v):
    b, q_len = qb.shape[:2]
    qg = qb.reshape(b, q_len, ATT_KV_HEADS, ATT_GROUPS, ATT_HEAD_DIM)
    s = jnp.einsum('bqkgd,btkd->bkgqt', qg, k, preferred_element_type=jnp.float32) * (ATT_HEAD_DIM ** -0.5)
    p = jax.nn.softmax(s, axis=-1).astype(v.dtype)
    o = jnp.einsum('bkgqt,btkd->bqkgd', p, v)
    return o.reshape(b, q_len, ATT_HEADS * ATT_HEAD_DIM)


def _gqa_axial_attention(h, hc, wqkv, wo, q_norm, k_norm, want_ctx):
    B, S, _ = h.shape
    split = [ATT_HEADS * ATT_HEAD_DIM, (ATT_HEADS + ATT_KV_HEADS) * ATT_HEAD_DIM]

    def project(u):
        b, t = u.shape[:2]
        q, k, v = jnp.split(u @ wqkv, split, axis=-1)
        q = _rmsnorm(q.reshape(b, t, ATT_HEADS, ATT_HEAD_DIM), q_norm)
        k = _rmsnorm(k.reshape(b, t, ATT_KV_HEADS, ATT_HEAD_DIM), k_norm)
        return q, k, v.reshape(b, t, ATT_KV_HEADS, ATT_HEAD_DIM)

    q, k, v = project(h)
    qc, kc, vc = project(hc)
    cos, sin = _axial_rope(S, ATT_HEAD_DIM)
    q = _apply_rope(q, cos, sin)
    k = _apply_rope(k, cos, sin)
    k_all = jnp.concatenate([k, kc], axis=1)
    v_all = jnp.concatenate([v, vc], axis=1)
    n_blk = S // ATT_BLOCK
    q_blk = jnp.moveaxis(q.reshape(B, n_blk, ATT_BLOCK, ATT_HEADS, ATT_HEAD_DIM), 1, 0)
    o = lax.map(lambda qb: _gqa_attend(qb, k_all, v_all), q_blk)
    y = jnp.moveaxis(o, 0, 1).reshape(B, S, ATT_HEADS * ATT_HEAD_DIM) @ wo
    yc = _gqa_attend(qc, kc, vc) @ wo if want_ctx else None
    return y, yc


def _neighbourhood_attention(h, hc, wqkv, wo, rpb, want_ctx):
    B, S, _ = h.shape
    rows = S // GRID_W
    wr = min(NA_WIN_H, rows)
    scale = NA_HEAD_DIM ** -0.5

    def project(u):
        shp = u.shape[:2] + (NA_HEADS, NA_HEAD_DIM)
        q, k, v = jnp.split(u @ wqkv, 3, axis=-1)
        return q.reshape(shp), k.reshape(shp), v.reshape(shp)

    q, k, v = project(h)
    qc, kc, vc = project(hc)
    grid = (B, rows, GRID_W, NA_HEADS, NA_HEAD_DIM)
    k_grid, v_grid = k.reshape(grid), v.reshape(grid)
    q_rows = jnp.moveaxis(q.reshape(grid), 1, 0)
    col = jnp.arange(GRID_W)
    col_idx = jnp.clip(col - NA_WIN_W // 2, 0, GRID_W - NA_WIN_W)[:, None] + jnp.arange(NA_WIN_W)[None, :]
    dc = col_idx - col[:, None] + (NA_WIN_W - 1)
    n_loc = wr * NA_WIN_W

    def row_block(args):
        r, qr = args
        r0 = jnp.clip(r - wr // 2, 0, rows - wr)
        k_win = lax.dynamic_slice_in_dim(k_grid, r0, wr, axis=1)[:, :, col_idx]
        v_win = lax.dynamic_slice_in_dim(v_grid, r0, wr, axis=1)[:, :, col_idx]
        dr = r0 + jnp.arange(wr) - r + (NA_WIN_H - 1)
        bias = jnp.transpose(rpb[:, dr[:, None, None], dc[None, :, :]], (0, 2, 1, 3))
        s_loc = jnp.einsum('bqhd,bjqwhd->bhqjw', qr, k_win, preferred_element_type=jnp.float32) * scale + bias
        s_ctx = jnp.einsum('bqhd,blhd->bhql', qr, kc, preferred_element_type=jnp.float32) * scale
        p = jax.nn.softmax(jnp.concatenate([s_loc.reshape(B, NA_HEADS, GRID_W, n_loc), s_ctx], axis=-1), axis=-1).astype(v.dtype)
        p_loc = p[..., :n_loc].reshape(B, NA_HEADS, GRID_W, wr, NA_WIN_W)
        o = jnp.einsum('bhqjw,bjqwhd->bqhd', p_loc, v_win) + jnp.einsum('bhql,blhd->bqhd', p[..., n_loc:], vc)
        return o.reshape(B, GRID_W, D_MODEL)

    o = lax.map(row_block, (jnp.arange(rows), q_rows))
    y = jnp.moveaxis(o, 0, 1).reshape(B, S, D_MODEL) @ wo
    yc = None
    if want_ctx:
        sc = jnp.einsum('bqhd,blhd->bhql', qc, kc, preferred_element_type=jnp.float32) * scale
        oc = jnp.einsum('bhql,blhd->bqhd', jax.nn.softmax(sc, axis=-1).astype(vc.dtype), vc)
        yc = oc.reshape(B, hc.shape[1], D_MODEL) @ wo
    return y, yc


def _token_shift_centred(x):
    prev = jnp.pad(x[:, :-1], ((0, 0), (1, 0), (0, 0)))
    nxt = jnp.pad(x[:, 1:], ((0, 0), (0, 1), (0, 0)))
    return 0.5 * (prev + nxt) - x


def _time_major_bidir(u):
    u = jnp.stack([u[0], jnp.flip(u[1], axis=1)])
    return jnp.moveaxis(u, 2, 0).astype(jnp.float32)


def _rwkv_prepare(h, p):
    B, T, _ = h.shape
    hs = (B, T, RW_HEADS, RW_HEAD)
    dhs = (N_DIR,) + hs
    xx = _token_shift_centred(h)
    xr, xw, xk, xv, xa, xg = [h + xx * p['mu'][j] for j in range(6)]
    r = (xr @ p['wr']).reshape(hs)
    k = xk @ p['wk']
    v = (xv @ p['wv']).reshape(hs)
    w = p['w0'][:, None, None] + jnp.einsum('nbtr,nrd->nbtd', jnp.tanh(jnp.einsum('btd,ndr->nbtr', xw, p['w1'])), p['w2'])
    w = -jax.nn.softplus(-w.astype(jnp.float32)) - 0.5
    a = jax.nn.sigmoid(p['a0'][:, None, None] + jnp.einsum('nbtr,nrd->nbtd', jnp.einsum('btd,ndr->nbtr', xa, p['a1']), p['a2']))
    g = jax.nn.sigmoid(xg @ p['g1']) @ p['g2']
    kk = (k * p['k_k']).reshape(hs).astype(jnp.float32)
    kk = kk * lax.rsqrt(jnp.maximum(jnp.sum(kk * kk, -1, keepdims=True), 1e-12))
    k_dir = (k[None] * (1.0 + (a - 1.0) * p['k_a'])).reshape(dhs)
    a = a.reshape(dhs)
    decay = jnp.exp(-jnp.exp(w)).reshape(dhs)
    both = lambda u: jnp.broadcast_to(u, dhs)
    scan_in = (_time_major_bidir(both(r)), _time_major_bidir(decay), _time_major_bidir(k_dir),
               _time_major_bidir(both(v)), _time_major_bidir(both(-kk)), _time_major_bidir(kk[None] * a))
    return scan_in, r, k_dir, v, g


def _rwkv_scan(scan_in, s0, emit):
    def step(s, inp):
        r_t, w_t, k_t, v_t, a_t, b_t = inp
        sa = jnp.einsum('nbhij,nbhj->nbhi', s, a_t)
        s = s * w_t[..., None, :] + sa[..., None] * b_t[..., None, :] + v_t[..., None] * k_t[..., None, :]
        return s, (jnp.einsum('nbhij,nbhj->nbhi', s, r_t) if emit else None)
    return lax.scan(step, s0, scan_in)


def _rwkv_output(ys, r, k_dir, v, g, p):
    y = ys[:, 0] + jnp.flip(ys[:, 1], axis=0)
    y = jnp.moveaxis(y, 0, 1)
    mu = jnp.mean(y, -1, keepdims=True)
    var = jnp.mean(jnp.square(y - mu), -1, keepdims=True)
    y = (y - mu) * lax.rsqrt(var + RW_GN_EPS) * p['lnx_g'].reshape(RW_HEADS, RW_HEAD) + p['lnx_b'].reshape(RW_HEADS, RW_HEAD)
    bonus = jnp.sum(jnp.sum(r[None] * k_dir * p['r_k'], -1, keepdims=True), axis=0) * v
    y = (y + bonus).astype(g.dtype).reshape(g.shape)
    return (y * g) @ p['wo']


def _bidir_rwkv7(h, hc, p, want_ctx):
    B = h.shape[0]
    ctx_in, rc, kc, vc, gc = _rwkv_prepare(hc, p)
    lat_in, r, k, v, g = _rwkv_prepare(h, p)
    s0 = jnp.zeros((N_DIR, B, RW_HEADS, RW_HEAD, RW_HEAD), jnp.float32)
    s_ctx, ys_ctx = _rwkv_scan(ctx_in, s0, want_ctx)
    _, ys = _rwkv_scan(lat_in, s_ctx, True)
    y = _rwkv_output(ys, r, k, v, g, p)
    yc = _rwkv_output(ys_ctx, rc, kc, vc, gc, p) if want_ctx else None
    return y, yc


def setup_inputs(seed: int = 0) -> dict:
    key = jax.random.key(seed)
    keys = iter(jax.random.split(key, 64))

    def nrm(shape, scale):
        return jax.random.normal(next(keys), shape, jnp.float32) * scale

    D = D_MODEL
    sD = D ** -0.5
    return {
        'x': nrm((BATCH, SEQ, D), 1.0),
        'c': nrm((BATCH, D), 1.0),
        'ctx': nrm((BATCH, CTX_LEN, D), 1.0),
        'c_ctx': nrm((D,), 1.0),
        'mod_w': nrm((DEPTH, D, N_MOD * D), 0.5 * sD),
        'mod_b': nrm((DEPTH, N_MOD * D), 0.01),
        'post_ln_g': 1.0 + nrm((DEPTH, 2, D), 0.02),
        'post_ln_b': nrm((DEPTH, 2, D), 0.02),
        'mlp_w1': nrm((DEPTH, D, D_FF), sD),
        'mlp_w2': nrm((DEPTH, D_FF, D), D_FF ** -0.5 * DEEPNORM_BETA),
        'att_wqkv': nrm((N_LAYERS_A, D, ATT_QKV), sD),
        'att_wo': nrm((N_LAYERS_A, ATT_HEADS * ATT_HEAD_DIM, D), sD * DEEPNORM_BETA),
        'att_q_norm': 1.0 + nrm((N_LAYERS_A, ATT_HEAD_DIM), 0.02),
        'att_k_norm': 1.0 + nrm((N_LAYERS_A, ATT_HEAD_DIM), 0.02),
        'na_wqkv': nrm((N_LAYERS_B, D, 3 * D), sD),
        'na_wo': nrm((N_LAYERS_B, D, D), sD * DEEPNORM_BETA),
        'na_rpb': nrm((N_LAYERS_B, NA_HEADS, 2 * NA_WIN_H - 1, 2 * NA_WIN_W - 1), 0.1),
        'rw_mu': jax.random.uniform(next(keys), (N_LAYERS_C, 6, D), jnp.float32),
        'rw_wr': nrm((N_LAYERS_C, D, D), sD),
        'rw_wk': nrm((N_LAYERS_C, D, D), sD),
        'rw_wv': nrm((N_LAYERS_C, D, D), sD),
        'rw_wo': nrm((N_LAYERS_C, D, D), sD * DEEPNORM_BETA),
        'rw_w0': -2.0 + nrm((N_LAYERS_C, N_DIR, D), 0.5),
        'rw_w1': nrm((N_LAYERS_C, N_DIR, D, RW_DECAY_LORA), sD),
        'rw_w2': nrm((N_LAYERS_C, N_DIR, RW_DECAY_LORA, D), 0.5 * RW_DECAY_LORA ** -0.5),
        'rw_a0': nrm((N_LAYERS_C, N_DIR, D), 0.5),
        'rw_a1': nrm((N_LAYERS_C, N_DIR, D, RW_AAA_LORA), sD),
        'rw_a2': nrm((N_LAYERS_C, N_DIR, RW_AAA_LORA, D), 0.5 * RW_AAA_LORA ** -0.5),
        'rw_g1': nrm((N_LAYERS_C, D, RW_GATE_LORA), sD),
        'rw_g2': nrm((N_LAYERS_C, RW_GATE_LORA, D), RW_GATE_LORA ** -0.5),
        'rw_k_k': 0.85 + nrm((N_LAYERS_C, D), 0.02),
        'rw_k_a': 1.0 + nrm((N_LAYERS_C, D), 0.02),
        'rw_r_k': nrm((N_LAYERS_C, RW_HEADS, RW_HEAD), 0.1),
        'rw_lnx_g': 1.0 + nrm((N_LAYERS_C, D), 0.02),
        'rw_lnx_b': nrm((N_LAYERS_C, D), 0.02),
    }


def reference(x, c, ctx, c_ctx, mod_w, mod_b, post_ln_g, post_ln_b, mlp_w1, mlp_w2,
              att_wqkv, att_wo, att_q_norm, att_k_norm, na_wqkv, na_wo, na_rpb,
              rw_mu, rw_wr, rw_wk, rw_wv, rw_wo, rw_w0, rw_w1, rw_w2, rw_a0, rw_a1, rw_a2,
              rw_g1, rw_g2, rw_k_k, rw_k_a, rw_r_k, rw_lnx_g, rw_lnx_b):
    xc = ctx
    for i in range(DEPTH):
        kind, slot = i % N_MIXERS, i // N_MIXERS
        want_ctx = i < DEPTH - 1
        sh1, sc1, gt1, sh2, sc2, gt2 = _modulation(c, mod_w[i], mod_b[i])
        csh1, csc1, cgt1, csh2, csc2, cgt2 = _modulation(c_ctx, mod_w[i], mod_b[i])
        h = x * (1.0 + sc1) + sh1
        hc = xc * (1.0 + csc1) + csh1
        if kind == 0:
            y, yc = _gqa_axial_attention(h, hc, att_wqkv[slot], att_wo[slot], att_q_norm[slot], att_k_norm[slot], want_ctx)
        elif kind == 1:
            y, yc = _neighbourhood_attention(h, hc, na_wqkv[slot], na_wo[slot], na_rpb[slot], want_ctx)
        else:
            p = {'mu': rw_mu[slot], 'wr': rw_wr[slot], 'wk': rw_wk[slot], 'wv': rw_wv[slot], 'wo': rw_wo[slot],
                 'w0': rw_w0[slot], 'w1': rw_w1[slot], 'w2': rw_w2[slot], 'a0': rw_a0[slot], 'a1': rw_a1[slot],
                 'a2': rw_a2[slot], 'g1': rw_g1[slot], 'g2': rw_g2[slot], 'k_k': rw_k_k[slot], 'k_a': rw_k_a[slot],
                 'r_k': rw_r_k[slot], 'lnx_g': rw_lnx_g[slot], 'lnx_b': rw_lnx_b[slot]}
            y, yc = _bidir_rwkv7(h, hc, p, want_ctx)
        x = _layernorm(DEEPNORM_ALPHA * x + gt1 * y, post_ln_g[i, 0], post_ln_b[i, 0])
        x = _layernorm(DEEPNORM_ALPHA * x + gt2 * _sqrelu_mlp(x * (1.0 + sc2) + sh2, mlp_w1[i], mlp_w2[i]),
                       post_ln_g[i, 1], post_ln_b[i, 1])
        if want_ctx:
            xc = _layernorm(DEEPNORM_ALPHA * xc + cgt1 * yc, post_ln_g[i, 0], post_ln_b[i, 0])
            xc = _layernorm(DEEPNORM_ALPHA * xc + cgt2 * _sqrelu_mlp(xc * (1.0 + csc2) + csh2, mlp_w1[i], mlp_w2[i]),
                            post_ln_g[i, 1], post_ln_b[i, 1])
    return x
```

```python
import functools

import numpy as np
import jax
import jax.numpy as jnp
from jax import lax
from jax.experimental import pallas as pl
from jax.experimental.pallas import tpu as pltpu

F32 = jnp.float32
MXU_DTYPE = jnp.bfloat16

GRID_W = 64
N_MOD = 6
N_MIXERS = 3
LN_EPS = 1e-6
ATT_HEAD_DIM = 128
ATT_KV_HEADS = 2
ROPE_BASE = 10000.0
NA_HEAD_DIM = 64
NA_WIN_H = 8
NA_WIN_W = 16
NA_Q_ROWS = 8
NA_K_ROWS = 16
RW_HEAD = 64
RW_GN_EPS = 64e-5
RW_CHUNK = 64
LANES = 128
NEG_BIG = -1e30

VMEM_LIMIT = 56 * 1024 * 1024


def _cparams(*sem):
    return pltpu.CompilerParams(dimension_semantics=sem, vmem_limit_bytes=VMEM_LIMIT)


def _mm(a, b):
    return jnp.dot(a.astype(MXU_DTYPE), b.astype(MXU_DTYPE), preferred_element_type=F32)


def _mm_nt(a, b):
    return lax.dot_general(a.astype(MXU_DTYPE), b.astype(MXU_DTYPE), (((1,), (1,)), ((), ())),
                           preferred_element_type=F32)


def _mm_tn(a, b):
    return lax.dot_general(a.astype(MXU_DTYPE), b.astype(MXU_DTYPE), (((0,), (0,)), ((), ())),
                           preferred_element_type=F32)


def _split(a, n):
    parts = []
    for _ in range(n - 1):
        p = a.astype(MXU_DTYPE)
        parts.append(p)
        a = a - p.astype(F32)
    parts.append(a.astype(MXU_DTYPE))
    return parts


def _mm_lhs_exact(m01, a, n=3):
    m01 = m01.astype(MXU_DTYPE)
    return sum(jnp.dot(m01, p, preferred_element_type=F32) for p in _split(a, n))


def _mm_rhs_exact(a, m01, n=2):
    m01 = m01.astype(MXU_DTYPE)
    return sum(jnp.dot(p, m01, preferred_element_type=F32) for p in _split(a, n))


def _mm_hi(a, b):
    a1, a2 = _split(a, 2)
    b1, b2 = _split(b, 2)
    d = functools.partial(jnp.dot, preferred_element_type=F32)
    return d(a1, b1) + d(a1, b2) + d(a2, b1)


def _sigmoid(x):
    return 1.0 / (1.0 + jnp.exp(-x))


def _layernorm(z, g, b):
    mu = jnp.mean(z, -1, keepdims=True)
    dz = z - mu
    var = jnp.mean(dz * dz, -1, keepdims=True)
    return dz * lax.rsqrt(var + LN_EPS) * g + b


def _mod_part(mod_ref, j, d):
    return mod_ref[0, :, j * d:(j + 1) * d]


def _head_ones(n=LANES):
    r = lax.broadcasted_iota(jnp.int32, (n, n), 0)
    c = lax.broadcasted_iota(jnp.int32, (n, n), 1)
    return ((r >= RW_HEAD) == (c >= RW_HEAD)).astype(F32)


def _modulation_kernel(c_ref, w_ref, b_ref, o_ref):
    c = c_ref[...]
    o_ref[0] = _mm_hi(c * _sigmoid(c), w_ref[0]) + b_ref[0]


def _modulation(cond, mod_w, mod_b):
    depth, d, n = mod_w.shape
    rows = cond.shape[0]
    tn = n // 4
    return pl.pallas_call(
        _modulation_kernel,
        grid=(depth, n // tn),
        in_specs=[pl.BlockSpec((rows, d), lambda i, j: (0, 0)),
                  pl.BlockSpec((1, d, tn), lambda i, j: (i, 0, j)),
                  pl.BlockSpec((1, 1, tn), lambda i, j: (i, 0, j))],
        out_specs=pl.BlockSpec((1, rows, tn), lambda i, j: (i, 0, j)),
        out_shape=jax.ShapeDtypeStruct((depth, rows, n), F32),
        compiler_params=_cparams("parallel", "parallel"),
        name="modulation",
    )(cond, mod_w, mod_b.reshape(depth, 1, n))


def _out_ln_kernel(o_ref, w_ref, x_ref, mod_ref, g_ref, b_ref, y_ref, *, d, alpha):
    y = _mm(o_ref[0], w_ref[...])
    z = alpha * x_ref[0] + _mod_part(mod_ref, 2, d) * y
    y_ref[0] = _layernorm(z, g_ref[...], b_ref[...])


def _mod_spec(d, per_batch):
    if per_batch:
        return pl.BlockSpec((1, 1, N_MOD * d), lambda b, *_: (b, 0, 0))
    return pl.BlockSpec((1, 1, N_MOD * d), lambda b, *_: (0, 0, 0))


def _out_ln(o, wo, x, mod, per_batch, ln_g, ln_b, alpha):
    bsz, t, d = x.shape
    tm = min(512, t)
    tok = lambda b, i: (b, i, 0)
    const = lambda b, i: (0, 0)
    return pl.pallas_call(
        functools.partial(_out_ln_kernel, d=d, alpha=alpha),
        grid=(bsz, t // tm),
        in_specs=[pl.BlockSpec((1, tm, d), tok), pl.BlockSpec((d, d), const), pl.BlockSpec((1, tm, d), tok),
                  _mod_spec(d, per_batch), pl.BlockSpec((1, d), const), pl.BlockSpec((1, d), const)],
        out_specs=pl.BlockSpec((1, tm, d), tok),
        out_shape=jax.ShapeDtypeStruct((bsz, t, d), F32),
        compiler_params=_cparams("parallel", "parallel"),
        name="out_ln",
    )(o, wo, x, mod, ln_g, ln_b)


def _mlp_kernel(x_ref, mod_ref, w1_ref, w2_ref, g_ref, b_ref, y_ref, h_scr, acc_scr, *, d, alpha):
    j = pl.program_id(2)

    @pl.when(j == 0)
    def _():
        h = x_ref[0] * (1.0 + _mod_part(mod_ref, 4, d)) + _mod_part(mod_ref, 3, d)
        h_scr[...] = h.astype(h_scr.dtype)
        acc_scr[...] = jnp.zeros_like(acc_scr)

    a = jnp.square(jnp.maximum(_mm(h_scr[...], w1_ref[...]), 0.0))
    acc_scr[...] += _mm(a, w2_ref[...])

    @pl.when(j == pl.num_programs(2) - 1)
    def _():
        z = alpha * x_ref[0] + _mod_part(mod_ref, 5, d) * acc_scr[...]
        y_ref[0] = _layernorm(z, g_ref[...], b_ref[...])


def _mlp(x, mod, per_batch, w1, w2, ln_g, ln_b, alpha):
    bsz, t, d = x.shape
    ff = w1.shape[1]
    tm = min(1024, t)
    tf = min(512, ff)
    tok = lambda b, i, j: (b, i, 0)
    const = lambda b, i, j: (0, 0)
    return pl.pallas_call(
        functools.partial(_mlp_kernel, d=d, alpha=alpha),
        grid=(bsz, t // tm, ff // tf),
        in_specs=[pl.BlockSpec((1, tm, d), tok), _mod_spec(d, per_batch),
                  pl.BlockSpec((d, tf), lambda b, i, j: (0, j)), pl.BlockSpec((tf, d), lambda b, i, j: (j, 0)),
                  pl.BlockSpec((1, d), const), pl.BlockSpec((1, d), const)],
        out_specs=pl.BlockSpec((1, tm, d), tok),
        out_shape=jax.ShapeDtypeStruct((bsz, t, d), F32),
        scratch_shapes=[pltpu.VMEM((tm, d), MXU_DTYPE), pltpu.VMEM((tm, d), F32)],
        compiler_params=_cparams("parallel", "parallel", "arbitrary"),
        name="mlp",
    )(x, mod, w1, w2, ln_g, ln_b)


def _gqa_qkv_kernel(x_ref, mod_ref, w_ref, gq_ref, gk_ref, cos_ref, sa_ref, sb_ref, q_ref, k_ref, v_ref,
                    *, d, n_q, n_kv):
    h = x_ref[0] * (1.0 + _mod_part(mod_ref, 1, d)) + _mod_part(mod_ref, 0, d)
    acc = _mm(h, w_ref[...])
    cos, sa, sb = cos_ref[...], sa_ref[...], sb_ref[...]
    hd = ATT_HEAD_DIM

    def norm_rope(u, g):
        u = u * lax.rsqrt(jnp.mean(u * u, -1, keepdims=True) + LN_EPS) * g
        return u * cos + pltpu.roll(u, hd - hd // 4, 1) * sa + pltpu.roll(u, hd // 4, 1) * sb

    for i in range(n_q):
        q_ref[0, :, i * hd:(i + 1) * hd] = norm_rope(acc[:, i * hd:(i + 1) * hd], gq_ref[...]).astype(q_ref.dtype)
    for i in range(n_kv):
        c0 = (n_q + i) * hd
        k_ref[0, :, i * hd:(i + 1) * hd] = norm_rope(acc[:, c0:c0 + hd], gk_ref[...]).astype(k_ref.dtype)
    v_ref[0] = acc[:, (n_q + n_kv) * hd:].astype(v_ref.dtype)


def _gqa_qkv(x, mod, per_batch, w, gq, gk, cos, sa, sb):
    bsz, t, d = x.shape
    hd = ATT_HEAD_DIM
    n_kv = ATT_KV_HEADS
    n_q = w.shape[1] // hd - 2 * n_kv
    tm = min(512, t)
    tok = lambda b, i: (b, i, 0)
    const = lambda b, i: (0, 0)
    tab = pl.BlockSpec((tm, hd), lambda b, i: (i, 0))
    return pl.pallas_call(
        functools.partial(_gqa_qkv_kernel, d=d, n_q=n_q, n_kv=n_kv),
        grid=(bsz, t // tm),
        in_specs=[pl.BlockSpec((1, tm, d), tok), _mod_spec(d, per_batch), pl.BlockSpec(w.shape, const),
                  pl.BlockSpec((1, hd), const), pl.BlockSpec((1, hd), const), tab, tab, tab],
        out_specs=[pl.BlockSpec((1, tm, n_q * hd), tok), pl.BlockSpec((1, tm, n_kv * hd), tok),
                   pl.BlockSpec((1, tm, n_kv * hd), tok)],
        out_shape=[jax.ShapeDtypeStruct((bsz, t, n_q * hd), MXU_DTYPE),
                   jax.ShapeDtypeStruct((bsz, t, n_kv * hd), MXU_DTYPE),
                   jax.ShapeDtypeStruct((bsz, t, n_kv * hd), MXU_DTYPE)],
        compiler_params=_cparams("parallel", "parallel"),
        name="gqa_qkv",
    )(x, mod, w, gq, gk, cos, sa, sb)


def _gqa_att_kernel(*refs, n_sets, groups):
    q_ref, o_ref = refs[0], refs[-1]
    kv = refs[1:-1]
    hd = ATT_HEAD_DIM
    for g in range(groups):
        qh = q_ref[0, :, g * hd:(g + 1) * hd]
        ss = [_mm_nt(qh, kv[2 * j][0]) for j in range(n_sets)]
        m = functools.reduce(jnp.maximum, [jnp.max(s, -1, keepdims=True) for s in ss])
        ps = [jnp.exp(s - m) for s in ss]
        l = sum(jnp.sum(p, -1, keepdims=True) for p in ps)
        o = sum(_mm(ps[j], kv[2 * j + 1][0]) for j in range(n_sets))
        o_ref[0, :, g * hd:(g + 1) * hd] = (o / l).astype(o_ref.dtype)


def _gqa_att(q, kv_sets):
    bsz, s, dq = q.shape
    hd = ATT_HEAD_DIM
    n_kv = ATT_KV_HEADS
    groups = dq // hd // n_kv
    tq = min(256, s)
    in_specs = [pl.BlockSpec((1, tq, groups * hd), lambda b, kh, i: (b, i, kh))]
    args = [q]
    for k, v in kv_sets:
        t = k.shape[1]
        spec = pl.BlockSpec((1, t, hd), lambda b, kh, i: (b, 0, kh))
        in_specs += [spec, spec]
        args += [k, v]
    return pl.pallas_call(
        functools.partial(_gqa_att_kernel, n_sets=len(kv_sets), groups=groups),
        grid=(bsz, n_kv, s // tq),
        in_specs=in_specs,
        out_specs=pl.BlockSpec((1, tq, groups * hd), lambda b, kh, i: (b, i, kh)),
        out_shape=jax.ShapeDtypeStruct((bsz, s, dq), MXU_DTYPE),
        compiler_params=_cparams("parallel", "parallel", "parallel"),
        name="gqa_att",
    )(*args)


def _rope_tables(n_tokens, head_dim):
    t = np.arange(n_tokens)
    row = (t // GRID_W).astype(np.float32)
    col = (t % GRID_W).astype(np.float32)
    half = head_dim // 2
    freqs = jnp.asarray(ROPE_BASE, F32) ** (-jnp.arange(0, half, 2, dtype=F32) / half)
    ang_r = jnp.asarray(row)[:, None] * freqs[None, :]
    ang_c = jnp.asarray(col)[:, None] * freqs[None, :]
    ang = jnp.concatenate([ang_r, ang_r, ang_c, ang_c], axis=-1)
    cos, sin = jnp.cos(ang), jnp.sin(ang)
    first = (np.arange(head_dim) % half) < half // 2
    return cos, jnp.where(first, -sin, 0.0), jnp.where(first, 0.0, sin)


def _gqa_layer(x, xc, mod, mod_c, wqkv, wo, q_norm, k_norm, want_ctx):
    s, tc = x.shape[1], xc.shape[1]
    hd = ATT_HEAD_DIM
    gq = (q_norm * hd ** -0.5).reshape(1, hd)
    gk = k_norm.reshape(1, hd)
    cos, sa, sb = _rope_tables(s, hd)
    one, zero = jnp.ones((tc, hd), F32), jnp.zeros((tc, hd), F32)
    q, k, v = _gqa_qkv(x, mod, True, wqkv, gq, gk, cos, sa, sb)
    qc, kc, vc = _gqa_qkv(xc, mod_c, False, wqkv, gq, gk, one, zero, zero)
    o = _gqa_att(q, [(k, v), (kc, vc)])
    oc = _gqa_att(qc, [(kc, vc)]) if want_ctx else None
    return o, oc, wo


def _na_qkv_kernel(x_ref, mod_ref, w_ref, q_ref, k_ref, v_ref, *, d, scale):
    h = x_ref[0] * (1.0 + _mod_part(mod_ref, 1, d)) + _mod_part(mod_ref, 0, d)
    acc = _mm(h, w_ref[...])
    q_ref[0] = (acc[:, :d] * scale).astype(q_ref.dtype)
    k_ref[0] = acc[:, d:2 * d].astype(k_ref.dtype)
    v_ref[0] = acc[:, 2 * d:].astype(v_ref.dtype)


def _na_qkv(x, mod, per_batch, w):
    bsz, t, d = x.shape
    tm = min(512, t)
    tok = lambda b, i: (b, i, 0)
    out = jax.ShapeDtypeStruct((bsz, t, d), MXU_DTYPE)
    return pl.pallas_call(
        functools.partial(_na_qkv_kernel, d=d, scale=NA_HEAD_DIM ** -0.5),
        grid=(bsz, t // tm),
        in_specs=[pl.BlockSpec((1, tm, d), tok), _mod_spec(d, per_batch), pl.BlockSpec(w.shape, lambda b, i: (0, 0))],
        out_specs=[pl.BlockSpec((1, tm, d), tok)] * 3,
        out_shape=[out, out, out],
        compiler_params=_cparams("parallel", "parallel"),
        name="na_qkv",
    )(x, mod, w)


def _pair_softmax_att(q, ks, vs, biases):
    lane = lax.broadcasted_iota(jnp.int32, q.shape, 1)
    first = lane < NA_HEAD_DIM
    zero = jnp.zeros_like(q)
    outs = []
    for h in range(2):
        qm = jnp.where(first, q, zero) if h == 0 else jnp.where(first, zero, q)
        ss = []
        for j, k in enumerate(ks):
            s = _mm_nt(qm, k)
            if biases[h][j] is not None:
                s = s + biases[h][j]
            ss.append(s)
        m = functools.reduce(jnp.maximum, [jnp.max(s, -1, keepdims=True) for s in ss])
        ps = [jnp.exp(s - m) for s in ss]
        l = sum(jnp.sum(p, -1, keepdims=True) for p in ps)
        o = sum(_mm(p, v) for p, v in zip(ps, vs))
        outs.append(o / l)
    return jnp.where(lane < NA_HEAD_DIM, outs[0], outs[1])


def _na_att_kernel(q_ref, k0, k1, k2, k3, v0, v1, v2, v3, kc_ref, vc_ref, bias_ref, o_ref):
    ks = [k0[0], k1[0], k2[0], k3[0], kc_ref[0]]
    vs = [v0[0], v1[0], v2[0], v3[0], vc_ref[0]]
    kb = k0.shape[1]
    biases = [[bias_ref[0, h, :, j * kb:(j + 1) * kb] for j in range(4)] + [None] for h in range(2)]
    o_ref[0] = _pair_softmax_att(q_ref[0], ks, vs, biases).astype(o_ref.dtype)


def _na_bias_tables(rpb, rows):
    n_blk = rows // NA_Q_ROWS
    drs, dcs, valids = [], [], []
    cq = np.arange(GRID_W)
    ck = np.arange(GRID_W)
    c0 = np.clip(cq - NA_WIN_W // 2, 0, GRID_W - NA_WIN_W)
    vcol = (ck[None, :] >= c0[:, None]) & (ck[None, :] < c0[:, None] + NA_WIN_W)
    dcol = np.clip(ck[None, :] - cq[:, None] + NA_WIN_W - 1, 0, 2 * NA_WIN_W - 2)
    for i in (0, min(1, n_blk - 1), n_blk - 1):
        row_q = NA_Q_ROWS * i + np.arange(NA_Q_ROWS)
        start = np.clip(NA_Q_ROWS * i - NA_WIN_H // 2, 0, rows - NA_K_ROWS)
        row_k = start + np.arange(NA_K_ROWS)
        r0 = np.clip(row_q - NA_WIN_H // 2, 0, rows - NA_WIN_H)
        vrow = (row_k[None, :] >= r0[:, None]) & (row_k[None, :] < r0[:, None] + NA_WIN_H)
        drow = np.clip(row_k[None, :] - row_q[:, None] + NA_WIN_H - 1, 0, 2 * NA_WIN_H - 2)
        shape = (NA_Q_ROWS, GRID_W, NA_K_ROWS, GRID_W)
        full = lambda a_row, a_col: (np.broadcast_to(a_row[:, None, :, None], shape),
                                     np.broadcast_to(a_col[None, :, None, :], shape))
        vr, vc = full(vrow, vcol)
        dr, dc = full(drow, dcol)
        nq, nk = NA_Q_ROWS * GRID_W, NA_K_ROWS * GRID_W
        valids.append((vr & vc).reshape(nq, nk))
        drs.append(dr.reshape(nq, nk))
        dcs.append(dc.reshape(nq, nk))
    dr, dc, valid = np.stack(drs), np.stack(dcs), np.stack(valids)
    bias = jnp.where(valid[None], rpb[:, dr, dc], NEG_BIG)
    return jnp.transpose(bias, (1, 0, 2, 3))


def _na_att(q, k, v, kc, vc, bias):
    bsz, s, d = q.shape
    tc = kc.shape[1]
    rows = s // GRID_W
    n_blk = rows // NA_Q_ROWS
    tq = NA_Q_ROWS * GRID_W
    kb = NA_K_ROWS * GRID_W // 4
    n_kb = s // kb
    pairs = d // LANES

    def kv_spec(j):
        return pl.BlockSpec((1, kb, LANES),
                            lambda hp, i, b: (b, jnp.clip(2 * i - 1, 0, n_kb - 4) + j, hp))

    ctx_spec = pl.BlockSpec((1, tc, LANES), lambda hp, i, b: (b, 0, hp))
    bias_spec = pl.BlockSpec(
        (1, 2, tq, 4 * kb),
        lambda hp, i, b: ((i > 0).astype(jnp.int32) + (i == n_blk - 1).astype(jnp.int32), hp, 0, 0))
    q_spec = pl.BlockSpec((1, tq, LANES), lambda hp, i, b: (b, i, hp))
    return pl.pallas_call(
        _na_att_kernel,
        grid=(pairs, n_blk, bsz),
        in_specs=[q_spec] + [kv_spec(j) for j in range(4)] + [kv_spec(j) for j in range(4)]
                 + [ctx_spec, ctx_spec, bias_spec],
        out_specs=q_spec,
        out_shape=jax.ShapeDtypeStruct((bsz, s, d), MXU_DTYPE),
        compiler_params=_cparams("parallel", "parallel", "parallel"),
        name="na_att",
    )(q, k, k, k, k, v, v, v, v, kc, vc, bias)


def _pair_att_kernel(q_ref, k_ref, v_ref, o_ref):
    o_ref[0] = _pair_softmax_att(q_ref[0], [k_ref[0]], [v_ref[0]], [[None], [None]]).astype(o_ref.dtype)


def _pair_att(q, k, v):
    bsz, t, d = q.shape
    spec = pl.BlockSpec((1, t, LANES), lambda b, hp: (b, 0, hp))
    return pl.pallas_call(
        _pair_att_kernel,
        grid=(bsz, d // LANES),
        in_specs=[spec, spec, spec],
        out_specs=spec,
        out_shape=jax.ShapeDtypeStruct((bsz, t, d), MXU_DTYPE),
        compiler_params=_cparams("parallel", "parallel"),
        name="na_ctx_att",
    )(q, k, v)


def _na_layer(x, xc, mod, mod_c, wqkv, wo, rpb, want_ctx):
    rows = x.shape[1] // GRID_W
    assert rows % NA_Q_ROWS == 0 and rows >= NA_K_ROWS
    q, k, v = _na_qkv(x, mod, True, wqkv)
    qc, kc, vc = _na_qkv(xc, mod_c, False, wqkv)
    o = _na_att(q, k, v, kc, vc, _na_bias_tables(rpb, rows))
    oc = _pair_att(qc, kc, vc) if want_ctx else None
    return o, oc, wo


def _rw_proj_kernel(x_ref, xp_ref, xn_ref, mod_ref, mu_ref, wr_ref, wk_ref, wv_ref, w1_ref, w2_ref, a1_ref, a2_ref,
                    g1_ref, g2_ref, w0_ref, a0_ref, r_ref, k_ref, v_ref, g_ref, lw_ref, as_ref, *, d):
    t = pl.program_id(1)
    nt = pl.num_programs(1)
    scale = 1.0 + _mod_part(mod_ref, 1, d)
    shift = _mod_part(mod_ref, 0, d)
    h = x_ref[0] * scale + shift
    tm = h.shape[0]
    h_prev = (xp_ref[0, 7:8, :] * scale + shift) * (t > 0).astype(F32)
    h_next = (xn_ref[0, 0:1, :] * scale + shift) * (t < nt - 1).astype(F32)
    row = lax.broadcasted_iota(jnp.int32, h.shape, 0)
    prev = jnp.where(row == 0, h_prev, pltpu.roll(h, 1, 0))
    nxt = jnp.where(row == tm - 1, h_next, pltpu.roll(h, tm - 1, 0))
    xx = 0.5 * (prev + nxt) - h
    mix = lambda j: h + xx * mu_ref[j:j + 1, :]

    r_ref[0] = _mm(mix(0), wr_ref[...]).astype(r_ref.dtype)
    k_ref[0] = _mm(mix(2), wk_ref[...]).astype(k_ref.dtype)
    v_ref[0] = _mm(mix(3), wv_ref[...]).astype(v_ref.dtype)
    g_ref[0] = _mm(_sigmoid(_mm(mix(5), g1_ref[...])), g2_ref[...]).astype(g_ref.dtype)

    tw = jnp.tanh(_mm(mix(1), w1_ref[...]))
    al = _mm(mix(4), a1_ref[...])
    first = lax.broadcasted_iota(jnp.int32, tw.shape, 1) < tw.shape[1] // 2
    zero = jnp.zeros_like(tw)
    for n in range(2):
        pick = lambda u: jnp.where(first, u, zero) if n == 0 else jnp.where(first, zero, u)
        z = -(w0_ref[n:n + 1, :] + _mm(pick(tw), w2_ref[...]))
        softplus = jnp.maximum(z, 0.0) + jnp.log(1.0 + jnp.exp(-jnp.abs(z)))
        lw_ref[n, 0] = -jnp.exp(-softplus - 0.5)
        as_ref[n, 0] = _sigmoid(a0_ref[n:n + 1, :] + _mm(pick(al), a2_ref[...])).astype(as_ref.dtype)


def _rw_proj(x, mod, per_batch, p):
    bsz, t, d = x.shape
    tm = min(256, t)
    n8 = t // 8
    tok = lambda b, i: (b, i, 0)
    dtok = lambda b, i: (0, b, i, 0)
    full = lambda a: pl.BlockSpec(a.shape, lambda b, i: (0,) * a.ndim)
    halo_prev = pl.BlockSpec((1, 8, d), lambda b, i: (b, jnp.maximum(i * (tm // 8) - 1, 0), 0))
    halo_next = pl.BlockSpec((1, 8, d), lambda b, i: (b, jnp.minimum((i + 1) * (tm // 8), n8 - 1), 0))
    weights = [p['mu'], p['wr'], p['wk'], p['wv'], p['w1'], p['w2'], p['a1'], p['a2'], p['g1'], p['g2'],
               p['w0'], p['a0']]
    one = jax.ShapeDtypeStruct((bsz, t, d), F32)
    two = jax.ShapeDtypeStruct((2, bsz, t, d), F32)
    return pl.pallas_call(
        functools.partial(_rw_proj_kernel, d=d),
        grid=(bsz, t // tm),
        in_specs=[pl.BlockSpec((1, tm, d), tok), halo_prev, halo_next, _mod_spec(d, per_batch)]
                 + [full(a) for a in weights],
        out_specs=[pl.BlockSpec((1, tm, d), tok)] * 4 + [pl.BlockSpec((2, 1, tm, d), dtok)] * 2,
        out_shape=[one, one, one, one, two, two],
        compiler_params=_cparams("parallel", "parallel"),
        name="rw_proj",
    )(x, x, x, mod, *weights)


def _rw_scan_kernel(r_ref, k_ref, v_ref, lw_ref, as_ref, kk_ref, ka_ref, z0_ref, y_ref, zf_ref, z_scr, *, pairs):
    n = pl.program_id(0)
    c = pl.program_id(3)
    L = RW_CHUNK
    H = RW_HEAD

    @pl.when(c == 0)
    def _():
        z_scr[...] = z0_ref[0, 0]

    sign = 1 - 2 * n
    t_i = lax.broadcasted_iota(jnp.int32, (L, L), 0)
    s_i = lax.broadcasted_iota(jnp.int32, (L, L), 1)
    tri_incl = (sign * (s_i - t_i) <= 0).astype(F32)
    r2 = lax.broadcasted_iota(jnp.int32, (2 * L, 2 * L), 0)
    c2 = lax.broadcasted_iota(jnp.int32, (2 * L, 2 * L), 1)
    same = (r2 >= L) == (c2 >= L)
    dt = jnp.where(same, sign * ((c2 & (L - 1)) - (r2 & (L - 1))), L)
    strict2 = dt < 0
    incl2 = dt <= 0
    eye2 = (r2 == c2).astype(F32)
    same_head = _head_ones()
    first = lax.broadcasted_iota(jnp.int32, (L, LANES), 1) < H
    zero = jnp.zeros((L, LANES), F32)

    head0 = lambda u: jnp.where(first, u, zero)
    head1 = lambda u: jnp.where(first, zero, u)
    stack = lambda u: jnp.concatenate([u, u], axis=0)
    unstack = lambda u: jnp.where(first, u[:L], u[L:])

    for p in range(pairs):
        sl = slice(p * LANES, (p + 1) * LANES)
        lw = lw_ref[0, 0, :, sl]
        r = r_ref[0, :, sl].astype(F32)
        k = k_ref[0, :, sl].astype(F32)
        v = v_ref[0, :, sl].astype(F32)
        a_s = as_ref[0, 0, :, sl].astype(F32)

        c_in = _mm_lhs_exact(tri_incl, lw)
        c_ex = c_in - lw
        c_all = jnp.sum(lw, axis=0, keepdims=True)

        kk = k * kk_ref[:, sl]
        ss = _mm_rhs_exact(kk * kk, same_head)
        kk = kk * lax.rsqrt(jnp.maximum(ss, 1e-12))
        b_v = kk * a_s
        k_d = k * (1.0 + (a_s - 1.0) * ka_ref[:, sl])

        e_neg = jnp.exp(-c_in)
        e_rem = jnp.exp(c_all - c_in)
        a_t = -kk * jnp.exp(c_ex)
        r_t = r * jnp.exp(c_in)
        b_t, k_t = b_v * e_neg, k_d * e_neg
        b_h, k_h = b_v * e_rem, k_d * e_rem

        lhs = jnp.concatenate([head0(a_t), head1(a_t), head0(r_t), head1(r_t)], axis=0)
        rhs = jnp.concatenate([b_t, b_t, k_t, k_t], axis=0)
        x = _mm_nt(lhs, rhs)
        m_ab = jnp.where(strict2, x[:2 * L, :2 * L], 0.0)
        m_ak = jnp.where(strict2, x[:2 * L, 2 * L:], 0.0)
        n_rb = jnp.where(incl2, x[2 * L:, :2 * L], 0.0)
        n_rk = jnp.where(incl2, x[2 * L:, 2 * L:], 0.0)

        inv = eye2 + m_ab
        pw = m_ab
        for _ in range(int(np.log2(L)) - 1):
            pw = _mm(pw, pw)
            inv = inv + _mm(inv, pw)

        v2 = stack(v)
        w = unstack(_mm(m_ak, v2))
        tw = _mm(inv, jnp.concatenate([stack(a_t), stack(w)], axis=1))
        p1, p2 = unstack(tw[:, :LANES]), unstack(tw[:, LANES:])
        nw = _mm(n_rb, jnp.concatenate([stack(p1), stack(p2)], axis=1))
        q1 = r_t + unstack(nw[:, :LANES])
        q2 = unstack(nw[:, LANES:] + _mm(n_rk, v2))
        g_t = eye2 * jnp.exp(c_all) + same_head * _mm_tn(b_h, p1)
        h_t = same_head * (_mm_tn(b_h, p2) + _mm_tn(k_h, v))

        z = z_scr[p]
        y_ref[0, 0, :, sl] = _mm(q1, z) + q2
        z_scr[p] = _mm(g_t, z) + h_t

    @pl.when(c == pl.num_programs(3) - 1)
    def _():
        zf_ref[0, 0] = z_scr[...]


def _rw_scan(r, k, v, lw, a_s, kk_w, ka_w, z0):
    bsz, t, d = r.shape
    L = RW_CHUNK
    n_c = t // L
    pairs = 4
    n_grp = d // (pairs * LANES)
    w = pairs * LANES
    cidx = lambda n, c: c + n * (n_c - 1 - 2 * c)
    tok = pl.BlockSpec((1, L, w), lambda n, b, g, c: (b, cidx(n, c), g))
    dtok = pl.BlockSpec((1, 1, L, w), lambda n, b, g, c: (n, b, cidx(n, c), g))
    vec = pl.BlockSpec((1, w), lambda n, b, g, c: (0, g))
    state = pl.BlockSpec((1, 1, pairs, LANES, LANES), lambda n, b, g, c: (n, b, g, 0, 0))
    return pl.pallas_call(
        functools.partial(_rw_scan_kernel, pairs=pairs),
        grid=(2, bsz, n_grp, n_c),
        in_specs=[tok, tok, tok, dtok, dtok, vec, vec, state],
        out_specs=[dtok, state],
        out_shape=[jax.ShapeDtypeStruct((2, bsz, t, d), F32),
                   jax.ShapeDtypeStruct((2, bsz, d // LANES, LANES, LANES), F32)],
        scratch_shapes=[pltpu.VMEM((pairs, LANES, LANES), F32)],
        compiler_params=_cparams("parallel", "parallel", "parallel", "arbitrary"),
        name="rw_scan",
    )(r, k, v, lw, a_s, kk_w, ka_w, z0)


def _rw_out_kernel(y_ref, r_ref, k_ref, v_ref, g_ref, as_ref, ka_ref, rk_ref, lg_ref, lb_ref, wo_ref, x_ref, mod_ref,
                   g_ln_ref, b_ln_ref, o_ref, u_scr, *, d, alpha):
    same_head = _head_ones()
    inv_n = 1.0 / RW_HEAD
    for p in range(d // LANES):
        sl = slice(p * LANES, (p + 1) * LANES)
        y = y_ref[0, 0, :, sl] + y_ref[1, 0, :, sl]
        mu = _mm_rhs_exact(y, same_head) * inv_n
        dy = y - mu
        var = _mm_rhs_exact(dy * dy, same_head) * inv_n
        yn = dy * lax.rsqrt(var + RW_GN_EPS) * lg_ref[:, sl] + lb_ref[:, sl]
        k = k_ref[0, :, sl].astype(F32)
        ka = ka_ref[:, sl]
        k_sum = k * ((1.0 + (as_ref[0, 0, :, sl].astype(F32) - 1.0) * ka)
                     + (1.0 + (as_ref[1, 0, :, sl].astype(F32) - 1.0) * ka))
        bonus = _mm_rhs_exact(r_ref[0, :, sl].astype(F32) * k_sum * rk_ref[:, sl], same_head)
        u = (yn + bonus * v_ref[0, :, sl].astype(F32)) * g_ref[0, :, sl].astype(F32)
        u_scr[:, sl] = u.astype(u_scr.dtype)
    z = alpha * x_ref[0] + _mod_part(mod_ref, 2, d) * _mm(u_scr[...], wo_ref[...])
    o_ref[0] = _layernorm(z, g_ln_ref[...], b_ln_ref[...])


def _rw_out(y, r, k, v, g, a_s, p, x, mod, per_batch, ln_g, ln_b, alpha):
    bsz, t, d = x.shape
    tm = min(256, t)
    tok = pl.BlockSpec((1, tm, d), lambda b, i: (b, i, 0))
    dtok = pl.BlockSpec((2, 1, tm, d), lambda b, i: (0, b, i, 0))
    vec = pl.BlockSpec((1, d), lambda b, i: (0, 0))
    return pl.pallas_call(
        functools.partial(_rw_out_kernel, d=d, alpha=alpha),
        grid=(bsz, t // tm),
        in_specs=[dtok, tok, tok, tok, tok, dtok, vec, vec, vec, vec, pl.BlockSpec((d, d), lambda b, i: (0, 0)),
                  tok, _mod_spec(d, per_batch), vec, vec],
        out_specs=tok,
        out_shape=jax.ShapeDtypeStruct((bsz, t, d), F32),
        scratch_shapes=[pltpu.VMEM((tm, d), MXU_DTYPE)],
        compiler_params=_cparams("parallel", "parallel"),
        name="rw_out",
    )(y, r, k, v, g, a_s, p['k_a'], p['r_k'], p['lnx_g'], p['lnx_b'], p['wo'], x, mod, ln_g, ln_b)


def _rw_layer(x, xc, mod, mod_c, p, want_ctx, ln_g, ln_b, alpha):
    bsz, _, d = x.shape
    rc, kc, vc, gc, lwc, asc = _rw_proj(xc, mod_c, False, p)
    r, k, v, g, lw, a_s = _rw_proj(x, mod, True, p)
    z0 = jnp.zeros((2, bsz, d // LANES, LANES, LANES), F32)
    yc, zc = _rw_scan(rc, kc, vc, lwc, asc, p['k_k'], p['k_a'], z0)
    y, _ = _rw_scan(r, k, v, lw, a_s, p['k_k'], p['k_a'], zc)
    x_new = _rw_out(y, r, k, v, g, a_s, p, x, mod, True, ln_g, ln_b, alpha)
    xc_new = _rw_out(yc, rc, kc, vc, gc, asc, p, xc, mod_c, False, ln_g, ln_b, alpha) if want_ctx else None
    return x_new, xc_new


def kernel(x, c, ctx, c_ctx, mod_w, mod_b, post_ln_g, post_ln_b, mlp_w1, mlp_w2, att_wqkv, att_wo, att_q_norm, att_k_norm, na_wqkv, na_wo, na_rpb, rw_mu, rw_wr, rw_wk, rw_wv, rw_wo, rw_w0, rw_w1, rw_w2, rw_a0, rw_a1, rw_a2, rw_g1, rw_g2, rw_k_k, rw_k_a, rw_r_k, rw_lnx_g, rw_lnx_b):
    depth = mod_w.shape[0]
    bsz, _, d = x.shape
    alpha = (2.0 * depth) ** 0.25
    cast = lambda a: a.astype(MXU_DTYPE)

    cond_rows = -(-(bsz + 1) // 8) * 8
    cond = jnp.zeros((cond_rows, d), F32).at[:bsz].set(c).at[bsz].set(c_ctx)
    mods = _modulation(cond, mod_w, mod_b)

    xc = ctx
    for i in range(depth):
        kind, slot = i % N_MIXERS, i // N_MIXERS
        want_ctx = i < depth - 1
        mod = mods[i, :bsz].reshape(bsz, 1, N_MOD * d)
        mod_c = mods[i, bsz:bsz + 1].reshape(1, 1, N_MOD * d)
        g1, b1 = post_ln_g[i, 0:1], post_ln_b[i, 0:1]
        g2, b2 = post_ln_g[i, 1:2], post_ln_b[i, 1:2]
        if kind == 2:
            cat = lambda a: jnp.concatenate([a[0], a[1]], axis=-1)
            p = {'mu': rw_mu[slot], 'wr': cast(rw_wr[slot]), 'wk': cast(rw_wk[slot]), 'wv': cast(rw_wv[slot]),
                 'wo': cast(rw_wo[slot]), 'w0': rw_w0[slot], 'a0': rw_a0[slot],
                 'w1': cast(cat(rw_w1[slot])), 'a1': cast(cat(rw_a1[slot])),
                 'w2': cast(rw_w2[slot].reshape(-1, d)), 'a2': cast(rw_a2[slot].reshape(-1, d)),
                 'g1': cast(rw_g1[slot]), 'g2': cast(rw_g2[slot]),
                 'k_k': rw_k_k[slot].reshape(1, d), 'k_a': rw_k_a[slot].reshape(1, d),
                 'r_k': rw_r_k[slot].reshape(1, d), 'lnx_g': rw_lnx_g[slot].reshape(1, d),
                 'lnx_b': rw_lnx_b[slot].reshape(1, d)}
            x, xc_new = _rw_layer(x, xc, mod, mod_c, p, want_ctx, g1, b1, alpha)
        else:
            if kind == 0:
                o, oc, wo = _gqa_layer(x, xc, mod, mod_c, cast(att_wqkv[slot]), cast(att_wo[slot]),
                                       att_q_norm[slot], att_k_norm[slot], want_ctx)
            else:
                o, oc, wo = _na_layer(x, xc, mod, mod_c, cast(na_wqkv[slot]), cast(na_wo[slot]), na_rpb[slot],
                                      want_ctx)
            x = _out_ln(o, wo, x, mod, True, g1, b1, alpha)
            xc_new = _out_ln(oc, wo, xc, mod_c, False, g1, b1, alpha) if want_ctx else None
        w1, w2 = cast(mlp_w1[i]), cast(mlp_w2[i])
        x = _mlp(x, mod, True, w1, w2, g2, b2, alpha)
        if want_ctx:
            xc = _mlp(xc_new, mod_c, False, w1, w2, g2, b2, alpha)
    return x
```

```python
import functools

import numpy as np
import jax
import jax.numpy as jnp
from jax import lax
from jax.experimental import pallas as pl
from jax.experimental.pallas import tpu as pltpu

F32 = jnp.float32
MXU_DTYPE = jnp.bfloat16

GRID_W = 64
N_MOD = 6
N_MIXERS = 3
LN_EPS = 1e-6
ATT_HEAD_DIM = 128
ATT_KV_HEADS = 2
ROPE_BASE = 10000.0
GQA_KEY_CHUNK = 512
NA_HEAD_DIM = 64
NA_WIN_H = 8
NA_WIN_W = 16
NA_Q_ROWS = 8
NA_K_ROWS = 16
RW_HEAD = 64
RW_GN_EPS = 64e-5
RW_CHUNK = 64
LANES = 128
NEG_BIG = -1e30
LOG2E = 1.4426950408889634

VMEM_LIMIT = 56 * 1024 * 1024


def _cparams(*sem):
    return pltpu.CompilerParams(dimension_semantics=sem, vmem_limit_bytes=VMEM_LIMIT)


def _mm(a, b):
    return jnp.dot(a.astype(MXU_DTYPE), b.astype(MXU_DTYPE), preferred_element_type=F32)


def _mm_nt(a, b):
    return lax.dot_general(a.astype(MXU_DTYPE), b.astype(MXU_DTYPE), (((1,), (1,)), ((), ())),
                           preferred_element_type=F32)


def _mm_tn(a, b):
    return lax.dot_general(a.astype(MXU_DTYPE), b.astype(MXU_DTYPE), (((0,), (0,)), ((), ())),
                           preferred_element_type=F32)


def _split(a, n):
    parts = []
    for _ in range(n - 1):
        p = a.astype(MXU_DTYPE)
        parts.append(p)
        a = a - p.astype(F32)
    parts.append(a.astype(MXU_DTYPE))
    return parts


def _mm_lhs_exact(m01, a, n=3):
    m01 = m01.astype(MXU_DTYPE)
    return sum(jnp.dot(m01, p, preferred_element_type=F32) for p in _split(a, n))


def _mm_rhs_exact(a, m01, n=2):
    m01 = m01.astype(MXU_DTYPE)
    return sum(jnp.dot(p, m01, preferred_element_type=F32) for p in _split(a, n))


def _mm_hi(a, b):
    a1, a2 = _split(a, 2)
    b1, b2 = _split(b, 2)
    d = functools.partial(jnp.dot, preferred_element_type=F32)
    return d(a1, b1) + d(a1, b2) + d(a2, b1)


def _sigmoid(x):
    return 1.0 / (1.0 + jnp.exp(-x))


def _layernorm(z, g, b):
    mu = jnp.mean(z, -1, keepdims=True)
    dz = z - mu
    var = jnp.mean(dz * dz, -1, keepdims=True)
    return dz * lax.rsqrt(var + LN_EPS) * g + b


def _mod_part(mod_ref, j, d):
    return mod_ref[0, :, j * d:(j + 1) * d]


def _head_ones(n=LANES):
    r = lax.broadcasted_iota(jnp.int32, (n, n), 0)
    c = lax.broadcasted_iota(jnp.int32, (n, n), 1)
    return ((r >= RW_HEAD) == (c >= RW_HEAD)).astype(F32)


def _modulation_kernel(c_ref, w_ref, b_ref, o_ref):
    c = c_ref[...]
    o_ref[0] = _mm_hi(c * _sigmoid(c), w_ref[0]) + b_ref[0]


def _modulation(cond, mod_w, mod_b):
    depth, d, n = mod_w.shape
    rows = cond.shape[0]
    tn = n // 4
    return pl.pallas_call(
        _modulation_kernel,
        grid=(depth, n // tn),
        in_specs=[pl.BlockSpec((rows, d), lambda i, j: (0, 0)),
                  pl.BlockSpec((1, d, tn), lambda i, j: (i, 0, j)),
                  pl.BlockSpec((1, 1, tn), lambda i, j: (i, 0, j))],
        out_specs=pl.BlockSpec((1, rows, tn), lambda i, j: (i, 0, j)),
        out_shape=jax.ShapeDtypeStruct((depth, rows, n), F32),
        compiler_params=_cparams("parallel", "parallel"),
        name="modulation",
    )(cond, mod_w, mod_b.reshape(depth, 1, n))


def _out_ln_kernel(o_ref, w_ref, x_ref, mod_ref, g_ref, b_ref, y_ref, *, d, alpha):
    y = _mm(o_ref[0], w_ref[...])
    z = alpha * x_ref[0] + _mod_part(mod_ref, 2, d) * y
    y_ref[0] = _layernorm(z, g_ref[...], b_ref[...])


def _mod_spec(d, per_batch):
    if per_batch:
        return pl.BlockSpec((1, 1, N_MOD * d), lambda b, *_: (b, 0, 0))
    return pl.BlockSpec((1, 1, N_MOD * d), lambda b, *_: (0, 0, 0))


def _out_ln(o, wo, x, mod, per_batch, ln_g, ln_b, alpha):
    bsz, t, d = x.shape
    tm = min(512, t)
    tok = lambda b, i: (b, i, 0)
    const = lambda b, i: (0, 0)
    return pl.pallas_call(
        functools.partial(_out_ln_kernel, d=d, alpha=alpha),
        grid=(bsz, t // tm),
        in_specs=[pl.BlockSpec((1, tm, d), tok), pl.BlockSpec((d, d), const), pl.BlockSpec((1, tm, d), tok),
                  _mod_spec(d, per_batch), pl.BlockSpec((1, d), const), pl.BlockSpec((1, d), const)],
        out_specs=pl.BlockSpec((1, tm, d), tok),
        out_shape=jax.ShapeDtypeStruct((bsz, t, d), F32),
        compiler_params=_cparams("parallel", "parallel"),
        name="out_ln",
    )(o, wo, x, mod, ln_g, ln_b)


def _mlp_kernel(x_ref, mod_ref, w1_ref, w2_ref, g_ref, b_ref, y_ref, h_scr, acc_scr, *, d, alpha):
    j = pl.program_id(2)

    @pl.when(j == 0)
    def _():
        h = x_ref[0] * (1.0 + _mod_part(mod_ref, 4, d)) + _mod_part(mod_ref, 3, d)
        h_scr[...] = h.astype(h_scr.dtype)
        acc_scr[...] = jnp.zeros_like(acc_scr)

    a = jnp.square(jnp.maximum(_mm(h_scr[...], w1_ref[...]), 0.0))
    acc_scr[...] += _mm(a, w2_ref[...])

    @pl.when(j == pl.num_programs(2) - 1)
    def _():
        z = alpha * x_ref[0] + _mod_part(mod_ref, 5, d) * acc_scr[...]
        y_ref[0] = _layernorm(z, g_ref[...], b_ref[...])


def _mlp(x, mod, per_batch, w1, w2, ln_g, ln_b, alpha):
    bsz, t, d = x.shape
    ff = w1.shape[1]
    tm = min(1024, t)
    tf = min(512, ff)
    tok = lambda b, i, j: (b, i, 0)
    const = lambda b, i, j: (0, 0)
    return pl.pallas_call(
        functools.partial(_mlp_kernel, d=d, alpha=alpha),
        grid=(bsz, t // tm, ff // tf),
        in_specs=[pl.BlockSpec((1, tm, d), tok), _mod_spec(d, per_batch),
                  pl.BlockSpec((d, tf), lambda b, i, j: (0, j)), pl.BlockSpec((tf, d), lambda b, i, j: (j, 0)),
                  pl.BlockSpec((1, d), const), pl.BlockSpec((1, d), const)],
        out_specs=pl.BlockSpec((1, tm, d), tok),
        out_shape=jax.ShapeDtypeStruct((bsz, t, d), F32),
        scratch_shapes=[pltpu.VMEM((tm, d), MXU_DTYPE), pltpu.VMEM((tm, d), F32)],
        compiler_params=_cparams("parallel", "parallel", "arbitrary"),
        name="mlp",
    )(x, mod, w1, w2, ln_g, ln_b)


def _gqa_qkv_kernel(x_ref, mod_ref, w_ref, gq_ref, gk_ref, cos_ref, sa_ref, sb_ref, q_ref, k_ref, v_ref,
                    *, d, n_q, n_kv):
    h = x_ref[0] * (1.0 + _mod_part(mod_ref, 1, d)) + _mod_part(mod_ref, 0, d)
    acc = _mm(h, w_ref[...])
    cos, sa, sb = cos_ref[...], sa_ref[...], sb_ref[...]
    hd = ATT_HEAD_DIM

    def norm_rope(u, g):
        u = u * lax.rsqrt(jnp.mean(u * u, -1, keepdims=True) + LN_EPS) * g
        return u * cos + pltpu.roll(u, hd - hd // 4, 1) * sa + pltpu.roll(u, hd // 4, 1) * sb

    for i in range(n_q):
        q_ref[0, :, i * hd:(i + 1) * hd] = norm_rope(acc[:, i * hd:(i + 1) * hd], gq_ref[...]).astype(q_ref.dtype)
    for i in range(n_kv):
        c0 = (n_q + i) * hd
        k_ref[0, :, i * hd:(i + 1) * hd] = norm_rope(acc[:, c0:c0 + hd], gk_ref[...]).astype(k_ref.dtype)
    ones = jnp.ones((acc.shape[0], hd), v_ref.dtype)
    for i in range(n_kv):
        c0 = (n_q + n_kv + i) * hd
        v_ref[0, :, 2 * i * hd:(2 * i + 1) * hd] = acc[:, c0:c0 + hd].astype(v_ref.dtype)
        v_ref[0, :, (2 * i + 1) * hd:(2 * i + 2) * hd] = ones


def _gqa_qkv(x, mod, per_batch, w, gq, gk, cos, sa, sb):
    bsz, t, d = x.shape
    hd = ATT_HEAD_DIM
    n_kv = ATT_KV_HEADS
    n_q = w.shape[1] // hd - 2 * n_kv
    tm = min(512, t)
    tok = lambda b, i: (b, i, 0)
    const = lambda b, i: (0, 0)
    tab = pl.BlockSpec((tm, hd), lambda b, i: (i, 0))
    return pl.pallas_call(
        functools.partial(_gqa_qkv_kernel, d=d, n_q=n_q, n_kv=n_kv),
        grid=(bsz, t // tm),
        in_specs=[pl.BlockSpec((1, tm, d), tok), _mod_spec(d, per_batch), pl.BlockSpec(w.shape, const),
                  pl.BlockSpec((1, hd), const), pl.BlockSpec((1, hd), const), tab, tab, tab],
        out_specs=[pl.BlockSpec((1, tm, n_q * hd), tok), pl.BlockSpec((1, tm, n_kv * hd), tok),
                   pl.BlockSpec((1, tm, 2 * n_kv * hd), tok)],
        out_shape=[jax.ShapeDtypeStruct((bsz, t, n_q * hd), MXU_DTYPE),
                   jax.ShapeDtypeStruct((bsz, t, n_kv * hd), MXU_DTYPE),
                   jax.ShapeDtypeStruct((bsz, t, 2 * n_kv * hd), MXU_DTYPE)],
        compiler_params=_cparams("parallel", "parallel"),
        name="gqa_qkv",
    )(x, mod, w, gq, gk, cos, sa, sb)


def _gqa_att_kernel(*refs, n_sets, groups, tk):
    q_ref, o_ref = refs[0], refs[-1]
    kv = refs[1:-1]
    hd = ATT_HEAD_DIM
    tq = q_ref.shape[1]
    q = jnp.concatenate([q_ref[0, :, g * hd:(g + 1) * hd] for g in range(groups)], axis=0)
    m = acc = None
    for j in range(n_sets):
        k_ref, v_ref = kv[2 * j], kv[2 * j + 1]
        t = k_ref.shape[1]
        for c0 in range(0, t, tk):
            c1 = min(c0 + tk, t)
            s = _mm_nt(q, k_ref[0, c0:c1, :])
            m_c = jnp.max(s, -1, keepdims=True)
            if m is None:
                m = m_c
                acc = _mm(jnp.exp2(s - m), v_ref[0, c0:c1, :])
            else:
                m_new = jnp.maximum(m, m_c)
                acc = acc * jnp.exp2(m - m_new) + _mm(jnp.exp2(s - m_new), v_ref[0, c0:c1, :])
                m = m_new
    o = acc[:, :hd] / acc[:, hd:]
    for g in range(groups):
        o_ref[0, :, g * hd:(g + 1) * hd] = o[g * tq:(g + 1) * tq].astype(o_ref.dtype)


def _gqa_att(q, kv_sets):
    bsz, s, dq = q.shape
    hd = ATT_HEAD_DIM
    n_kv = ATT_KV_HEADS
    groups = dq // hd // n_kv
    tq = min(256, s)
    in_specs = [pl.BlockSpec((1, tq, groups * hd), lambda b, kh, i: (b, i, kh))]
    args = [q]
    for k, v in kv_sets:
        t = k.shape[1]
        in_specs += [pl.BlockSpec((1, t, hd), lambda b, kh, i: (b, 0, kh)),
                     pl.BlockSpec((1, t, 2 * hd), lambda b, kh, i: (b, 0, kh))]
        args += [k, v]
    return pl.pallas_call(
        functools.partial(_gqa_att_kernel, n_sets=len(kv_sets), groups=groups, tk=GQA_KEY_CHUNK),
        grid=(bsz, n_kv, s // tq),
        in_specs=in_specs,
        out_specs=pl.BlockSpec((1, tq, groups * hd), lambda b, kh, i: (b, i, kh)),
        out_shape=jax.ShapeDtypeStruct((bsz, s, dq), MXU_DTYPE),
        compiler_params=_cparams("parallel", "parallel", "parallel"),
        name="gqa_att",
    )(*args)


def _rope_tables(n_tokens, head_dim):
    t = np.arange(n_tokens)
    row = (t // GRID_W).astype(np.float32)
    col = (t % GRID_W).astype(np.float32)
    half = head_dim // 2
    freqs = jnp.asarray(ROPE_BASE, F32) ** (-jnp.arange(0, half, 2, dtype=F32) / half)
    ang_r = jnp.asarray(row)[:, None] * freqs[None, :]
    ang_c = jnp.asarray(col)[:, None] * freqs[None, :]
    ang = jnp.concatenate([ang_r, ang_r, ang_c, ang_c], axis=-1)
    cos, sin = jnp.cos(ang), jnp.sin(ang)
    first = (np.arange(head_dim) % half) < half // 2
    return cos, jnp.where(first, -sin, 0.0), jnp.where(first, 0.0, sin)


def _gqa_layer(x, xc, mod, mod_c, wqkv, wo, q_norm, k_norm, want_ctx):
    s, tc = x.shape[1], xc.shape[1]
    hd = ATT_HEAD_DIM
    gq = (q_norm * (hd ** -0.5 * LOG2E)).reshape(1, hd)
    gk = k_norm.reshape(1, hd)
    cos, sa, sb = _rope_tables(s, hd)
    one, zero = jnp.ones((tc, hd), F32), jnp.zeros((tc, hd), F32)
    q, k, v = _gqa_qkv(x, mod, True, wqkv, gq, gk, cos, sa, sb)
    qc, kc, vc = _gqa_qkv(xc, mod_c, False, wqkv, gq, gk, one, zero, zero)
    o = _gqa_att(q, [(k, v), (kc, vc)])
    oc = _gqa_att(qc, [(kc, vc)]) if want_ctx else None
    return o, oc, wo


def _na_qkv_kernel(x_ref, mod_ref, w_ref, q_ref, k_ref, v_ref, *, d, scale):
    h = x_ref[0] * (1.0 + _mod_part(mod_ref, 1, d)) + _mod_part(mod_ref, 0, d)
    acc = _mm(h, w_ref[...])
    q_ref[0] = (acc[:, :d] * scale).astype(q_ref.dtype)
    k_ref[0] = acc[:, d:2 * d].astype(k_ref.dtype)
    ones = jnp.ones((acc.shape[0], LANES), v_ref.dtype)
    for p in range(d // LANES):
        c0 = 2 * d + p * LANES
        v_ref[0, :, 2 * p * LANES:(2 * p + 1) * LANES] = acc[:, c0:c0 + LANES].astype(v_ref.dtype)
        v_ref[0, :, (2 * p + 1) * LANES:(2 * p + 2) * LANES] = ones


def _na_qkv(x, mod, per_batch, w):
    bsz, t, d = x.shape
    tm = min(512, t)
    tok = lambda b, i: (b, i, 0)
    out = jax.ShapeDtypeStruct((bsz, t, d), MXU_DTYPE)
    return pl.pallas_call(
        functools.partial(_na_qkv_kernel, d=d, scale=NA_HEAD_DIM ** -0.5 * LOG2E),
        grid=(bsz, t // tm),
        in_specs=[pl.BlockSpec((1, tm, d), tok), _mod_spec(d, per_batch), pl.BlockSpec(w.shape, lambda b, i: (0, 0))],
        out_specs=[pl.BlockSpec((1, tm, d), tok)] * 2 + [pl.BlockSpec((1, tm, 2 * d), tok)],
        out_shape=[out, out, jax.ShapeDtypeStruct((bsz, t, 2 * d), MXU_DTYPE)],
        compiler_params=_cparams("parallel", "parallel"),
        name="na_qkv",
    )(x, mod, w)


def _pair_softmax_att(q, ks, vs, biases):
    lane = lax.broadcasted_iota(jnp.int32, q.shape, 1)
    first = lane < NA_HEAD_DIM
    zero = jnp.zeros_like(q)
    qm = [jnp.where(first, q, zero), jnp.where(first, zero, q)]
    m, acc = [None, None], [None, None]
    for j, (k, v) in enumerate(zip(ks, vs)):
        for h in range(2):
            s = _mm_nt(qm[h], k)
            if biases[h][j] is not None:
                s = s + biases[h][j]
            m_c = jnp.max(s, -1, keepdims=True)
            if m[h] is None:
                m[h] = m_c
                acc[h] = _mm(jnp.exp2(s - m_c), v)
            else:
                m_new = jnp.maximum(m[h], m_c)
                acc[h] = acc[h] * jnp.exp2(m[h] - m_new) + _mm(jnp.exp2(s - m_new), v)
                m[h] = m_new
    outs = [a[:, :LANES] / a[:, LANES:] for a in acc]
    return jnp.where(first, outs[0], outs[1])


def _na_att_kernel(q_ref, k0, k1, k2, k3, v0, v1, v2, v3, kc_ref, vc_ref, bias_ref, o_ref):
    ks = [kc_ref[0], k0[0], k1[0], k2[0], k3[0]]
    vs = [vc_ref[0], v0[0], v1[0], v2[0], v3[0]]
    kb = k0.shape[1]
    biases = [[None] + [bias_ref[0, h, :, j * kb:(j + 1) * kb] for j in range(4)] for h in range(2)]
    o_ref[0] = _pair_softmax_att(q_ref[0], ks, vs, biases).astype(o_ref.dtype)


def _na_bias_tables(rpb, rows, scale):
    n_blk = rows // NA_Q_ROWS
    heads = rpb.shape[0]
    half_h, half_w = NA_WIN_H // 2, NA_WIN_W // 2
    pad = GRID_W
    rpb_p = jnp.pad(rpb, ((0, 0), (0, 0), (pad, pad)))
    cmat = jnp.stack([rpb_p[:, :, pad + NA_WIN_W - 1 - cq: pad + NA_WIN_W - 1 - cq + GRID_W]
                      for cq in range(GRID_W)], axis=2)
    cq, ck = np.arange(GRID_W)[:, None], np.arange(GRID_W)[None, :]
    c0 = np.clip(cq - half_w, 0, GRID_W - NA_WIN_W)
    vcol = (ck >= c0) & (ck < c0 + NA_WIN_W)
    cmat = jnp.where(vcol[None, None], cmat * scale, NEG_BIG)
    n_d = 2 * NA_WIN_H - 1
    cmat = jnp.concatenate([jnp.full((heads, NA_K_ROWS, GRID_W, GRID_W), NEG_BIG, F32), cmat,
                            jnp.full((heads, NA_K_ROWS, GRID_W, GRID_W), NEG_BIG, F32)], axis=1)
    types = []
    for i in (0, min(1, n_blk - 1), n_blk - 1):
        start = int(np.clip(NA_Q_ROWS * i - half_h, 0, rows - NA_K_ROWS))
        row_k = start + np.arange(NA_K_ROWS)
        per_row = []
        for rq in range(NA_Q_ROWS):
            row_q = NA_Q_ROWS * i + rq
            r0 = int(np.clip(row_q - half_h, 0, rows - NA_WIN_H))
            vrow = (row_k >= r0) & (row_k < r0 + NA_WIN_H)
            d0 = start - row_q + NA_WIN_H - 1 + NA_K_ROWS
            assert 0 <= d0 and d0 + NA_K_ROWS <= n_d + 2 * NA_K_ROWS
            blk = jnp.where(vrow[None, :, None, None], cmat[:, d0:d0 + NA_K_ROWS], NEG_BIG)
            per_row.append(jnp.transpose(blk, (0, 2, 1, 3)))
        types.append(jnp.stack(per_row, axis=1).reshape(heads, NA_Q_ROWS * GRID_W, NA_K_ROWS * GRID_W))
    return jnp.stack(types)


def _na_att(q, k, v, kc, vc, bias):
    bsz, s, d = q.shape
    tc = kc.shape[1]
    rows = s // GRID_W
    n_blk = rows // NA_Q_ROWS
    tq = NA_Q_ROWS * GRID_W
    kb = NA_K_ROWS * GRID_W // 4
    n_kb = s // kb
    pairs = d // LANES

    def kv_spec(j, width):
        return pl.BlockSpec((1, kb, width),
                            lambda hp, i, b: (b, jnp.clip(2 * i - 1, 0, n_kb - 4) + j, hp))

    ctx_spec = lambda width: pl.BlockSpec((1, tc, width), lambda hp, i, b: (b, 0, hp))
    bias_spec = pl.BlockSpec(
        (1, 2, tq, 4 * kb),
        lambda hp, i, b: ((i > 0).astype(jnp.int32) + (i == n_blk - 1).astype(jnp.int32), hp, 0, 0))
    q_spec = pl.BlockSpec((1, tq, LANES), lambda hp, i, b: (b, i, hp))
    return pl.pallas_call(
        _na_att_kernel,
        grid=(pairs, n_blk, bsz),
        in_specs=[q_spec] + [kv_spec(j, LANES) for j in range(4)] + [kv_spec(j, 2 * LANES) for j in range(4)]
                 + [ctx_spec(LANES), ctx_spec(2 * LANES), bias_spec],
        out_specs=q_spec,
        out_shape=jax.ShapeDtypeStruct((bsz, s, d), MXU_DTYPE),
        compiler_params=_cparams("parallel", "parallel", "parallel"),
        name="na_att",
    )(q, k, k, k, k, v, v, v, v, kc, vc, bias)


def _pair_att_kernel(q_ref, k_ref, v_ref, o_ref):
    o_ref[0] = _pair_softmax_att(q_ref[0], [k_ref[0]], [v_ref[0]], [[None], [None]]).astype(o_ref.dtype)


def _pair_att(q, k, v):
    bsz, t, d = q.shape
    spec = pl.BlockSpec((1, t, LANES), lambda b, hp: (b, 0, hp))
    return pl.pallas_call(
        _pair_att_kernel,
        grid=(bsz, d // LANES),
        in_specs=[spec, spec, pl.BlockSpec((1, t, 2 * LANES), lambda b, hp: (b, 0, hp))],
        out_specs=spec,
        out_shape=jax.ShapeDtypeStruct((bsz, t, d), MXU_DTYPE),
        compiler_params=_cparams("parallel", "parallel"),
        name="na_ctx_att",
    )(q, k, v)


def _na_layer(x, xc, mod, mod_c, wqkv, wo, rpb, want_ctx):
    rows = x.shape[1] // GRID_W
    assert rows % NA_Q_ROWS == 0 and rows >= NA_K_ROWS
    q, k, v = _na_qkv(x, mod, True, wqkv)
    qc, kc, vc = _na_qkv(xc, mod_c, False, wqkv)
    o = _na_att(q, k, v, kc, vc, _na_bias_tables(rpb, rows, LOG2E))
    oc = _pair_att(qc, kc, vc) if want_ctx else None
    return o, oc, wo


def _rw_proj_kernel(x_ref, xp_ref, xn_ref, mod_ref, mu_ref, wr_ref, wk_ref, wv_ref, w1_ref, w2_ref, a1_ref, a2_ref,
                    g1_ref, g2_ref, w0_ref, a0_ref, r_ref, k_ref, v_ref, g_ref, lw_ref, as_ref, *, d):
    t = pl.program_id(1)
    nt = pl.num_programs(1)
    scale = 1.0 + _mod_part(mod_ref, 1, d)
    shift = _mod_part(mod_ref, 0, d)
    h = x_ref[0] * scale + shift
    tm = h.shape[0]
    h_prev = (xp_ref[0, 7:8, :] * scale + shift) * (t > 0).astype(F32)
    h_next = (xn_ref[0, 0:1, :] * scale + shift) * (t < nt - 1).astype(F32)
    row = lax.broadcasted_iota(jnp.int32, h.shape, 0)
    prev = jnp.where(row == 0, h_prev, pltpu.roll(h, 1, 0))
    nxt = jnp.where(row == tm - 1, h_next, pltpu.roll(h, tm - 1, 0))
    xx = 0.5 * (prev + nxt) - h
    mix = lambda j: h + xx * mu_ref[j:j + 1, :]

    r_ref[0] = _mm(mix(0), wr_ref[...]).astype(r_ref.dtype)
    k_ref[0] = _mm(mix(2), wk_ref[...]).astype(k_ref.dtype)
    v_ref[0] = _mm(mix(3), wv_ref[...]).astype(v_ref.dtype)
    g_ref[0] = _mm(_sigmoid(_mm(mix(5), g1_ref[...])), g2_ref[...]).astype(g_ref.dtype)

    tw = jnp.tanh(_mm(mix(1), w1_ref[...]))
    al = _mm(mix(4), a1_ref[...])
    first = lax.broadcasted_iota(jnp.int32, tw.shape, 1) < tw.shape[1] // 2
    zero = jnp.zeros_like(tw)
    for n in range(2):
        pick = lambda u: jnp.where(first, u, zero) if n == 0 else jnp.where(first, zero, u)
        z = -(w0_ref[n:n + 1, :] + _mm(pick(tw), w2_ref[...]))
        softplus = jnp.maximum(z, 0.0) + jnp.log(1.0 + jnp.exp(-jnp.abs(z)))
        lw_ref[n, 0] = -jnp.exp(-softplus - 0.5)
        as_ref[n, 0] = _sigmoid(a0_ref[n:n + 1, :] + _mm(pick(al), a2_ref[...])).astype(as_ref.dtype)


def _rw_proj(x, mod, per_batch, p):
    bsz, t, d = x.shape
    tm = min(256, t)
    n8 = t // 8
    tok = lambda b, i: (b, i, 0)
    dtok = lambda b, i: (0, b, i, 0)
    full = lambda a: pl.BlockSpec(a.shape, lambda b, i: (0,) * a.ndim)
    halo_prev = pl.BlockSpec((1, 8, d), lambda b, i: (b, jnp.maximum(i * (tm // 8) - 1, 0), 0))
    halo_next = pl.BlockSpec((1, 8, d), lambda b, i: (b, jnp.minimum((i + 1) * (tm // 8), n8 - 1), 0))
    weights = [p['mu'], p['wr'], p['wk'], p['wv'], p['w1'], p['w2'], p['a1'], p['a2'], p['g1'], p['g2'],
               p['w0'], p['a0']]
    one = jax.ShapeDtypeStruct((bsz, t, d), F32)
    two = jax.ShapeDtypeStruct((2, bsz, t, d), F32)
    return pl.pallas_call(
        functools.partial(_rw_proj_kernel, d=d),
        grid=(bsz, t // tm),
        in_specs=[pl.BlockSpec((1, tm, d), tok), halo_prev, halo_next, _mod_spec(d, per_batch)]
                 + [full(a) for a in weights],
        out_specs=[pl.BlockSpec((1, tm, d), tok)] * 4 + [pl.BlockSpec((2, 1, tm, d), dtok)] * 2,
        out_shape=[one, one, one, one, two, two],
        compiler_params=_cparams("parallel", "parallel"),
        name="rw_proj",
    )(x, x, x, mod, *weights)


def _rw_scan_kernel(r_ref, k_ref, v_ref, lw_ref, as_ref, kk_ref, ka_ref, z0_ref, y_ref, zf_ref, z_scr, *, pairs):
    n = pl.program_id(0)
    c = pl.program_id(2)
    L = RW_CHUNK
    H = RW_HEAD

    @pl.when(c == 0)
    def _():
        z_scr[...] = z0_ref[0, 0]

    sign = 1 - 2 * n
    t_i = lax.broadcasted_iota(jnp.int32, (L, L), 0)
    s_i = lax.broadcasted_iota(jnp.int32, (L, L), 1)
    tri_incl = (sign * (s_i - t_i) <= 0).astype(F32)
    r2 = lax.broadcasted_iota(jnp.int32, (2 * L, 2 * L), 0)
    c2 = lax.broadcasted_iota(jnp.int32, (2 * L, 2 * L), 1)
    same = (r2 >= L) == (c2 >= L)
    dt = jnp.where(same, sign * ((c2 & (L - 1)) - (r2 & (L - 1))), L)
    strict2 = dt < 0
    incl2 = dt <= 0
    eye2 = (r2 == c2).astype(F32)
    same_head = _head_ones()
    first = lax.broadcasted_iota(jnp.int32, (L, LANES), 1) < H
    zero = jnp.zeros((L, LANES), F32)

    head0 = lambda u: jnp.where(first, u, zero)
    head1 = lambda u: jnp.where(first, zero, u)
    stack = lambda u: jnp.concatenate([u, u], axis=0)
    unstack = lambda u: jnp.where(first, u[:L], u[L:])

    each = lambda fn, *lists: [fn(*args) for args in zip(*lists)]
    sls = [slice(p * LANES, (p + 1) * LANES) for p in range(pairs)]
    cat0 = lambda *u: jnp.concatenate(u, axis=0)
    cat1 = lambda *u: jnp.concatenate(u, axis=1)

    lw = [lw_ref[0, 0, :, sl] for sl in sls]
    r = [r_ref[0, :, sl].astype(F32) for sl in sls]
    k = [k_ref[0, :, sl].astype(F32) for sl in sls]
    v = [v_ref[0, :, sl].astype(F32) for sl in sls]
    a_s = [as_ref[0, 0, :, sl].astype(F32) for sl in sls]

    c_in = each(lambda u: _mm_lhs_exact(tri_incl, u, 2), lw)
    c_all = each(lambda u: jnp.sum(u, axis=0, keepdims=True), lw)
    kk = each(lambda u, sl: u * kk_ref[:, sl], k, sls)
    ss = each(lambda u: _mm(u * u, same_head), kk)
    kk = each(lambda u, s: u * lax.rsqrt(jnp.maximum(s, 1e-12)), kk, ss)
    b_v = each(lambda u, a: u * a, kk, a_s)
    k_d = each(lambda u, a, sl: u * (1.0 + (a - 1.0) * ka_ref[:, sl]), k, a_s, sls)
    e_neg = each(lambda ci: jnp.exp(-ci), c_in)
    e_rem = each(lambda ci, ca: jnp.exp(ca - ci), c_in, c_all)
    a_t = each(lambda u, ci, l: -u * jnp.exp(ci - l), kk, c_in, lw)
    r_t = each(lambda u, ci: u * jnp.exp(ci), r, c_in)
    b_t = each(lambda u, e: u * e, b_v, e_neg)
    k_t = each(lambda u, e: u * e, k_d, e_neg)
    b_h = each(lambda u, e: u * e, b_v, e_rem)
    k_h = each(lambda u, e: u * e, k_d, e_rem)

    x = each(lambda a, rr, b, kt: _mm_nt(cat0(head0(a), head1(a), head0(rr), head1(rr)), cat0(b, b, kt, kt)),
             a_t, r_t, b_t, k_t)
    m_ab = each(lambda u: jnp.where(strict2, u[:2 * L, :2 * L], 0.0), x)
    m_ak = each(lambda u: jnp.where(strict2, u[:2 * L, 2 * L:], 0.0), x)
    n_rb = each(lambda u: jnp.where(incl2, u[2 * L:, :2 * L], 0.0), x)
    n_rk = each(lambda u: jnp.where(incl2, u[2 * L:, 2 * L:], 0.0), x)

    inv = each(lambda m: eye2 + m, m_ab)
    pw = each(lambda m: _mm(m, m), m_ab)
    for _ in range(int(np.log2(L)) - 2):
        st = each(lambda a, b: _mm(cat0(a, b), a), pw, inv)
        pw = each(lambda s: s[:2 * L], st)
        inv = each(lambda b, s: b + s[2 * L:], inv, st)
    inv = each(lambda b, a: b + _mm(b, a), inv, pw)

    v2 = each(stack, v)
    mv = each(lambda a, b, u: _mm(cat0(a, b), u), m_ak, n_rk, v2)
    w = each(lambda u: unstack(u[:2 * L]), mv)
    tw = each(lambda i, a, u: _mm(i, cat1(stack(a), stack(u))), inv, a_t, w)
    p1 = each(lambda u: unstack(u[:, :LANES]), tw)
    p2 = each(lambda u: unstack(u[:, LANES:]), tw)
    nw = each(lambda m, a, b: _mm(m, cat1(stack(a), stack(b))), n_rb, p1, p2)
    q1 = each(lambda rr, u: rr + unstack(u[:, :LANES]), r_t, nw)
    q2 = each(lambda u, m: unstack(u[:, LANES:] + m[2 * L:]), nw, mv)
    gh = each(lambda b, kh, a, c_, u: _mm_tn(cat0(b, kh), cat0(cat1(a, c_), cat1(zero, u))), b_h, k_h, p1, p2, v)
    g_t = each(lambda ca, u: eye2 * jnp.exp(ca) + same_head * u[:, :LANES], c_all, gh)
    h_t = each(lambda u: same_head * u[:, LANES:], gh)

    for p in range(pairs):
        yz = _mm(cat0(q1[p], g_t[p]), z_scr[p])
        y_ref[0, 0, :, sls[p]] = yz[:L] + q2[p]
        z_scr[p] = yz[L:] + h_t[p]

    @pl.when(c == pl.num_programs(2) - 1)
    def _():
        zf_ref[0, 0] = z_scr[...]


def _rw_scan(r, k, v, lw, a_s, kk_w, ka_w, z0):
    bsz, t, d = r.shape
    L = RW_CHUNK
    n_c = t // L
    pairs = d // LANES
    cidx = lambda n, c: c + n * (n_c - 1 - 2 * c)
    tok = pl.BlockSpec((1, L, d), lambda n, b, c: (b, cidx(n, c), 0))
    dtok = pl.BlockSpec((1, 1, L, d), lambda n, b, c: (n, b, cidx(n, c), 0))
    vec = pl.BlockSpec((1, d), lambda n, b, c: (0, 0))
    state = pl.BlockSpec((1, 1, pairs, LANES, LANES), lambda n, b, c: (n, b, 0, 0, 0))
    return pl.pallas_call(
        functools.partial(_rw_scan_kernel, pairs=pairs),
        grid=(2, bsz, n_c),
        in_specs=[tok, tok, tok, dtok, dtok, vec, vec, state],
        out_specs=[dtok, state],
        out_shape=[jax.ShapeDtypeStruct((2, bsz, t, d), F32),
                   jax.ShapeDtypeStruct((2, bsz, pairs, LANES, LANES), F32)],
        scratch_shapes=[pltpu.VMEM((pairs, LANES, LANES), F32)],
        compiler_params=_cparams("parallel", "parallel", "arbitrary"),
        name="rw_scan",
    )(r, k, v, lw, a_s, kk_w, ka_w, z0)


def _rw_out_kernel(y_ref, r_ref, k_ref, v_ref, g_ref, as_ref, ka_ref, rk_ref, lg_ref, lb_ref, wo_ref, x_ref, mod_ref,
                   g_ln_ref, b_ln_ref, o_ref, u_scr, *, d, alpha):
    same_head = _head_ones()
    inv_n = 1.0 / RW_HEAD
    for p in range(d // LANES):
        sl = slice(p * LANES, (p + 1) * LANES)
        y = y_ref[0, 0, :, sl] + y_ref[1, 0, :, sl]
        mu = _mm_rhs_exact(y, same_head) * inv_n
        dy = y - mu
        var = _mm(dy * dy, same_head) * inv_n
        yn = dy * lax.rsqrt(var + RW_GN_EPS) * lg_ref[:, sl] + lb_ref[:, sl]
        k = k_ref[0, :, sl].astype(F32)
        ka = ka_ref[:, sl]
        k_sum = k * ((1.0 + (as_ref[0, 0, :, sl].astype(F32) - 1.0) * ka)
                     + (1.0 + (as_ref[1, 0, :, sl].astype(F32) - 1.0) * ka))
        bonus = _mm(r_ref[0, :, sl].astype(F32) * k_sum * rk_ref[:, sl], same_head)
        u = (yn + bonus * v_ref[0, :, sl].astype(F32)) * g_ref[0, :, sl].astype(F32)
        u_scr[:, sl] = u.astype(u_scr.dtype)
    z = alpha * x_ref[0] + _mod_part(mod_ref, 2, d) * _mm(u_scr[...], wo_ref[...])
    o_ref[0] = _layernorm(z, g_ln_ref[...], b_ln_ref[...])


def _rw_out(y, r, k, v, g, a_s, p, x, mod, per_batch, ln_g, ln_b, alpha):
    bsz, t, d = x.shape
    tm = min(256, t)
    tok = pl.BlockSpec((1, tm, d), lambda b, i: (b, i, 0))
    dtok = pl.BlockSpec((2, 1, tm, d), lambda b, i: (0, b, i, 0))
    vec = pl.BlockSpec((1, d), lambda b, i: (0, 0))
    return pl.pallas_call(
        functools.partial(_rw_out_kernel, d=d, alpha=alpha),
        grid=(bsz, t // tm),
        in_specs=[dtok, tok, tok, tok, tok, dtok, vec, vec, vec, vec, pl.BlockSpec((d, d), lambda b, i: (0, 0)),
                  tok, _mod_spec(d, per_batch), vec, vec],
        out_specs=tok,
        out_shape=jax.ShapeDtypeStruct((bsz, t, d), F32),
        scratch_shapes=[pltpu.VMEM((tm, d), MXU_DTYPE)],
        compiler_params=_cparams("parallel", "parallel"),
        name="rw_out",
    )(y, r, k, v, g, a_s, p['k_a'], p['r_k'], p['lnx_g'], p['lnx_b'], p['wo'], x, mod, ln_g, ln_b)


def _rw_layer(x, xc, mod, mod_c, p, want_ctx, ln_g, ln_b, alpha):
    bsz, _, d = x.shape
    rc, kc, vc, gc, lwc, asc = _rw_proj(xc, mod_c, False, p)
    r, k, v, g, lw, a_s = _rw_proj(x, mod, True, p)
    z0 = jnp.zeros((2, bsz, d // LANES, LANES, LANES), F32)
    yc, zc = _rw_scan(rc, kc, vc, lwc, asc, p['k_k'], p['k_a'], z0)
    y, _ = _rw_scan(r, k, v, lw, a_s, p['k_k'], p['k_a'], zc)
    x_new = _rw_out(y, r, k, v, g, a_s, p, x, mod, True, ln_g, ln_b, alpha)
    xc_new = _rw_out(yc, rc, kc, vc, gc, asc, p, xc, mod_c, False, ln_g, ln_b, alpha) if want_ctx else None
    return x_new, xc_new


def kernel(x, c, ctx, c_ctx, mod_w, mod_b, post_ln_g, post_ln_b, mlp_w1, mlp_w2, att_wqkv, att_wo, att_q_norm, att_k_norm, na_wqkv, na_wo, na_rpb, rw_mu, rw_wr, rw_wk, rw_wv, rw_wo, rw_w0, rw_w1, rw_w2, rw_a0, rw_a1, rw_a2, rw_g1, rw_g2, rw_k_k, rw_k_a, rw_r_k, rw_lnx_g, rw_lnx_b):
    depth = mod_w.shape[0]
    bsz, _, d = x.shape
    alpha = (2.0 * depth) ** 0.25
    cast = lambda a: a.astype(MXU_DTYPE)

    cond_rows = -(-(bsz + 1) // 8) * 8
    cond = jnp.zeros((cond_rows, d), F32).at[:bsz].set(c).at[bsz].set(c_ctx)
    mods = _modulation(cond, mod_w, mod_b)

    xc = ctx
    for i in range(depth):
        kind, slot = i % N_MIXERS, i // N_MIXERS
        want_ctx = i < depth - 1
        mod = mods[i, :bsz].reshape(bsz, 1, N_MOD * d)
        mod_c = mods[i, bsz:bsz + 1].reshape(1, 1, N_MOD * d)
        g1, b1 = post_ln_g[i, 0:1], post_ln_b[i, 0:1]
        g2, b2 = post_ln_g[i, 1:2], post_ln_b[i, 1:2]
        if kind == 2:
            cat = lambda a: jnp.concatenate([a[0], a[1]], axis=-1)
            p = {'mu': rw_mu[slot], 'wr': cast(rw_wr[slot]), 'wk': cast(rw_wk[slot]), 'wv': cast(rw_wv[slot]),
                 'wo': cast(rw_wo[slot]), 'w0': rw_w0[slot], 'a0': rw_a0[slot],
                 'w1': cast(cat(rw_w1[slot])), 'a1': cast(cat(rw_a1[slot])),
                 'w2': cast(rw_w2[slot].reshape(-1, d)), 'a2': cast(rw_a2[slot].reshape(-1, d)),
                 'g1': cast(rw_g1[slot]), 'g2': cast(rw_g2[slot]),
                 'k_k': rw_k_k[slot].reshape(1, d), 'k_a': rw_k_a[slot].reshape(1, d),
                 'r_k': rw_r_k[slot].reshape(1, d), 'lnx_g': rw_lnx_g[slot].reshape(1, d),
                 'lnx_b': rw_lnx_b[slot].reshape(1, d)}
            x, xc_new = _rw_layer(x, xc, mod, mod_c, p, want_ctx, g1, b1, alpha)
        else:
            if kind == 0:
                o, oc, wo = _gqa_layer(x, xc, mod, mod_c, cast(att_wqkv[slot]), cast(att_wo[slot]),
                                       att_q_norm[slot], att_k_norm[slot], want_ctx)
            else:
                o, oc, wo = _na_layer(x, xc, mod, mod_c, cast(na_wqkv[slot]), cast(na_wo[slot]), na_rpb[slot],
                                      want_ctx)
            x = _out_ln(o, wo, x, mod, True, g1, b1, alpha)
            xc_new = _out_ln(oc, wo, xc, mod_c, False, g1, b1, alpha) if want_ctx else None
        w1, w2 = cast(mlp_w1[i]), cast(mlp_w2[i])
        x = _mlp(x, mod, True, w1, w2, g2, b2, alpha)
        if want_ctx:
            xc = _mlp(xc_new, mod_c, False, w1, w2, g2, b2, alpha)
    return x
```

```python
import functools

import numpy as np
import jax
import jax.numpy as jnp
from jax import lax
from jax.experimental import pallas as pl
from jax.experimental.pallas import tpu as pltpu

F32 = jnp.float32
MXU_DTYPE = jnp.bfloat16

GRID_W = 64
N_MOD = 6
N_MIXERS = 3
LN_EPS = 1e-6
ATT_HEAD_DIM = 128
ATT_KV_HEADS = 2
ROPE_BASE = 10000.0
GQA_KEY_CHUNK = 512
NA_HEAD_DIM = 64
NA_WIN_H = 8
NA_WIN_W = 16
NA_Q_ROWS = 8
NA_K_ROWS = 16
RW_HEAD = 64
RW_GN_EPS = 64e-5
RW_CHUNK = 64
LANES = 128
NEG_BIG = -1e30
LOG2E = 1.4426950408889634

VMEM_LIMIT = 56 * 1024 * 1024


def _cparams(*sem):
    return pltpu.CompilerParams(dimension_semantics=sem, vmem_limit_bytes=VMEM_LIMIT)


def _mm(a, b):
    return jnp.dot(a.astype(MXU_DTYPE), b.astype(MXU_DTYPE), preferred_element_type=F32)


def _mm_nt(a, b):
    return lax.dot_general(a.astype(MXU_DTYPE), b.astype(MXU_DTYPE), (((1,), (1,)), ((), ())),
                           preferred_element_type=F32)


def _mm_tn(a, b):
    return lax.dot_general(a.astype(MXU_DTYPE), b.astype(MXU_DTYPE), (((0,), (0,)), ((), ())),
                           preferred_element_type=F32)


def _split(a, n):
    parts = []
    for _ in range(n - 1):
        p = a.astype(MXU_DTYPE)
        parts.append(p)
        a = a - p.astype(F32)
    parts.append(a.astype(MXU_DTYPE))
    return parts


def _mm_lhs_exact(m01, a, n=3):
    m01 = m01.astype(MXU_DTYPE)
    return sum(jnp.dot(m01, p, preferred_element_type=F32) for p in _split(a, n))


def _mm_rhs_exact(a, m01, n=2):
    m01 = m01.astype(MXU_DTYPE)
    return sum(jnp.dot(p, m01, preferred_element_type=F32) for p in _split(a, n))


def _mm_hi(a, b):
    a1, a2 = _split(a, 2)
    b1, b2 = _split(b, 2)
    d = functools.partial(jnp.dot, preferred_element_type=F32)
    return d(a1, b1) + d(a1, b2) + d(a2, b1)


def _sigmoid(x):
    return 1.0 / (1.0 + jnp.exp(-x))


def _layernorm(z, g, b):
    mu = jnp.mean(z, -1, keepdims=True)
    dz = z - mu
    var = jnp.mean(dz * dz, -1, keepdims=True)
    return dz * lax.rsqrt(var + LN_EPS) * g + b


def _mod_part(mod_ref, j, d):
    return mod_ref[0, :, j * d:(j + 1) * d]


def _head_ones(n=LANES):
    r = lax.broadcasted_iota(jnp.int32, (n, n), 0)
    c = lax.broadcasted_iota(jnp.int32, (n, n), 1)
    return ((r >= RW_HEAD) == (c >= RW_HEAD)).astype(F32)


def _modulation_kernel(c_ref, w_ref, b_ref, o_ref):
    c = c_ref[...]
    o_ref[0] = _mm_hi(c * _sigmoid(c), w_ref[0]) + b_ref[0]


def _modulation(cond, mod_w, mod_b):
    depth, d, n = mod_w.shape
    rows = cond.shape[0]
    tn = n // 4
    return pl.pallas_call(
        _modulation_kernel,
        grid=(depth, n // tn),
        in_specs=[pl.BlockSpec((rows, d), lambda i, j: (0, 0)),
                  pl.BlockSpec((1, d, tn), lambda i, j: (i, 0, j)),
                  pl.BlockSpec((1, 1, tn), lambda i, j: (i, 0, j))],
        out_specs=pl.BlockSpec((1, rows, tn), lambda i, j: (i, 0, j)),
        out_shape=jax.ShapeDtypeStruct((depth, rows, n), F32),
        compiler_params=_cparams("parallel", "parallel"),
        name="modulation",
    )(cond, mod_w, mod_b.reshape(depth, 1, n))


def _out_ln_kernel(o_ref, w_ref, x_ref, mod_ref, g_ref, b_ref, y_ref, *, d, alpha):
    y = _mm(o_ref[0], w_ref[...])
    z = alpha * x_ref[0] + _mod_part(mod_ref, 2, d) * y
    y_ref[0] = _layernorm(z, g_ref[...], b_ref[...])


def _mod_spec(d, per_batch):
    if per_batch:
        return pl.BlockSpec((1, 1, N_MOD * d), lambda b, *_: (b, 0, 0))
    return pl.BlockSpec((1, 1, N_MOD * d), lambda b, *_: (0, 0, 0))


def _out_ln(o, wo, x, mod, per_batch, ln_g, ln_b, alpha):
    bsz, t, d = x.shape
    tm = min(512, t)
    tok = lambda b, i: (b, i, 0)
    const = lambda b, i: (0, 0)
    return pl.pallas_call(
        functools.partial(_out_ln_kernel, d=d, alpha=alpha),
        grid=(bsz, t // tm),
        in_specs=[pl.BlockSpec((1, tm, d), tok), pl.BlockSpec((d, d), const), pl.BlockSpec((1, tm, d), tok),
                  _mod_spec(d, per_batch), pl.BlockSpec((1, d), const), pl.BlockSpec((1, d), const)],
        out_specs=pl.BlockSpec((1, tm, d), tok),
        out_shape=jax.ShapeDtypeStruct((bsz, t, d), F32),
        compiler_params=_cparams("parallel", "parallel"),
        name="out_ln",
    )(o, wo, x, mod, ln_g, ln_b)


def _mlp_kernel(x_ref, mod_ref, w1_ref, w2_ref, g_ref, b_ref, y_ref, h_scr, acc_scr, *, d, alpha):
    j = pl.program_id(2)

    @pl.when(j == 0)
    def _():
        h = x_ref[0] * (1.0 + _mod_part(mod_ref, 4, d)) + _mod_part(mod_ref, 3, d)
        h_scr[...] = h.astype(h_scr.dtype)
        acc_scr[...] = jnp.zeros_like(acc_scr)

    a = jnp.square(jnp.maximum(_mm(h_scr[...], w1_ref[...]), 0.0))
    acc_scr[...] += _mm(a, w2_ref[...])

    @pl.when(j == pl.num_programs(2) - 1)
    def _():
        z = alpha * x_ref[0] + _mod_part(mod_ref, 5, d) * acc_scr[...]
        y_ref[0] = _layernorm(z, g_ref[...], b_ref[...])


def _mlp(x, mod, per_batch, w1, w2, ln_g, ln_b, alpha):
    bsz, t, d = x.shape
    ff = w1.shape[1]
    tm = min(1024, t)
    tf = min(2048, ff)
    tok = lambda b, i, j: (b, i, 0)
    const = lambda b, i, j: (0, 0)
    return pl.pallas_call(
        functools.partial(_mlp_kernel, d=d, alpha=alpha),
        grid=(bsz, t // tm, ff // tf),
        in_specs=[pl.BlockSpec((1, tm, d), tok), _mod_spec(d, per_batch),
                  pl.BlockSpec((d, tf), lambda b, i, j: (0, j)), pl.BlockSpec((tf, d), lambda b, i, j: (j, 0)),
                  pl.BlockSpec((1, d), const), pl.BlockSpec((1, d), const)],
        out_specs=pl.BlockSpec((1, tm, d), tok),
        out_shape=jax.ShapeDtypeStruct((bsz, t, d), F32),
        scratch_shapes=[pltpu.VMEM((tm, d), MXU_DTYPE), pltpu.VMEM((tm, d), F32)],
        compiler_params=_cparams("parallel", "parallel", "arbitrary"),
        name="mlp",
    )(x, mod, w1, w2, ln_g, ln_b)


def _gqa_qkv_kernel(x_ref, mod_ref, w_ref, gq_ref, gk_ref, cos_ref, sa_ref, sb_ref, q_ref, k_ref, v_ref,
                    *, d, n_q, n_kv):
    h = x_ref[0] * (1.0 + _mod_part(mod_ref, 1, d)) + _mod_part(mod_ref, 0, d)
    acc = _mm(h, w_ref[...])
    cos, sa, sb = cos_ref[...], sa_ref[...], sb_ref[...]
    hd = ATT_HEAD_DIM

    def norm_rope(u, g):
        u = u * lax.rsqrt(jnp.mean(u * u, -1, keepdims=True) + LN_EPS) * g
        return u * cos + pltpu.roll(u, hd - hd // 4, 1) * sa + pltpu.roll(u, hd // 4, 1) * sb

    for i in range(n_q):
        q_ref[0, :, i * hd:(i + 1) * hd] = norm_rope(acc[:, i * hd:(i + 1) * hd], gq_ref[...]).astype(q_ref.dtype)
    for i in range(n_kv):
        c0 = (n_q + i) * hd
        k_ref[0, :, i * hd:(i + 1) * hd] = norm_rope(acc[:, c0:c0 + hd], gk_ref[...]).astype(k_ref.dtype)
    ones = jnp.ones((acc.shape[0], hd), v_ref.dtype)
    for i in range(n_kv):
        c0 = (n_q + n_kv + i) * hd
        v_ref[0, :, 2 * i * hd:(2 * i + 1) * hd] = acc[:, c0:c0 + hd].astype(v_ref.dtype)
        v_ref[0, :, (2 * i + 1) * hd:(2 * i + 2) * hd] = ones


def _gqa_qkv(x, mod, per_batch, w, gq, gk, cos, sa, sb):
    bsz, t, d = x.shape
    hd = ATT_HEAD_DIM
    n_kv = ATT_KV_HEADS
    n_q = w.shape[1] // hd - 2 * n_kv
    tm = min(512, t)
    tok = lambda b, i: (b, i, 0)
    const = lambda b, i: (0, 0)
    tab = pl.BlockSpec((tm, hd), lambda b, i: (i, 0))
    return pl.pallas_call(
        functools.partial(_gqa_qkv_kernel, d=d, n_q=n_q, n_kv=n_kv),
        grid=(bsz, t // tm),
        in_specs=[pl.BlockSpec((1, tm, d), tok), _mod_spec(d, per_batch), pl.BlockSpec(w.shape, const),
                  pl.BlockSpec((1, hd), const), pl.BlockSpec((1, hd), const), tab, tab, tab],
        out_specs=[pl.BlockSpec((1, tm, n_q * hd), tok), pl.BlockSpec((1, tm, n_kv * hd), tok),
                   pl.BlockSpec((1, tm, 2 * n_kv * hd), tok)],
        out_shape=[jax.ShapeDtypeStruct((bsz, t, n_q * hd), MXU_DTYPE),
                   jax.ShapeDtypeStruct((bsz, t, n_kv * hd), MXU_DTYPE),
                   jax.ShapeDtypeStruct((bsz, t, 2 * n_kv * hd), MXU_DTYPE)],
        compiler_params=_cparams("parallel", "parallel"),
        name="gqa_qkv",
    )(x, mod, w, gq, gk, cos, sa, sb)


def _gqa_att_kernel(*refs, n_sets, groups, tk):
    q_ref, o_ref = refs[0], refs[-1]
    kv = refs[1:-1]
    hd = ATT_HEAD_DIM
    tq = q_ref.shape[1]
    q = jnp.concatenate([q_ref[0, :, g * hd:(g + 1) * hd] for g in range(groups)], axis=0)
    m = acc = None
    for j in range(n_sets):
        k_ref, v_ref = kv[2 * j], kv[2 * j + 1]
        t = k_ref.shape[1]
        for c0 in range(0, t, tk):
            c1 = min(c0 + tk, t)
            s = _mm_nt(q, k_ref[0, c0:c1, :])
            m_c = jnp.max(s, -1, keepdims=True)
            if m is None:
                m = m_c
                acc = _mm(jnp.exp2(s - m), v_ref[0, c0:c1, :])
            else:
                m_new = jnp.maximum(m, m_c)
                acc = acc * jnp.exp2(m - m_new) + _mm(jnp.exp2(s - m_new), v_ref[0, c0:c1, :])
                m = m_new
    o = acc[:, :hd] / acc[:, hd:]
    for g in range(groups):
        o_ref[0, :, g * hd:(g + 1) * hd] = o[g * tq:(g + 1) * tq].astype(o_ref.dtype)


def _gqa_att(q, kv_sets):
    bsz, s, dq = q.shape
    hd = ATT_HEAD_DIM
    n_kv = ATT_KV_HEADS
    groups = dq // hd // n_kv
    tq = min(256, s)
    in_specs = [pl.BlockSpec((1, tq, groups * hd), lambda b, kh, i: (b, i, kh))]
    args = [q]
    for k, v in kv_sets:
        t = k.shape[1]
        in_specs += [pl.BlockSpec((1, t, hd), lambda b, kh, i: (b, 0, kh)),
                     pl.BlockSpec((1, t, 2 * hd), lambda b, kh, i: (b, 0, kh))]
        args += [k, v]
    return pl.pallas_call(
        functools.partial(_gqa_att_kernel, n_sets=len(kv_sets), groups=groups, tk=GQA_KEY_CHUNK),
        grid=(bsz, n_kv, s // tq),
        in_specs=in_specs,
        out_specs=pl.BlockSpec((1, tq, groups * hd), lambda b, kh, i: (b, i, kh)),
        out_shape=jax.ShapeDtypeStruct((bsz, s, dq), MXU_DTYPE),
        compiler_params=_cparams("parallel", "parallel", "parallel"),
        name="gqa_att",
    )(*args)


def _rope_tables(n_tokens, head_dim):
    t = np.arange(n_tokens)
    row = (t // GRID_W).astype(np.float32)
    col = (t % GRID_W).astype(np.float32)
    half = head_dim // 2
    freqs = jnp.asarray(ROPE_BASE, F32) ** (-jnp.arange(0, half, 2, dtype=F32) / half)
    ang_r = jnp.asarray(row)[:, None] * freqs[None, :]
    ang_c = jnp.asarray(col)[:, None] * freqs[None, :]
    ang = jnp.concatenate([ang_r, ang_r, ang_c, ang_c], axis=-1)
    cos, sin = jnp.cos(ang), jnp.sin(ang)
    first = (np.arange(head_dim) % half) < half // 2
    return cos, jnp.where(first, -sin, 0.0), jnp.where(first, 0.0, sin)


def _gqa_layer(x, xc, mod, mod_c, wqkv, wo, q_norm, k_norm, want_ctx):
    s, tc = x.shape[1], xc.shape[1]
    hd = ATT_HEAD_DIM
    gq = (q_norm * (hd ** -0.5 * LOG2E)).reshape(1, hd)
    gk = k_norm.reshape(1, hd)
    cos, sa, sb = _rope_tables(s, hd)
    one, zero = jnp.ones((tc, hd), F32), jnp.zeros((tc, hd), F32)
    q, k, v = _gqa_qkv(x, mod, True, wqkv, gq, gk, cos, sa, sb)
    qc, kc, vc = _gqa_qkv(xc, mod_c, False, wqkv, gq, gk, one, zero, zero)
    o = _gqa_att(q, [(k, v), (kc, vc)])
    oc = _gqa_att(qc, [(kc, vc)]) if want_ctx else None
    return o, oc, wo


def _na_qkv_kernel(x_ref, mod_ref, w_ref, q_ref, k_ref, v_ref, *, d, scale):
    h = x_ref[0] * (1.0 + _mod_part(mod_ref, 1, d)) + _mod_part(mod_ref, 0, d)
    acc = _mm(h, w_ref[...])
    q_ref[0] = (acc[:, :d] * scale).astype(q_ref.dtype)
    k_ref[0] = acc[:, d:2 * d].astype(k_ref.dtype)
    ones = jnp.ones((acc.shape[0], LANES), v_ref.dtype)
    for p in range(d // LANES):
        c0 = 2 * d + p * LANES
        v_ref[0, :, 2 * p * LANES:(2 * p + 1) * LANES] = acc[:, c0:c0 + LANES].astype(v_ref.dtype)
        v_ref[0, :, (2 * p + 1) * LANES:(2 * p + 2) * LANES] = ones


def _na_qkv(x, mod, per_batch, w):
    bsz, t, d = x.shape
    tm = min(512, t)
    tok = lambda b, i: (b, i, 0)
    out = jax.ShapeDtypeStruct((bsz, t, d), MXU_DTYPE)
    return pl.pallas_call(
        functools.partial(_na_qkv_kernel, d=d, scale=NA_HEAD_DIM ** -0.5 * LOG2E),
        grid=(bsz, t // tm),
        in_specs=[pl.BlockSpec((1, tm, d), tok), _mod_spec(d, per_batch), pl.BlockSpec(w.shape, lambda b, i: (0, 0))],
        out_specs=[pl.BlockSpec((1, tm, d), tok)] * 2 + [pl.BlockSpec((1, tm, 2 * d), tok)],
        out_shape=[out, out, jax.ShapeDtypeStruct((bsz, t, 2 * d), MXU_DTYPE)],
        compiler_params=_cparams("parallel", "parallel"),
        name="na_qkv",
    )(x, mod, w)


def _pair_softmax_att(q, ks, vs, biases):
    lane = lax.broadcasted_iota(jnp.int32, q.shape, 1)
    first = lane < NA_HEAD_DIM
    zero = jnp.zeros_like(q)
    qm = [jnp.where(first, q, zero), jnp.where(first, zero, q)]
    m, acc = [None, None], [None, None]
    for j, (k, v) in enumerate(zip(ks, vs)):
        for h in range(2):
            s = _mm_nt(qm[h], k)
            if biases[h][j] is not None:
                s = s + biases[h][j]
            m_c = jnp.max(s, -1, keepdims=True)
            if m[h] is None:
                m[h] = m_c
                acc[h] = _mm(jnp.exp2(s - m_c), v)
            else:
                m_new = jnp.maximum(m[h], m_c)
                acc[h] = acc[h] * jnp.exp2(m[h] - m_new) + _mm(jnp.exp2(s - m_new), v)
                m[h] = m_new
    outs = [a[:, :LANES] / a[:, LANES:] for a in acc]
    return jnp.where(first, outs[0], outs[1])


def _na_att_kernel(q_ref, k0, k1, k2, k3, v0, v1, v2, v3, kc_ref, vc_ref, bias_ref, o_ref):
    ks = [kc_ref[0], k0[0], k1[0], k2[0], k3[0]]
    vs = [vc_ref[0], v0[0], v1[0], v2[0], v3[0]]
    kb = k0.shape[1]
    biases = [[None] + [bias_ref[0, h, :, j * kb:(j + 1) * kb] for j in range(4)] for h in range(2)]
    o_ref[0] = _pair_softmax_att(q_ref[0], ks, vs, biases).astype(o_ref.dtype)


def _na_bias_tables(rpb, rows, scale):
    n_blk = rows // NA_Q_ROWS
    heads = rpb.shape[0]
    half_h, half_w = NA_WIN_H // 2, NA_WIN_W // 2
    pad = GRID_W
    rpb_p = jnp.pad(rpb, ((0, 0), (0, 0), (pad, pad)))
    cmat = jnp.stack([rpb_p[:, :, pad + NA_WIN_W - 1 - cq: pad + NA_WIN_W - 1 - cq + GRID_W]
                      for cq in range(GRID_W)], axis=2)
    cq, ck = np.arange(GRID_W)[:, None], np.arange(GRID_W)[None, :]
    c0 = np.clip(cq - half_w, 0, GRID_W - NA_WIN_W)
    vcol = (ck >= c0) & (ck < c0 + NA_WIN_W)
    cmat = jnp.where(vcol[None, None], cmat * scale, NEG_BIG)
    n_d = 2 * NA_WIN_H - 1
    cmat = jnp.concatenate([jnp.full((heads, NA_K_ROWS, GRID_W, GRID_W), NEG_BIG, F32), cmat,
                            jnp.full((heads, NA_K_ROWS, GRID_W, GRID_W), NEG_BIG, F32)], axis=1)
    types = []
    for i in (0, min(1, n_blk - 1), n_blk - 1):
        start = int(np.clip(NA_Q_ROWS * i - half_h, 0, rows - NA_K_ROWS))
        row_k = start + np.arange(NA_K_ROWS)
        per_row = []
        for rq in range(NA_Q_ROWS):
            row_q = NA_Q_ROWS * i + rq
            r0 = int(np.clip(row_q - half_h, 0, rows - NA_WIN_H))
            vrow = (row_k >= r0) & (row_k < r0 + NA_WIN_H)
            d0 = start - row_q + NA_WIN_H - 1 + NA_K_ROWS
            assert 0 <= d0 and d0 + NA_K_ROWS <= n_d + 2 * NA_K_ROWS
            blk = jnp.where(vrow[None, :, None, None], cmat[:, d0:d0 + NA_K_ROWS], NEG_BIG)
            per_row.append(jnp.transpose(blk, (0, 2, 1, 3)))
        types.append(jnp.stack(per_row, axis=1).reshape(heads, NA_Q_ROWS * GRID_W, NA_K_ROWS * GRID_W))
    return jnp.stack(types)


def _na_att(q, k, v, kc, vc, bias):
    bsz, s, d = q.shape
    tc = kc.shape[1]
    rows = s // GRID_W
    n_blk = rows // NA_Q_ROWS
    tq = NA_Q_ROWS * GRID_W
    kb = NA_K_ROWS * GRID_W // 4
    n_kb = s // kb
    pairs = d // LANES

    def kv_spec(j, width):
        return pl.BlockSpec((1, kb, width),
                            lambda hp, i, b: (b, jnp.clip(2 * i - 1, 0, n_kb - 4) + j, hp))

    ctx_spec = lambda width: pl.BlockSpec((1, tc, width), lambda hp, i, b: (b, 0, hp))
    bias_spec = pl.BlockSpec(
        (1, 2, tq, 4 * kb),
        lambda hp, i, b: ((i > 0).astype(jnp.int32) + (i == n_blk - 1).astype(jnp.int32), hp, 0, 0))
    q_spec = pl.BlockSpec((1, tq, LANES), lambda hp, i, b: (b, i, hp))
    return pl.pallas_call(
        _na_att_kernel,
        grid=(pairs, n_blk, bsz),
        in_specs=[q_spec] + [kv_spec(j, LANES) for j in range(4)] + [kv_spec(j, 2 * LANES) for j in range(4)]
                 + [ctx_spec(LANES), ctx_spec(2 * LANES), bias_spec],
        out_specs=q_spec,
        out_shape=jax.ShapeDtypeStruct((bsz, s, d), MXU_DTYPE),
        compiler_params=_cparams("parallel", "parallel", "parallel"),
        name="na_att",
    )(q, k, k, k, k, v, v, v, v, kc, vc, bias)


def _pair_att_kernel(q_ref, k_ref, v_ref, o_ref):
    o_ref[0] = _pair_softmax_att(q_ref[0], [k_ref[0]], [v_ref[0]], [[None], [None]]).astype(o_ref.dtype)


def _pair_att(q, k, v):
    bsz, t, d = q.shape
    spec = pl.BlockSpec((1, t, LANES), lambda b, hp: (b, 0, hp))
    return pl.pallas_call(
        _pair_att_kernel,
        grid=(bsz, d // LANES),
        in_specs=[spec, spec, pl.BlockSpec((1, t, 2 * LANES), lambda b, hp: (b, 0, hp))],
        out_specs=spec,
        out_shape=jax.ShapeDtypeStruct((bsz, t, d), MXU_DTYPE),
        compiler_params=_cparams("parallel", "parallel"),
        name="na_ctx_att",
    )(q, k, v)


def _na_layer(x, xc, mod, mod_c, wqkv, wo, rpb, want_ctx):
    rows = x.shape[1] // GRID_W
    assert rows % NA_Q_ROWS == 0 and rows >= NA_K_ROWS
    q, k, v = _na_qkv(x, mod, True, wqkv)
    qc, kc, vc = _na_qkv(xc, mod_c, False, wqkv)
    o = _na_att(q, k, v, kc, vc, _na_bias_tables(rpb, rows, LOG2E))
    oc = _pair_att(qc, kc, vc) if want_ctx else None
    return o, oc, wo


def _rw_proj_kernel(x_ref, xp_ref, xn_ref, mod_ref, mu_ref, wr_ref, wk_ref, wv_ref, w1_ref, w2_ref, a1_ref, a2_ref,
                    g1_ref, g2_ref, w0_ref, a0_ref, r_ref, k_ref, v_ref, g_ref, lw_ref, as_ref, *, d):
    t = pl.program_id(1)
    nt = pl.num_programs(1)
    scale = 1.0 + _mod_part(mod_ref, 1, d)
    shift = _mod_part(mod_ref, 0, d)
    h = x_ref[0] * scale + shift
    tm = h.shape[0]
    h_prev = (xp_ref[0, 7:8, :] * scale + shift) * (t > 0).astype(F32)
    h_next = (xn_ref[0, 0:1, :] * scale + shift) * (t < nt - 1).astype(F32)
    row = lax.broadcasted_iota(jnp.int32, h.shape, 0)
    prev = jnp.where(row == 0, h_prev, pltpu.roll(h, 1, 0))
    nxt = jnp.where(row == tm - 1, h_next, pltpu.roll(h, tm - 1, 0))
    xx = 0.5 * (prev + nxt) - h
    mix = lambda j: h + xx * mu_ref[j:j + 1, :]

    r_ref[0] = _mm(mix(0), wr_ref[...]).astype(r_ref.dtype)
    k_ref[0] = _mm(mix(2), wk_ref[...]).astype(k_ref.dtype)
    v_ref[0] = _mm(mix(3), wv_ref[...]).astype(v_ref.dtype)
    g_ref[0] = _mm(_sigmoid(_mm(mix(5), g1_ref[...])), g2_ref[...]).astype(g_ref.dtype)

    tw = jnp.tanh(_mm(mix(1), w1_ref[...]))
    al = _mm(mix(4), a1_ref[...])
    first = lax.broadcasted_iota(jnp.int32, tw.shape, 1) < tw.shape[1] // 2
    zero = jnp.zeros_like(tw)
    for n in range(2):
        pick = lambda u: jnp.where(first, u, zero) if n == 0 else jnp.where(first, zero, u)
        z = -(w0_ref[n:n + 1, :] + _mm(pick(tw), w2_ref[...]))
        softplus = jnp.maximum(z, 0.0) + jnp.log(1.0 + jnp.exp(-jnp.abs(z)))
        lw_ref[n, 0] = -jnp.exp(-softplus - 0.5)
        as_ref[n, 0] = _sigmoid(a0_ref[n:n + 1, :] + _mm(pick(al), a2_ref[...])).astype(as_ref.dtype)


def _rw_proj(x, mod, per_batch, p):
    bsz, t, d = x.shape
    tm = min(256, t)
    n8 = t // 8
    tok = lambda b, i: (b, i, 0)
    dtok = lambda b, i: (0, b, i, 0)
    full = lambda a: pl.BlockSpec(a.shape, lambda b, i: (0,) * a.ndim)
    halo_prev = pl.BlockSpec((1, 8, d), lambda b, i: (b, jnp.maximum(i * (tm // 8) - 1, 0), 0))
    halo_next = pl.BlockSpec((1, 8, d), lambda b, i: (b, jnp.minimum((i + 1) * (tm // 8), n8 - 1), 0))
    weights = [p['mu'], p['wr'], p['wk'], p['wv'], p['w1'], p['w2'], p['a1'], p['a2'], p['g1'], p['g2'],
               p['w0'], p['a0']]
    one = jax.ShapeDtypeStruct((bsz, t, d), MXU_DTYPE)
    two = lambda dtype: jax.ShapeDtypeStruct((2, bsz, t, d), dtype)
    return pl.pallas_call(
        functools.partial(_rw_proj_kernel, d=d),
        grid=(bsz, t // tm),
        in_specs=[pl.BlockSpec((1, tm, d), tok), halo_prev, halo_next, _mod_spec(d, per_batch)]
                 + [full(a) for a in weights],
        out_specs=[pl.BlockSpec((1, tm, d), tok)] * 4 + [pl.BlockSpec((2, 1, tm, d), dtok)] * 2,
        out_shape=[one, one, one, one, two(F32), two(MXU_DTYPE)],
        compiler_params=_cparams("parallel", "parallel"),
        name="rw_proj",
    )(x, x, x, mod, *weights)


def _rw_scan_kernel(r_ref, k_ref, v_ref, lw_ref, as_ref, kk_ref, ka_ref, z0_ref, y_ref, zf_ref, z_scr,
                    *, pairs, sub, reverse):
    c = pl.program_id(1)
    L = RW_CHUNK
    H = RW_HEAD
    assert 2 * L == LANES and 2 * H == LANES

    @pl.when(c == 0)
    def _():
        z_scr[...] = z0_ref[0, 0]

    sign = -1 if reverse else 1
    t_i = lax.broadcasted_iota(jnp.int32, (L, 2 * L), 0)
    s_i = lax.broadcasted_iota(jnp.int32, (L, 2 * L), 1) & (L - 1)
    dt = sign * (s_i - t_i)
    strict_c = dt < 0
    incl_c = dt <= 0
    tri2 = incl_c.astype(F32)
    eye_c = (dt == 0).astype(F32)
    r2 = lax.broadcasted_iota(jnp.int32, (LANES, LANES), 0)
    c2 = lax.broadcasted_iota(jnp.int32, (LANES, LANES), 1)
    eye2 = (r2 == c2).astype(F32)
    same_head = _head_ones()
    first = lax.broadcasted_iota(jnp.int32, (L, LANES), 1) < H
    zero = jnp.zeros((L, LANES), F32)

    head0 = lambda u: jnp.where(first, u, zero)
    head1 = lambda u: jnp.where(first, zero, u)
    bd = lambda u: jnp.concatenate([head0(u), head1(u)], axis=0)

    each = lambda fn, *lists: [fn(*args) for args in zip(*lists)]
    probs = [(u, p) for u in range(sub) for p in range(pairs)]
    rows = [slice(u * L, (u + 1) * L) for u, _ in probs]
    sls = [slice(p * LANES, (p + 1) * LANES) for _, p in probs]
    cat0 = lambda *u: jnp.concatenate(u, axis=0)
    cat1 = lambda *u: jnp.concatenate(u, axis=1)

    lw = [lw_ref[0, 0, rw, sl] for rw, sl in zip(rows, sls)]
    r = [r_ref[0, rw, sl].astype(F32) for rw, sl in zip(rows, sls)]
    k = [k_ref[0, rw, sl].astype(F32) for rw, sl in zip(rows, sls)]
    v = [v_ref[0, rw, sl].astype(F32) for rw, sl in zip(rows, sls)]
    a_s = [as_ref[0, 0, rw, sl].astype(F32) for rw, sl in zip(rows, sls)]

    c_in = each(lambda u: _mm(tri2, cat0(*_split(u, 2))), lw)
    c_all = each(lambda u: jnp.sum(u, axis=0, keepdims=True), lw)
    kk = each(lambda u, sl: u * kk_ref[:, sl], k, sls)
    ss = each(lambda u: _mm(u * u, same_head), kk)
    kk = each(lambda u, s: u * lax.rsqrt(jnp.maximum(s, 1e-12)), kk, ss)
    b_v = each(lambda u, a: u * a, kk, a_s)
    k_d = each(lambda u, a, sl: u * (1.0 + (a - 1.0) * ka_ref[:, sl]), k, a_s, sls)
    e_neg = each(lambda ci: jnp.exp(-ci), c_in)
    e_rem = each(lambda ci, ca: jnp.exp(ca - ci), c_in, c_all)
    a_t = each(lambda u, ci, l: -u * jnp.exp(ci - l), kk, c_in, lw)
    r_t = each(lambda u, ci: u * jnp.exp(ci), r, c_in)
    b_t = each(lambda u, e: u * e, b_v, e_neg)
    k_t = each(lambda u, e: u * e, k_d, e_neg)
    b_h = each(lambda u, e: u * e, b_v, e_rem)
    k_h = each(lambda u, e: u * e, k_d, e_rem)

    x = each(lambda a, rr, b, kt: _mm_nt(cat0(a, rr), cat0(bd(b), bd(kt))), a_t, r_t, b_t, k_t)
    m_ab = each(lambda u: jnp.where(strict_c, u[:L, :2 * L], 0.0), x)
    m_ak = each(lambda u: jnp.where(strict_c, u[:L, 2 * L:], 0.0), x)
    n_rb = each(lambda u: jnp.where(incl_c, u[L:, :2 * L], 0.0), x)
    n_rk = each(lambda u: jnp.where(incl_c, u[L:, 2 * L:], 0.0), x)

    inv = each(lambda m: eye_c + m, m_ab)
    pw = each(lambda m: _mm(m, bd(m)), m_ab)
    for _ in range(int(np.log2(L)) - 2):
        st = each(lambda a, b: _mm(cat0(a, b), bd(a)), pw, inv)
        pw = each(lambda s: s[:L], st)
        inv = each(lambda b, s: b + s[L:], inv, st)
    inv = each(lambda b, a: b + _mm(b, bd(a)), inv, pw)

    mv = each(lambda a, b, u: _mm(cat0(a, b), bd(u)), m_ak, n_rk, v)
    tw = each(lambda i, a, m: _mm(i, cat1(bd(a), bd(m[:L]))), inv, a_t, mv)
    p1 = each(lambda u: u[:, :LANES], tw)
    p2 = each(lambda u: u[:, LANES:], tw)
    nw = each(lambda m, a, b: _mm(m, cat1(bd(a), bd(b))), n_rb, p1, p2)
    q1 = each(lambda rr, u: rr + u[:, :LANES], r_t, nw)
    q2 = each(lambda u, m: u[:, LANES:] + m[L:], nw, mv)
    gh = each(lambda b, kh, a, c_, u: _mm_tn(cat0(b, kh), cat0(cat1(a, c_), cat1(zero, u))), b_h, k_h, p1, p2, v)
    g_t = each(lambda ca, u: eye2 * jnp.exp(ca) + same_head * u[:, :LANES], c_all, gh)
    h_t = each(lambda u: same_head * u[:, LANES:], gh)

    order = range(sub - 1, -1, -1) if reverse else range(sub)
    for p in range(pairs):
        z = z_scr[p]
        for u in order:
            i = u * pairs + p
            yz = _mm(cat0(q1[i], g_t[i]), z)
            y_ref[0, rows[i], sls[i]] = yz[:L] + q2[i]
            z = yz[L:] + h_t[i]
        z_scr[p] = z

    @pl.when(c == pl.num_programs(1) - 1)
    def _():
        zf_ref[0] = z_scr[...]


def _rw_scan(r, k, v, lw, a_s, kk_w, ka_w, z0, reverse):
    bsz, t, d = r.shape
    sub = 2
    rows = sub * RW_CHUNK
    n_c = t // rows
    n = int(reverse)
    pairs = d // LANES
    cidx = (lambda c: n_c - 1 - c) if reverse else (lambda c: c)
    tok = pl.BlockSpec((1, rows, d), lambda b, c: (b, cidx(c), 0))
    dtok = pl.BlockSpec((1, 1, rows, d), lambda b, c: (n, b, cidx(c), 0))
    vec = pl.BlockSpec((1, d), lambda b, c: (0, 0))
    return pl.pallas_call(
        functools.partial(_rw_scan_kernel, pairs=pairs, sub=sub, reverse=reverse),
        grid=(bsz, n_c),
        in_specs=[tok, tok, tok, dtok, dtok, vec, vec,
                  pl.BlockSpec((1, 1, pairs, LANES, LANES), lambda b, c: (n, b, 0, 0, 0))],
        out_specs=[tok, pl.BlockSpec((1, pairs, LANES, LANES), lambda b, c: (b, 0, 0, 0))],
        out_shape=[jax.ShapeDtypeStruct((bsz, t, d), F32),
                   jax.ShapeDtypeStruct((bsz, pairs, LANES, LANES), F32)],
        scratch_shapes=[pltpu.VMEM((pairs, LANES, LANES), F32)],
        compiler_params=_cparams("parallel", "arbitrary"),
        name="rw_scan",
    )(r, k, v, lw, a_s, kk_w, ka_w, z0)


def _rw_out_kernel(yf_ref, yb_ref, r_ref, k_ref, v_ref, g_ref, as_ref, ka_ref, rk_ref, lg_ref, lb_ref, wo_ref, x_ref,
                   mod_ref, g_ln_ref, b_ln_ref, o_ref, u_scr, *, d, alpha):
    same_head = _head_ones()
    inv_n = 1.0 / RW_HEAD
    for p in range(d // LANES):
        sl = slice(p * LANES, (p + 1) * LANES)
        y = yf_ref[0, :, sl] + yb_ref[0, :, sl]
        mu = _mm_rhs_exact(y, same_head) * inv_n
        dy = y - mu
        var = _mm(dy * dy, same_head) * inv_n
        yn = dy * lax.rsqrt(var + RW_GN_EPS) * lg_ref[:, sl] + lb_ref[:, sl]
        k = k_ref[0, :, sl].astype(F32)
        ka = ka_ref[:, sl]
        k_sum = k * ((1.0 + (as_ref[0, 0, :, sl].astype(F32) - 1.0) * ka)
                     + (1.0 + (as_ref[1, 0, :, sl].astype(F32) - 1.0) * ka))
        bonus = _mm(r_ref[0, :, sl].astype(F32) * k_sum * rk_ref[:, sl], same_head)
        u = (yn + bonus * v_ref[0, :, sl].astype(F32)) * g_ref[0, :, sl].astype(F32)
        u_scr[:, sl] = u.astype(u_scr.dtype)
    z = alpha * x_ref[0] + _mod_part(mod_ref, 2, d) * _mm(u_scr[...], wo_ref[...])
    o_ref[0] = _layernorm(z, g_ln_ref[...], b_ln_ref[...])


def _rw_out(y_f, y_b, r, k, v, g, a_s, p, x, mod, per_batch, ln_g, ln_b, alpha):
    bsz, t, d = x.shape
    tm = min(256, t)
    tok = pl.BlockSpec((1, tm, d), lambda b, i: (b, i, 0))
    dtok = pl.BlockSpec((2, 1, tm, d), lambda b, i: (0, b, i, 0))
    vec = pl.BlockSpec((1, d), lambda b, i: (0, 0))
    return pl.pallas_call(
        functools.partial(_rw_out_kernel, d=d, alpha=alpha),
        grid=(bsz, t // tm),
        in_specs=[tok, tok, tok, tok, tok, tok, dtok, vec, vec, vec, vec, pl.BlockSpec((d, d), lambda b, i: (0, 0)),
                  tok, _mod_spec(d, per_batch), vec, vec],
        out_specs=tok,
        out_shape=jax.ShapeDtypeStruct((bsz, t, d), F32),
        scratch_shapes=[pltpu.VMEM((tm, d), MXU_DTYPE)],
        compiler_params=_cparams("parallel", "parallel"),
        name="rw_out",
    )(y_f, y_b, r, k, v, g, a_s, p['k_a'], p['r_k'], p['lnx_g'], p['lnx_b'], p['wo'], x, mod, ln_g, ln_b)


def _rw_layer(x, xc, mod, mod_c, p, want_ctx, ln_g, ln_b, alpha):
    bsz, _, d = x.shape
    rc, kc, vc, gc, lwc, asc = _rw_proj(xc, mod_c, False, p)
    r, k, v, g, lw, a_s = _rw_proj(x, mod, True, p)
    z0 = jnp.zeros((2, bsz, d // LANES, LANES, LANES), F32)
    scan = lambda args, z, rev: _rw_scan(*args, p['k_k'], p['k_a'], z, rev)
    yc_f, zc_f = scan((rc, kc, vc, lwc, asc), z0, False)
    yc_b, zc_b = scan((rc, kc, vc, lwc, asc), z0, True)
    zc = jnp.stack([zc_f, zc_b])
    y_f, _ = scan((r, k, v, lw, a_s), zc, False)
    y_b, _ = scan((r, k, v, lw, a_s), zc, True)
    x_new = _rw_out(y_f, y_b, r, k, v, g, a_s, p, x, mod, True, ln_g, ln_b, alpha)
    xc_new = (_rw_out(yc_f, yc_b, rc, kc, vc, gc, asc, p, xc, mod_c, False, ln_g, ln_b, alpha)
              if want_ctx else None)
    return x_new, xc_new


def kernel(x, c, ctx, c_ctx, mod_w, mod_b, post_ln_g, post_ln_b, mlp_w1, mlp_w2, att_wqkv, att_wo, att_q_norm, att_k_norm, na_wqkv, na_wo, na_rpb, rw_mu, rw_wr, rw_wk, rw_wv, rw_wo, rw_w0, rw_w1, rw_w2, rw_a0, rw_a1, rw_a2, rw_g1, rw_g2, rw_k_k, rw_k_a, rw_r_k, rw_lnx_g, rw_lnx_b):
    depth = mod_w.shape[0]
    bsz, _, d = x.shape
    alpha = (2.0 * depth) ** 0.25
    cast = lambda a: a.astype(MXU_DTYPE)

    cond_rows = -(-(bsz + 1) // 8) * 8
    cond = jnp.zeros((cond_rows, d), F32).at[:bsz].set(c).at[bsz].set(c_ctx)
    mods = _modulation(cond, mod_w, mod_b)

    xc = ctx
    for i in range(depth):
        kind, slot = i % N_MIXERS, i // N_MIXERS
        want_ctx = i < depth - 1
        mod = mods[i, :bsz].reshape(bsz, 1, N_MOD * d)
        mod_c = mods[i, bsz:bsz + 1].reshape(1, 1, N_MOD * d)
        g1, b1 = post_ln_g[i, 0:1], post_ln_b[i, 0:1]
        g2, b2 = post_ln_g[i, 1:2], post_ln_b[i, 1:2]
        if kind == 2:
            cat = lambda a: jnp.concatenate([a[0], a[1]], axis=-1)
            p = {'mu': rw_mu[slot], 'wr': cast(rw_wr[slot]), 'wk': cast(rw_wk[slot]), 'wv': cast(rw_wv[slot]),
                 'wo': cast(rw_wo[slot]), 'w0': rw_w0[slot], 'a0': rw_a0[slot],
                 'w1': cast(cat(rw_w1[slot])), 'a1': cast(cat(rw_a1[slot])),
                 'w2': cast(rw_w2[slot].reshape(-1, d)), 'a2': cast(rw_a2[slot].reshape(-1, d)),
                 'g1': cast(rw_g1[slot]), 'g2': cast(rw_g2[slot]),
                 'k_k': rw_k_k[slot].reshape(1, d), 'k_a': rw_k_a[slot].reshape(1, d),
                 'r_k': rw_r_k[slot].reshape(1, d), 'lnx_g': rw_lnx_g[slot].reshape(1, d),
                 'lnx_b': rw_lnx_b[slot].reshape(1, d)}
            x, xc_new = _rw_layer(x, xc, mod, mod_c, p, want_ctx, g1, b1, alpha)
        else:
            if kind == 0:
                o, oc, wo = _gqa_layer(x, xc, mod, mod_c, cast(att_wqkv[slot]), cast(att_wo[slot]),
                                       att_q_norm[slot], att_k_norm[slot], want_ctx)
            else:
                o, oc, wo = _na_layer(x, xc, mod, mod_c, cast(na_wqkv[slot]), cast(na_wo[slot]), na_rpb[slot],
                                      want_ctx)
            x = _out_ln(o, wo, x, mod, True, g1, b1, alpha)
            xc_new = _out_ln(oc, wo, xc, mod_c, False, g1, b1, alpha) if want_ctx else None
        w1, w2 = cast(mlp_w1[i]), cast(mlp_w2[i])
        x = _mlp(x, mod, True, w1, w2, g2, b2, alpha)
        if want_ctx:
            xc = _mlp(xc_new, mod_c, False, w1, w2, g2, b2, alpha)
    return x
```

```python
import functools

import numpy as np
import jax
import jax.numpy as jnp
from jax import lax
from jax.experimental import pallas as pl
from jax.experimental.pallas import tpu as pltpu

F32 = jnp.float32
MXU_DTYPE = jnp.bfloat16

GRID_W = 64
N_MOD = 6
N_MIXERS = 3
LN_EPS = 1e-6
ATT_HEAD_DIM = 128
ATT_KV_HEADS = 2
ROPE_BASE = 10000.0
GQA_KEY_CHUNK = 512
NA_HEAD_DIM = 64
NA_WIN_H = 8
NA_WIN_W = 16
NA_Q_ROWS = 8
NA_K_ROWS = 16
RW_HEAD = 64
RW_GN_EPS = 64e-5
RW_CHUNK = 64
RW_STAGE_LAG = 0
LANES = 128
NEG_BIG = -1e30
LOG2E = 1.4426950408889634

VMEM_LIMIT = 56 * 1024 * 1024


def _cparams(*sem):
    return pltpu.CompilerParams(dimension_semantics=sem, vmem_limit_bytes=VMEM_LIMIT)


def _mm(a, b):
    return jnp.dot(a.astype(MXU_DTYPE), b.astype(MXU_DTYPE), preferred_element_type=F32)


def _mm_nt(a, b):
    return lax.dot_general(a.astype(MXU_DTYPE), b.astype(MXU_DTYPE), (((1,), (1,)), ((), ())),
                           preferred_element_type=F32)


def _mm_tn(a, b):
    return lax.dot_general(a.astype(MXU_DTYPE), b.astype(MXU_DTYPE), (((0,), (0,)), ((), ())),
                           preferred_element_type=F32)


def _split(a, n):
    parts = []
    for _ in range(n - 1):
        p = a.astype(MXU_DTYPE)
        parts.append(p)
        a = a - p.astype(F32)
    parts.append(a.astype(MXU_DTYPE))
    return parts


def _mm_lhs_exact(m01, a, n=3):
    m01 = m01.astype(MXU_DTYPE)
    return sum(jnp.dot(m01, p, preferred_element_type=F32) for p in _split(a, n))


def _mm_rhs_exact(a, m01, n=2):
    m01 = m01.astype(MXU_DTYPE)
    return sum(jnp.dot(p, m01, preferred_element_type=F32) for p in _split(a, n))


def _mm_hi(a, b):
    a1, a2 = _split(a, 2)
    b1, b2 = _split(b, 2)
    d = functools.partial(jnp.dot, preferred_element_type=F32)
    return d(a1, b1) + d(a1, b2) + d(a2, b1)


def _sigmoid(x):
    return 1.0 / (1.0 + jnp.exp(-x))


def _layernorm(z, g, b):
    mu = jnp.mean(z, -1, keepdims=True)
    dz = z - mu
    var = jnp.mean(dz * dz, -1, keepdims=True)
    return dz * lax.rsqrt(var + LN_EPS) * g + b


def _mod_part(mod_ref, j, d):
    return mod_ref[0, :, j * d:(j + 1) * d]


def _head_ones(n=LANES):
    r = lax.broadcasted_iota(jnp.int32, (n, n), 0)
    c = lax.broadcasted_iota(jnp.int32, (n, n), 1)
    return ((r >= RW_HEAD) == (c >= RW_HEAD)).astype(F32)


def _modulation_kernel(c_ref, w_ref, b_ref, o_ref):
    c = c_ref[...]
    o_ref[0] = _mm_hi(c * _sigmoid(c), w_ref[0]) + b_ref[0]


def _modulation(cond, mod_w, mod_b):
    depth, d, n = mod_w.shape
    rows = cond.shape[0]
    tn = n // 4
    return pl.pallas_call(
        _modulation_kernel,
        grid=(depth, n // tn),
        in_specs=[pl.BlockSpec((rows, d), lambda i, j: (0, 0)),
                  pl.BlockSpec((1, d, tn), lambda i, j: (i, 0, j)),
                  pl.BlockSpec((1, 1, tn), lambda i, j: (i, 0, j))],
        out_specs=pl.BlockSpec((1, rows, tn), lambda i, j: (i, 0, j)),
        out_shape=jax.ShapeDtypeStruct((depth, rows, n), F32),
        compiler_params=_cparams("parallel", "parallel"),
        name="modulation",
    )(cond, mod_w, mod_b.reshape(depth, 1, n))


def _out_ln_kernel(o_ref, w_ref, x_ref, mod_ref, g_ref, b_ref, y_ref, *, d, alpha):
    y = _mm(o_ref[0], w_ref[...])
    z = alpha * x_ref[0] + _mod_part(mod_ref, 2, d) * y
    y_ref[0] = _layernorm(z, g_ref[...], b_ref[...])


def _mod_spec(d, per_batch):
    if per_batch:
        return pl.BlockSpec((1, 1, N_MOD * d), lambda b, *_: (b, 0, 0))
    return pl.BlockSpec((1, 1, N_MOD * d), lambda b, *_: (0, 0, 0))


def _out_ln(o, wo, x, mod, per_batch, ln_g, ln_b, alpha):
    bsz, t, d = x.shape
    tm = min(512, t)
    tok = lambda b, i: (b, i, 0)
    const = lambda b, i: (0, 0)
    return pl.pallas_call(
        functools.partial(_out_ln_kernel, d=d, alpha=alpha),
        grid=(bsz, t // tm),
        in_specs=[pl.BlockSpec((1, tm, d), tok), pl.BlockSpec((d, d), const), pl.BlockSpec((1, tm, d), tok),
                  _mod_spec(d, per_batch), pl.BlockSpec((1, d), const), pl.BlockSpec((1, d), const)],
        out_specs=pl.BlockSpec((1, tm, d), tok),
        out_shape=jax.ShapeDtypeStruct((bsz, t, d), F32),
        compiler_params=_cparams("parallel", "parallel"),
        name="out_ln",
    )(o, wo, x, mod, ln_g, ln_b)


def _mlp_kernel(x_ref, mod_ref, w1_ref, w2_ref, g_ref, b_ref, y_ref, h_scr, acc_scr, *, d, alpha):
    j = pl.program_id(2)

    @pl.when(j == 0)
    def _():
        h = x_ref[0] * (1.0 + _mod_part(mod_ref, 4, d)) + _mod_part(mod_ref, 3, d)
        h_scr[...] = h.astype(h_scr.dtype)
        acc_scr[...] = jnp.zeros_like(acc_scr)

    a = jnp.square(jnp.maximum(_mm(h_scr[...], w1_ref[...]), 0.0))
    acc_scr[...] += _mm(a, w2_ref[...])

    @pl.when(j == pl.num_programs(2) - 1)
    def _():
        z = alpha * x_ref[0] + _mod_part(mod_ref, 5, d) * acc_scr[...]
        y_ref[0] = _layernorm(z, g_ref[...], b_ref[...])


def _mlp(x, mod, per_batch, w1, w2, ln_g, ln_b, alpha):
    bsz, t, d = x.shape
    ff = w1.shape[1]
    tm = min(1024, t)
    tf = min(2048, ff)
    tok = lambda b, i, j: (b, i, 0)
    const = lambda b, i, j: (0, 0)
    return pl.pallas_call(
        functools.partial(_mlp_kernel, d=d, alpha=alpha),
        grid=(bsz, t // tm, ff // tf),
        in_specs=[pl.BlockSpec((1, tm, d), tok), _mod_spec(d, per_batch),
                  pl.BlockSpec((d, tf), lambda b, i, j: (0, j)), pl.BlockSpec((tf, d), lambda b, i, j: (j, 0)),
                  pl.BlockSpec((1, d), const), pl.BlockSpec((1, d), const)],
        out_specs=pl.BlockSpec((1, tm, d), tok),
        out_shape=jax.ShapeDtypeStruct((bsz, t, d), F32),
        scratch_shapes=[pltpu.VMEM((tm, d), MXU_DTYPE), pltpu.VMEM((tm, d), F32)],
        compiler_params=_cparams("parallel", "parallel", "arbitrary"),
        name="mlp",
    )(x, mod, w1, w2, ln_g, ln_b)


def _gqa_qkv_kernel(x_ref, mod_ref, w_ref, gq_ref, gk_ref, cos_ref, sin_ref, q_ref, k_ref, v_ref,
                    *, d, n_q, n_kv):
    h = x_ref[0] * (1.0 + _mod_part(mod_ref, 1, d)) + _mod_part(mod_ref, 0, d)
    acc = _mm(h, w_ref[...])
    cos, sin = cos_ref[...], sin_ref[...]
    hd = ATT_HEAD_DIM
    src = lax.broadcasted_iota(jnp.int32, (hd, hd), 0)
    dst = lax.broadcasted_iota(jnp.int32, (hd, hd), 1)
    first = (dst & (hd // 2 - 1)) < hd // 4
    rot = (jnp.where(first & (src == dst + hd // 4), -1.0, 0.0)
           + jnp.where(jnp.logical_not(first) & (src == dst - hd // 4), 1.0, 0.0))

    def norm_rope(u, g):
        u = u * lax.rsqrt(jnp.mean(u * u, -1, keepdims=True) + LN_EPS) * g
        return u * cos + _mm(u, rot) * sin

    for i in range(n_q):
        q_ref[0, :, i * hd:(i + 1) * hd] = norm_rope(acc[:, i * hd:(i + 1) * hd], gq_ref[...]).astype(q_ref.dtype)
    for i in range(n_kv):
        c0 = (n_q + i) * hd
        k_ref[0, :, i * hd:(i + 1) * hd] = norm_rope(acc[:, c0:c0 + hd], gk_ref[...]).astype(k_ref.dtype)
    ones = jnp.ones((acc.shape[0], hd), v_ref.dtype)
    for i in range(n_kv):
        c0 = (n_q + n_kv + i) * hd
        v_ref[0, :, 2 * i * hd:(2 * i + 1) * hd] = acc[:, c0:c0 + hd].astype(v_ref.dtype)
        v_ref[0, :, (2 * i + 1) * hd:(2 * i + 2) * hd] = ones


def _gqa_qkv(x, mod, per_batch, w, gq, gk, cos, sin):
    bsz, t, d = x.shape
    hd = ATT_HEAD_DIM
    n_kv = ATT_KV_HEADS
    n_q = w.shape[1] // hd - 2 * n_kv
    tm = min(512, t)
    tok = lambda b, i: (b, i, 0)
    const = lambda b, i: (0, 0)
    tab = pl.BlockSpec((tm, hd), lambda b, i: (i, 0))
    return pl.pallas_call(
        functools.partial(_gqa_qkv_kernel, d=d, n_q=n_q, n_kv=n_kv),
        grid=(bsz, t // tm),
        in_specs=[pl.BlockSpec((1, tm, d), tok), _mod_spec(d, per_batch), pl.BlockSpec(w.shape, const),
                  pl.BlockSpec((1, hd), const), pl.BlockSpec((1, hd), const), tab, tab],
        out_specs=[pl.BlockSpec((1, tm, n_q * hd), tok), pl.BlockSpec((1, tm, n_kv * hd), tok),
                   pl.BlockSpec((1, tm, 2 * n_kv * hd), tok)],
        out_shape=[jax.ShapeDtypeStruct((bsz, t, n_q * hd), MXU_DTYPE),
                   jax.ShapeDtypeStruct((bsz, t, n_kv * hd), MXU_DTYPE),
                   jax.ShapeDtypeStruct((bsz, t, 2 * n_kv * hd), MXU_DTYPE)],
        compiler_params=_cparams("parallel", "parallel"),
        name="gqa_qkv",
    )(x, mod, w, gq, gk, cos, sin)


def _gqa_att_kernel(*refs, n_sets, groups, tk):
    q_ref, o_ref = refs[0], refs[-1]
    kv = refs[1:-1]
    hd = ATT_HEAD_DIM
    tq = q_ref.shape[1]
    q = jnp.concatenate([q_ref[0, :, g * hd:(g + 1) * hd] for g in range(groups)], axis=0)
    m = acc = None
    for j in range(n_sets):
        k_ref, v_ref = kv[2 * j], kv[2 * j + 1]
        t = k_ref.shape[1]
        for c0 in range(0, t, tk):
            c1 = min(c0 + tk, t)
            s = _mm_nt(q, k_ref[0, c0:c1, :])
            m_c = jnp.max(s, -1, keepdims=True)
            if m is None:
                m = m_c
                acc = _mm(jnp.exp2(s - m), v_ref[0, c0:c1, :])
            else:
                m_new = jnp.maximum(m, m_c)
                acc = acc * jnp.exp2(m - m_new) + _mm(jnp.exp2(s - m_new), v_ref[0, c0:c1, :])
                m = m_new
    o = acc[:, :hd] / acc[:, hd:]
    for g in range(groups):
        o_ref[0, :, g * hd:(g + 1) * hd] = o[g * tq:(g + 1) * tq].astype(o_ref.dtype)


def _gqa_att(q, kv_sets):
    bsz, s, dq = q.shape
    hd = ATT_HEAD_DIM
    n_kv = ATT_KV_HEADS
    groups = dq // hd // n_kv
    tq = min(256, s)
    in_specs = [pl.BlockSpec((1, tq, groups * hd), lambda b, kh, i: (b, i, kh))]
    args = [q]
    for k, v in kv_sets:
        t = k.shape[1]
        in_specs += [pl.BlockSpec((1, t, hd), lambda b, kh, i: (b, 0, kh)),
                     pl.BlockSpec((1, t, 2 * hd), lambda b, kh, i: (b, 0, kh))]
        args += [k, v]
    return pl.pallas_call(
        functools.partial(_gqa_att_kernel, n_sets=len(kv_sets), groups=groups, tk=GQA_KEY_CHUNK),
        grid=(bsz, n_kv, s // tq),
        in_specs=in_specs,
        out_specs=pl.BlockSpec((1, tq, groups * hd), lambda b, kh, i: (b, i, kh)),
        out_shape=jax.ShapeDtypeStruct((bsz, s, dq), MXU_DTYPE),
        compiler_params=_cparams("parallel", "parallel", "parallel"),
        name="gqa_att",
    )(*args)


def _rope_tables(n_tokens, head_dim):
    t = np.arange(n_tokens)
    row = (t // GRID_W).astype(np.float32)
    col = (t % GRID_W).astype(np.float32)
    half = head_dim // 2
    freqs = jnp.asarray(ROPE_BASE, F32) ** (-jnp.arange(0, half, 2, dtype=F32) / half)
    ang_r = jnp.asarray(row)[:, None] * freqs[None, :]
    ang_c = jnp.asarray(col)[:, None] * freqs[None, :]
    ang = jnp.concatenate([ang_r, ang_r, ang_c, ang_c], axis=-1)
    return jnp.cos(ang), jnp.sin(ang)


def _gqa_layer(x, xc, mod, mod_c, wqkv, wo, q_norm, k_norm, want_ctx):
    s, tc = x.shape[1], xc.shape[1]
    hd = ATT_HEAD_DIM
    gq = (q_norm * (hd ** -0.5 * LOG2E)).reshape(1, hd)
    gk = k_norm.reshape(1, hd)
    cos, sin = _rope_tables(s, hd)
    one, zero = jnp.ones((tc, hd), F32), jnp.zeros((tc, hd), F32)
    q, k, v = _gqa_qkv(x, mod, True, wqkv, gq, gk, cos, sin)
    qc, kc, vc = _gqa_qkv(xc, mod_c, False, wqkv, gq, gk, one, zero)
    o = _gqa_att(q, [(k, v), (kc, vc)])
    oc = _gqa_att(qc, [(kc, vc)]) if want_ctx else None
    return o, oc, wo


def _na_qkv_kernel(x_ref, mod_ref, w_ref, q_ref, k_ref, v_ref, *, d, scale):
    h = x_ref[0] * (1.0 + _mod_part(mod_ref, 1, d)) + _mod_part(mod_ref, 0, d)
    acc = _mm(h, w_ref[...])
    q_ref[0] = (acc[:, :d] * scale).astype(q_ref.dtype)
    k_ref[0] = acc[:, d:2 * d].astype(k_ref.dtype)
    ones = jnp.ones((acc.shape[0], LANES), v_ref.dtype)
    for p in range(d // LANES):
        c0 = 2 * d + p * LANES
        v_ref[0, :, 2 * p * LANES:(2 * p + 1) * LANES] = acc[:, c0:c0 + LANES].astype(v_ref.dtype)
        v_ref[0, :, (2 * p + 1) * LANES:(2 * p + 2) * LANES] = ones


def _na_qkv(x, mod, per_batch, w):
    bsz, t, d = x.shape
    tm = min(512, t)
    tok = lambda b, i: (b, i, 0)
    out = jax.ShapeDtypeStruct((bsz, t, d), MXU_DTYPE)
    return pl.pallas_call(
        functools.partial(_na_qkv_kernel, d=d, scale=NA_HEAD_DIM ** -0.5 * LOG2E),
        grid=(bsz, t // tm),
        in_specs=[pl.BlockSpec((1, tm, d), tok), _mod_spec(d, per_batch), pl.BlockSpec(w.shape, lambda b, i: (0, 0))],
        out_specs=[pl.BlockSpec((1, tm, d), tok)] * 2 + [pl.BlockSpec((1, tm, 2 * d), tok)],
        out_shape=[out, out, jax.ShapeDtypeStruct((bsz, t, 2 * d), MXU_DTYPE)],
        compiler_params=_cparams("parallel", "parallel"),
        name="na_qkv",
    )(x, mod, w)


def _pair_softmax_att(q, ks, vs, biases):
    lane = lax.broadcasted_iota(jnp.int32, q.shape, 1)
    first = lane < NA_HEAD_DIM
    zero = jnp.zeros_like(q)
    qm = [jnp.where(first, q, zero), jnp.where(first, zero, q)]
    m, acc = [None, None], [None, None]
    for j, (k, v) in enumerate(zip(ks, vs)):
        for h in range(2):
            s = _mm_nt(qm[h], k)
            if biases[h][j] is not None:
                s = s + biases[h][j]
            m_c = jnp.max(s, -1, keepdims=True)
            if m[h] is None:
                m[h] = m_c
                acc[h] = _mm(jnp.exp2(s - m_c), v)
            else:
                m_new = jnp.maximum(m[h], m_c)
                acc[h] = acc[h] * jnp.exp2(m[h] - m_new) + _mm(jnp.exp2(s - m_new), v)
                m[h] = m_new
    outs = [a[:, :LANES] / a[:, LANES:] for a in acc]
    return jnp.where(first, outs[0], outs[1])


def _na_att_kernel(q_ref, k0, k1, k2, k3, v0, v1, v2, v3, kc_ref, vc_ref, bias_ref, o_ref):
    ks = [kc_ref[0], k0[0], k1[0], k2[0], k3[0]]
    vs = [vc_ref[0], v0[0], v1[0], v2[0], v3[0]]
    kb = k0.shape[1]
    biases = [[None] + [bias_ref[0, h, :, j * kb:(j + 1) * kb] for j in range(4)] for h in range(2)]
    o_ref[0] = _pair_softmax_att(q_ref[0], ks, vs, biases).astype(o_ref.dtype)


def _na_bias_tables(rpb, rows, scale):
    n_blk = rows // NA_Q_ROWS
    heads = rpb.shape[0]
    half_h, half_w = NA_WIN_H // 2, NA_WIN_W // 2
    pad = GRID_W
    rpb_p = jnp.pad(rpb, ((0, 0), (0, 0), (pad, pad)))
    cmat = jnp.stack([rpb_p[:, :, pad + NA_WIN_W - 1 - cq: pad + NA_WIN_W - 1 - cq + GRID_W]
                      for cq in range(GRID_W)], axis=2)
    cq, ck = np.arange(GRID_W)[:, None], np.arange(GRID_W)[None, :]
    c0 = np.clip(cq - half_w, 0, GRID_W - NA_WIN_W)
    vcol = (ck >= c0) & (ck < c0 + NA_WIN_W)
    cmat = jnp.where(vcol[None, None], cmat * scale, NEG_BIG)
    n_d = 2 * NA_WIN_H - 1
    cmat = jnp.concatenate([jnp.full((heads, NA_K_ROWS, GRID_W, GRID_W), NEG_BIG, F32), cmat,
                            jnp.full((heads, NA_K_ROWS, GRID_W, GRID_W), NEG_BIG, F32)], axis=1)
    types = []
    for i in (0, min(1, n_blk - 1), n_blk - 1):
        start = int(np.clip(NA_Q_ROWS * i - half_h, 0, rows - NA_K_ROWS))
        row_k = start + np.arange(NA_K_ROWS)
        per_row = []
        for rq in range(NA_Q_ROWS):
            row_q = NA_Q_ROWS * i + rq
            r0 = int(np.clip(row_q - half_h, 0, rows - NA_WIN_H))
            vrow = (row_k >= r0) & (row_k < r0 + NA_WIN_H)
            d0 = start - row_q + NA_WIN_H - 1 + NA_K_ROWS
            assert 0 <= d0 and d0 + NA_K_ROWS <= n_d + 2 * NA_K_ROWS
            blk = jnp.where(vrow[None, :, None, None], cmat[:, d0:d0 + NA_K_ROWS], NEG_BIG)
            per_row.append(jnp.transpose(blk, (0, 2, 1, 3)))
        types.append(jnp.stack(per_row, axis=1).reshape(heads, NA_Q_ROWS * GRID_W, NA_K_ROWS * GRID_W))
    return jnp.stack(types)


def _na_att(q, k, v, kc, vc, bias):
    bsz, s, d = q.shape
    tc = kc.shape[1]
    rows = s // GRID_W
    n_blk = rows // NA_Q_ROWS
    tq = NA_Q_ROWS * GRID_W
    kb = NA_K_ROWS * GRID_W // 4
    n_kb = s // kb
    pairs = d // LANES

    def kv_spec(j, width):
        return pl.BlockSpec((1, kb, width),
                            lambda hp, i, b: (b, jnp.clip(2 * i - 1, 0, n_kb - 4) + j, hp))

    ctx_spec = lambda width: pl.BlockSpec((1, tc, width), lambda hp, i, b: (b, 0, hp))
    bias_spec = pl.BlockSpec(
        (1, 2, tq, 4 * kb),
        lambda hp, i, b: ((i > 0).astype(jnp.int32) + (i == n_blk - 1).astype(jnp.int32), hp, 0, 0))
    q_spec = pl.BlockSpec((1, tq, LANES), lambda hp, i, b: (b, i, hp))
    return pl.pallas_call(
        _na_att_kernel,
        grid=(pairs, n_blk, bsz),
        in_specs=[q_spec] + [kv_spec(j, LANES) for j in range(4)] + [kv_spec(j, 2 * LANES) for j in range(4)]
                 + [ctx_spec(LANES), ctx_spec(2 * LANES), bias_spec],
        out_specs=q_spec,
        out_shape=jax.ShapeDtypeStruct((bsz, s, d), MXU_DTYPE),
        compiler_params=_cparams("parallel", "parallel", "parallel"),
        name="na_att",
    )(q, k, k, k, k, v, v, v, v, kc, vc, bias)


def _pair_att_kernel(q_ref, k_ref, v_ref, o_ref):
    o_ref[0] = _pair_softmax_att(q_ref[0], [k_ref[0]], [v_ref[0]], [[None], [None]]).astype(o_ref.dtype)


def _pair_att(q, k, v):
    bsz, t, d = q.shape
    spec = pl.BlockSpec((1, t, LANES), lambda b, hp: (b, 0, hp))
    return pl.pallas_call(
        _pair_att_kernel,
        grid=(bsz, d // LANES),
        in_specs=[spec, spec, pl.BlockSpec((1, t, 2 * LANES), lambda b, hp: (b, 0, hp))],
        out_specs=spec,
        out_shape=jax.ShapeDtypeStruct((bsz, t, d), MXU_DTYPE),
        compiler_params=_cparams("parallel", "parallel"),
        name="na_ctx_att",
    )(q, k, v)


def _na_layer(x, xc, mod, mod_c, wqkv, wo, rpb, want_ctx):
    rows = x.shape[1] // GRID_W
    assert rows % NA_Q_ROWS == 0 and rows >= NA_K_ROWS
    q, k, v = _na_qkv(x, mod, True, wqkv)
    qc, kc, vc = _na_qkv(xc, mod_c, False, wqkv)
    o = _na_att(q, k, v, kc, vc, _na_bias_tables(rpb, rows, LOG2E))
    oc = _pair_att(qc, kc, vc) if want_ctx else None
    return o, oc, wo


def _rw_proj_kernel(x_ref, xp_ref, xn_ref, mod_ref, mu_ref, wr_ref, wk_ref, wv_ref, w1_ref, w2_ref, a1_ref, a2_ref,
                    g1_ref, g2_ref, w0_ref, a0_ref, r_ref, k_ref, v_ref, g_ref, lw_ref, as_ref, *, d):
    t = pl.program_id(1)
    nt = pl.num_programs(1)
    scale = 1.0 + _mod_part(mod_ref, 1, d)
    shift = _mod_part(mod_ref, 0, d)
    h = x_ref[0] * scale + shift
    tm = h.shape[0]
    h_prev = (xp_ref[0, 7:8, :] * scale + shift) * (t > 0).astype(F32)
    h_next = (xn_ref[0, 0:1, :] * scale + shift) * (t < nt - 1).astype(F32)
    row = lax.broadcasted_iota(jnp.int32, h.shape, 0)
    prev = jnp.where(row == 0, h_prev, pltpu.roll(h, 1, 0))
    nxt = jnp.where(row == tm - 1, h_next, pltpu.roll(h, tm - 1, 0))
    xx = 0.5 * (prev + nxt) - h
    mix = lambda j: h + xx * mu_ref[j:j + 1, :]

    r_ref[0] = _mm(mix(0), wr_ref[...]).astype(r_ref.dtype)
    k_ref[0] = _mm(mix(2), wk_ref[...]).astype(k_ref.dtype)
    v_ref[0] = _mm(mix(3), wv_ref[...]).astype(v_ref.dtype)
    g_ref[0] = _mm(_sigmoid(_mm(mix(5), g1_ref[...])), g2_ref[...]).astype(g_ref.dtype)

    tw = jnp.tanh(_mm(mix(1), w1_ref[...]))
    al = _mm(mix(4), a1_ref[...])
    first = lax.broadcasted_iota(jnp.int32, tw.shape, 1) < tw.shape[1] // 2
    zero = jnp.zeros_like(tw)
    for n in range(2):
        pick = lambda u: jnp.where(first, u, zero) if n == 0 else jnp.where(first, zero, u)
        z = -(w0_ref[n:n + 1, :] + _mm(pick(tw), w2_ref[...]))
        softplus = jnp.maximum(z, 0.0) + jnp.log(1.0 + jnp.exp(-jnp.abs(z)))
        lw_ref[n, 0] = -jnp.exp(-softplus - 0.5)
        as_ref[n, 0] = _sigmoid(a0_ref[n:n + 1, :] + _mm(pick(al), a2_ref[...])).astype(as_ref.dtype)


def _rw_proj(x, mod, per_batch, p):
    bsz, t, d = x.shape
    tm = min(512, t)
    n8 = t // 8
    tok = lambda b, i: (b, i, 0)
    dtok = lambda b, i: (0, b, i, 0)
    full = lambda a: pl.BlockSpec(a.shape, lambda b, i: (0,) * a.ndim)
    halo_prev = pl.BlockSpec((1, 8, d), lambda b, i: (b, jnp.maximum(i * (tm // 8) - 1, 0), 0))
    halo_next = pl.BlockSpec((1, 8, d), lambda b, i: (b, jnp.minimum((i + 1) * (tm // 8), n8 - 1), 0))
    weights = [p['mu'], p['wr'], p['wk'], p['wv'], p['w1'], p['w2'], p['a1'], p['a2'], p['g1'], p['g2'],
               p['w0'], p['a0']]
    one = jax.ShapeDtypeStruct((bsz, t, d), MXU_DTYPE)
    two = lambda dtype: jax.ShapeDtypeStruct((2, bsz, t, d), dtype)
    return pl.pallas_call(
        functools.partial(_rw_proj_kernel, d=d),
        grid=(bsz, t // tm),
        in_specs=[pl.BlockSpec((1, tm, d), tok), halo_prev, halo_next, _mod_spec(d, per_batch)]
                 + [full(a) for a in weights],
        out_specs=[pl.BlockSpec((1, tm, d), tok)] * 4 + [pl.BlockSpec((2, 1, tm, d), dtok)] * 2,
        out_shape=[one, one, one, one, two(F32), two(MXU_DTYPE)],
        compiler_params=_cparams("parallel", "parallel"),
        name="rw_proj",
    )(x, x, x, mod, *weights)


def _rw_scan_kernel(r_ref, k_ref, v_ref, lw_ref, as_ref, kk_ref, ka_ref, z0_ref, y_ref, zf_ref, z_scr,
                    *, pairs, sub, reverse):
    c = pl.program_id(1)
    L = RW_CHUNK
    H = RW_HEAD
    assert 2 * L == LANES and 2 * H == LANES

    @pl.when(c == 0)
    def _():
        z_scr[...] = z0_ref[0, 0]

    sign = -1 if reverse else 1
    t_i = lax.broadcasted_iota(jnp.int32, (L, 2 * L), 0)
    s_i = lax.broadcasted_iota(jnp.int32, (L, 2 * L), 1) & (L - 1)
    dt = sign * (s_i - t_i)
    strict_c = dt < 0
    incl_c = dt <= 0
    tri2 = incl_c.astype(F32)
    eye_c = (dt == 0).astype(F32)
    r2 = lax.broadcasted_iota(jnp.int32, (LANES, LANES), 0)
    c2 = lax.broadcasted_iota(jnp.int32, (LANES, LANES), 1)
    eye2 = (r2 == c2).astype(F32)
    same_head = _head_ones()
    first = lax.broadcasted_iota(jnp.int32, (L, LANES), 1) < H
    zero = jnp.zeros((L, LANES), F32)

    head0 = lambda u: jnp.where(first, u, zero)
    head1 = lambda u: jnp.where(first, zero, u)
    bd = lambda u: jnp.concatenate([head0(u), head1(u)], axis=0)

    each = lambda fn, *lists: [fn(*args) for args in zip(*lists)]
    sls = [slice(p * LANES, (p + 1) * LANES) for p in range(pairs)]
    cat0 = lambda *u: jnp.concatenate(u, axis=0)
    cat1 = lambda *u: jnp.concatenate(u, axis=1)

    zero_w = jnp.zeros((LANES, LANES), F32)

    def mm_two(lhs, w):
        out = []
        for i in range(0, len(lhs), 2):
            o = _mm(cat1(lhs[i], lhs[i + 1]), cat0(cat1(w[i], zero_w), cat1(zero_w, w[i + 1])))
            out += [o[:, :LANES], o[:, LANES:]]
        return out

    def mm_two_shared_lhs(lhs, w):
        out = []
        for i in range(0, len(w), 2):
            o = _mm(lhs, cat1(w[i], w[i + 1]))
            out += [o[:, :LANES], o[:, LANES:]]
        return out

    def chunk_terms(rw):
        lw = [lw_ref[0, 0, rw, sl] for sl in sls]
        r = [r_ref[0, rw, sl].astype(F32) for sl in sls]
        k = [k_ref[0, rw, sl].astype(F32) for sl in sls]
        v = [v_ref[0, rw, sl].astype(F32) for sl in sls]
        a_s = [as_ref[0, 0, rw, sl].astype(F32) for sl in sls]

        c_in = mm_two_shared_lhs(tri2, each(lambda u: cat0(*_split(u, 2)), lw))
        c_all = each(lambda u: jnp.sum(u, axis=0, keepdims=True), lw)
        kk = each(lambda u, sl: u * kk_ref[:, sl], k, sls)
        ss = mm_two(each(lambda u: u * u, kk), [same_head] * len(kk))
        kk = each(lambda u, s: u * lax.rsqrt(jnp.maximum(s, 1e-12)), kk, ss)
        b_v = each(lambda u, a: u * a, kk, a_s)
        k_d = each(lambda u, a, sl: u * (1.0 + (a - 1.0) * ka_ref[:, sl]), k, a_s, sls)
        e_neg = each(lambda ci: jnp.exp(-ci), c_in)
        e_rem = each(lambda ci, ca: jnp.exp(ca - ci), c_in, c_all)
        a_t = each(lambda u, ci, l: -u * jnp.exp(ci - l), kk, c_in, lw)
        r_t = each(lambda u, ci: u * jnp.exp(ci), r, c_in)
        b_t = each(lambda u, e: u * e, b_v, e_neg)
        k_t = each(lambda u, e: u * e, k_d, e_neg)
        b_h = each(lambda u, e: u * e, b_v, e_rem)
        k_h = each(lambda u, e: u * e, k_d, e_rem)
        yield None

        x = each(lambda a, rr, b, kt: _mm_nt(cat0(a, rr), cat0(bd(b), bd(kt))), a_t, r_t, b_t, k_t)
        m_ab = each(lambda u: jnp.where(strict_c, u[:L, :2 * L], 0.0), x)
        m_ak = each(lambda u: jnp.where(strict_c, u[:L, 2 * L:], 0.0), x)
        n_rb = each(lambda u: jnp.where(incl_c, u[L:, :2 * L], 0.0), x)
        n_rk = each(lambda u: jnp.where(incl_c, u[L:, 2 * L:], 0.0), x)

        inv = each(lambda m: eye_c + m, m_ab)
        yield None
        pw = mm_two(m_ab, each(bd, m_ab))
        for _ in range(int(np.log2(L)) - 2):
            yield None
            st = mm_two(each(cat0, pw, inv), each(bd, pw))
            pw = each(lambda s: s[:L], st)
            inv = each(lambda b, s: b + s[L:], inv, st)
        yield None
        inv = each(lambda b, s: b + s, inv, mm_two(inv, each(bd, pw)))
        mv = mm_two(each(cat0, m_ak, n_rk), each(bd, v))
        yield None
        tw = each(lambda i, a, m: _mm(i, cat1(bd(a), bd(m[:L]))), inv, a_t, mv)
        p1 = each(lambda u: u[:, :LANES], tw)
        p2 = each(lambda u: u[:, LANES:], tw)
        yield None
        nw = each(lambda m, a, b: _mm(m, cat1(bd(a), bd(b))), n_rb, p1, p2)
        q1 = each(lambda rr, u: rr + u[:, :LANES], r_t, nw)
        q2 = each(lambda u, m: u[:, LANES:] + m[L:], nw, mv)
        gh = each(lambda b, kh, a, c_, u: _mm_tn(cat0(b, kh), cat0(cat1(a, c_), cat1(zero, u))),
                  b_h, k_h, p1, p2, v)
        g_t = each(lambda ca, u: eye2 * jnp.exp(ca) + same_head * u[:, :LANES], c_all, gh)
        h_t = each(lambda u: same_head * u[:, LANES:], gh)
        yield q1, q2, g_t, h_t

    chunk_rows = [slice(u * L, (u + 1) * L) for u in (range(sub - 1, -1, -1) if reverse else range(sub))]
    gens = [chunk_terms(rw) for rw in chunk_rows]
    terms = [None] * sub
    tick = 0
    while any(t is None for t in terms):
        for u, gen in enumerate(gens):
            if terms[u] is None and tick >= u * RW_STAGE_LAG:
                terms[u] = next(gen)
        tick += 1
    z = [z_scr[p] for p in range(pairs)]
    for rw, (q1, q2, g_t, h_t) in zip(chunk_rows, terms):
        yz = mm_two(each(cat0, q1, g_t), z)
        for p in range(pairs):
            y_ref[0, rw, sls[p]] = yz[p][:L] + q2[p]
            z[p] = yz[p][L:] + h_t[p]
    for p in range(pairs):
        z_scr[p] = z[p]

    @pl.when(c == pl.num_programs(1) - 1)
    def _():
        zf_ref[0] = z_scr[...]


def _rw_scan(r, k, v, lw, a_s, kk_w, ka_w, z0, reverse):
    bsz, t, d = r.shape
    sub = 2
    rows = sub * RW_CHUNK
    n_c = t // rows
    n = int(reverse)
    pairs = d // LANES
    cidx = (lambda c: n_c - 1 - c) if reverse else (lambda c: c)
    tok = pl.BlockSpec((1, rows, d), lambda b, c: (b, cidx(c), 0))
    dtok = pl.BlockSpec((1, 1, rows, d), lambda b, c: (n, b, cidx(c), 0))
    vec = pl.BlockSpec((1, d), lambda b, c: (0, 0))
    return pl.pallas_call(
        functools.partial(_rw_scan_kernel, pairs=pairs, sub=sub, reverse=reverse),
        grid=(bsz, n_c),
        in_specs=[tok, tok, tok, dtok, dtok, vec, vec,
                  pl.BlockSpec((1, 1, pairs, LANES, LANES), lambda b, c: (n, b, 0, 0, 0))],
        out_specs=[tok, pl.BlockSpec((1, pairs, LANES, LANES), lambda b, c: (b, 0, 0, 0))],
        out_shape=[jax.ShapeDtypeStruct((bsz, t, d), F32),
                   jax.ShapeDtypeStruct((bsz, pairs, LANES, LANES), F32)],
        scratch_shapes=[pltpu.VMEM((pairs, LANES, LANES), F32)],
        compiler_params=_cparams("parallel", "arbitrary"),
        name="rw_scan",
    )(r, k, v, lw, a_s, kk_w, ka_w, z0)


def _rw_out_kernel(yf_ref, yb_ref, r_ref, k_ref, v_ref, g_ref, as_ref, ka_ref, rk_ref, lg_ref, lb_ref, wo_ref, x_ref,
                   mod_ref, g_ln_ref, b_ln_ref, o_ref, u_scr, *, d, alpha):
    same_head = _head_ones()
    inv_n = 1.0 / RW_HEAD
    for p in range(d // LANES):
        sl = slice(p * LANES, (p + 1) * LANES)
        y = yf_ref[0, :, sl] + yb_ref[0, :, sl]
        mu = _mm_rhs_exact(y, same_head) * inv_n
        dy = y - mu
        var = _mm(dy * dy, same_head) * inv_n
        yn = dy * lax.rsqrt(var + RW_GN_EPS) * lg_ref[:, sl] + lb_ref[:, sl]
        k = k_ref[0, :, sl].astype(F32)
        ka = ka_ref[:, sl]
        k_sum = k * ((1.0 + (as_ref[0, 0, :, sl].astype(F32) - 1.0) * ka)
                     + (1.0 + (as_ref[1, 0, :, sl].astype(F32) - 1.0) * ka))
        bonus = _mm(r_ref[0, :, sl].astype(F32) * k_sum * rk_ref[:, sl], same_head)
        u = (yn + bonus * v_ref[0, :, sl].astype(F32)) * g_ref[0, :, sl].astype(F32)
        u_scr[:, sl] = u.astype(u_scr.dtype)
    z = alpha * x_ref[0] + _mod_part(mod_ref, 2, d) * _mm(u_scr[...], wo_ref[...])
    o_ref[0] = _layernorm(z, g_ln_ref[...], b_ln_ref[...])


def _rw_out(y_f, y_b, r, k, v, g, a_s, p, x, mod, per_batch, ln_g, ln_b, alpha):
    bsz, t, d = x.shape
    tm = min(256, t)
    tok = pl.BlockSpec((1, tm, d), lambda b, i: (b, i, 0))
    dtok = pl.BlockSpec((2, 1, tm, d), lambda b, i: (0, b, i, 0))
    vec = pl.BlockSpec((1, d), lambda b, i: (0, 0))
    return pl.pallas_call(
        functools.partial(_rw_out_kernel, d=d, alpha=alpha),
        grid=(bsz, t // tm),
        in_specs=[tok, tok, tok, tok, tok, tok, dtok, vec, vec, vec, vec, pl.BlockSpec((d, d), lambda b, i: (0, 0)),
                  tok, _mod_spec(d, per_batch), vec, vec],
        out_specs=tok,
        out_shape=jax.ShapeDtypeStruct((bsz, t, d), F32),
        scratch_shapes=[pltpu.VMEM((tm, d), MXU_DTYPE)],
        compiler_params=_cparams("parallel", "parallel"),
        name="rw_out",
    )(y_f, y_b, r, k, v, g, a_s, p['k_a'], p['r_k'], p['lnx_g'], p['lnx_b'], p['wo'], x, mod, ln_g, ln_b)


def _rw_layer(x, xc, mod, mod_c, p, want_ctx, ln_g, ln_b, alpha):
    bsz, _, d = x.shape
    rc, kc, vc, gc, lwc, asc = _rw_proj(xc, mod_c, False, p)
    r, k, v, g, lw, a_s = _rw_proj(x, mod, True, p)
    z0 = jnp.zeros((2, bsz, d // LANES, LANES, LANES), F32)
    scan = lambda args, z, rev: _rw_scan(*args, p['k_k'], p['k_a'], z, rev)
    yc_f, zc_f = scan((rc, kc, vc, lwc, asc), z0, False)
    yc_b, zc_b = scan((rc, kc, vc, lwc, asc), z0, True)
    zc = jnp.stack([zc_f, zc_b])
    y_f, _ = scan((r, k, v, lw, a_s), zc, False)
    y_b, _ = scan((r, k, v, lw, a_s), zc, True)
    x_new = _rw_out(y_f, y_b, r, k, v, g, a_s, p, x, mod, True, ln_g, ln_b, alpha)
    xc_new = (_rw_out(yc_f, yc_b, rc, kc, vc, gc, asc, p, xc, mod_c, False, ln_g, ln_b, alpha)
              if want_ctx else None)
    return x_new, xc_new


def kernel(x, c, ctx, c_ctx, mod_w, mod_b, post_ln_g, post_ln_b, mlp_w1, mlp_w2, att_wqkv, att_wo, att_q_norm, att_k_norm, na_wqkv, na_wo, na_rpb, rw_mu, rw_wr, rw_wk, rw_wv, rw_wo, rw_w0, rw_w1, rw_w2, rw_a0, rw_a1, rw_a2, rw_g1, rw_g2, rw_k_k, rw_k_a, rw_r_k, rw_lnx_g, rw_lnx_b):
    depth = mod_w.shape[0]
    bsz, _, d = x.shape
    alpha = (2.0 * depth) ** 0.25
    cast = lambda a: a.astype(MXU_DTYPE)

    cond_rows = -(-(bsz + 1) // 8) * 8
    cond = jnp.zeros((cond_rows, d), F32).at[:bsz].set(c).at[bsz].set(c_ctx)
    mods = _modulation(cond, mod_w, mod_b)

    xc = ctx
    for i in range(depth):
        kind, slot = i % N_MIXERS, i // N_MIXERS
        want_ctx = i < depth - 1
        mod = mods[i, :bsz].reshape(bsz, 1, N_MOD * d)
        mod_c = mods[i, bsz:bsz + 1].reshape(1, 1, N_MOD * d)
        g1, b1 = post_ln_g[i, 0:1], post_ln_b[i, 0:1]
        g2, b2 = post_ln_g[i, 1:2], post_ln_b[i, 1:2]
        if kind == 2:
            cat = lambda a: jnp.concatenate([a[0], a[1]], axis=-1)
            p = {'mu': rw_mu[slot], 'wr': cast(rw_wr[slot]), 'wk': cast(rw_wk[slot]), 'wv': cast(rw_wv[slot]),
                 'wo': cast(rw_wo[slot]), 'w0': rw_w0[slot], 'a0': rw_a0[slot],
                 'w1': cast(cat(rw_w1[slot])), 'a1': cast(cat(rw_a1[slot])),
                 'w2': cast(rw_w2[slot].reshape(-1, d)), 'a2': cast(rw_a2[slot].reshape(-1, d)),
                 'g1': cast(rw_g1[slot]), 'g2': cast(rw_g2[slot]),
                 'k_k': rw_k_k[slot].reshape(1, d), 'k_a': rw_k_a[slot].reshape(1, d),
                 'r_k': rw_r_k[slot].reshape(1, d), 'lnx_g': rw_lnx_g[slot].reshape(1, d),
                 'lnx_b': rw_lnx_b[slot].reshape(1, d)}
            x, xc_new = _rw_layer(x, xc, mod, mod_c, p, want_ctx, g1, b1, alpha)
        else:
            if kind == 0:
                o, oc, wo = _gqa_layer(x, xc, mod, mod_c, cast(att_wqkv[slot]), cast(att_wo[slot]),
                                       att_q_norm[slot], att_k_norm[slot], want_ctx)
            else:
                o, oc, wo = _na_layer(x, xc, mod, mod_c, cast(na_wqkv[slot]), cast(na_wo[slot]), na_rpb[slot],
                                      want_ctx)
            x = _out_ln(o, wo, x, mod, True, g1, b1, alpha)
            xc_new = _out_ln(oc, wo, xc, mod_c, False, g1, b1, alpha) if want_ctx else None
        w1, w2 = cast(mlp_w1[i]), cast(mlp_w2[i])
        x = _mlp(x, mod, True, w1, w2, g2, b2, alpha)
        if want_ctx:
            xc = _mlp(xc_new, mod_c, False, w1, w2, g2, b2, alpha)
    return x
```

```python
import functools

import numpy as np
import jax
import jax.numpy as jnp
from jax import lax
from jax.experimental import pallas as pl
from jax.experimental.pallas import tpu as pltpu

F32 = jnp.float32
MXU_DTYPE = jnp.bfloat16

GRID_W = 64
N_MOD = 6
N_MIXERS = 3
LN_EPS = 1e-6
ATT_HEAD_DIM = 128
ATT_KV_HEADS = 2
ROPE_BASE = 10000.0
GQA_KEY_CHUNK = 512
NA_HEAD_DIM = 64
NA_WIN_H = 8
NA_WIN_W = 16
NA_Q_ROWS = 32
RW_HEAD = 64
RW_GN_EPS = 64e-5
RW_CHUNK = 64
LANES = 128
NEG_BIG = -1e30
LOG2E = 1.4426950408889634

VMEM_LIMIT = 56 * 1024 * 1024


def _cparams(*sem):
    return pltpu.CompilerParams(dimension_semantics=sem, vmem_limit_bytes=VMEM_LIMIT)


def _mm(a, b):
    return jnp.dot(a.astype(MXU_DTYPE), b.astype(MXU_DTYPE), preferred_element_type=F32)


def _mm_nt(a, b):
    return lax.dot_general(a.astype(MXU_DTYPE), b.astype(MXU_DTYPE), (((1,), (1,)), ((), ())),
                           preferred_element_type=F32)


def _mm_tn(a, b):
    return lax.dot_general(a.astype(MXU_DTYPE), b.astype(MXU_DTYPE), (((0,), (0,)), ((), ())),
                           preferred_element_type=F32)


def _split(a, n):
    parts = []
    for _ in range(n - 1):
        p = a.astype(MXU_DTYPE)
        parts.append(p)
        a = a - p.astype(F32)
    parts.append(a.astype(MXU_DTYPE))
    return parts


def _mm_lhs_exact(m01, a, n=3):
    m01 = m01.astype(MXU_DTYPE)
    return sum(jnp.dot(m01, p, preferred_element_type=F32) for p in _split(a, n))


def _mm_rhs_exact(a, m01, n=2):
    m01 = m01.astype(MXU_DTYPE)
    return sum(jnp.dot(p, m01, preferred_element_type=F32) for p in _split(a, n))


def _mm_hi(a, b):
    a1, a2 = _split(a, 2)
    b1, b2 = _split(b, 2)
    d = functools.partial(jnp.dot, preferred_element_type=F32)
    return d(a1, b1) + d(a1, b2) + d(a2, b1)


def _sigmoid(x):
    return 1.0 / (1.0 + jnp.exp(-x))


def _layernorm(z, g, b):
    mu = jnp.mean(z, -1, keepdims=True)
    dz = z - mu
    var = jnp.mean(dz * dz, -1, keepdims=True)
    return dz * lax.rsqrt(var + LN_EPS) * g + b


def _mod_part(mod_ref, j, d):
    return mod_ref[0, :, j * d:(j + 1) * d]


def _head_ones(n=LANES):
    r = lax.broadcasted_iota(jnp.int32, (n, n), 0)
    c = lax.broadcasted_iota(jnp.int32, (n, n), 1)
    return ((r >= RW_HEAD) == (c >= RW_HEAD)).astype(F32)


def _modulation_kernel(c_ref, w_ref, b_ref, o_ref):
    c = c_ref[...]
    o_ref[0] = _mm_hi(c * _sigmoid(c), w_ref[0]) + b_ref[0]


def _modulation(cond, mod_w, mod_b):
    depth, d, n = mod_w.shape
    rows = cond.shape[0]
    tn = n // 4
    return pl.pallas_call(
        _modulation_kernel,
        grid=(depth, n // tn),
        in_specs=[pl.BlockSpec((rows, d), lambda i, j: (0, 0)),
                  pl.BlockSpec((1, d, tn), lambda i, j: (i, 0, j)),
                  pl.BlockSpec((1, 1, tn), lambda i, j: (i, 0, j))],
        out_specs=pl.BlockSpec((1, rows, tn), lambda i, j: (i, 0, j)),
        out_shape=jax.ShapeDtypeStruct((depth, rows, n), F32),
        compiler_params=_cparams("parallel", "parallel"),
        name="modulation",
    )(cond, mod_w, mod_b.reshape(depth, 1, n))


def _out_ln_kernel(o_ref, w_ref, x_ref, mod_ref, g_ref, b_ref, y_ref, *, d, alpha):
    y = _mm(o_ref[0], w_ref[...])
    z = alpha * x_ref[0] + _mod_part(mod_ref, 2, d) * y
    y_ref[0] = _layernorm(z, g_ref[...], b_ref[...])


def _mod_spec(d, per_batch):
    if per_batch:
        return pl.BlockSpec((1, 1, N_MOD * d), lambda b, *_: (b, 0, 0))
    return pl.BlockSpec((1, 1, N_MOD * d), lambda b, *_: (0, 0, 0))


def _out_ln(o, wo, x, mod, per_batch, ln_g, ln_b, alpha):
    bsz, t, d = x.shape
    tm = min(512, t)
    tok = lambda b, i: (b, i, 0)
    const = lambda b, i: (0, 0)
    return pl.pallas_call(
        functools.partial(_out_ln_kernel, d=d, alpha=alpha),
        grid=(bsz, t // tm),
        in_specs=[pl.BlockSpec((1, tm, d), tok), pl.BlockSpec((d, d), const), pl.BlockSpec((1, tm, d), tok),
                  _mod_spec(d, per_batch), pl.BlockSpec((1, d), const), pl.BlockSpec((1, d), const)],
        out_specs=pl.BlockSpec((1, tm, d), tok),
        out_shape=jax.ShapeDtypeStruct((bsz, t, d), F32),
        compiler_params=_cparams("parallel", "parallel"),
        name="out_ln",
    )(o, wo, x, mod, ln_g, ln_b)


def _mlp_kernel(x_ref, mod_ref, w1_ref, w2_ref, g_ref, b_ref, y_ref, h_scr, acc_scr, *, d, alpha):
    j = pl.program_id(2)

    @pl.when(j == 0)
    def _():
        h = x_ref[0] * (1.0 + _mod_part(mod_ref, 4, d)) + _mod_part(mod_ref, 3, d)
        h_scr[...] = h.astype(h_scr.dtype)
        acc_scr[...] = jnp.zeros_like(acc_scr)

    a = jnp.square(jnp.maximum(_mm(h_scr[...], w1_ref[...]), 0.0))
    acc_scr[...] += _mm(a, w2_ref[...])

    @pl.when(j == pl.num_programs(2) - 1)
    def _():
        z = alpha * x_ref[0] + _mod_part(mod_ref, 5, d) * acc_scr[...]
        y_ref[0] = _layernorm(z, g_ref[...], b_ref[...])


def _mlp(x, mod, per_batch, w1, w2, ln_g, ln_b, alpha):
    bsz, t, d = x.shape
    ff = w1.shape[1]
    tm = min(1024, t)
    tf = min(2048, ff)
    tok = lambda b, i, j: (b, i, 0)
    const = lambda b, i, j: (0, 0)
    return pl.pallas_call(
        functools.partial(_mlp_kernel, d=d, alpha=alpha),
        grid=(bsz, t // tm, ff // tf),
        in_specs=[pl.BlockSpec((1, tm, d), tok), _mod_spec(d, per_batch),
                  pl.BlockSpec((d, tf), lambda b, i, j: (0, j)), pl.BlockSpec((tf, d), lambda b, i, j: (j, 0)),
                  pl.BlockSpec((1, d), const), pl.BlockSpec((1, d), const)],
        out_specs=pl.BlockSpec((1, tm, d), tok),
        out_shape=jax.ShapeDtypeStruct((bsz, t, d), F32),
        scratch_shapes=[pltpu.VMEM((tm, d), MXU_DTYPE), pltpu.VMEM((tm, d), F32)],
        compiler_params=_cparams("parallel", "parallel", "arbitrary"),
        name="mlp",
    )(x, mod, w1, w2, ln_g, ln_b)


def _gqa_qkv_kernel(x_ref, mod_ref, w_ref, gq_ref, gk_ref, cos_ref, sin_ref, q_ref, k_ref, v_ref,
                    *, d, n_q, n_kv):
    h = x_ref[0] * (1.0 + _mod_part(mod_ref, 1, d)) + _mod_part(mod_ref, 0, d)
    acc = _mm(h, w_ref[...])
    cos, sin = cos_ref[...], sin_ref[...]
    hd = ATT_HEAD_DIM
    src = lax.broadcasted_iota(jnp.int32, (hd, hd), 0)
    dst = lax.broadcasted_iota(jnp.int32, (hd, hd), 1)
    first = (dst & (hd // 2 - 1)) < hd // 4
    rot = (jnp.where(first & (src == dst + hd // 4), -1.0, 0.0)
           + jnp.where(jnp.logical_not(first) & (src == dst - hd // 4), 1.0, 0.0))

    def norm_rope(u, g):
        u = u * lax.rsqrt(jnp.mean(u * u, -1, keepdims=True) + LN_EPS) * g
        return u * cos + _mm(u, rot) * sin

    for i in range(n_q):
        q_ref[0, :, i * hd:(i + 1) * hd] = norm_rope(acc[:, i * hd:(i + 1) * hd], gq_ref[...]).astype(q_ref.dtype)
    for i in range(n_kv):
        c0 = (n_q + i) * hd
        k_ref[0, :, i * hd:(i + 1) * hd] = norm_rope(acc[:, c0:c0 + hd], gk_ref[...]).astype(k_ref.dtype)
    ones = jnp.ones((acc.shape[0], hd), v_ref.dtype)
    for i in range(n_kv):
        c0 = (n_q + n_kv + i) * hd
        v_ref[0, :, 2 * i * hd:(2 * i + 1) * hd] = acc[:, c0:c0 + hd].astype(v_ref.dtype)
        v_ref[0, :, (2 * i + 1) * hd:(2 * i + 2) * hd] = ones


def _gqa_qkv(x, mod, per_batch, w, gq, gk, cos, sin):
    bsz, t, d = x.shape
    hd = ATT_HEAD_DIM
    n_kv = ATT_KV_HEADS
    n_q = w.shape[1] // hd - 2 * n_kv
    tm = min(512, t)
    tok = lambda b, i: (b, i, 0)
    const = lambda b, i: (0, 0)
    tab = pl.BlockSpec((tm, hd), lambda b, i: (i, 0))
    return pl.pallas_call(
        functools.partial(_gqa_qkv_kernel, d=d, n_q=n_q, n_kv=n_kv),
        grid=(bsz, t // tm),
        in_specs=[pl.BlockSpec((1, tm, d), tok), _mod_spec(d, per_batch), pl.BlockSpec(w.shape, const),
                  pl.BlockSpec((1, hd), const), pl.BlockSpec((1, hd), const), tab, tab],
        out_specs=[pl.BlockSpec((1, tm, n_q * hd), tok), pl.BlockSpec((1, tm, n_kv * hd), tok),
                   pl.BlockSpec((1, tm, 2 * n_kv * hd), tok)],
        out_shape=[jax.ShapeDtypeStruct((bsz, t, n_q * hd), MXU_DTYPE),
                   jax.ShapeDtypeStruct((bsz, t, n_kv * hd), MXU_DTYPE),
                   jax.ShapeDtypeStruct((bsz, t, 2 * n_kv * hd), MXU_DTYPE)],
        compiler_params=_cparams("parallel", "parallel"),
        name="gqa_qkv",
    )(x, mod, w, gq, gk, cos, sin)


def _gqa_att_kernel(*refs, n_sets, groups, tk):
    q_ref, o_ref = refs[0], refs[-1]
    kv = refs[1:-1]
    hd = ATT_HEAD_DIM
    tq = q_ref.shape[1]
    q = jnp.concatenate([q_ref[0, :, g * hd:(g + 1) * hd] for g in range(groups)], axis=0)
    m = acc = None
    for j in range(n_sets):
        k_ref, v_ref = kv[2 * j], kv[2 * j + 1]
        t = k_ref.shape[1]
        for c0 in range(0, t, tk):
            c1 = min(c0 + tk, t)
            s = _mm_nt(q, k_ref[0, c0:c1, :])
            m_c = jnp.max(s, -1, keepdims=True)
            if m is None:
                m = m_c
                acc = _mm(jnp.exp2(s - m), v_ref[0, c0:c1, :])
            else:
                m_new = jnp.maximum(m, m_c)
                acc = acc * jnp.exp2(m - m_new) + _mm(jnp.exp2(s - m_new), v_ref[0, c0:c1, :])
                m = m_new
    o = acc[:, :hd] / acc[:, hd:]
    for g in range(groups):
        o_ref[0, :, g * hd:(g + 1) * hd] = o[g * tq:(g + 1) * tq].astype(o_ref.dtype)


def _gqa_att(q, kv_sets):
    bsz, s, dq = q.shape
    hd = ATT_HEAD_DIM
    n_kv = ATT_KV_HEADS
    groups = dq // hd // n_kv
    tq = min(256, s)
    in_specs = [pl.BlockSpec((1, tq, groups * hd), lambda b, kh, i: (b, i, kh))]
    args = [q]
    for k, v in kv_sets:
        t = k.shape[1]
        in_specs += [pl.BlockSpec((1, t, hd), lambda b, kh, i: (b, 0, kh)),
                     pl.BlockSpec((1, t, 2 * hd), lambda b, kh, i: (b, 0, kh))]
        args += [k, v]
    return pl.pallas_call(
        functools.partial(_gqa_att_kernel, n_sets=len(kv_sets), groups=groups, tk=GQA_KEY_CHUNK),
        grid=(bsz, n_kv, s // tq),
        in_specs=in_specs,
        out_specs=pl.BlockSpec((1, tq, groups * hd), lambda b, kh, i: (b, i, kh)),
        out_shape=jax.ShapeDtypeStruct((bsz, s, dq), MXU_DTYPE),
        compiler_params=_cparams("parallel", "parallel", "parallel"),
        name="gqa_att",
    )(*args)


def _rope_tables(n_tokens, head_dim):
    t = np.arange(n_tokens)
    row = (t // GRID_W).astype(np.float32)
    col = (t % GRID_W).astype(np.float32)
    half = head_dim // 2
    freqs = jnp.asarray(ROPE_BASE, F32) ** (-jnp.arange(0, half, 2, dtype=F32) / half)
    ang_r = jnp.asarray(row)[:, None] * freqs[None, :]
    ang_c = jnp.asarray(col)[:, None] * freqs[None, :]
    ang = jnp.concatenate([ang_r, ang_r, ang_c, ang_c], axis=-1)
    return jnp.cos(ang), jnp.sin(ang)


def _gqa_layer(x, xc, mod, mod_c, wqkv, wo, q_norm, k_norm, want_ctx):
    s, tc = x.shape[1], xc.shape[1]
    hd = ATT_HEAD_DIM
    gq = (q_norm * (hd ** -0.5 * LOG2E)).reshape(1, hd)
    gk = k_norm.reshape(1, hd)
    cos, sin = _rope_tables(s, hd)
    one, zero = jnp.ones((tc, hd), F32), jnp.zeros((tc, hd), F32)
    q, k, v = _gqa_qkv(x, mod, True, wqkv, gq, gk, cos, sin)
    qc, kc, vc = _gqa_qkv(xc, mod_c, False, wqkv, gq, gk, one, zero)
    o = _gqa_att(q, [(k, v), (kc, vc)])
    oc = _gqa_att(qc, [(kc, vc)]) if want_ctx else None
    return o, oc, wo


def _na_qkv_kernel(x_ref, mod_ref, w_ref, q_ref, k_ref, v_ref, *, d, scale):
    h = x_ref[0] * (1.0 + _mod_part(mod_ref, 1, d)) + _mod_part(mod_ref, 0, d)
    acc = _mm(h, w_ref[...])
    q_ref[0] = (acc[:, :d] * scale).astype(q_ref.dtype)
    k_ref[0] = acc[:, d:2 * d].astype(k_ref.dtype)
    ones = jnp.ones((acc.shape[0], LANES), v_ref.dtype)
    for p in range(d // LANES):
        c0 = 2 * d + p * LANES
        v_ref[0, :, 2 * p * LANES:(2 * p + 1) * LANES] = acc[:, c0:c0 + LANES].astype(v_ref.dtype)
        v_ref[0, :, (2 * p + 1) * LANES:(2 * p + 2) * LANES] = ones


def _na_qkv(x, mod, per_batch, w):
    bsz, t, d = x.shape
    tm = min(512, t)
    tok = lambda b, i: (b, i, 0)
    out = jax.ShapeDtypeStruct((bsz, t, d), MXU_DTYPE)
    return pl.pallas_call(
        functools.partial(_na_qkv_kernel, d=d, scale=NA_HEAD_DIM ** -0.5 * LOG2E),
        grid=(bsz, t // tm),
        in_specs=[pl.BlockSpec((1, tm, d), tok), _mod_spec(d, per_batch), pl.BlockSpec(w.shape, lambda b, i: (0, 0))],
        out_specs=[pl.BlockSpec((1, tm, d), tok)] * 2 + [pl.BlockSpec((1, tm, 2 * d), tok)],
        out_shape=[out, out, jax.ShapeDtypeStruct((bsz, t, 2 * d), MXU_DTYPE)],
        compiler_params=_cparams("parallel", "parallel"),
        name="na_qkv",
    )(x, mod, w)


def _pair_softmax_att(q, ks, vs, biases):
    lane = lax.broadcasted_iota(jnp.int32, q.shape, 1)
    first = lane < NA_HEAD_DIM
    zero = jnp.zeros_like(q)
    qm = [jnp.where(first, q, zero), jnp.where(first, zero, q)]
    m, acc = [None, None], [None, None]
    for j, (k, v) in enumerate(zip(ks, vs)):
        for h in range(2):
            s = _mm_nt(qm[h], k)
            if biases[h][j] is not None:
                s = s + biases[h][j]
            m_c = jnp.max(s, -1, keepdims=True)
            if m[h] is None:
                m[h] = m_c
                acc[h] = _mm(jnp.exp2(s - m_c), v)
            else:
                m_new = jnp.maximum(m[h], m_c)
                acc[h] = acc[h] * jnp.exp2(m[h] - m_new) + _mm(jnp.exp2(s - m_new), v)
                m[h] = m_new
    outs = [a[:, :LANES] / a[:, LANES:] for a in acc]
    return jnp.where(first, outs[0], outs[1])


def _na_att_kernel(q_ref, k_ref, v_ref, kc_ref, vc_ref, bias_ref, o_ref, *, rows):
    i = pl.program_id(2)
    q_rows = q_ref.shape[1] // GRID_W
    win = NA_WIN_H * GRID_W
    w = GRID_W
    first = lax.broadcasted_iota(jnp.int32, (w, LANES), 1) < NA_HEAD_DIM
    qm, kw, vw, bias = [], [], [], []
    for rq in range(q_rows):
        row_q = i * q_rows + rq
        r0 = jnp.clip(row_q - NA_WIN_H // 2, 0, rows - NA_WIN_H)
        start = pl.multiple_of(r0 * w, w)
        q = q_ref[0, rq * w:(rq + 1) * w, :]
        zero = jnp.zeros_like(q)
        qm.append(jnp.concatenate([jnp.where(first, q, zero), jnp.where(first, zero, q)], axis=0))
        kw.append(k_ref[0, pl.ds(start, win), :])
        vw.append(v_ref[0, pl.ds(start, win), :])
        e = r0 - row_q + NA_WIN_H - 1
        bias.append(jnp.concatenate([bias_ref[0, 0, e], bias_ref[0, 1, e]], axis=0))
    each = lambda fn, *lists: [fn(*args) for args in zip(*lists)]
    s_ctx = _mm_nt(jnp.concatenate(qm, axis=0), kc_ref[0])
    s_ctx = [s_ctx[2 * w * rq:2 * w * (rq + 1)] for rq in range(q_rows)]
    s_loc = each(lambda a, b, c: _mm_nt(a, b) + c, qm, kw, bias)
    m = each(lambda a, b: jnp.maximum(jnp.max(a, -1, keepdims=True), jnp.max(b, -1, keepdims=True)), s_loc, s_ctx)
    acc_ctx = _mm(jnp.concatenate(each(lambda a, mm: jnp.exp2(a - mm), s_ctx, m), axis=0), vc_ref[0])
    acc = each(lambda a, mm, v: _mm(jnp.exp2(a - mm), v), s_loc, m, vw)
    for rq in range(q_rows):
        a = acc[rq] + acc_ctx[2 * w * rq:2 * w * (rq + 1)]
        o = a[:, :LANES] / a[:, LANES:]
        o_ref[0, rq * w:(rq + 1) * w, :] = jnp.where(first, o[:w], o[w:]).astype(o_ref.dtype)


def _na_bias_tables(rpb, scale):
    pad = GRID_W
    rpb_p = jnp.pad(rpb, ((0, 0), (0, 0), (pad, pad)))
    cmat = jnp.stack([rpb_p[:, :, pad + NA_WIN_W - 1 - cq: pad + NA_WIN_W - 1 - cq + GRID_W]
                      for cq in range(GRID_W)], axis=2)
    cq, ck = np.arange(GRID_W)[:, None], np.arange(GRID_W)[None, :]
    c0 = np.clip(cq - NA_WIN_W // 2, 0, GRID_W - NA_WIN_W)
    vcol = (ck >= c0) & (ck < c0 + NA_WIN_W)
    cmat = jnp.where(vcol[None, None], cmat * scale, NEG_BIG)
    return jnp.stack([jnp.concatenate([cmat[:, e + j] for j in range(NA_WIN_H)], axis=-1)
                      for e in range(NA_WIN_H)], axis=1)


def _na_att(q, k, v, kc, vc, bias):
    bsz, s, d = q.shape
    tc = kc.shape[1]
    rows = s // GRID_W
    q_rows = min(NA_Q_ROWS, rows)
    assert rows % q_rows == 0 and rows >= NA_WIN_H
    tq = q_rows * GRID_W
    pairs = d // LANES
    q_spec = pl.BlockSpec((1, tq, LANES), lambda hp, b, i: (b, i, hp))
    whole = lambda t, width: pl.BlockSpec((1, t, width), lambda hp, b, i: (b, 0, hp))
    bias_spec = pl.BlockSpec((1, 2) + bias.shape[1:], lambda hp, b, i: (0, hp, 0, 0, 0))
    return pl.pallas_call(
        functools.partial(_na_att_kernel, rows=rows),
        grid=(pairs, bsz, rows // q_rows),
        in_specs=[q_spec, whole(s, LANES), whole(s, 2 * LANES), whole(tc, LANES), whole(tc, 2 * LANES), bias_spec],
        out_specs=q_spec,
        out_shape=jax.ShapeDtypeStruct((bsz, s, d), MXU_DTYPE),
        compiler_params=_cparams("parallel", "parallel", "parallel"),
        name="na_att",
    )(q, k, v, kc, vc, bias[None])


def _pair_att_kernel(q_ref, k_ref, v_ref, o_ref):
    o_ref[0] = _pair_softmax_att(q_ref[0], [k_ref[0]], [v_ref[0]], [[None], [None]]).astype(o_ref.dtype)


def _pair_att(q, k, v):
    bsz, t, d = q.shape
    spec = pl.BlockSpec((1, t, LANES), lambda b, hp: (b, 0, hp))
    return pl.pallas_call(
        _pair_att_kernel,
        grid=(bsz, d // LANES),
        in_specs=[spec, spec, pl.BlockSpec((1, t, 2 * LANES), lambda b, hp: (b, 0, hp))],
        out_specs=spec,
        out_shape=jax.ShapeDtypeStruct((bsz, t, d), MXU_DTYPE),
        compiler_params=_cparams("parallel", "parallel"),
        name="na_ctx_att",
    )(q, k, v)


def _na_layer(x, xc, mod, mod_c, wqkv, wo, rpb, want_ctx):
    q, k, v = _na_qkv(x, mod, True, wqkv)
    qc, kc, vc = _na_qkv(xc, mod_c, False, wqkv)
    o = _na_att(q, k, v, kc, vc, _na_bias_tables(rpb, LOG2E))
    oc = _pair_att(qc, kc, vc) if want_ctx else None
    return o, oc, wo


def _rw_proj_kernel(x_ref, xp_ref, xn_ref, mod_ref, mu_ref, wr_ref, wk_ref, wv_ref, w1_ref, w2_ref, a1_ref, a2_ref,
                    g1_ref, g2_ref, w0_ref, a0_ref, r_ref, k_ref, v_ref, g_ref, lw_ref, as_ref, *, d):
    t = pl.program_id(1)
    nt = pl.num_programs(1)
    scale = 1.0 + _mod_part(mod_ref, 1, d)
    shift = _mod_part(mod_ref, 0, d)
    h = x_ref[0] * scale + shift
    tm = h.shape[0]
    h_prev = (xp_ref[0, 7:8, :] * scale + shift) * (t > 0).astype(F32)
    h_next = (xn_ref[0, 0:1, :] * scale + shift) * (t < nt - 1).astype(F32)
    row = lax.broadcasted_iota(jnp.int32, h.shape, 0)
    prev = jnp.where(row == 0, h_prev, pltpu.roll(h, 1, 0))
    nxt = jnp.where(row == tm - 1, h_next, pltpu.roll(h, tm - 1, 0))
    xx = 0.5 * (prev + nxt) - h
    mix = lambda j: h + xx * mu_ref[j:j + 1, :]

    r_ref[0] = _mm(mix(0), wr_ref[...]).astype(r_ref.dtype)
    k_ref[0] = _mm(mix(2), wk_ref[...]).astype(k_ref.dtype)
    v_ref[0] = _mm(mix(3), wv_ref[...]).astype(v_ref.dtype)
    g_ref[0] = _mm(_sigmoid(_mm(mix(5), g1_ref[...])), g2_ref[...]).astype(g_ref.dtype)

    tw = jnp.tanh(_mm(mix(1), w1_ref[...]))
    al = _mm(mix(4), a1_ref[...])
    first = lax.broadcasted_iota(jnp.int32, tw.shape, 1) < tw.shape[1] // 2
    zero = jnp.zeros_like(tw)
    for n in range(2):
        pick = lambda u: jnp.where(first, u, zero) if n == 0 else jnp.where(first, zero, u)
        z = -(w0_ref[n:n + 1, :] + _mm(pick(tw), w2_ref[...]))
        softplus = jnp.maximum(z, 0.0) + jnp.log(1.0 + jnp.exp(-jnp.abs(z)))
        lw_ref[n, 0] = -jnp.exp(-softplus - 0.5)
        as_ref[n, 0] = _sigmoid(a0_ref[n:n + 1, :] + _mm(pick(al), a2_ref[...])).astype(as_ref.dtype)


def _rw_proj(x, mod, per_batch, p):
    bsz, t, d = x.shape
    tm = min(512, t)
    n8 = t // 8
    tok = lambda b, i: (b, i, 0)
    dtok = lambda b, i: (0, b, i, 0)
    full = lambda a: pl.BlockSpec(a.shape, lambda b, i: (0,) * a.ndim)
    halo_prev = pl.BlockSpec((1, 8, d), lambda b, i: (b, jnp.maximum(i * (tm // 8) - 1, 0), 0))
    halo_next = pl.BlockSpec((1, 8, d), lambda b, i: (b, jnp.minimum((i + 1) * (tm // 8), n8 - 1), 0))
    weights = [p['mu'], p['wr'], p['wk'], p['wv'], p['w1'], p['w2'], p['a1'], p['a2'], p['g1'], p['g2'],
               p['w0'], p['a0']]
    one = jax.ShapeDtypeStruct((bsz, t, d), MXU_DTYPE)
    two = lambda dtype: jax.ShapeDtypeStruct((2, bsz, t, d), dtype)
    return pl.pallas_call(
        functools.partial(_rw_proj_kernel, d=d),
        grid=(bsz, t // tm),
        in_specs=[pl.BlockSpec((1, tm, d), tok), halo_prev, halo_next, _mod_spec(d, per_batch)]
                 + [full(a) for a in weights],
        out_specs=[pl.BlockSpec((1, tm, d), tok)] * 4 + [pl.BlockSpec((2, 1, tm, d), dtok)] * 2,
        out_shape=[one, one, one, one, two(F32), two(MXU_DTYPE)],
        compiler_params=_cparams("parallel", "parallel"),
        name="rw_proj",
    )(x, x, x, mod, *weights)


def _rw_scan_kernel(r_ref, k_ref, v_ref, lw_ref, as_ref, kk_ref, ka_ref, z0_ref, y_ref, zf_ref, z_scr,
                    *, pairs, sub, reverse):
    c = pl.program_id(1)
    L = RW_CHUNK
    H = RW_HEAD
    assert 2 * L == LANES and 2 * H == LANES

    @pl.when(c == 0)
    def _():
        z_scr[...] = z0_ref[0, 0]

    sign = -1 if reverse else 1
    t_i = lax.broadcasted_iota(jnp.int32, (L, 2 * L), 0)
    s_i = lax.broadcasted_iota(jnp.int32, (L, 2 * L), 1) & (L - 1)
    dt = sign * (s_i - t_i)
    strict_c = dt < 0
    incl_c = dt <= 0
    tri2 = incl_c.astype(F32)
    eye_c = (dt == 0).astype(F32)
    r2 = lax.broadcasted_iota(jnp.int32, (LANES, LANES), 0)
    c2 = lax.broadcasted_iota(jnp.int32, (LANES, LANES), 1)
    eye2 = (r2 == c2).astype(F32)
    same_head = _head_ones()
    first = lax.broadcasted_iota(jnp.int32, (L, LANES), 1) < H
    zero = jnp.zeros((L, LANES), F32)

    head0 = lambda u: jnp.where(first, u, zero)
    head1 = lambda u: jnp.where(first, zero, u)
    bd = lambda u: jnp.concatenate([head0(u), head1(u)], axis=0)

    each = lambda fn, *lists: [fn(*args) for args in zip(*lists)]
    sls = [slice(p * LANES, (p + 1) * LANES) for p in range(pairs)]
    cat0 = lambda *u: jnp.concatenate(u, axis=0)
    cat1 = lambda *u: jnp.concatenate(u, axis=1)

    mm_each = lambda lhs, w: [_mm(a, b) for a, b in zip(lhs, w)]

    def chunk_terms(rw):
        lw = [lw_ref[0, 0, rw, sl] for sl in sls]
        r = [r_ref[0, rw, sl].astype(F32) for sl in sls]
        k = [k_ref[0, rw, sl].astype(F32) for sl in sls]
        v = [v_ref[0, rw, sl].astype(F32) for sl in sls]
        a_s = [as_ref[0, 0, rw, sl].astype(F32) for sl in sls]

        c_in = each(lambda u: _mm(tri2, cat0(*_split(u, 2))), lw)
        c_all = each(lambda u: jnp.sum(u, axis=0, keepdims=True), lw)
        kk = each(lambda u, sl: u * kk_ref[:, sl], k, sls)
        ss = each(lambda u: _mm(u * u, same_head), kk)
        kk = each(lambda u, s: u * lax.rsqrt(jnp.maximum(s, 1e-12)), kk, ss)
        b_v = each(lambda u, a: u * a, kk, a_s)
        k_d = each(lambda u, a, sl: u * (1.0 + (a - 1.0) * ka_ref[:, sl]), k, a_s, sls)
        e_neg = each(lambda ci: jnp.exp(-ci), c_in)
        e_rem = each(lambda ci, ca: jnp.exp(ca - ci), c_in, c_all)
        a_t = each(lambda u, ci, l: -u * jnp.exp(ci - l), kk, c_in, lw)
        r_t = each(lambda u, ci: u * jnp.exp(ci), r, c_in)
        b_t = each(lambda u, e: u * e, b_v, e_neg)
        k_t = each(lambda u, e: u * e, k_d, e_neg)
        b_h = each(lambda u, e: u * e, b_v, e_rem)
        k_h = each(lambda u, e: u * e, k_d, e_rem)
        yield None

        x = each(lambda a, rr, b, kt: _mm_nt(cat0(a, rr), cat0(bd(b), bd(kt))), a_t, r_t, b_t, k_t)
        m_ab = each(lambda u: jnp.where(strict_c, u[:L, :2 * L], 0.0), x)
        m_ak = each(lambda u: jnp.where(strict_c, u[:L, 2 * L:], 0.0), x)
        n_rb = each(lambda u: jnp.where(incl_c, u[L:, :2 * L], 0.0), x)
        n_rk = each(lambda u: jnp.where(incl_c, u[L:, 2 * L:], 0.0), x)

        inv = each(lambda m: eye_c + m, m_ab)
        yield None
        pw = mm_each(m_ab, each(bd, m_ab))
        for _ in range(int(np.log2(L)) - 2):
            yield None
            st = mm_each(each(cat0, pw, inv), each(bd, pw))
            pw = each(lambda s: s[:L], st)
            inv = each(lambda b, s: b + s[L:], inv, st)
        yield None
        inv = each(lambda b, s: b + s, inv, mm_each(inv, each(bd, pw)))
        mv = mm_each(each(cat0, m_ak, n_rk), each(bd, v))
        yield None
        tw = each(lambda i, a, m: _mm(i, cat1(bd(a), bd(m[:L]))), inv, a_t, mv)
        p1 = each(lambda u: u[:, :LANES], tw)
        p2 = each(lambda u: u[:, LANES:], tw)
        yield None
        nw = each(lambda m, a, b: _mm(m, cat1(bd(a), bd(b))), n_rb, p1, p2)
        q1 = each(lambda rr, u: rr + u[:, :LANES], r_t, nw)
        q2 = each(lambda u, m: u[:, LANES:] + m[L:], nw, mv)
        gh = each(lambda b, kh, a, c_, u: _mm_tn(cat0(b, kh), cat0(cat1(a, c_), cat1(zero, u))),
                  b_h, k_h, p1, p2, v)
        g_t = each(lambda ca, u: eye2 * jnp.exp(ca) + same_head * u[:, :LANES], c_all, gh)
        h_t = each(lambda u: same_head * u[:, LANES:], gh)
        yield q1, q2, g_t, h_t

    chunk_rows = [slice(u * L, (u + 1) * L) for u in (range(sub - 1, -1, -1) if reverse else range(sub))]
    gens = [chunk_terms(rw) for rw in chunk_rows]
    terms = [None] * sub
    while any(t is None for t in terms):
        terms = [next(gen) for gen in gens]
    z = [z_scr[p] for p in range(pairs)]
    for rw, (q1, q2, g_t, h_t) in zip(chunk_rows, terms):
        yz = mm_each(each(cat0, q1, g_t), z)
        for p in range(pairs):
            y_ref[0, rw, sls[p]] = yz[p][:L] + q2[p]
            z[p] = yz[p][L:] + h_t[p]
    for p in range(pairs):
        z_scr[p] = z[p]

    @pl.when(c == pl.num_programs(1) - 1)
    def _():
        zf_ref[0] = z_scr[...]


def _rw_scan(r, k, v, lw, a_s, kk_w, ka_w, z0, reverse):
    bsz, t, d = r.shape
    sub = 2
    rows = sub * RW_CHUNK
    n_c = t // rows
    n = int(reverse)
    pairs = d // LANES
    cidx = (lambda c: n_c - 1 - c) if reverse else (lambda c: c)
    tok = pl.BlockSpec((1, rows, d), lambda b, c: (b, cidx(c), 0))
    dtok = pl.BlockSpec((1, 1, rows, d), lambda b, c: (n, b, cidx(c), 0))
    vec = pl.BlockSpec((1, d), lambda b, c: (0, 0))
    return pl.pallas_call(
        functools.partial(_rw_scan_kernel, pairs=pairs, sub=sub, reverse=reverse),
        grid=(bsz, n_c),
        in_specs=[tok, tok, tok, dtok, dtok, vec, vec,
                  pl.BlockSpec((1, 1, pairs, LANES, LANES), lambda b, c: (n, b, 0, 0, 0))],
        out_specs=[tok, pl.BlockSpec((1, pairs, LANES, LANES), lambda b, c: (b, 0, 0, 0))],
        out_shape=[jax.ShapeDtypeStruct((bsz, t, d), F32),
                   jax.ShapeDtypeStruct((bsz, pairs, LANES, LANES), F32)],
        scratch_shapes=[pltpu.VMEM((pairs, LANES, LANES), F32)],
        compiler_params=_cparams("parallel", "arbitrary"),
        name="rw_scan",
    )(r, k, v, lw, a_s, kk_w, ka_w, z0)


def _rw_out_kernel(yf_ref, yb_ref, r_ref, k_ref, v_ref, g_ref, as_ref, ka_ref, rk_ref, lg_ref, lb_ref, wo_ref, x_ref,
                   mod_ref, g_ln_ref, b_ln_ref, o_ref, u_scr, *, d, alpha):
    same_head = _head_ones()
    inv_n = 1.0 / RW_HEAD
    for p in range(d // LANES):
        sl = slice(p * LANES, (p + 1) * LANES)
        y = yf_ref[0, :, sl] + yb_ref[0, :, sl]
        mu = _mm_rhs_exact(y, same_head) * inv_n
        dy = y - mu
        var = _mm(dy * dy, same_head) * inv_n
        yn = dy * lax.rsqrt(var + RW_GN_EPS) * lg_ref[:, sl] + lb_ref[:, sl]
        k = k_ref[0, :, sl].astype(F32)
        ka = ka_ref[:, sl]
        k_sum = k * ((1.0 + (as_ref[0, 0, :, sl].astype(F32) - 1.0) * ka)
                     + (1.0 + (as_ref[1, 0, :, sl].astype(F32) - 1.0) * ka))
        bonus = _mm(r_ref[0, :, sl].astype(F32) * k_sum * rk_ref[:, sl], same_head)
        u = (yn + bonus * v_ref[0, :, sl].astype(F32)) * g_ref[0, :, sl].astype(F32)
        u_scr[:, sl] = u.astype(u_scr.dtype)
    z = alpha * x_ref[0] + _mod_part(mod_ref, 2, d) * _mm(u_scr[...], wo_ref[...])
    o_ref[0] = _layernorm(z, g_ln_ref[...], b_ln_ref[...])


def _rw_out(y_f, y_b, r, k, v, g, a_s, p, x, mod, per_batch, ln_g, ln_b, alpha):
    bsz, t, d = x.shape
    tm = min(256, t)
    tok = pl.BlockSpec((1, tm, d), lambda b, i: (b, i, 0))
    dtok = pl.BlockSpec((2, 1, tm, d), lambda b, i: (0, b, i, 0))
    vec = pl.BlockSpec((1, d), lambda b, i: (0, 0))
    return pl.pallas_call(
        functools.partial(_rw_out_kernel, d=d, alpha=alpha),
        grid=(bsz, t // tm),
        in_specs=[tok, tok, tok, tok, tok, tok, dtok, vec, vec, vec, vec, pl.BlockSpec((d, d), lambda b, i: (0, 0)),
                  tok, _mod_spec(d, per_batch), vec, vec],
        out_specs=tok,
        out_shape=jax.ShapeDtypeStruct((bsz, t, d), F32),
        scratch_shapes=[pltpu.VMEM((tm, d), MXU_DTYPE)],
        compiler_params=_cparams("parallel", "parallel"),
        name="rw_out",
    )(y_f, y_b, r, k, v, g, a_s, p['k_a'], p['r_k'], p['lnx_g'], p['lnx_b'], p['wo'], x, mod, ln_g, ln_b)


def _rw_layer(x, xc, mod, mod_c, p, want_ctx, ln_g, ln_b, alpha):
    bsz, _, d = x.shape
    rc, kc, vc, gc, lwc, asc = _rw_proj(xc, mod_c, False, p)
    r, k, v, g, lw, a_s = _rw_proj(x, mod, True, p)
    z0 = jnp.zeros((2, bsz, d // LANES, LANES, LANES), F32)
    scan = lambda args, z, rev: _rw_scan(*args, p['k_k'], p['k_a'], z, rev)
    yc_f, zc_f = scan((rc, kc, vc, lwc, asc), z0, False)
    yc_b, zc_b = scan((rc, kc, vc, lwc, asc), z0, True)
    zc = jnp.stack([zc_f, zc_b])
    y_f, _ = scan((r, k, v, lw, a_s), zc, False)
    y_b, _ = scan((r, k, v, lw, a_s), zc, True)
    x_new = _rw_out(y_f, y_b, r, k, v, g, a_s, p, x, mod, True, ln_g, ln_b, alpha)
    xc_new = (_rw_out(yc_f, yc_b, rc, kc, vc, gc, asc, p, xc, mod_c, False, ln_g, ln_b, alpha)
              if want_ctx else None)
    return x_new, xc_new


def kernel(x, c, ctx, c_ctx, mod_w, mod_b, post_ln_g, post_ln_b, mlp_w1, mlp_w2, att_wqkv, att_wo, att_q_norm, att_k_norm, na_wqkv, na_wo, na_rpb, rw_mu, rw_wr, rw_wk, rw_wv, rw_wo, rw_w0, rw_w1, rw_w2, rw_a0, rw_a1, rw_a2, rw_g1, rw_g2, rw_k_k, rw_k_a, rw_r_k, rw_lnx_g, rw_lnx_b):
    depth = mod_w.shape[0]
    bsz, _, d = x.shape
    alpha = (2.0 * depth) ** 0.25
    cast = lambda a: a.astype(MXU_DTYPE)

    cond_rows = -(-(bsz + 1) // 8) * 8
    cond = jnp.zeros((cond_rows, d), F32).at[:bsz].set(c).at[bsz].set(c_ctx)
    mods = _modulation(cond, mod_w, mod_b)

    xc = ctx
    for i in range(depth):
        kind, slot = i % N_MIXERS, i // N_MIXERS
        want_ctx = i < depth - 1
        mod = mods[i, :bsz].reshape(bsz, 1, N_MOD * d)
        mod_c = mods[i, bsz:bsz + 1].reshape(1, 1, N_MOD * d)
        g1, b1 = post_ln_g[i, 0:1], post_ln_b[i, 0:1]
        g2, b2 = post_ln_g[i, 1:2], post_ln_b[i, 1:2]
        if kind == 2:
            cat = lambda a: jnp.concatenate([a[0], a[1]], axis=-1)
            p = {'mu': rw_mu[slot], 'wr': cast(rw_wr[slot]), 'wk': cast(rw_wk[slot]), 'wv': cast(rw_wv[slot]),
                 'wo': cast(rw_wo[slot]), 'w0': rw_w0[slot], 'a0': rw_a0[slot],
                 'w1': cast(cat(rw_w1[slot])), 'a1': cast(cat(rw_a1[slot])),
                 'w2': cast(rw_w2[slot].reshape(-1, d)), 'a2': cast(rw_a2[slot].reshape(-1, d)),
                 'g1': cast(rw_g1[slot]), 'g2': cast(rw_g2[slot]),
                 'k_k': rw_k_k[slot].reshape(1, d), 'k_a': rw_k_a[slot].reshape(1, d),
                 'r_k': rw_r_k[slot].reshape(1, d), 'lnx_g': rw_lnx_g[slot].reshape(1, d),
                 'lnx_b': rw_lnx_b[slot].reshape(1, d)}
            x, xc_new = _rw_layer(x, xc, mod, mod_c, p, want_ctx, g1, b1, alpha)
        else:
            if kind == 0:
                o, oc, wo = _gqa_layer(x, xc, mod, mod_c, cast(att_wqkv[slot]), cast(att_wo[slot]),
                                       att_q_norm[slot], att_k_norm[slot], want_ctx)
            else:
                o, oc, wo = _na_layer(x, xc, mod, mod_c, cast(na_wqkv[slot]), cast(na_wo[slot]), na_rpb[slot],
                                      want_ctx)
            x = _out_ln(o, wo, x, mod, True, g1, b1, alpha)
            xc_new = _out_ln(oc, wo, xc, mod_c, False, g1, b1, alpha) if want_ctx else None
        w1, w2 = cast(mlp_w1[i]), cast(mlp_w2[i])
        x = _mlp(x, mod, True, w1, w2, g2, b2, alpha)
        if want_ctx:
            xc = _mlp(xc_new, mod_c, False, w1, w2, g2, b2, alpha)
    return x
```

```python
import functools

import numpy as np
import jax
import jax.numpy as jnp
from jax import lax
from jax.experimental import pallas as pl
from jax.experimental.pallas import tpu as pltpu

F32 = jnp.float32
MXU_DTYPE = jnp.bfloat16

GRID_W = 64
N_MOD = 6
N_MIXERS = 3
LN_EPS = 1e-6
ATT_HEAD_DIM = 128
ATT_KV_HEADS = 2
ROPE_BASE = 10000.0
GQA_KEY_CHUNK = 512
NA_HEAD_DIM = 64
NA_WIN_H = 8
NA_WIN_W = 16
NA_Q_ROWS = 32
RW_HEAD = 64
RW_GN_EPS = 64e-5
RW_CHUNK = 64
RW_STAGING_LAG = 2
LANES = 128
NEG_BIG = -1e30
LOG2E = 1.4426950408889634

VMEM_LIMIT = 56 * 1024 * 1024


def _cparams(*sem):
    return pltpu.CompilerParams(dimension_semantics=sem, vmem_limit_bytes=VMEM_LIMIT)


def _mm(a, b):
    return jnp.dot(a.astype(MXU_DTYPE), b.astype(MXU_DTYPE), preferred_element_type=F32)


def _mm_nt(a, b):
    return lax.dot_general(a.astype(MXU_DTYPE), b.astype(MXU_DTYPE), (((1,), (1,)), ((), ())),
                           preferred_element_type=F32)


def _mm_tn(a, b):
    return lax.dot_general(a.astype(MXU_DTYPE), b.astype(MXU_DTYPE), (((0,), (0,)), ((), ())),
                           preferred_element_type=F32)


def _split(a, n):
    parts = []
    for _ in range(n - 1):
        p = a.astype(MXU_DTYPE)
        parts.append(p)
        a = a - p.astype(F32)
    parts.append(a.astype(MXU_DTYPE))
    return parts


def _mm_lhs_exact(m01, a, n=3):
    m01 = m01.astype(MXU_DTYPE)
    return sum(jnp.dot(m01, p, preferred_element_type=F32) for p in _split(a, n))


def _mm_rhs_exact(a, m01, n=2):
    m01 = m01.astype(MXU_DTYPE)
    return sum(jnp.dot(p, m01, preferred_element_type=F32) for p in _split(a, n))


def _mm_hi(a, b):
    a1, a2 = _split(a, 2)
    b1, b2 = _split(b, 2)
    d = functools.partial(jnp.dot, preferred_element_type=F32)
    return d(a1, b1) + d(a1, b2) + d(a2, b1)


def _sigmoid(x):
    return 1.0 / (1.0 + jnp.exp(-x))


def _layernorm(z, g, b):
    mu = jnp.mean(z, -1, keepdims=True)
    dz = z - mu
    var = jnp.mean(dz * dz, -1, keepdims=True)
    return dz * lax.rsqrt(var + LN_EPS) * g + b


def _mod_part(mod_ref, j, d):
    return mod_ref[0, :, j * d:(j + 1) * d]


def _head_ones(n=LANES):
    r = lax.broadcasted_iota(jnp.int32, (n, n), 0)
    c = lax.broadcasted_iota(jnp.int32, (n, n), 1)
    return ((r // RW_HEAD) == (c // RW_HEAD)).astype(F32)


def _modulation_kernel(c_ref, w_ref, b_ref, o_ref):
    c = c_ref[...]
    o_ref[0] = _mm_hi(c * _sigmoid(c), w_ref[0]) + b_ref[0]


def _modulation(cond, mod_w, mod_b):
    depth, d, n = mod_w.shape
    rows = cond.shape[0]
    tn = n // 4
    return pl.pallas_call(
        _modulation_kernel,
        grid=(depth, n // tn),
        in_specs=[pl.BlockSpec((rows, d), lambda i, j: (0, 0)),
                  pl.BlockSpec((1, d, tn), lambda i, j: (i, 0, j)),
                  pl.BlockSpec((1, 1, tn), lambda i, j: (i, 0, j))],
        out_specs=pl.BlockSpec((1, rows, tn), lambda i, j: (i, 0, j)),
        out_shape=jax.ShapeDtypeStruct((depth, rows, n), F32),
        compiler_params=_cparams("parallel", "parallel"),
        name="modulation",
    )(cond, mod_w, mod_b.reshape(depth, 1, n))


def _out_ln_kernel(o_ref, w_ref, x_ref, mod_ref, g_ref, b_ref, y_ref, *, d, alpha):
    y = _mm(o_ref[0], w_ref[...])
    z = alpha * x_ref[0] + _mod_part(mod_ref, 2, d) * y
    y_ref[0] = _layernorm(z, g_ref[...], b_ref[...])


def _mod_spec(d, per_batch):
    if per_batch:
        return pl.BlockSpec((1, 1, N_MOD * d), lambda b, *_: (b, 0, 0))
    return pl.BlockSpec((1, 1, N_MOD * d), lambda b, *_: (0, 0, 0))


def _out_ln(o, wo, x, mod, per_batch, ln_g, ln_b, alpha):
    bsz, t, d = x.shape
    tm = min(512, t)
    tok = lambda b, i: (b, i, 0)
    const = lambda b, i: (0, 0)
    return pl.pallas_call(
        functools.partial(_out_ln_kernel, d=d, alpha=alpha),
        grid=(bsz, t // tm),
        in_specs=[pl.BlockSpec((1, tm, d), tok), pl.BlockSpec((d, d), const), pl.BlockSpec((1, tm, d), tok),
                  _mod_spec(d, per_batch), pl.BlockSpec((1, d), const), pl.BlockSpec((1, d), const)],
        out_specs=pl.BlockSpec((1, tm, d), tok),
        out_shape=jax.ShapeDtypeStruct((bsz, t, d), F32),
        compiler_params=_cparams("parallel", "parallel"),
        name="out_ln",
    )(o, wo, x, mod, ln_g, ln_b)


def _mlp_kernel(x_ref, mod_ref, w1_ref, w2_ref, g_ref, b_ref, y_ref, h_scr, acc_scr, *, d, alpha):
    j = pl.program_id(2)

    @pl.when(j == 0)
    def _():
        h = x_ref[0] * (1.0 + _mod_part(mod_ref, 4, d)) + _mod_part(mod_ref, 3, d)
        h_scr[...] = h.astype(h_scr.dtype)
        acc_scr[...] = jnp.zeros_like(acc_scr)

    a = jnp.square(jnp.maximum(_mm(h_scr[...], w1_ref[...]), 0.0))
    acc_scr[...] += _mm(a, w2_ref[...])

    @pl.when(j == pl.num_programs(2) - 1)
    def _():
        z = alpha * x_ref[0] + _mod_part(mod_ref, 5, d) * acc_scr[...]
        y_ref[0] = _layernorm(z, g_ref[...], b_ref[...])


def _mlp(x, mod, per_batch, w1, w2, ln_g, ln_b, alpha):
    bsz, t, d = x.shape
    ff = w1.shape[1]
    tm = min(1024, t)
    tf = min(2048, ff)
    tok = lambda b, i, j: (b, i, 0)
    const = lambda b, i, j: (0, 0)
    return pl.pallas_call(
        functools.partial(_mlp_kernel, d=d, alpha=alpha),
        grid=(bsz, t // tm, ff // tf),
        in_specs=[pl.BlockSpec((1, tm, d), tok), _mod_spec(d, per_batch),
                  pl.BlockSpec((d, tf), lambda b, i, j: (0, j)), pl.BlockSpec((tf, d), lambda b, i, j: (j, 0)),
                  pl.BlockSpec((1, d), const), pl.BlockSpec((1, d), const)],
        out_specs=pl.BlockSpec((1, tm, d), tok),
        out_shape=jax.ShapeDtypeStruct((bsz, t, d), F32),
        scratch_shapes=[pltpu.VMEM((tm, d), MXU_DTYPE), pltpu.VMEM((tm, d), F32)],
        compiler_params=_cparams("parallel", "parallel", "arbitrary"),
        name="mlp",
    )(x, mod, w1, w2, ln_g, ln_b)


def _gqa_qkv_kernel(x_ref, mod_ref, w_ref, gq_ref, gk_ref, cos_ref, sin_ref, q_ref, k_ref, v_ref,
                    *, d, n_q, n_kv):
    h = x_ref[0] * (1.0 + _mod_part(mod_ref, 1, d)) + _mod_part(mod_ref, 0, d)
    acc = _mm(h, w_ref[...])
    cos, sin = cos_ref[...], sin_ref[...]
    hd = ATT_HEAD_DIM
    src = lax.broadcasted_iota(jnp.int32, (hd, hd), 0)
    dst = lax.broadcasted_iota(jnp.int32, (hd, hd), 1)
    first = (dst & (hd // 2 - 1)) < hd // 4
    rot = (jnp.where(first & (src == dst + hd // 4), -1.0, 0.0)
           + jnp.where(jnp.logical_not(first) & (src == dst - hd // 4), 1.0, 0.0))

    def norm_rope(u, g):
        u = u * lax.rsqrt(jnp.mean(u * u, -1, keepdims=True) + LN_EPS) * g
        return u * cos + _mm(u, rot) * sin

    for i in range(n_q):
        q_ref[0, :, i * hd:(i + 1) * hd] = norm_rope(acc[:, i * hd:(i + 1) * hd], gq_ref[...]).astype(q_ref.dtype)
    for i in range(n_kv):
        c0 = (n_q + i) * hd
        k_ref[0, :, i * hd:(i + 1) * hd] = norm_rope(acc[:, c0:c0 + hd], gk_ref[...]).astype(k_ref.dtype)
    ones = jnp.ones((acc.shape[0], hd), v_ref.dtype)
    for i in range(n_kv):
        c0 = (n_q + n_kv + i) * hd
        v_ref[0, :, 2 * i * hd:(2 * i + 1) * hd] = acc[:, c0:c0 + hd].astype(v_ref.dtype)
        v_ref[0, :, (2 * i + 1) * hd:(2 * i + 2) * hd] = ones


def _gqa_qkv(x, mod, per_batch, w, gq, gk, cos, sin):
    bsz, t, d = x.shape
    hd = ATT_HEAD_DIM
    n_kv = ATT_KV_HEADS
    n_q = w.shape[1] // hd - 2 * n_kv
    tm = min(512, t)
    tok = lambda b, i: (b, i, 0)
    const = lambda b, i: (0, 0)
    tab = pl.BlockSpec((tm, hd), lambda b, i: (i, 0))
    return pl.pallas_call(
        functools.partial(_gqa_qkv_kernel, d=d, n_q=n_q, n_kv=n_kv),
        grid=(bsz, t // tm),
        in_specs=[pl.BlockSpec((1, tm, d), tok), _mod_spec(d, per_batch), pl.BlockSpec(w.shape, const),
                  pl.BlockSpec((1, hd), const), pl.BlockSpec((1, hd), const), tab, tab],
        out_specs=[pl.BlockSpec((1, tm, n_q * hd), tok), pl.BlockSpec((1, tm, n_kv * hd), tok),
                   pl.BlockSpec((1, tm, 2 * n_kv * hd), tok)],
        out_shape=[jax.ShapeDtypeStruct((bsz, t, n_q * hd), MXU_DTYPE),
                   jax.ShapeDtypeStruct((bsz, t, n_kv * hd), MXU_DTYPE),
                   jax.ShapeDtypeStruct((bsz, t, 2 * n_kv * hd), MXU_DTYPE)],
        compiler_params=_cparams("parallel", "parallel"),
        name="gqa_qkv",
    )(x, mod, w, gq, gk, cos, sin)


def _gqa_att_kernel(*refs, n_sets, groups, tk):
    q_ref, o_ref = refs[0], refs[-1]
    kv = refs[1:-1]
    hd = ATT_HEAD_DIM
    tq = q_ref.shape[1]
    q = jnp.concatenate([q_ref[0, :, g * hd:(g + 1) * hd] for g in range(groups)], axis=0)
    m = acc = None
    for j in range(n_sets):
        k_ref, v_ref = kv[2 * j], kv[2 * j + 1]
        t = k_ref.shape[1]
        for c0 in range(0, t, tk):
            c1 = min(c0 + tk, t)
            s = _mm_nt(q, k_ref[0, c0:c1, :])
            m_c = jnp.max(s, -1, keepdims=True)
            if m is None:
                m = m_c
                acc = _mm(jnp.exp2(s - m), v_ref[0, c0:c1, :])
            else:
                m_new = jnp.maximum(m, m_c)
                acc = acc * jnp.exp2(m - m_new) + _mm(jnp.exp2(s - m_new), v_ref[0, c0:c1, :])
                m = m_new
    o = acc[:, :hd] / acc[:, hd:]
    for g in range(groups):
        o_ref[0, :, g * hd:(g + 1) * hd] = o[g * tq:(g + 1) * tq].astype(o_ref.dtype)


def _gqa_att(q, kv_sets):
    bsz, s, dq = q.shape
    hd = ATT_HEAD_DIM
    n_kv = ATT_KV_HEADS
    groups = dq // hd // n_kv
    tq = min(512, s)
    in_specs = [pl.BlockSpec((1, tq, groups * hd), lambda b, kh, i: (b, i, kh))]
    args = [q]
    for k, v in kv_sets:
        t = k.shape[1]
        in_specs += [pl.BlockSpec((1, t, hd), lambda b, kh, i: (b, 0, kh)),
                     pl.BlockSpec((1, t, 2 * hd), lambda b, kh, i: (b, 0, kh))]
        args += [k, v]
    return pl.pallas_call(
        functools.partial(_gqa_att_kernel, n_sets=len(kv_sets), groups=groups, tk=GQA_KEY_CHUNK),
        grid=(bsz, n_kv, s // tq),
        in_specs=in_specs,
        out_specs=pl.BlockSpec((1, tq, groups * hd), lambda b, kh, i: (b, i, kh)),
        out_shape=jax.ShapeDtypeStruct((bsz, s, dq), MXU_DTYPE),
        compiler_params=_cparams("parallel", "parallel", "parallel"),
        name="gqa_att",
    )(*args)


def _rope_tables(n_tokens, head_dim):
    t = np.arange(n_tokens)
    row = (t // GRID_W).astype(np.float32)
    col = (t % GRID_W).astype(np.float32)
    half = head_dim // 2
    freqs = jnp.asarray(ROPE_BASE, F32) ** (-jnp.arange(0, half, 2, dtype=F32) / half)
    ang_r = jnp.asarray(row)[:, None] * freqs[None, :]
    ang_c = jnp.asarray(col)[:, None] * freqs[None, :]
    ang = jnp.concatenate([ang_r, ang_r, ang_c, ang_c], axis=-1)
    return jnp.cos(ang), jnp.sin(ang)


def _gqa_layer(x, xc, mod, mod_c, wqkv, wo, q_norm, k_norm, want_ctx):
    s, tc = x.shape[1], xc.shape[1]
    hd = ATT_HEAD_DIM
    gq = (q_norm * (hd ** -0.5 * LOG2E)).reshape(1, hd)
    gk = k_norm.reshape(1, hd)
    cos, sin = _rope_tables(s, hd)
    one, zero = jnp.ones((tc, hd), F32), jnp.zeros((tc, hd), F32)
    q, k, v = _gqa_qkv(x, mod, True, wqkv, gq, gk, cos, sin)
    qc, kc, vc = _gqa_qkv(xc, mod_c, False, wqkv, gq, gk, one, zero)
    o = _gqa_att(q, [(k, v), (kc, vc)])
    oc = _gqa_att(qc, [(kc, vc)]) if want_ctx else None
    return o, oc, wo


def _na_qkv_kernel(x_ref, mod_ref, w_ref, q_ref, k_ref, v_ref, *, d, scale):
    h = x_ref[0] * (1.0 + _mod_part(mod_ref, 1, d)) + _mod_part(mod_ref, 0, d)
    acc = _mm(h, w_ref[...])
    q_ref[0] = (acc[:, :d] * scale).astype(q_ref.dtype)
    k_ref[0] = acc[:, d:2 * d].astype(k_ref.dtype)
    ones = jnp.ones((acc.shape[0], LANES), v_ref.dtype)
    for p in range(d // LANES):
        c0 = 2 * d + p * LANES
        v_ref[0, :, 2 * p * LANES:(2 * p + 1) * LANES] = acc[:, c0:c0 + LANES].astype(v_ref.dtype)
        v_ref[0, :, (2 * p + 1) * LANES:(2 * p + 2) * LANES] = ones


def _na_qkv(x, mod, per_batch, w):
    bsz, t, d = x.shape
    tm = min(512, t)
    tok = lambda b, i: (b, i, 0)
    out = jax.ShapeDtypeStruct((bsz, t, d), MXU_DTYPE)
    return pl.pallas_call(
        functools.partial(_na_qkv_kernel, d=d, scale=NA_HEAD_DIM ** -0.5 * LOG2E),
        grid=(bsz, t // tm),
        in_specs=[pl.BlockSpec((1, tm, d), tok), _mod_spec(d, per_batch), pl.BlockSpec(w.shape, lambda b, i: (0, 0))],
        out_specs=[pl.BlockSpec((1, tm, d), tok)] * 2 + [pl.BlockSpec((1, tm, 2 * d), tok)],
        out_shape=[out, out, jax.ShapeDtypeStruct((bsz, t, 2 * d), MXU_DTYPE)],
        compiler_params=_cparams("parallel", "parallel"),
        name="na_qkv",
    )(x, mod, w)


def _pair_softmax_att(q, ks, vs, biases):
    lane = lax.broadcasted_iota(jnp.int32, q.shape, 1)
    first = lane < NA_HEAD_DIM
    zero = jnp.zeros_like(q)
    qm = [jnp.where(first, q, zero), jnp.where(first, zero, q)]
    m, acc = [None, None], [None, None]
    for j, (k, v) in enumerate(zip(ks, vs)):
        for h in range(2):
            s = _mm_nt(qm[h], k)
            if biases[h][j] is not None:
                s = s + biases[h][j]
            m_c = jnp.max(s, -1, keepdims=True)
            if m[h] is None:
                m[h] = m_c
                acc[h] = _mm(jnp.exp2(s - m_c), v)
            else:
                m_new = jnp.maximum(m[h], m_c)
                acc[h] = acc[h] * jnp.exp2(m[h] - m_new) + _mm(jnp.exp2(s - m_new), v)
                m[h] = m_new
    outs = [a[:, :LANES] / a[:, LANES:] for a in acc]
    return jnp.where(first, outs[0], outs[1])


def _na_att_kernel(q_ref, k_ref, v_ref, kc_ref, vc_ref, bias_ref, o_ref, *, rows):
    i = pl.program_id(2)
    q_rows = q_ref.shape[1] // GRID_W
    win = NA_WIN_H * GRID_W
    w = GRID_W
    first = lax.broadcasted_iota(jnp.int32, (w, LANES), 1) < NA_HEAD_DIM
    qm, kw, vw, bias = [], [], [], []
    for rq in range(q_rows):
        row_q = i * q_rows + rq
        r0 = jnp.clip(row_q - NA_WIN_H // 2, 0, rows - NA_WIN_H)
        start = pl.multiple_of(r0 * w, w)
        q = q_ref[0, rq * w:(rq + 1) * w, :]
        zero = jnp.zeros_like(q)
        qm.append(jnp.concatenate([jnp.where(first, q, zero), jnp.where(first, zero, q)], axis=0))
        kw.append(k_ref[0, pl.ds(start, win), :])
        vw.append(v_ref[0, pl.ds(start, win), :])
        e = r0 - row_q + NA_WIN_H - 1
        bias.append(jnp.concatenate([bias_ref[0, 0, e], bias_ref[0, 1, e]], axis=0))
    each = lambda fn, *lists: [fn(*args) for args in zip(*lists)]
    s_ctx = _mm_nt(jnp.concatenate(qm, axis=0), kc_ref[0])
    s_ctx = [s_ctx[2 * w * rq:2 * w * (rq + 1)] for rq in range(q_rows)]
    s_loc = each(lambda a, b, c: _mm_nt(a, b) + c, qm, kw, bias)
    m = each(lambda a, b: jnp.maximum(jnp.max(a, -1, keepdims=True), jnp.max(b, -1, keepdims=True)), s_loc, s_ctx)
    acc_ctx = _mm(jnp.concatenate(each(lambda a, mm: jnp.exp2(a - mm), s_ctx, m), axis=0), vc_ref[0])
    acc = each(lambda a, mm, v: _mm(jnp.exp2(a - mm), v), s_loc, m, vw)
    for rq in range(q_rows):
        a = acc[rq] + acc_ctx[2 * w * rq:2 * w * (rq + 1)]
        o = a[:, :LANES] / a[:, LANES:]
        o_ref[0, rq * w:(rq + 1) * w, :] = jnp.where(first, o[:w], o[w:]).astype(o_ref.dtype)


def _na_bias_tables(rpb, scale):
    pad = GRID_W
    rpb_p = jnp.pad(rpb, ((0, 0), (0, 0), (pad, pad)))
    cmat = jnp.stack([rpb_p[:, :, pad + NA_WIN_W - 1 - cq: pad + NA_WIN_W - 1 - cq + GRID_W]
                      for cq in range(GRID_W)], axis=2)
    cq, ck = np.arange(GRID_W)[:, None], np.arange(GRID_W)[None, :]
    c0 = np.clip(cq - NA_WIN_W // 2, 0, GRID_W - NA_WIN_W)
    vcol = (ck >= c0) & (ck < c0 + NA_WIN_W)
    cmat = jnp.where(vcol[None, None], cmat * scale, NEG_BIG)
    return jnp.stack([jnp.concatenate([cmat[:, e + j] for j in range(NA_WIN_H)], axis=-1)
                      for e in range(NA_WIN_H)], axis=1)


def _na_att(q, k, v, kc, vc, bias):
    bsz, s, d = q.shape
    tc = kc.shape[1]
    rows = s // GRID_W
    q_rows = min(NA_Q_ROWS, rows)
    assert rows % q_rows == 0 and rows >= NA_WIN_H
    tq = q_rows * GRID_W
    pairs = d // LANES
    q_spec = pl.BlockSpec((1, tq, LANES), lambda hp, b, i: (b, i, hp))
    whole = lambda t, width: pl.BlockSpec((1, t, width), lambda hp, b, i: (b, 0, hp))
    bias_spec = pl.BlockSpec((1, 2) + bias.shape[1:], lambda hp, b, i: (0, hp, 0, 0, 0))
    return pl.pallas_call(
        functools.partial(_na_att_kernel, rows=rows),
        grid=(pairs, bsz, rows // q_rows),
        in_specs=[q_spec, whole(s, LANES), whole(s, 2 * LANES), whole(tc, LANES), whole(tc, 2 * LANES), bias_spec],
        out_specs=q_spec,
        out_shape=jax.ShapeDtypeStruct((bsz, s, d), MXU_DTYPE),
        compiler_params=_cparams("parallel", "parallel", "parallel"),
        name="na_att",
    )(q, k, v, kc, vc, bias[None])


def _pair_att_kernel(q_ref, k_ref, v_ref, o_ref):
    o_ref[0] = _pair_softmax_att(q_ref[0], [k_ref[0]], [v_ref[0]], [[None], [None]]).astype(o_ref.dtype)


def _pair_att(q, k, v):
    bsz, t, d = q.shape
    spec = pl.BlockSpec((1, t, LANES), lambda b, hp: (b, 0, hp))
    return pl.pallas_call(
        _pair_att_kernel,
        grid=(bsz, d // LANES),
        in_specs=[spec, spec, pl.BlockSpec((1, t, 2 * LANES), lambda b, hp: (b, 0, hp))],
        out_specs=spec,
        out_shape=jax.ShapeDtypeStruct((bsz, t, d), MXU_DTYPE),
        compiler_params=_cparams("parallel", "parallel"),
        name="na_ctx_att",
    )(q, k, v)


def _na_layer(x, xc, mod, mod_c, wqkv, wo, rpb, want_ctx):
    q, k, v = _na_qkv(x, mod, True, wqkv)
    qc, kc, vc = _na_qkv(xc, mod_c, False, wqkv)
    o = _na_att(q, k, v, kc, vc, _na_bias_tables(rpb, LOG2E))
    oc = _pair_att(qc, kc, vc) if want_ctx else None
    return o, oc, wo


def _rw_proj_kernel(x_ref, xp_ref, xn_ref, mod_ref, mu_ref, wr_ref, wk_ref, wv_ref, w1_ref, w2_ref, a1_ref, a2_ref,
                    g1_ref, g2_ref, w0_ref, a0_ref, r_ref, k_ref, v_ref, g_ref, lw_ref, as_ref, *, d):
    t = pl.program_id(1)
    nt = pl.num_programs(1)
    scale = 1.0 + _mod_part(mod_ref, 1, d)
    shift = _mod_part(mod_ref, 0, d)
    h = x_ref[0] * scale + shift
    tm = h.shape[0]
    h_prev = (xp_ref[0, 7:8, :] * scale + shift) * (t > 0).astype(F32)
    h_next = (xn_ref[0, 0:1, :] * scale + shift) * (t < nt - 1).astype(F32)
    row = lax.broadcasted_iota(jnp.int32, h.shape, 0)
    prev = jnp.where(row == 0, h_prev, pltpu.roll(h, 1, 0))
    nxt = jnp.where(row == tm - 1, h_next, pltpu.roll(h, tm - 1, 0))
    xx = 0.5 * (prev + nxt) - h
    mix = lambda j: h + xx * mu_ref[j:j + 1, :]

    r_ref[0] = _mm(mix(0), wr_ref[...]).astype(r_ref.dtype)
    k_ref[0] = _mm(mix(2), wk_ref[...]).astype(k_ref.dtype)
    v_ref[0] = _mm(mix(3), wv_ref[...]).astype(v_ref.dtype)
    g_ref[0] = _mm(_sigmoid(_mm(mix(5), g1_ref[...])), g2_ref[...]).astype(g_ref.dtype)

    tw = jnp.tanh(_mm(mix(1), w1_ref[...]))
    al = _mm(mix(4), a1_ref[...])
    first = lax.broadcasted_iota(jnp.int32, tw.shape, 1) < tw.shape[1] // 2
    zero = jnp.zeros_like(tw)
    for n in range(2):
        pick = lambda u: jnp.where(first, u, zero) if n == 0 else jnp.where(first, zero, u)
        z = -(w0_ref[n:n + 1, :] + _mm(pick(tw), w2_ref[...]))
        softplus = jnp.maximum(z, 0.0) + jnp.log(1.0 + jnp.exp(-jnp.abs(z)))
        lw_ref[n, 0] = -jnp.exp(-softplus - 0.5)
        as_ref[n, 0] = _sigmoid(a0_ref[n:n + 1, :] + _mm(pick(al), a2_ref[...])).astype(as_ref.dtype)


def _rw_proj(x, mod, per_batch, p):
    bsz, t, d = x.shape
    tm = min(512, t)
    n8 = t // 8
    tok = lambda b, i: (b, i, 0)
    dtok = lambda b, i: (0, b, i, 0)
    full = lambda a: pl.BlockSpec(a.shape, lambda b, i: (0,) * a.ndim)
    halo_prev = pl.BlockSpec((1, 8, d), lambda b, i: (b, jnp.maximum(i * (tm // 8) - 1, 0), 0))
    halo_next = pl.BlockSpec((1, 8, d), lambda b, i: (b, jnp.minimum((i + 1) * (tm // 8), n8 - 1), 0))
    weights = [p['mu'], p['wr'], p['wk'], p['wv'], p['w1'], p['w2'], p['a1'], p['a2'], p['g1'], p['g2'],
               p['w0'], p['a0']]
    one = jax.ShapeDtypeStruct((bsz, t, d), MXU_DTYPE)
    two = lambda dtype: jax.ShapeDtypeStruct((2, bsz, t, d), dtype)
    return pl.pallas_call(
        functools.partial(_rw_proj_kernel, d=d),
        grid=(bsz, t // tm),
        in_specs=[pl.BlockSpec((1, tm, d), tok), halo_prev, halo_next, _mod_spec(d, per_batch)]
                 + [full(a) for a in weights],
        out_specs=[pl.BlockSpec((1, tm, d), tok)] * 4 + [pl.BlockSpec((2, 1, tm, d), dtok)] * 2,
        out_shape=[one, one, one, one, two(F32), two(MXU_DTYPE)],
        compiler_params=_cparams("parallel", "parallel"),
        name="rw_proj",
    )(x, x, x, mod, *weights)


def _rw_scan_kernel(r_ref, k_ref, v_ref, lw_ref, as_ref, kk_ref, ka_ref, z0_ref, y_ref, zf_ref, z_scr, pre_scr,
                    eg_scr, *, pairs, sub, reverse):
    c = pl.program_id(1)
    L = RW_CHUNK
    H = RW_HEAD
    assert 2 * L == LANES and 2 * H == LANES

    @pl.when(c == 0)
    def _():
        z_scr[...] = z0_ref[0, 0]
        pre_scr[...] = jnp.zeros_like(pre_scr)
        eg_scr[...] = jnp.zeros_like(eg_scr)

    sign = -1 if reverse else 1
    t_i = lax.broadcasted_iota(jnp.int32, (L, 2 * L), 0)
    s_i = lax.broadcasted_iota(jnp.int32, (L, 2 * L), 1) & (L - 1)
    dt = sign * (s_i - t_i)
    strict_c = dt < 0
    incl_c = dt <= 0
    tri2 = incl_c.astype(F32)
    eye_c = (dt == 0).astype(F32)
    r2 = lax.broadcasted_iota(jnp.int32, (LANES, LANES), 0)
    c2 = lax.broadcasted_iota(jnp.int32, (LANES, LANES), 1)
    eye2 = (r2 == c2).astype(F32)
    same_head = _head_ones()
    first = lax.broadcasted_iota(jnp.int32, (L, LANES), 1) < H
    zero = jnp.zeros((L, LANES), F32)

    head0 = lambda u: jnp.where(first, u, zero)
    head1 = lambda u: jnp.where(first, zero, u)
    bd = lambda u: jnp.concatenate([head0(u), head1(u)], axis=0)

    each = lambda fn, *lists: [fn(*args) for args in zip(*lists)]
    sls = [slice(p * LANES, (p + 1) * LANES) for p in range(pairs)]
    cat0 = lambda *u: jnp.concatenate(u, axis=0)
    cat1 = lambda *u: jnp.concatenate(u, axis=1)

    mm_each = lambda lhs, w: [_mm(a, b) for a, b in zip(lhs, w)]

    def stage_operands(u, slot):
        rw = slice(u * L, (u + 1) * L)

        def put(j, arrs):
            for sl, arr in zip(sls, arrs):
                pre_scr[slot, j, rw, sl] = arr

        lw = [lw_ref[0, 0, rw, sl] for sl in sls]
        c_in = each(lambda u_: _mm(tri2, cat0(*_split(u_, 2))), lw)
        c_all = each(lambda u_: jnp.sum(u_, axis=0, keepdims=True), lw)
        for sl, ca in zip(sls, c_all):
            eg_scr[slot, u * 8:u * 8 + 1, sl] = jnp.exp(ca)
        put(6, [v_ref[0, rw, sl].astype(F32) for sl in sls])
        yield None
        k = [k_ref[0, rw, sl].astype(F32) for sl in sls]
        kk = each(lambda u_, sl: u_ * kk_ref[:, sl], k, sls)
        ss = each(lambda u_: _mm(u_ * u_, same_head), kk)
        yield None
        put(1, each(lambda sl, ci: r_ref[0, rw, sl].astype(F32) * jnp.exp(ci), sls, c_in))
        yield None
        kk = each(lambda u_, s: u_ * lax.rsqrt(jnp.maximum(s, 1e-12)), kk, ss)
        put(0, each(lambda u_, ci, l: -u_ * jnp.exp(ci - l), kk, c_in, lw))
        yield None
        a_s = [as_ref[0, 0, rw, sl].astype(F32) for sl in sls]
        b_v = each(lambda u_, a: u_ * a, kk, a_s)
        k_d = each(lambda u_, a, sl: u_ * (1.0 + (a - 1.0) * ka_ref[:, sl]), k, a_s, sls)
        yield None
        e_neg = each(lambda ci: jnp.exp(-ci), c_in)
        put(2, each(lambda u_, e: u_ * e, b_v, e_neg))
        put(3, each(lambda u_, e: u_ * e, k_d, e_neg))
        yield None
        e_rem = each(lambda ci, ca: jnp.exp(ca - ci), c_in, c_all)
        put(4, each(lambda u_, e: u_ * e, b_v, e_rem))
        put(5, each(lambda u_, e: u_ * e, k_d, e_rem))
        yield True

    def chunk_terms(u, slot):
        rw = slice(u * L, (u + 1) * L)
        a_t, r_t, b_t, k_t, b_h, k_h, v = [[pre_scr[slot, j, rw, sl] for sl in sls] for j in range(7)]
        e_all = [eg_scr[slot, u * 8:u * 8 + 1, sl] for sl in sls]
        yield None

        x = each(lambda a, rr, b, kt: _mm_nt(cat0(a, rr), cat0(bd(b), bd(kt))), a_t, r_t, b_t, k_t)
        m_ab = each(lambda u: jnp.where(strict_c, u[:L, :2 * L], 0.0), x)
        m_ak = each(lambda u: jnp.where(strict_c, u[:L, 2 * L:], 0.0), x)
        n_rb = each(lambda u: jnp.where(incl_c, u[L:, :2 * L], 0.0), x)
        n_rk = each(lambda u: jnp.where(incl_c, u[L:, 2 * L:], 0.0), x)

        inv = each(lambda m: eye_c + m, m_ab)
        yield None
        pw = mm_each(m_ab, each(bd, m_ab))
        for _ in range(int(np.log2(L)) - 2):
            yield None
            st = mm_each(each(cat0, pw, inv), each(bd, pw))
            pw = each(lambda s: s[:L], st)
            inv = each(lambda b, s: b + s[L:], inv, st)
        yield None
        inv = each(lambda b, s: b + s, inv, mm_each(inv, each(bd, pw)))
        mv = mm_each(each(cat0, m_ak, n_rk), each(bd, v))
        yield None
        tw = each(lambda i, a, m: _mm(i, cat1(bd(a), bd(m[:L]))), inv, a_t, mv)
        p1 = each(lambda u: u[:, :LANES], tw)
        p2 = each(lambda u: u[:, LANES:], tw)
        yield None
        nw = each(lambda m, a, b: _mm(m, cat1(bd(a), bd(b))), n_rb, p1, p2)
        q1 = each(lambda rr, u: rr + u[:, :LANES], r_t, nw)
        q2 = each(lambda u, m: u[:, LANES:] + m[L:], nw, mv)
        gh = each(lambda b, kh, a, c_, u: _mm_tn(cat0(b, kh), cat0(cat1(a, c_), cat1(zero, u))),
                  b_h, k_h, p1, p2, v)
        g_t = each(lambda ea, u_: eye2 * ea + same_head * u_[:, :LANES], e_all, gh)
        h_t = each(lambda u_: same_head * u_[:, LANES:], gh)
        yield q1, q2, g_t, h_t

    order = list(range(sub - 1, -1, -1) if reverse else range(sub))

    def step(read_slot, write_slot):
        gens = [chunk_terms(u, read_slot) for u in order]
        stagers = [stage_operands(u, write_slot) for u in order]
        terms = [None] * sub
        staged = [None] * sub
        rounds = 0
        while any(t is None for t in terms) or any(s is None for s in staged):
            if any(t is None for t in terms):
                terms = [next(gen) for gen in gens]
            if any(s is None for s in staged) and rounds >= RW_STAGING_LAG:
                staged = [next(gen) for gen in stagers]
            rounds += 1
        live = c > 0
        z = [z_scr[p] for p in range(pairs)]
        for u, (q1, q2, g_t, h_t) in zip(order, terms):
            yz = mm_each(each(cat0, q1, g_t), z)
            for p in range(pairs):
                y_ref[0, u * L:(u + 1) * L, sls[p]] = yz[p][:L] + q2[p]
                z[p] = jnp.where(live, yz[p][L:] + h_t[p], z[p])
        for p in range(pairs):
            z_scr[p] = z[p]

    for parity in range(2):
        pl.when(c % 2 == parity)(functools.partial(step, 1 - parity, parity))

    @pl.when(c == pl.num_programs(1) - 1)
    def _():
        zf_ref[0] = z_scr[...]


def _rw_scan(r, k, v, lw, a_s, kk_w, ka_w, z0, reverse):
    bsz, t, d = r.shape
    sub = min(4, t // RW_CHUNK)
    rows = sub * RW_CHUNK
    n_c = t // rows
    n = int(reverse)
    pairs = d // LANES
    blk = (lambda c: n_c - 1 - c) if reverse else (lambda c: c)
    src = lambda c: blk(jnp.minimum(c, n_c - 1))
    dst = lambda c: blk(jnp.maximum(c - 1, 0))
    tok = pl.BlockSpec((1, rows, d), lambda b, c: (b, src(c), 0))
    dtok = pl.BlockSpec((1, 1, rows, d), lambda b, c: (n, b, src(c), 0))
    vec = pl.BlockSpec((1, d), lambda b, c: (0, 0))
    return pl.pallas_call(
        functools.partial(_rw_scan_kernel, pairs=pairs, sub=sub, reverse=reverse),
        grid=(bsz, n_c + 1),
        in_specs=[tok, tok, tok, dtok, dtok, vec, vec,
                  pl.BlockSpec((1, 1, pairs, LANES, LANES), lambda b, c: (n, b, 0, 0, 0))],
        out_specs=[pl.BlockSpec((1, rows, d), lambda b, c: (b, dst(c), 0)),
                   pl.BlockSpec((1, pairs, LANES, LANES), lambda b, c: (b, 0, 0, 0))],
        out_shape=[jax.ShapeDtypeStruct((bsz, t, d), F32),
                   jax.ShapeDtypeStruct((bsz, pairs, LANES, LANES), F32)],
        scratch_shapes=[pltpu.VMEM((pairs, LANES, LANES), F32),
                        pltpu.VMEM((2, 7, rows, d), F32),
                        pltpu.VMEM((2, 8 * sub, d), F32)],
        compiler_params=_cparams("parallel", "arbitrary"),
        name="rw_scan",
    )(r, k, v, lw, a_s, kk_w, ka_w, z0)


def _rw_out_kernel(yf_ref, yb_ref, r_ref, k_ref, v_ref, g_ref, as_ref, ka_ref, rk_ref, lg_ref, lb_ref, wo_ref, x_ref,
                   mod_ref, g_ln_ref, b_ln_ref, o_ref, u_scr, *, d, alpha):
    width = 2 * LANES
    same_head = _head_ones(width)
    inv_n = 1.0 / RW_HEAD
    for p in range(d // width):
        sl = slice(p * width, (p + 1) * width)
        y = yf_ref[0, :, sl] + yb_ref[0, :, sl]
        mu = _mm_rhs_exact(y, same_head) * inv_n
        dy = y - mu
        var = _mm(dy * dy, same_head) * inv_n
        yn = dy * lax.rsqrt(var + RW_GN_EPS) * lg_ref[:, sl] + lb_ref[:, sl]
        k = k_ref[0, :, sl].astype(F32)
        ka = ka_ref[:, sl]
        k_sum = k * ((1.0 + (as_ref[0, 0, :, sl].astype(F32) - 1.0) * ka)
                     + (1.0 + (as_ref[1, 0, :, sl].astype(F32) - 1.0) * ka))
        bonus = _mm(r_ref[0, :, sl].astype(F32) * k_sum * rk_ref[:, sl], same_head)
        u = (yn + bonus * v_ref[0, :, sl].astype(F32)) * g_ref[0, :, sl].astype(F32)
        u_scr[:, sl] = u.astype(u_scr.dtype)
    z = alpha * x_ref[0] + _mod_part(mod_ref, 2, d) * _mm(u_scr[...], wo_ref[...])
    o_ref[0] = _layernorm(z, g_ln_ref[...], b_ln_ref[...])


def _rw_out(y_f, y_b, r, k, v, g, a_s, p, x, mod, per_batch, ln_g, ln_b, alpha):
    bsz, t, d = x.shape
    tm = min(256, t)
    tok = pl.BlockSpec((1, tm, d), lambda b, i: (b, i, 0))
    dtok = pl.BlockSpec((2, 1, tm, d), lambda b, i: (0, b, i, 0))
    vec = pl.BlockSpec((1, d), lambda b, i: (0, 0))
    return pl.pallas_call(
        functools.partial(_rw_out_kernel, d=d, alpha=alpha),
        grid=(bsz, t // tm),
        in_specs=[tok, tok, tok, tok, tok, tok, dtok, vec, vec, vec, vec, pl.BlockSpec((d, d), lambda b, i: (0, 0)),
                  tok, _mod_spec(d, per_batch), vec, vec],
        out_specs=tok,
        out_shape=jax.ShapeDtypeStruct((bsz, t, d), F32),
        scratch_shapes=[pltpu.VMEM((tm, d), MXU_DTYPE)],
        compiler_params=_cparams("parallel", "parallel"),
        name="rw_out",
    )(y_f, y_b, r, k, v, g, a_s, p['k_a'], p['r_k'], p['lnx_g'], p['lnx_b'], p['wo'], x, mod, ln_g, ln_b)


def _rw_layer(x, xc, mod, mod_c, p, want_ctx, ln_g, ln_b, alpha):
    bsz, _, d = x.shape
    rc, kc, vc, gc, lwc, asc = _rw_proj(xc, mod_c, False, p)
    r, k, v, g, lw, a_s = _rw_proj(x, mod, True, p)
    z0 = jnp.zeros((2, bsz, d // LANES, LANES, LANES), F32)
    scan = lambda args, z, rev: _rw_scan(*args, p['k_k'], p['k_a'], z, rev)
    yc_f, zc_f = scan((rc, kc, vc, lwc, asc), z0, False)
    yc_b, zc_b = scan((rc, kc, vc, lwc, asc), z0, True)
    zc = jnp.stack([zc_f, zc_b])
    y_f, _ = scan((r, k, v, lw, a_s), zc, False)
    y_b, _ = scan((r, k, v, lw, a_s), zc, True)
    x_new = _rw_out(y_f, y_b, r, k, v, g, a_s, p, x, mod, True, ln_g, ln_b, alpha)
    xc_new = (_rw_out(yc_f, yc_b, rc, kc, vc, gc, asc, p, xc, mod_c, False, ln_g, ln_b, alpha)
              if want_ctx else None)
    return x_new, xc_new


def kernel(x, c, ctx, c_ctx, mod_w, mod_b, post_ln_g, post_ln_b, mlp_w1, mlp_w2, att_wqkv, att_wo, att_q_norm, att_k_norm, na_wqkv, na_wo, na_rpb, rw_mu, rw_wr, rw_wk, rw_wv, rw_wo, rw_w0, rw_w1, rw_w2, rw_a0, rw_a1, rw_a2, rw_g1, rw_g2, rw_k_k, rw_k_a, rw_r_k, rw_lnx_g, rw_lnx_b):
    depth = mod_w.shape[0]
    bsz, _, d = x.shape
    alpha = (2.0 * depth) ** 0.25
    cast = lambda a: a.astype(MXU_DTYPE)

    cond_rows = -(-(bsz + 1) // 8) * 8
    cond = jnp.zeros((cond_rows, d), F32).at[:bsz].set(c).at[bsz].set(c_ctx)
    mods = _modulation(cond, mod_w, mod_b)

    xc = ctx
    for i in range(depth):
        kind, slot = i % N_MIXERS, i // N_MIXERS
        want_ctx = i < depth - 1
        mod = mods[i, :bsz].reshape(bsz, 1, N_MOD * d)
        mod_c = mods[i, bsz:bsz + 1].reshape(1, 1, N_MOD * d)
        g1, b1 = post_ln_g[i, 0:1], post_ln_b[i, 0:1]
        g2, b2 = post_ln_g[i, 1:2], post_ln_b[i, 1:2]
        if kind == 2:
            cat = lambda a: jnp.concatenate([a[0], a[1]], axis=-1)
            p = {'mu': rw_mu[slot], 'wr': cast(rw_wr[slot]), 'wk': cast(rw_wk[slot]), 'wv': cast(rw_wv[slot]),
                 'wo': cast(rw_wo[slot]), 'w0': rw_w0[slot], 'a0': rw_a0[slot],
                 'w1': cast(cat(rw_w1[slot])), 'a1': cast(cat(rw_a1[slot])),
                 'w2': cast(rw_w2[slot].reshape(-1, d)), 'a2': cast(rw_a2[slot].reshape(-1, d)),
                 'g1': cast(rw_g1[slot]), 'g2': cast(rw_g2[slot]),
                 'k_k': rw_k_k[slot].reshape(1, d), 'k_a': rw_k_a[slot].reshape(1, d),
                 'r_k': rw_r_k[slot].reshape(1, d), 'lnx_g': rw_lnx_g[slot].reshape(1, d),
                 'lnx_b': rw_lnx_b[slot].reshape(1, d)}
            x, xc_new = _rw_layer(x, xc, mod, mod_c, p, want_ctx, g1, b1, alpha)
        else:
            if kind == 0:
                o, oc, wo = _gqa_layer(x, xc, mod, mod_c, cast(att_wqkv[slot]), cast(att_wo[slot]),
                                       att_q_norm[slot], att_k_norm[slot], want_ctx)
            else:
                o, oc, wo = _na_layer(x, xc, mod, mod_c, cast(na_wqkv[slot]), cast(na_wo[slot]), na_rpb[slot],
                                      want_ctx)
            x = _out_ln(o, wo, x, mod, True, g1, b1, alpha)
            xc_new = _out_ln(oc, wo, xc, mod_c, False, g1, b1, alpha) if want_ctx else None
        w1, w2 = cast(mlp_w1[i]), cast(mlp_w2[i])
        x = _mlp(x, mod, True, w1, w2, g2, b2, alpha)
        if want_ctx:
            xc = _mlp(xc_new, mod_c, False, w1, w2, g2, b2, alpha)
    return x
```

```python
import functools

import numpy as np
import jax
import jax.numpy as jnp
from jax import lax
from jax.experimental import pallas as pl
from jax.experimental.pallas import tpu as pltpu

F32 = jnp.float32
MXU_DTYPE = jnp.bfloat16

GRID_W = 64
N_MOD = 6
N_MIXERS = 3
LN_EPS = 1e-6
ATT_HEAD_DIM = 128
ATT_KV_HEADS = 2
ROPE_BASE = 10000.0
GQA_KEY_CHUNK = 512
NA_HEAD_DIM = 64
NA_WIN_H = 8
NA_WIN_W = 16
NA_Q_ROWS = 32
RW_HEAD = 64
RW_GN_EPS = 64e-5
RW_CHUNK = 64
LANES = 128
NEG_BIG = -1e30
LOG2E = 1.4426950408889634

VMEM_LIMIT = 56 * 1024 * 1024


def _cparams(*sem):
    return pltpu.CompilerParams(dimension_semantics=sem, vmem_limit_bytes=VMEM_LIMIT)


def _mm(a, b):
    return jnp.dot(a.astype(MXU_DTYPE), b.astype(MXU_DTYPE), preferred_element_type=F32)


def _mm_nt(a, b):
    return lax.dot_general(a.astype(MXU_DTYPE), b.astype(MXU_DTYPE), (((1,), (1,)), ((), ())),
                           preferred_element_type=F32)


def _mm_tn(a, b):
    return lax.dot_general(a.astype(MXU_DTYPE), b.astype(MXU_DTYPE), (((0,), (0,)), ((), ())),
                           preferred_element_type=F32)


def _split(a, n):
    parts = []
    for _ in range(n - 1):
        p = a.astype(MXU_DTYPE)
        parts.append(p)
        a = a - p.astype(F32)
    parts.append(a.astype(MXU_DTYPE))
    return parts


def _mm_lhs_exact(m01, a, n=3):
    m01 = m01.astype(MXU_DTYPE)
    return sum(jnp.dot(m01, p, preferred_element_type=F32) for p in _split(a, n))


def _mm_rhs_exact(a, m01, n=2):
    m01 = m01.astype(MXU_DTYPE)
    return sum(jnp.dot(p, m01, preferred_element_type=F32) for p in _split(a, n))


def _mm_hi(a, b):
    a1, a2 = _split(a, 2)
    b1, b2 = _split(b, 2)
    d = functools.partial(jnp.dot, preferred_element_type=F32)
    return d(a1, b1) + d(a1, b2) + d(a2, b1)


def _sigmoid(x):
    return 1.0 / (1.0 + jnp.exp(-x))


def _layernorm(z, g, b):
    mu = jnp.mean(z, -1, keepdims=True)
    dz = z - mu
    var = jnp.mean(dz * dz, -1, keepdims=True)
    return dz * lax.rsqrt(var + LN_EPS) * g + b


def _mod_part(mod_ref, j, d):
    return mod_ref[0, :, j * d:(j + 1) * d]


def _head_ones(n=LANES):
    r = lax.broadcasted_iota(jnp.int32, (n, n), 0)
    c = lax.broadcasted_iota(jnp.int32, (n, n), 1)
    return ((r // RW_HEAD) == (c // RW_HEAD)).astype(F32)


def _modulation_kernel(c_ref, w_ref, b_ref, o_ref):
    c = c_ref[...]
    o_ref[0] = _mm_hi(c * _sigmoid(c), w_ref[0]) + b_ref[0]


def _modulation(cond, mod_w, mod_b):
    depth, d, n = mod_w.shape
    rows = cond.shape[0]
    tn = n // 4
    return pl.pallas_call(
        _modulation_kernel,
        grid=(depth, n // tn),
        in_specs=[pl.BlockSpec((rows, d), lambda i, j: (0, 0)),
                  pl.BlockSpec((1, d, tn), lambda i, j: (i, 0, j)),
                  pl.BlockSpec((1, 1, tn), lambda i, j: (i, 0, j))],
        out_specs=pl.BlockSpec((1, rows, tn), lambda i, j: (i, 0, j)),
        out_shape=jax.ShapeDtypeStruct((depth, rows, n), F32),
        compiler_params=_cparams("parallel", "parallel"),
        name="modulation",
    )(cond, mod_w, mod_b.reshape(depth, 1, n))


def _out_ln_kernel(o_ref, w_ref, x_ref, mod_ref, g_ref, b_ref, y_ref, *, d, alpha):
    y = _mm(o_ref[0], w_ref[...])
    z = alpha * x_ref[0] + _mod_part(mod_ref, 2, d) * y
    y_ref[0] = _layernorm(z, g_ref[...], b_ref[...])


def _mod_spec(d, per_batch):
    if per_batch:
        return pl.BlockSpec((1, 1, N_MOD * d), lambda b, *_: (b, 0, 0))
    return pl.BlockSpec((1, 1, N_MOD * d), lambda b, *_: (0, 0, 0))


def _out_ln(o, wo, x, mod, per_batch, ln_g, ln_b, alpha):
    bsz, t, d = x.shape
    tm = min(512, t)
    tok = lambda b, i: (b, i, 0)
    const = lambda b, i: (0, 0)
    return pl.pallas_call(
        functools.partial(_out_ln_kernel, d=d, alpha=alpha),
        grid=(bsz, t // tm),
        in_specs=[pl.BlockSpec((1, tm, d), tok), pl.BlockSpec((d, d), const), pl.BlockSpec((1, tm, d), tok),
                  _mod_spec(d, per_batch), pl.BlockSpec((1, d), const), pl.BlockSpec((1, d), const)],
        out_specs=pl.BlockSpec((1, tm, d), tok),
        out_shape=jax.ShapeDtypeStruct((bsz, t, d), F32),
        compiler_params=_cparams("parallel", "parallel"),
        name="out_ln",
    )(o, wo, x, mod, ln_g, ln_b)


def _mlp_kernel(x_ref, mod_ref, w1_ref, w2_ref, g_ref, b_ref, y_ref, h_scr, acc_scr, *, d, alpha):
    j = pl.program_id(2)

    @pl.when(j == 0)
    def _():
        h = x_ref[0] * (1.0 + _mod_part(mod_ref, 4, d)) + _mod_part(mod_ref, 3, d)
        h_scr[...] = h.astype(h_scr.dtype)
        acc_scr[...] = jnp.zeros_like(acc_scr)

    a = jnp.square(jnp.maximum(_mm(h_scr[...], w1_ref[...]), 0.0))
    acc_scr[...] += _mm(a, w2_ref[...])

    @pl.when(j == pl.num_programs(2) - 1)
    def _():
        z = alpha * x_ref[0] + _mod_part(mod_ref, 5, d) * acc_scr[...]
        y_ref[0] = _layernorm(z, g_ref[...], b_ref[...])


def _mlp(x, mod, per_batch, w1, w2, ln_g, ln_b, alpha):
    bsz, t, d = x.shape
    ff = w1.shape[1]
    tm = min(1024, t)
    tf = min(2048, ff)
    tok = lambda b, i, j: (b, i, 0)
    const = lambda b, i, j: (0, 0)
    return pl.pallas_call(
        functools.partial(_mlp_kernel, d=d, alpha=alpha),
        grid=(bsz, t // tm, ff // tf),
        in_specs=[pl.BlockSpec((1, tm, d), tok), _mod_spec(d, per_batch),
                  pl.BlockSpec((d, tf), lambda b, i, j: (0, j)), pl.BlockSpec((tf, d), lambda b, i, j: (j, 0)),
                  pl.BlockSpec((1, d), const), pl.BlockSpec((1, d), const)],
        out_specs=pl.BlockSpec((1, tm, d), tok),
        out_shape=jax.ShapeDtypeStruct((bsz, t, d), F32),
        scratch_shapes=[pltpu.VMEM((tm, d), MXU_DTYPE), pltpu.VMEM((tm, d), F32)],
        compiler_params=_cparams("parallel", "parallel", "arbitrary"),
        name="mlp",
    )(x, mod, w1, w2, ln_g, ln_b)


def _gqa_qkv_kernel(x_ref, mod_ref, w_ref, gq_ref, gk_ref, cos_ref, sin_ref, q_ref, k_ref, v_ref,
                    *, d, n_q, n_kv):
    h = x_ref[0] * (1.0 + _mod_part(mod_ref, 1, d)) + _mod_part(mod_ref, 0, d)
    acc = _mm(h, w_ref[...])
    cos, sin = cos_ref[...], sin_ref[...]
    hd = ATT_HEAD_DIM
    src = lax.broadcasted_iota(jnp.int32, (hd, hd), 0)
    dst = lax.broadcasted_iota(jnp.int32, (hd, hd), 1)
    first = (dst & (hd // 2 - 1)) < hd // 4
    rot = (jnp.where(first & (src == dst + hd // 4), -1.0, 0.0)
           + jnp.where(jnp.logical_not(first) & (src == dst - hd // 4), 1.0, 0.0))

    def norm_rope(u, g):
        u = u * lax.rsqrt(jnp.mean(u * u, -1, keepdims=True) + LN_EPS) * g
        return u * cos + _mm(u, rot) * sin

    for i in range(n_q):
        q_ref[0, :, i * hd:(i + 1) * hd] = norm_rope(acc[:, i * hd:(i + 1) * hd], gq_ref[...]).astype(q_ref.dtype)
    for i in range(n_kv):
        c0 = (n_q + i) * hd
        k_ref[0, :, i * hd:(i + 1) * hd] = norm_rope(acc[:, c0:c0 + hd], gk_ref[...]).astype(k_ref.dtype)
    ones = jnp.ones((acc.shape[0], hd), v_ref.dtype)
    for i in range(n_kv):
        c0 = (n_q + n_kv + i) * hd
        v_ref[0, :, 2 * i * hd:(2 * i + 1) * hd] = acc[:, c0:c0 + hd].astype(v_ref.dtype)
        v_ref[0, :, (2 * i + 1) * hd:(2 * i + 2) * hd] = ones


def _gqa_qkv(x, mod, per_batch, w, gq, gk, cos, sin):
    bsz, t, d = x.shape
    hd = ATT_HEAD_DIM
    n_kv = ATT_KV_HEADS
    n_q = w.shape[1] // hd - 2 * n_kv
    tm = min(512, t)
    tok = lambda b, i: (b, i, 0)
    const = lambda b, i: (0, 0)
    tab = pl.BlockSpec((tm, hd), lambda b, i: (i, 0))
    return pl.pallas_call(
        functools.partial(_gqa_qkv_kernel, d=d, n_q=n_q, n_kv=n_kv),
        grid=(bsz, t // tm),
        in_specs=[pl.BlockSpec((1, tm, d), tok), _mod_spec(d, per_batch), pl.BlockSpec(w.shape, const),
                  pl.BlockSpec((1, hd), const), pl.BlockSpec((1, hd), const), tab, tab],
        out_specs=[pl.BlockSpec((1, tm, n_q * hd), tok), pl.BlockSpec((1, tm, n_kv * hd), tok),
                   pl.BlockSpec((1, tm, 2 * n_kv * hd), tok)],
        out_shape=[jax.ShapeDtypeStruct((bsz, t, n_q * hd), MXU_DTYPE),
                   jax.ShapeDtypeStruct((bsz, t, n_kv * hd), MXU_DTYPE),
                   jax.ShapeDtypeStruct((bsz, t, 2 * n_kv * hd), MXU_DTYPE)],
        compiler_params=_cparams("parallel", "parallel"),
        name="gqa_qkv",
    )(x, mod, w, gq, gk, cos, sin)


def _gqa_att_kernel(*refs, n_sets, groups, tk):
    q_ref, o_ref = refs[0], refs[-1]
    kv = refs[1:-1]
    hd = ATT_HEAD_DIM
    tq = q_ref.shape[1]
    q = jnp.concatenate([q_ref[0, :, g * hd:(g + 1) * hd] for g in range(groups)], axis=0)
    m = acc = None
    for j in range(n_sets):
        k_ref, v_ref = kv[2 * j], kv[2 * j + 1]
        t = k_ref.shape[1]
        for c0 in range(0, t, tk):
            c1 = min(c0 + tk, t)
            s = _mm_nt(q, k_ref[0, c0:c1, :])
            m_c = jnp.max(s, -1, keepdims=True)
            if m is None:
                m = m_c
                acc = _mm(jnp.exp2(s - m), v_ref[0, c0:c1, :])
            else:
                m_new = jnp.maximum(m, m_c)
                acc = acc * jnp.exp2(m - m_new) + _mm(jnp.exp2(s - m_new), v_ref[0, c0:c1, :])
                m = m_new
    o = acc[:, :hd] / acc[:, hd:]
    for g in range(groups):
        o_ref[0, :, g * hd:(g + 1) * hd] = o[g * tq:(g + 1) * tq].astype(o_ref.dtype)


def _gqa_att(q, kv_sets):
    bsz, s, dq = q.shape
    hd = ATT_HEAD_DIM
    n_kv = ATT_KV_HEADS
    groups = dq // hd // n_kv
    tq = min(512, s)
    in_specs = [pl.BlockSpec((1, tq, groups * hd), lambda b, kh, i: (b, i, kh))]
    args = [q]
    for k, v in kv_sets:
        t = k.shape[1]
        in_specs += [pl.BlockSpec((1, t, hd), lambda b, kh, i: (b, 0, kh)),
                     pl.BlockSpec((1, t, 2 * hd), lambda b, kh, i: (b, 0, kh))]
        args += [k, v]
    return pl.pallas_call(
        functools.partial(_gqa_att_kernel, n_sets=len(kv_sets), groups=groups, tk=GQA_KEY_CHUNK),
        grid=(bsz, n_kv, s // tq),
        in_specs=in_specs,
        out_specs=pl.BlockSpec((1, tq, groups * hd), lambda b, kh, i: (b, i, kh)),
        out_shape=jax.ShapeDtypeStruct((bsz, s, dq), MXU_DTYPE),
        compiler_params=_cparams("parallel", "parallel", "parallel"),
        name="gqa_att",
    )(*args)


def _rope_tables(n_tokens, head_dim):
    t = np.arange(n_tokens)
    row = (t // GRID_W).astype(np.float32)
    col = (t % GRID_W).astype(np.float32)
    half = head_dim // 2
    freqs = jnp.asarray(ROPE_BASE, F32) ** (-jnp.arange(0, half, 2, dtype=F32) / half)
    ang_r = jnp.asarray(row)[:, None] * freqs[None, :]
    ang_c = jnp.asarray(col)[:, None] * freqs[None, :]
    ang = jnp.concatenate([ang_r, ang_r, ang_c, ang_c], axis=-1)
    return jnp.cos(ang), jnp.sin(ang)


def _gqa_layer(x, xc, mod, mod_c, wqkv, wo, q_norm, k_norm, want_ctx):
    s, tc = x.shape[1], xc.shape[1]
    hd = ATT_HEAD_DIM
    gq = (q_norm * (hd ** -0.5 * LOG2E)).reshape(1, hd)
    gk = k_norm.reshape(1, hd)
    cos, sin = _rope_tables(s, hd)
    one, zero = jnp.ones((tc, hd), F32), jnp.zeros((tc, hd), F32)
    q, k, v = _gqa_qkv(x, mod, True, wqkv, gq, gk, cos, sin)
    qc, kc, vc = _gqa_qkv(xc, mod_c, False, wqkv, gq, gk, one, zero)
    o = _gqa_att(q, [(k, v), (kc, vc)])
    oc = _gqa_att(qc, [(kc, vc)]) if want_ctx else None
    return o, oc, wo


def _na_qkv_kernel(x_ref, mod_ref, w_ref, q_ref, k_ref, v_ref, *, d, scale):
    h = x_ref[0] * (1.0 + _mod_part(mod_ref, 1, d)) + _mod_part(mod_ref, 0, d)
    acc = _mm(h, w_ref[...])
    q_ref[0] = (acc[:, :d] * scale).astype(q_ref.dtype)
    k_ref[0] = acc[:, d:2 * d].astype(k_ref.dtype)
    ones = jnp.ones((acc.shape[0], LANES), v_ref.dtype)
    for p in range(d // LANES):
        c0 = 2 * d + p * LANES
        v_ref[0, :, 2 * p * LANES:(2 * p + 1) * LANES] = acc[:, c0:c0 + LANES].astype(v_ref.dtype)
        v_ref[0, :, (2 * p + 1) * LANES:(2 * p + 2) * LANES] = ones


def _na_qkv(x, mod, per_batch, w):
    bsz, t, d = x.shape
    tm = min(512, t)
    tok = lambda b, i: (b, i, 0)
    out = jax.ShapeDtypeStruct((bsz, t, d), MXU_DTYPE)
    return pl.pallas_call(
        functools.partial(_na_qkv_kernel, d=d, scale=NA_HEAD_DIM ** -0.5 * LOG2E),
        grid=(bsz, t // tm),
        in_specs=[pl.BlockSpec((1, tm, d), tok), _mod_spec(d, per_batch), pl.BlockSpec(w.shape, lambda b, i: (0, 0))],
        out_specs=[pl.BlockSpec((1, tm, d), tok)] * 2 + [pl.BlockSpec((1, tm, 2 * d), tok)],
        out_shape=[out, out, jax.ShapeDtypeStruct((bsz, t, 2 * d), MXU_DTYPE)],
        compiler_params=_cparams("parallel", "parallel"),
        name="na_qkv",
    )(x, mod, w)


def _pair_softmax_att(q, ks, vs, biases):
    lane = lax.broadcasted_iota(jnp.int32, q.shape, 1)
    first = lane < NA_HEAD_DIM
    zero = jnp.zeros_like(q)
    qm = [jnp.where(first, q, zero), jnp.where(first, zero, q)]
    m, acc = [None, None], [None, None]
    for j, (k, v) in enumerate(zip(ks, vs)):
        for h in range(2):
            s = _mm_nt(qm[h], k)
            if biases[h][j] is not None:
                s = s + biases[h][j]
            m_c = jnp.max(s, -1, keepdims=True)
            if m[h] is None:
                m[h] = m_c
                acc[h] = _mm(jnp.exp2(s - m_c), v)
            else:
                m_new = jnp.maximum(m[h], m_c)
                acc[h] = acc[h] * jnp.exp2(m[h] - m_new) + _mm(jnp.exp2(s - m_new), v)
                m[h] = m_new
    outs = [a[:, :LANES] / a[:, LANES:] for a in acc]
    return jnp.where(first, outs[0], outs[1])


def _na_att_kernel(q_ref, k_ref, v_ref, kc_ref, vc_ref, bias_ref, o_ref, *, rows):
    i = pl.program_id(2)
    q_rows = q_ref.shape[1] // GRID_W
    win = NA_WIN_H * GRID_W
    w = GRID_W
    first = lax.broadcasted_iota(jnp.int32, (w, LANES), 1) < NA_HEAD_DIM
    qm, kw, vw, bias = [], [], [], []
    for rq in range(q_rows):
        row_q = i * q_rows + rq
        r0 = jnp.clip(row_q - NA_WIN_H // 2, 0, rows - NA_WIN_H)
        start = pl.multiple_of(r0 * w, w)
        q = q_ref[0, rq * w:(rq + 1) * w, :]
        zero = jnp.zeros_like(q)
        qm.append(jnp.concatenate([jnp.where(first, q, zero), jnp.where(first, zero, q)], axis=0))
        kw.append(k_ref[0, pl.ds(start, win), :])
        vw.append(v_ref[0, pl.ds(start, win), :])
        e = r0 - row_q + NA_WIN_H - 1
        bias.append(jnp.concatenate([bias_ref[0, 0, e], bias_ref[0, 1, e]], axis=0))
    each = lambda fn, *lists: [fn(*args) for args in zip(*lists)]
    s_ctx = _mm_nt(jnp.concatenate(qm, axis=0), kc_ref[0])
    s_ctx = [s_ctx[2 * w * rq:2 * w * (rq + 1)] for rq in range(q_rows)]
    s_loc = each(lambda a, b, c: _mm_nt(a, b) + c, qm, kw, bias)
    m = each(lambda a, b: jnp.maximum(jnp.max(a, -1, keepdims=True), jnp.max(b, -1, keepdims=True)), s_loc, s_ctx)
    acc_ctx = _mm(jnp.concatenate(each(lambda a, mm: jnp.exp2(a - mm), s_ctx, m), axis=0), vc_ref[0])
    acc = each(lambda a, mm, v: _mm(jnp.exp2(a - mm), v), s_loc, m, vw)
    for rq in range(q_rows):
        a = acc[rq] + acc_ctx[2 * w * rq:2 * w * (rq + 1)]
        o = a[:, :LANES] / a[:, LANES:]
        o_ref[0, rq * w:(rq + 1) * w, :] = jnp.where(first, o[:w], o[w:]).astype(o_ref.dtype)


def _na_bias_tables(rpb, scale):
    pad = GRID_W
    rpb_p = jnp.pad(rpb, ((0, 0), (0, 0), (pad, pad)))
    cmat = jnp.stack([rpb_p[:, :, pad + NA_WIN_W - 1 - cq: pad + NA_WIN_W - 1 - cq + GRID_W]
                      for cq in range(GRID_W)], axis=2)
    cq, ck = np.arange(GRID_W)[:, None], np.arange(GRID_W)[None, :]
    c0 = np.clip(cq - NA_WIN_W // 2, 0, GRID_W - NA_WIN_W)
    vcol = (ck >= c0) & (ck < c0 + NA_WIN_W)
    cmat = jnp.where(vcol[None, None], cmat * scale, NEG_BIG)
    return jnp.stack([jnp.concatenate([cmat[:, e + j] for j in range(NA_WIN_H)], axis=-1)
                      for e in range(NA_WIN_H)], axis=1)


def _na_att(q, k, v, kc, vc, bias):
    bsz, s, d = q.shape
    tc = kc.shape[1]
    rows = s // GRID_W
    q_rows = min(NA_Q_ROWS, rows)
    assert rows % q_rows == 0 and rows >= NA_WIN_H
    tq = q_rows * GRID_W
    pairs = d // LANES
    q_spec = pl.BlockSpec((1, tq, LANES), lambda hp, b, i: (b, i, hp))
    whole = lambda t, width: pl.BlockSpec((1, t, width), lambda hp, b, i: (b, 0, hp))
    bias_spec = pl.BlockSpec((1, 2) + bias.shape[1:], lambda hp, b, i: (0, hp, 0, 0, 0))
    return pl.pallas_call(
        functools.partial(_na_att_kernel, rows=rows),
        grid=(pairs, bsz, rows // q_rows),
        in_specs=[q_spec, whole(s, LANES), whole(s, 2 * LANES), whole(tc, LANES), whole(tc, 2 * LANES), bias_spec],
        out_specs=q_spec,
        out_shape=jax.ShapeDtypeStruct((bsz, s, d), MXU_DTYPE),
        compiler_params=_cparams("parallel", "parallel", "parallel"),
        name="na_att",
    )(q, k, v, kc, vc, bias[None])


def _pair_att_kernel(q_ref, k_ref, v_ref, o_ref):
    o_ref[0] = _pair_softmax_att(q_ref[0], [k_ref[0]], [v_ref[0]], [[None], [None]]).astype(o_ref.dtype)


def _pair_att(q, k, v):
    bsz, t, d = q.shape
    spec = pl.BlockSpec((1, t, LANES), lambda b, hp: (b, 0, hp))
    return pl.pallas_call(
        _pair_att_kernel,
        grid=(bsz, d // LANES),
        in_specs=[spec, spec, pl.BlockSpec((1, t, 2 * LANES), lambda b, hp: (b, 0, hp))],
        out_specs=spec,
        out_shape=jax.ShapeDtypeStruct((bsz, t, d), MXU_DTYPE),
        compiler_params=_cparams("parallel", "parallel"),
        name="na_ctx_att",
    )(q, k, v)


def _na_layer(x, xc, mod, mod_c, wqkv, wo, rpb, want_ctx):
    q, k, v = _na_qkv(x, mod, True, wqkv)
    qc, kc, vc = _na_qkv(xc, mod_c, False, wqkv)
    o = _na_att(q, k, v, kc, vc, _na_bias_tables(rpb, LOG2E))
    oc = _pair_att(qc, kc, vc) if want_ctx else None
    return o, oc, wo


def _rw_proj_kernel(x_ref, xp_ref, xn_ref, mod_ref, mu_ref, wr_ref, wk_ref, wv_ref, w1_ref, w2_ref, a1_ref, a2_ref,
                    g1_ref, g2_ref, w0_ref, a0_ref, r_ref, k_ref, v_ref, g_ref, lw_ref, as_ref, *, d):
    t = pl.program_id(1)
    nt = pl.num_programs(1)
    scale = 1.0 + _mod_part(mod_ref, 1, d)
    shift = _mod_part(mod_ref, 0, d)
    h = x_ref[0] * scale + shift
    tm = h.shape[0]
    h_prev = (xp_ref[0, 7:8, :] * scale + shift) * (t > 0).astype(F32)
    h_next = (xn_ref[0, 0:1, :] * scale + shift) * (t < nt - 1).astype(F32)
    row = lax.broadcasted_iota(jnp.int32, h.shape, 0)
    prev = jnp.where(row == 0, h_prev, pltpu.roll(h, 1, 0))
    nxt = jnp.where(row == tm - 1, h_next, pltpu.roll(h, tm - 1, 0))
    xx = 0.5 * (prev + nxt) - h
    mix = lambda j: h + xx * mu_ref[j:j + 1, :]

    r_ref[0] = _mm(mix(0), wr_ref[...]).astype(r_ref.dtype)
    k_ref[0] = _mm(mix(2), wk_ref[...]).astype(k_ref.dtype)
    v_ref[0] = _mm(mix(3), wv_ref[...]).astype(v_ref.dtype)
    g_ref[0] = _mm(_sigmoid(_mm(mix(5), g1_ref[...])), g2_ref[...]).astype(g_ref.dtype)

    tw = jnp.tanh(_mm(mix(1), w1_ref[...]))
    al = _mm(mix(4), a1_ref[...])
    first = lax.broadcasted_iota(jnp.int32, tw.shape, 1) < tw.shape[1] // 2
    zero = jnp.zeros_like(tw)
    for n in range(2):
        pick = lambda u: jnp.where(first, u, zero) if n == 0 else jnp.where(first, zero, u)
        z = -(w0_ref[n:n + 1, :] + _mm(pick(tw), w2_ref[...]))
        softplus = jnp.maximum(z, 0.0) + jnp.log(1.0 + jnp.exp(-jnp.abs(z)))
        lw_ref[n, 0] = -jnp.exp(-softplus - 0.5)
        as_ref[n, 0] = _sigmoid(a0_ref[n:n + 1, :] + _mm(pick(al), a2_ref[...])).astype(as_ref.dtype)


def _rw_proj(x, mod, per_batch, p):
    bsz, t, d = x.shape
    tm = min(512, t)
    n8 = t // 8
    tok = lambda b, i: (b, i, 0)
    dtok = lambda b, i: (0, b, i, 0)
    full = lambda a: pl.BlockSpec(a.shape, lambda b, i: (0,) * a.ndim)
    halo_prev = pl.BlockSpec((1, 8, d), lambda b, i: (b, jnp.maximum(i * (tm // 8) - 1, 0), 0))
    halo_next = pl.BlockSpec((1, 8, d), lambda b, i: (b, jnp.minimum((i + 1) * (tm // 8), n8 - 1), 0))
    weights = [p['mu'], p['wr'], p['wk'], p['wv'], p['w1'], p['w2'], p['a1'], p['a2'], p['g1'], p['g2'],
               p['w0'], p['a0']]
    one = jax.ShapeDtypeStruct((bsz, t, d), MXU_DTYPE)
    two = lambda dtype: jax.ShapeDtypeStruct((2, bsz, t, d), dtype)
    return pl.pallas_call(
        functools.partial(_rw_proj_kernel, d=d),
        grid=(bsz, t // tm),
        in_specs=[pl.BlockSpec((1, tm, d), tok), halo_prev, halo_next, _mod_spec(d, per_batch)]
                 + [full(a) for a in weights],
        out_specs=[pl.BlockSpec((1, tm, d), tok)] * 4 + [pl.BlockSpec((2, 1, tm, d), dtok)] * 2,
        out_shape=[one, one, one, one, two(F32), two(MXU_DTYPE)],
        compiler_params=_cparams("parallel", "parallel"),
        name="rw_proj",
    )(x, x, x, mod, *weights)


def _rw_scan_kernel(r_ref, k_ref, v_ref, lw_ref, as_ref, kk_ref, ka_ref, z0_ref, y_ref, zf_ref, z_scr,
                    *, pairs, sub, reverse):
    c = pl.program_id(1)
    L = RW_CHUNK
    H = RW_HEAD
    assert 2 * L == LANES and 2 * H == LANES

    @pl.when(c == 0)
    def _():
        z_scr[...] = z0_ref[0, 0]

    sign = -1 if reverse else 1
    t_i = lax.broadcasted_iota(jnp.int32, (L, 2 * L), 0)
    s_i = lax.broadcasted_iota(jnp.int32, (L, 2 * L), 1) & (L - 1)
    dt = sign * (s_i - t_i)
    strict_c = dt < 0
    incl_c = dt <= 0
    tri2 = incl_c.astype(F32)
    eye_c = (dt == 0).astype(F32)
    r2 = lax.broadcasted_iota(jnp.int32, (LANES, LANES), 0)
    c2 = lax.broadcasted_iota(jnp.int32, (LANES, LANES), 1)
    eye2 = (r2 == c2).astype(F32)
    same_head = _head_ones()
    first = lax.broadcasted_iota(jnp.int32, (L, LANES), 1) < H
    zero = jnp.zeros((L, LANES), F32)

    head0 = lambda u: jnp.where(first, u, zero)
    head1 = lambda u: jnp.where(first, zero, u)
    bd = lambda u: jnp.concatenate([head0(u), head1(u)], axis=0)

    each = lambda fn, *lists: [fn(*args) for args in zip(*lists)]
    sls = [slice(p * LANES, (p + 1) * LANES) for p in range(pairs)]
    cat0 = lambda *u: jnp.concatenate(u, axis=0)
    cat1 = lambda *u: jnp.concatenate(u, axis=1)

    mm_each = lambda lhs, w: [_mm(a, b) for a, b in zip(lhs, w)]

    def chunk_terms(rw):
        lw = [lw_ref[0, 0, rw, sl] for sl in sls]
        r = [r_ref[0, rw, sl].astype(F32) for sl in sls]
        k = [k_ref[0, rw, sl].astype(F32) for sl in sls]
        v = [v_ref[0, rw, sl].astype(F32) for sl in sls]
        a_s = [as_ref[0, 0, rw, sl].astype(F32) for sl in sls]

        c_in = each(lambda u: _mm(tri2, cat0(*_split(u, 2))), lw)
        c_all = each(lambda u: jnp.sum(u, axis=0, keepdims=True), lw)
        kk = each(lambda u, sl: u * kk_ref[:, sl], k, sls)
        ss = each(lambda u: _mm(u * u, same_head), kk)
        kk = each(lambda u, s: u * lax.rsqrt(jnp.maximum(s, 1e-12)), kk, ss)
        b_v = each(lambda u, a: u * a, kk, a_s)
        k_d = each(lambda u, a, sl: u * (1.0 + (a - 1.0) * ka_ref[:, sl]), k, a_s, sls)
        e_neg = each(lambda ci: jnp.exp(-ci), c_in)
        e_rem = each(lambda ci, ca: jnp.exp(ca - ci), c_in, c_all)
        a_t = each(lambda u, ci, l: -u * jnp.exp(ci - l), kk, c_in, lw)
        r_t = each(lambda u, ci: u * jnp.exp(ci), r, c_in)
        b_t = each(lambda u, e: u * e, b_v, e_neg)
        k_t = each(lambda u, e: u * e, k_d, e_neg)
        b_h = each(lambda u, e: u * e, b_v, e_rem)
        k_h = each(lambda u, e: u * e, k_d, e_rem)
        yield None

        x = each(lambda a, rr, b, kt: _mm_nt(cat0(a, rr), cat0(bd(b), bd(kt))), a_t, r_t, b_t, k_t)
        m_ab = each(lambda u: jnp.where(strict_c, u[:L, :2 * L], 0.0), x)
        m_ak = each(lambda u: jnp.where(strict_c, u[:L, 2 * L:], 0.0), x)
        n_rb = each(lambda u: jnp.where(incl_c, u[L:, :2 * L], 0.0), x)
        n_rk = each(lambda u: jnp.where(incl_c, u[L:, 2 * L:], 0.0), x)

        inv = each(lambda m: eye_c + m, m_ab)
        yield None
        pw = mm_each(m_ab, each(bd, m_ab))
        for _ in range(int(np.log2(L)) - 2):
            yield None
            st = mm_each(each(cat0, pw, inv), each(bd, pw))
            pw = each(lambda s: s[:L], st)
            inv = each(lambda b, s: b + s[L:], inv, st)
        yield None
        inv = each(lambda b, s: b + s, inv, mm_each(inv, each(bd, pw)))
        mv = mm_each(each(cat0, m_ak, n_rk), each(bd, v))
        yield None
        tw = each(lambda i, a, m: _mm(i, cat1(bd(a), bd(m[:L]))), inv, a_t, mv)
        p1 = each(lambda u: u[:, :LANES], tw)
        p2 = each(lambda u: u[:, LANES:], tw)
        yield None
        nw = each(lambda m, a, b: _mm(m, cat1(bd(a), bd(b))), n_rb, p1, p2)
        q1 = each(lambda rr, u: rr + u[:, :LANES], r_t, nw)
        q2 = each(lambda u, m: u[:, LANES:] + m[L:], nw, mv)
        gh = each(lambda b, kh, a, c_, u: _mm_tn(cat0(b, kh), cat0(cat1(a, c_), cat1(zero, u))),
                  b_h, k_h, p1, p2, v)
        g_t = each(lambda ca, u: eye2 * jnp.exp(ca) + same_head * u[:, :LANES], c_all, gh)
        h_t = each(lambda u: same_head * u[:, LANES:], gh)
        yield q1, q2, g_t, h_t

    chunk_rows = [slice(u * L, (u + 1) * L) for u in (range(sub - 1, -1, -1) if reverse else range(sub))]
    gens = [chunk_terms(rw) for rw in chunk_rows]
    terms = [None] * sub
    while any(t is None for t in terms):
        terms = [next(gen) for gen in gens]
    z = [z_scr[p] for p in range(pairs)]
    for rw, (q1, q2, g_t, h_t) in zip(chunk_rows, terms):
        yz = mm_each(each(cat0, q1, g_t), z)
        for p in range(pairs):
            y_ref[0, rw, sls[p]] = (yz[p][:L] + q2[p]).astype(y_ref.dtype)
            z[p] = yz[p][L:] + h_t[p]
    for p in range(pairs):
        z_scr[p] = z[p]

    @pl.when(c == pl.num_programs(1) - 1)
    def _():
        zf_ref[0] = z_scr[...]


def _rw_scan(r, k, v, lw, a_s, kk_w, ka_w, z0, reverse):
    bsz, t, d = r.shape
    sub = 2
    rows = sub * RW_CHUNK
    n_c = t // rows
    n = int(reverse)
    pairs = d // LANES
    cidx = (lambda c: n_c - 1 - c) if reverse else (lambda c: c)
    tok = pl.BlockSpec((1, rows, d), lambda b, c: (b, cidx(c), 0))
    dtok = pl.BlockSpec((1, 1, rows, d), lambda b, c: (n, b, cidx(c), 0))
    vec = pl.BlockSpec((1, d), lambda b, c: (0, 0))
    return pl.pallas_call(
        functools.partial(_rw_scan_kernel, pairs=pairs, sub=sub, reverse=reverse),
        grid=(bsz, n_c),
        in_specs=[tok, tok, tok, dtok, dtok, vec, vec,
                  pl.BlockSpec((1, 1, pairs, LANES, LANES), lambda b, c: (n, b, 0, 0, 0))],
        out_specs=[tok, pl.BlockSpec((1, pairs, LANES, LANES), lambda b, c: (b, 0, 0, 0))],
        out_shape=[jax.ShapeDtypeStruct((bsz, t, d), MXU_DTYPE),
                   jax.ShapeDtypeStruct((bsz, pairs, LANES, LANES), F32)],
        scratch_shapes=[pltpu.VMEM((pairs, LANES, LANES), F32)],
        compiler_params=_cparams("parallel", "arbitrary"),
        name="rw_scan",
    )(r, k, v, lw, a_s, kk_w, ka_w, z0)


def _rw_out_kernel(yf_ref, yb_ref, r_ref, k_ref, v_ref, g_ref, as_ref, ka_ref, rk_ref, lg_ref, lb_ref, wo_ref, x_ref,
                   mod_ref, g_ln_ref, b_ln_ref, o_ref, u_scr, *, d, alpha):
    width = 2 * LANES
    same_head = _head_ones(width)
    inv_n = 1.0 / RW_HEAD
    for p in range(d // width):
        sl = slice(p * width, (p + 1) * width)
        y = yf_ref[0, :, sl].astype(F32) + yb_ref[0, :, sl].astype(F32)
        mu = _mm_rhs_exact(y, same_head) * inv_n
        dy = y - mu
        var = _mm(dy * dy, same_head) * inv_n
        yn = dy * lax.rsqrt(var + RW_GN_EPS) * lg_ref[:, sl] + lb_ref[:, sl]
        k = k_ref[0, :, sl].astype(F32)
        ka = ka_ref[:, sl]
        k_sum = k * ((1.0 + (as_ref[0, 0, :, sl].astype(F32) - 1.0) * ka)
                     + (1.0 + (as_ref[1, 0, :, sl].astype(F32) - 1.0) * ka))
        bonus = _mm(r_ref[0, :, sl].astype(F32) * k_sum * rk_ref[:, sl], same_head)
        u = (yn + bonus * v_ref[0, :, sl].astype(F32)) * g_ref[0, :, sl].astype(F32)
        u_scr[:, sl] = u.astype(u_scr.dtype)
    z = alpha * x_ref[0] + _mod_part(mod_ref, 2, d) * _mm(u_scr[...], wo_ref[...])
    o_ref[0] = _layernorm(z, g_ln_ref[...], b_ln_ref[...])


def _rw_out(y_f, y_b, r, k, v, g, a_s, p, x, mod, per_batch, ln_g, ln_b, alpha):
    bsz, t, d = x.shape
    tm = min(256, t)
    tok = pl.BlockSpec((1, tm, d), lambda b, i: (b, i, 0))
    dtok = pl.BlockSpec((2, 1, tm, d), lambda b, i: (0, b, i, 0))
    vec = pl.BlockSpec((1, d), lambda b, i: (0, 0))
    return pl.pallas_call(
        functools.partial(_rw_out_kernel, d=d, alpha=alpha),
        grid=(bsz, t // tm),
        in_specs=[tok, tok, tok, tok, tok, tok, dtok, vec, vec, vec, vec, pl.BlockSpec((d, d), lambda b, i: (0, 0)),
                  tok, _mod_spec(d, per_batch), vec, vec],
        out_specs=tok,
        out_shape=jax.ShapeDtypeStruct((bsz, t, d), F32),
        scratch_shapes=[pltpu.VMEM((tm, d), MXU_DTYPE)],
        compiler_params=_cparams("parallel", "parallel"),
        name="rw_out",
    )(y_f, y_b, r, k, v, g, a_s, p['k_a'], p['r_k'], p['lnx_g'], p['lnx_b'], p['wo'], x, mod, ln_g, ln_b)


def _rw_layer(x, xc, mod, mod_c, p, want_ctx, ln_g, ln_b, alpha):
    bsz, _, d = x.shape
    rc, kc, vc, gc, lwc, asc = _rw_proj(xc, mod_c, False, p)
    r, k, v, g, lw, a_s = _rw_proj(x, mod, True, p)
    z0 = jnp.zeros((2, bsz, d // LANES, LANES, LANES), F32)
    scan = lambda args, z, rev: _rw_scan(*args, p['k_k'], p['k_a'], z, rev)
    yc_f, zc_f = scan((rc, kc, vc, lwc, asc), z0, False)
    yc_b, zc_b = scan((rc, kc, vc, lwc, asc), z0, True)
    zc = jnp.stack([zc_f, zc_b])
    y_f, _ = scan((r, k, v, lw, a_s), zc, False)
    y_b, _ = scan((r, k, v, lw, a_s), zc, True)
    x_new = _rw_out(y_f, y_b, r, k, v, g, a_s, p, x, mod, True, ln_g, ln_b, alpha)
    xc_new = (_rw_out(yc_f, yc_b, rc, kc, vc, gc, asc, p, xc, mod_c, False, ln_g, ln_b, alpha)
              if want_ctx else None)
    return x_new, xc_new


def kernel(x, c, ctx, c_ctx, mod_w, mod_b, post_ln_g, post_ln_b, mlp_w1, mlp_w2, att_wqkv, att_wo, att_q_norm, att_k_norm, na_wqkv, na_wo, na_rpb, rw_mu, rw_wr, rw_wk, rw_wv, rw_wo, rw_w0, rw_w1, rw_w2, rw_a0, rw_a1, rw_a2, rw_g1, rw_g2, rw_k_k, rw_k_a, rw_r_k, rw_lnx_g, rw_lnx_b):
    depth = mod_w.shape[0]
    bsz, _, d = x.shape
    alpha = (2.0 * depth) ** 0.25
    cast = lambda a: a.astype(MXU_DTYPE)

    cond_rows = -(-(bsz + 1) // 8) * 8
    cond = jnp.zeros((cond_rows, d), F32).at[:bsz].set(c).at[bsz].set(c_ctx)
    mods = _modulation(cond, mod_w, mod_b)

    xc = ctx
    for i in range(depth):
        kind, slot = i % N_MIXERS, i // N_MIXERS
        want_ctx = i < depth - 1
        mod = mods[i, :bsz].reshape(bsz, 1, N_MOD * d)
        mod_c = mods[i, bsz:bsz + 1].reshape(1, 1, N_MOD * d)
        g1, b1 = post_ln_g[i, 0:1], post_ln_b[i, 0:1]
        g2, b2 = post_ln_g[i, 1:2], post_ln_b[i, 1:2]
        if kind == 2:
            cat = lambda a: jnp.concatenate([a[0], a[1]], axis=-1)
            p = {'mu': rw_mu[slot], 'wr': cast(rw_wr[slot]), 'wk': cast(rw_wk[slot]), 'wv': cast(rw_wv[slot]),
                 'wo': cast(rw_wo[slot]), 'w0': rw_w0[slot], 'a0': rw_a0[slot],
                 'w1': cast(cat(rw_w1[slot])), 'a1': cast(cat(rw_a1[slot])),
                 'w2': cast(rw_w2[slot].reshape(-1, d)), 'a2': cast(rw_a2[slot].reshape(-1, d)),
                 'g1': cast(rw_g1[slot]), 'g2': cast(rw_g2[slot]),
                 'k_k': rw_k_k[slot].reshape(1, d), 'k_a': rw_k_a[slot].reshape(1, d),
                 'r_k': rw_r_k[slot].reshape(1, d), 'lnx_g': rw_lnx_g[slot].reshape(1, d),
                 'lnx_b': rw_lnx_b[slot].reshape(1, d)}
            x, xc_new = _rw_layer(x, xc, mod, mod_c, p, want_ctx, g1, b1, alpha)
        else:
            if kind == 0:
                o, oc, wo = _gqa_layer(x, xc, mod, mod_c, cast(att_wqkv[slot]), cast(att_wo[slot]),
                                       att_q_norm[slot], att_k_norm[slot], want_ctx)
            else:
                o, oc, wo = _na_layer(x, xc, mod, mod_c, cast(na_wqkv[slot]), cast(na_wo[slot]), na_rpb[slot],
                                      want_ctx)
            x = _out_ln(o, wo, x, mod, True, g1, b1, alpha)
            xc_new = _out_ln(oc, wo, xc, mod_c, False, g1, b1, alpha) if want_ctx else None
        w1, w2 = cast(mlp_w1[i]), cast(mlp_w2[i])
        x = _mlp(x, mod, True, w1, w2, g2, b2, alpha)
        if want_ctx:
            xc = _mlp(xc_new, mod_c, False, w1, w2, g2, b2, alpha)
    return x
```

```python
import functools

import numpy as np
import jax
import jax.numpy as jnp
from jax import lax
from jax.experimental import pallas as pl
from jax.experimental.pallas import tpu as pltpu

F32 = jnp.float32
MXU_DTYPE = jnp.bfloat16

GRID_W = 64
N_MOD = 6
N_MIXERS = 3
LN_EPS = 1e-6
ATT_HEAD_DIM = 128
ATT_KV_HEADS = 2
ROPE_BASE = 10000.0
GQA_KEY_CHUNK = 512
NA_HEAD_DIM = 64
NA_WIN_H = 8
NA_WIN_W = 16
NA_Q_ROWS = 32
RW_HEAD = 64
RW_GN_EPS = 64e-5
RW_CHUNK = 64
LANES = 128
NEG_BIG = -1e30
LOG2E = 1.4426950408889634

VMEM_LIMIT = 56 * 1024 * 1024

PROJ_ROWS = 512
MLP_ROWS = 1024
MLP_FF_TILE = 2048
GQA_Q_ROWS = 512
RW_OUT_ROWS = 256
RW_SCAN_CHUNKS = 4
MOD_COL_TILES = 4


def _cparams(*sem):
    return pltpu.CompilerParams(dimension_semantics=sem, vmem_limit_bytes=VMEM_LIMIT)


def _mm(a, b):
    return jnp.dot(a.astype(MXU_DTYPE), b.astype(MXU_DTYPE), preferred_element_type=F32)


def _mm_nt(a, b):
    return lax.dot_general(a.astype(MXU_DTYPE), b.astype(MXU_DTYPE), (((1,), (1,)), ((), ())),
                           preferred_element_type=F32)


def _mm_tn(a, b):
    return lax.dot_general(a.astype(MXU_DTYPE), b.astype(MXU_DTYPE), (((0,), (0,)), ((), ())),
                           preferred_element_type=F32)


def _split(a, n):
    parts = []
    for _ in range(n - 1):
        p = a.astype(MXU_DTYPE)
        parts.append(p)
        a = a - p.astype(F32)
    parts.append(a.astype(MXU_DTYPE))
    return parts


def _mm_rhs_exact(a, m01, n=2):
    m01 = m01.astype(MXU_DTYPE)
    return sum(jnp.dot(p, m01, preferred_element_type=F32) for p in _split(a, n))


def _mm_hi(a, b):
    a1, a2 = _split(a, 2)
    b1, b2 = _split(b, 2)
    d = functools.partial(jnp.dot, preferred_element_type=F32)
    return d(a1, b1) + d(a1, b2) + d(a2, b1)


def _sigmoid(x):
    return 1.0 / (1.0 + jnp.exp(-x))


def _layernorm(z, g, b):
    mu = jnp.mean(z, -1, keepdims=True)
    dz = z - mu
    var = jnp.mean(dz * dz, -1, keepdims=True)
    return dz * lax.rsqrt(var + LN_EPS) * g + b


def _mod_part(mod_ref, j, d):
    return mod_ref[0, :, j * d:(j + 1) * d]


def _head_ones(n=LANES):
    r = lax.broadcasted_iota(jnp.int32, (n, n), 0)
    c = lax.broadcasted_iota(jnp.int32, (n, n), 1)
    return ((r // RW_HEAD) == (c // RW_HEAD)).astype(F32)


def _modulation_kernel(c_ref, w_ref, b_ref, o_ref):
    c = c_ref[...]
    o_ref[0] = _mm_hi(c * _sigmoid(c), w_ref[0]) + b_ref[0]


def _modulation(cond, mod_w, mod_b):
    depth, d, n = mod_w.shape
    rows = cond.shape[0]
    tn = n // MOD_COL_TILES
    return pl.pallas_call(
        _modulation_kernel,
        grid=(depth, n // tn),
        in_specs=[pl.BlockSpec((rows, d), lambda i, j: (0, 0)),
                  pl.BlockSpec((1, d, tn), lambda i, j: (i, 0, j)),
                  pl.BlockSpec((1, 1, tn), lambda i, j: (i, 0, j))],
        out_specs=pl.BlockSpec((1, rows, tn), lambda i, j: (i, 0, j)),
        out_shape=jax.ShapeDtypeStruct((depth, rows, n), F32),
        compiler_params=_cparams("parallel", "parallel"),
        name="modulation",
    )(cond, mod_w, mod_b.reshape(depth, 1, n))


def _out_ln_kernel(o_ref, w_ref, x_ref, mod_ref, g_ref, b_ref, y_ref, *, d, alpha):
    y = _mm(o_ref[0], w_ref[...])
    z = alpha * x_ref[0] + _mod_part(mod_ref, 2, d) * y
    y_ref[0] = _layernorm(z, g_ref[...], b_ref[...])


def _mod_spec(d, per_batch):
    if per_batch:
        return pl.BlockSpec((1, 1, N_MOD * d), lambda b, *_: (b, 0, 0))
    return pl.BlockSpec((1, 1, N_MOD * d), lambda b, *_: (0, 0, 0))


def _out_ln(o, wo, x, mod, per_batch, ln_g, ln_b, alpha):
    bsz, t, d = x.shape
    tm = min(PROJ_ROWS, t)
    tok = lambda b, i: (b, i, 0)
    const = lambda b, i: (0, 0)
    return pl.pallas_call(
        functools.partial(_out_ln_kernel, d=d, alpha=alpha),
        grid=(bsz, t // tm),
        in_specs=[pl.BlockSpec((1, tm, d), tok), pl.BlockSpec((d, d), const), pl.BlockSpec((1, tm, d), tok),
                  _mod_spec(d, per_batch), pl.BlockSpec((1, d), const), pl.BlockSpec((1, d), const)],
        out_specs=pl.BlockSpec((1, tm, d), tok),
        out_shape=jax.ShapeDtypeStruct((bsz, t, d), F32),
        compiler_params=_cparams("parallel", "parallel"),
        name="out_ln",
    )(o, wo, x, mod, ln_g, ln_b)


def _mlp_kernel(x_ref, mod_ref, w1_ref, w2_ref, g_ref, b_ref, y_ref, h_scr, acc_scr, *, d, alpha):
    j = pl.program_id(2)

    @pl.when(j == 0)
    def _():
        h = x_ref[0] * (1.0 + _mod_part(mod_ref, 4, d)) + _mod_part(mod_ref, 3, d)
        h_scr[...] = h.astype(h_scr.dtype)
        acc_scr[...] = jnp.zeros_like(acc_scr)

    a = jnp.square(jnp.maximum(_mm(h_scr[...], w1_ref[...]), 0.0))
    acc_scr[...] += _mm(a, w2_ref[...])

    @pl.when(j == pl.num_programs(2) - 1)
    def _():
        z = alpha * x_ref[0] + _mod_part(mod_ref, 5, d) * acc_scr[...]
        y_ref[0] = _layernorm(z, g_ref[...], b_ref[...])


def _mlp(x, mod, per_batch, w1, w2, ln_g, ln_b, alpha):
    bsz, t, d = x.shape
    ff = w1.shape[1]
    tm = min(MLP_ROWS, t)
    tf = min(MLP_FF_TILE, ff)
    tok = lambda b, i, j: (b, i, 0)
    const = lambda b, i, j: (0, 0)
    return pl.pallas_call(
        functools.partial(_mlp_kernel, d=d, alpha=alpha),
        grid=(bsz, t // tm, ff // tf),
        in_specs=[pl.BlockSpec((1, tm, d), tok), _mod_spec(d, per_batch),
                  pl.BlockSpec((d, tf), lambda b, i, j: (0, j)), pl.BlockSpec((tf, d), lambda b, i, j: (j, 0)),
                  pl.BlockSpec((1, d), const), pl.BlockSpec((1, d), const)],
        out_specs=pl.BlockSpec((1, tm, d), tok),
        out_shape=jax.ShapeDtypeStruct((bsz, t, d), F32),
        scratch_shapes=[pltpu.VMEM((tm, d), MXU_DTYPE), pltpu.VMEM((tm, d), F32)],
        compiler_params=_cparams("parallel", "parallel", "arbitrary"),
        name="mlp",
    )(x, mod, w1, w2, ln_g, ln_b)


def _gqa_qkv_kernel(x_ref, mod_ref, w_ref, gq_ref, gk_ref, cos_ref, sin_ref, q_ref, k_ref, v_ref,
                    *, d, n_q, n_kv):
    h = x_ref[0] * (1.0 + _mod_part(mod_ref, 1, d)) + _mod_part(mod_ref, 0, d)
    acc = _mm(h, w_ref[...])
    cos, sin = cos_ref[...], sin_ref[...]
    hd = ATT_HEAD_DIM
    src = lax.broadcasted_iota(jnp.int32, (hd, hd), 0)
    dst = lax.broadcasted_iota(jnp.int32, (hd, hd), 1)
    first = (dst & (hd // 2 - 1)) < hd // 4
    rot = (jnp.where(first & (src == dst + hd // 4), -1.0, 0.0)
           + jnp.where(jnp.logical_not(first) & (src == dst - hd // 4), 1.0, 0.0))

    def norm_rope(u, g):
        u = u * lax.rsqrt(jnp.mean(u * u, -1, keepdims=True) + LN_EPS) * g
        return u * cos + _mm(u, rot) * sin

    for i in range(n_q):
        q_ref[0, :, i * hd:(i + 1) * hd] = norm_rope(acc[:, i * hd:(i + 1) * hd], gq_ref[...]).astype(q_ref.dtype)
    for i in range(n_kv):
        c0 = (n_q + i) * hd
        k_ref[0, :, i * hd:(i + 1) * hd] = norm_rope(acc[:, c0:c0 + hd], gk_ref[...]).astype(k_ref.dtype)
    ones = jnp.ones((acc.shape[0], hd), v_ref.dtype)
    for i in range(n_kv):
        c0 = (n_q + n_kv + i) * hd
        v_ref[0, :, 2 * i * hd:(2 * i + 1) * hd] = acc[:, c0:c0 + hd].astype(v_ref.dtype)
        v_ref[0, :, (2 * i + 1) * hd:(2 * i + 2) * hd] = ones


def _gqa_qkv(x, mod, per_batch, w, gq, gk, cos, sin):
    bsz, t, d = x.shape
    hd = ATT_HEAD_DIM
    n_kv = ATT_KV_HEADS
    n_q = w.shape[1] // hd - 2 * n_kv
    tm = min(PROJ_ROWS, t)
    tok = lambda b, i: (b, i, 0)
    const = lambda b, i: (0, 0)
    tab = pl.BlockSpec((tm, hd), lambda b, i: (i, 0))
    return pl.pallas_call(
        functools.partial(_gqa_qkv_kernel, d=d, n_q=n_q, n_kv=n_kv),
        grid=(bsz, t // tm),
        in_specs=[pl.BlockSpec((1, tm, d), tok), _mod_spec(d, per_batch), pl.BlockSpec(w.shape, const),
                  pl.BlockSpec((1, hd), const), pl.BlockSpec((1, hd), const), tab, tab],
        out_specs=[pl.BlockSpec((1, tm, n_q * hd), tok), pl.BlockSpec((1, tm, n_kv * hd), tok),
                   pl.BlockSpec((1, tm, 2 * n_kv * hd), tok)],
        out_shape=[jax.ShapeDtypeStruct((bsz, t, n_q * hd), MXU_DTYPE),
                   jax.ShapeDtypeStruct((bsz, t, n_kv * hd), MXU_DTYPE),
                   jax.ShapeDtypeStruct((bsz, t, 2 * n_kv * hd), MXU_DTYPE)],
        compiler_params=_cparams("parallel", "parallel"),
        name="gqa_qkv",
    )(x, mod, w, gq, gk, cos, sin)


def _gqa_att_kernel(*refs, n_sets, groups, tk):
    q_ref, o_ref = refs[0], refs[-1]
    kv = refs[1:-1]
    hd = ATT_HEAD_DIM
    tq = q_ref.shape[1]
    q = jnp.concatenate([q_ref[0, :, g * hd:(g + 1) * hd] for g in range(groups)], axis=0)
    m = acc = None
    for j in range(n_sets):
        k_ref, v_ref = kv[2 * j], kv[2 * j + 1]
        t = k_ref.shape[1]
        for c0 in range(0, t, tk):
            c1 = min(c0 + tk, t)
            s = _mm_nt(q, k_ref[0, c0:c1, :])
            m_c = jnp.max(s, -1, keepdims=True)
            if m is None:
                m = m_c
                acc = _mm(jnp.exp2(s - m), v_ref[0, c0:c1, :])
            else:
                m_new = jnp.maximum(m, m_c)
                acc = acc * jnp.exp2(m - m_new) + _mm(jnp.exp2(s - m_new), v_ref[0, c0:c1, :])
                m = m_new
    o = acc[:, :hd] / acc[:, hd:]
    for g in range(groups):
        o_ref[0, :, g * hd:(g + 1) * hd] = o[g * tq:(g + 1) * tq].astype(o_ref.dtype)


def _gqa_att(q, kv_sets):
    bsz, s, dq = q.shape
    hd = ATT_HEAD_DIM
    n_kv = ATT_KV_HEADS
    groups = dq // hd // n_kv
    tq = min(GQA_Q_ROWS, s)
    in_specs = [pl.BlockSpec((1, tq, groups * hd), lambda b, kh, i: (b, i, kh))]
    args = [q]
    for k, v in kv_sets:
        t = k.shape[1]
        in_specs += [pl.BlockSpec((1, t, hd), lambda b, kh, i: (b, 0, kh)),
                     pl.BlockSpec((1, t, 2 * hd), lambda b, kh, i: (b, 0, kh))]
        args += [k, v]
    return pl.pallas_call(
        functools.partial(_gqa_att_kernel, n_sets=len(kv_sets), groups=groups, tk=GQA_KEY_CHUNK),
        grid=(bsz, n_kv, s // tq),
        in_specs=in_specs,
        out_specs=pl.BlockSpec((1, tq, groups * hd), lambda b, kh, i: (b, i, kh)),
        out_shape=jax.ShapeDtypeStruct((bsz, s, dq), MXU_DTYPE),
        compiler_params=_cparams("parallel", "parallel", "parallel"),
        name="gqa_att",
    )(*args)


def _rope_tables(n_tokens, head_dim):
    t = np.arange(n_tokens)
    row = (t // GRID_W).astype(np.float32)
    col = (t % GRID_W).astype(np.float32)
    half = head_dim // 2
    freqs = jnp.asarray(ROPE_BASE, F32) ** (-jnp.arange(0, half, 2, dtype=F32) / half)
    ang_r = jnp.asarray(row)[:, None] * freqs[None, :]
    ang_c = jnp.asarray(col)[:, None] * freqs[None, :]
    ang = jnp.concatenate([ang_r, ang_r, ang_c, ang_c], axis=-1)
    return jnp.cos(ang), jnp.sin(ang)


def _gqa_layer(x, xc, mod, mod_c, wqkv, wo, q_norm, k_norm, want_ctx):
    s, tc = x.shape[1], xc.shape[1]
    hd = ATT_HEAD_DIM
    gq = (q_norm * (hd ** -0.5 * LOG2E)).reshape(1, hd)
    gk = k_norm.reshape(1, hd)
    cos, sin = _rope_tables(s, hd)
    one, zero = jnp.ones((tc, hd), F32), jnp.zeros((tc, hd), F32)
    q, k, v = _gqa_qkv(x, mod, True, wqkv, gq, gk, cos, sin)
    qc, kc, vc = _gqa_qkv(xc, mod_c, False, wqkv, gq, gk, one, zero)
    o = _gqa_att(q, [(k, v), (kc, vc)])
    oc = _gqa_att(qc, [(kc, vc)]) if want_ctx else None
    return o, oc, wo


def _na_qkv_kernel(x_ref, mod_ref, w_ref, q_ref, k_ref, v_ref, *, d, scale):
    h = x_ref[0] * (1.0 + _mod_part(mod_ref, 1, d)) + _mod_part(mod_ref, 0, d)
    acc = _mm(h, w_ref[...])
    q_ref[0] = (acc[:, :d] * scale).astype(q_ref.dtype)
    k_ref[0] = acc[:, d:2 * d].astype(k_ref.dtype)
    ones = jnp.ones((acc.shape[0], LANES), v_ref.dtype)
    for p in range(d // LANES):
        c0 = 2 * d + p * LANES
        v_ref[0, :, 2 * p * LANES:(2 * p + 1) * LANES] = acc[:, c0:c0 + LANES].astype(v_ref.dtype)
        v_ref[0, :, (2 * p + 1) * LANES:(2 * p + 2) * LANES] = ones


def _na_qkv(x, mod, per_batch, w):
    bsz, t, d = x.shape
    tm = min(PROJ_ROWS, t)
    tok = lambda b, i: (b, i, 0)
    out = jax.ShapeDtypeStruct((bsz, t, d), MXU_DTYPE)
    return pl.pallas_call(
        functools.partial(_na_qkv_kernel, d=d, scale=NA_HEAD_DIM ** -0.5 * LOG2E),
        grid=(bsz, t // tm),
        in_specs=[pl.BlockSpec((1, tm, d), tok), _mod_spec(d, per_batch), pl.BlockSpec(w.shape, lambda b, i: (0, 0))],
        out_specs=[pl.BlockSpec((1, tm, d), tok)] * 2 + [pl.BlockSpec((1, tm, 2 * d), tok)],
        out_shape=[out, out, jax.ShapeDtypeStruct((bsz, t, 2 * d), MXU_DTYPE)],
        compiler_params=_cparams("parallel", "parallel"),
        name="na_qkv",
    )(x, mod, w)


def _pair_softmax_att(q, ks, vs, biases):
    lane = lax.broadcasted_iota(jnp.int32, q.shape, 1)
    first = lane < NA_HEAD_DIM
    zero = jnp.zeros_like(q)
    qm = [jnp.where(first, q, zero), jnp.where(first, zero, q)]
    m, acc = [None, None], [None, None]
    for j, (k, v) in enumerate(zip(ks, vs)):
        for h in range(2):
            s = _mm_nt(qm[h], k)
            if biases[h][j] is not None:
                s = s + biases[h][j]
            m_c = jnp.max(s, -1, keepdims=True)
            if m[h] is None:
                m[h] = m_c
                acc[h] = _mm(jnp.exp2(s - m_c), v)
            else:
                m_new = jnp.maximum(m[h], m_c)
                acc[h] = acc[h] * jnp.exp2(m[h] - m_new) + _mm(jnp.exp2(s - m_new), v)
                m[h] = m_new
    outs = [a[:, :LANES] / a[:, LANES:] for a in acc]
    return jnp.where(first, outs[0], outs[1])


def _na_att_kernel(q_ref, k_ref, v_ref, kc_ref, vc_ref, bias_ref, o_ref, *, rows):
    i = pl.program_id(2)
    q_rows = q_ref.shape[1] // GRID_W
    win = NA_WIN_H * GRID_W
    w = GRID_W
    first = lax.broadcasted_iota(jnp.int32, (w, LANES), 1) < NA_HEAD_DIM
    qm, kw, vw, bias = [], [], [], []
    for rq in range(q_rows):
        row_q = i * q_rows + rq
        r0 = jnp.clip(row_q - NA_WIN_H // 2, 0, rows - NA_WIN_H)
        start = pl.multiple_of(r0 * w, w)
        q = q_ref[0, rq * w:(rq + 1) * w, :]
        zero = jnp.zeros_like(q)
        qm.append(jnp.concatenate([jnp.where(first, q, zero), jnp.where(first, zero, q)], axis=0))
        kw.append(k_ref[0, pl.ds(start, win), :])
        vw.append(v_ref[0, pl.ds(start, win), :])
        e = r0 - row_q + NA_WIN_H - 1
        bias.append(jnp.concatenate([bias_ref[0, 0, e], bias_ref[0, 1, e]], axis=0))
    each = lambda fn, *lists: [fn(*args) for args in zip(*lists)]
    s_ctx = _mm_nt(jnp.concatenate(qm, axis=0), kc_ref[0])
    s_ctx = [s_ctx[2 * w * rq:2 * w * (rq + 1)] for rq in range(q_rows)]
    s_loc = each(lambda a, b, c: _mm_nt(a, b) + c, qm, kw, bias)
    m = each(lambda a, b: jnp.maximum(jnp.max(a, -1, keepdims=True), jnp.max(b, -1, keepdims=True)), s_loc, s_ctx)
    acc_ctx = _mm(jnp.concatenate(each(lambda a, mm: jnp.exp2(a - mm), s_ctx, m), axis=0), vc_ref[0])
    acc = each(lambda a, mm, v: _mm(jnp.exp2(a - mm), v), s_loc, m, vw)
    for rq in range(q_rows):
        a = acc[rq] + acc_ctx[2 * w * rq:2 * w * (rq + 1)]
        o = a[:, :LANES] / a[:, LANES:]
        o_ref[0, rq * w:(rq + 1) * w, :] = jnp.where(first, o[:w], o[w:]).astype(o_ref.dtype)


def _na_bias_tables(rpb, scale):
    pad = GRID_W
    rpb_p = jnp.pad(rpb, ((0, 0), (0, 0), (pad, pad)))
    cmat = jnp.stack([rpb_p[:, :, pad + NA_WIN_W - 1 - cq: pad + NA_WIN_W - 1 - cq + GRID_W]
                      for cq in range(GRID_W)], axis=2)
    cq, ck = np.arange(GRID_W)[:, None], np.arange(GRID_W)[None, :]
    c0 = np.clip(cq - NA_WIN_W // 2, 0, GRID_W - NA_WIN_W)
    vcol = (ck >= c0) & (ck < c0 + NA_WIN_W)
    cmat = jnp.where(vcol[None, None], cmat * scale, NEG_BIG)
    return jnp.stack([jnp.concatenate([cmat[:, e + j] for j in range(NA_WIN_H)], axis=-1)
                      for e in range(NA_WIN_H)], axis=1)


def _na_att(q, k, v, kc, vc, bias):
    bsz, s, d = q.shape
    tc = kc.shape[1]
    rows = s // GRID_W
    q_rows = min(NA_Q_ROWS, rows)
    assert rows % q_rows == 0 and rows >= NA_WIN_H
    tq = q_rows * GRID_W
    pairs = d // LANES
    q_spec = pl.BlockSpec((1, tq, LANES), lambda hp, b, i: (b, i, hp))
    whole = lambda t, width: pl.BlockSpec((1, t, width), lambda hp, b, i: (b, 0, hp))
    bias_spec = pl.BlockSpec((1, 2) + bias.shape[1:], lambda hp, b, i: (0, hp, 0, 0, 0))
    return pl.pallas_call(
        functools.partial(_na_att_kernel, rows=rows),
        grid=(pairs, bsz, rows // q_rows),
        in_specs=[q_spec, whole(s, LANES), whole(s, 2 * LANES), whole(tc, LANES), whole(tc, 2 * LANES), bias_spec],
        out_specs=q_spec,
        out_shape=jax.ShapeDtypeStruct((bsz, s, d), MXU_DTYPE),
        compiler_params=_cparams("parallel", "parallel", "parallel"),
        name="na_att",
    )(q, k, v, kc, vc, bias[None])


def _pair_att_kernel(q_ref, k_ref, v_ref, o_ref):
    o_ref[0] = _pair_softmax_att(q_ref[0], [k_ref[0]], [v_ref[0]], [[None], [None]]).astype(o_ref.dtype)


def _pair_att(q, k, v):
    bsz, t, d = q.shape
    spec = pl.BlockSpec((1, t, LANES), lambda b, hp: (b, 0, hp))
    return pl.pallas_call(
        _pair_att_kernel,
        grid=(bsz, d // LANES),
        in_specs=[spec, spec, pl.BlockSpec((1, t, 2 * LANES), lambda b, hp: (b, 0, hp))],
        out_specs=spec,
        out_shape=jax.ShapeDtypeStruct((bsz, t, d), MXU_DTYPE),
        compiler_params=_cparams("parallel", "parallel"),
        name="na_ctx_att",
    )(q, k, v)


def _na_layer(x, xc, mod, mod_c, wqkv, wo, rpb, want_ctx):
    q, k, v = _na_qkv(x, mod, True, wqkv)
    qc, kc, vc = _na_qkv(xc, mod_c, False, wqkv)
    o = _na_att(q, k, v, kc, vc, _na_bias_tables(rpb, LOG2E))
    oc = _pair_att(qc, kc, vc) if want_ctx else None
    return o, oc, wo


def _rw_proj_kernel(x_ref, xp_ref, xn_ref, mod_ref, mu_ref, wr_ref, wk_ref, wv_ref, w1_ref, w2_ref, a1_ref, a2_ref,
                    g1_ref, g2_ref, w0_ref, a0_ref, r_ref, k_ref, v_ref, g_ref, lw_ref, as_ref, *, d):
    t = pl.program_id(1)
    nt = pl.num_programs(1)
    scale = 1.0 + _mod_part(mod_ref, 1, d)
    shift = _mod_part(mod_ref, 0, d)
    h = x_ref[0] * scale + shift
    tm = h.shape[0]
    h_prev = (xp_ref[0, 7:8, :] * scale + shift) * (t > 0).astype(F32)
    h_next = (xn_ref[0, 0:1, :] * scale + shift) * (t < nt - 1).astype(F32)
    row = lax.broadcasted_iota(jnp.int32, h.shape, 0)
    prev = jnp.where(row == 0, h_prev, pltpu.roll(h, 1, 0))
    nxt = jnp.where(row == tm - 1, h_next, pltpu.roll(h, tm - 1, 0))
    xx = 0.5 * (prev + nxt) - h
    mix = lambda j: h + xx * mu_ref[j:j + 1, :]

    r_ref[0] = _mm(mix(0), wr_ref[...]).astype(r_ref.dtype)
    k_ref[0] = _mm(mix(2), wk_ref[...]).astype(k_ref.dtype)
    v_ref[0] = _mm(mix(3), wv_ref[...]).astype(v_ref.dtype)
    g_ref[0] = _mm(_sigmoid(_mm(mix(5), g1_ref[...])), g2_ref[...]).astype(g_ref.dtype)

    tw = jnp.tanh(_mm(mix(1), w1_ref[...]))
    al = _mm(mix(4), a1_ref[...])
    first = lax.broadcasted_iota(jnp.int32, tw.shape, 1) < tw.shape[1] // 2
    zero = jnp.zeros_like(tw)
    for n in range(2):
        pick = lambda u: jnp.where(first, u, zero) if n == 0 else jnp.where(first, zero, u)
        z = -(w0_ref[n:n + 1, :] + _mm(pick(tw), w2_ref[...]))
        softplus = jnp.maximum(z, 0.0) + jnp.log(1.0 + jnp.exp(-jnp.abs(z)))
        lw_ref[n, 0] = -jnp.exp(-softplus - 0.5)
        as_ref[n, 0] = _sigmoid(a0_ref[n:n + 1, :] + _mm(pick(al), a2_ref[...])).astype(as_ref.dtype)


def _rw_proj(x, mod, per_batch, p):
    bsz, t, d = x.shape
    tm = min(PROJ_ROWS, t)
    n8 = t // 8
    tok = lambda b, i: (b, i, 0)
    dtok = lambda b, i: (0, b, i, 0)
    full = lambda a: pl.BlockSpec(a.shape, lambda b, i: (0,) * a.ndim)
    halo_prev = pl.BlockSpec((1, 8, d), lambda b, i: (b, jnp.maximum(i * (tm // 8) - 1, 0), 0))
    halo_next = pl.BlockSpec((1, 8, d), lambda b, i: (b, jnp.minimum((i + 1) * (tm // 8), n8 - 1), 0))
    weights = [p['mu'], p['wr'], p['wk'], p['wv'], p['w1'], p['w2'], p['a1'], p['a2'], p['g1'], p['g2'],
               p['w0'], p['a0']]
    one = jax.ShapeDtypeStruct((bsz, t, d), MXU_DTYPE)
    two = lambda dtype: jax.ShapeDtypeStruct((2, bsz, t, d), dtype)
    return pl.pallas_call(
        functools.partial(_rw_proj_kernel, d=d),
        grid=(bsz, t // tm),
        in_specs=[pl.BlockSpec((1, tm, d), tok), halo_prev, halo_next, _mod_spec(d, per_batch)]
                 + [full(a) for a in weights],
        out_specs=[pl.BlockSpec((1, tm, d), tok)] * 4 + [pl.BlockSpec((2, 1, tm, d), dtok)] * 2,
        out_shape=[one, one, one, one, two(F32), two(MXU_DTYPE)],
        compiler_params=_cparams("parallel", "parallel"),
        name="rw_proj",
    )(x, x, x, mod, *weights)


def _rw_scan_kernel(r_ref, k_ref, v_ref, lw_ref, as_ref, kk_ref, ka_ref, z0_ref, y_ref, zf_ref, z_scr,
                    *, pairs, sub, reverse):
    c = pl.program_id(1)
    L = RW_CHUNK
    H = RW_HEAD
    assert 2 * L == LANES and 2 * H == LANES

    @pl.when(c == 0)
    def _():
        z_scr[...] = z0_ref[0, 0]

    sign = -1 if reverse else 1
    t_i = lax.broadcasted_iota(jnp.int32, (L, 2 * L), 0)
    s_i = lax.broadcasted_iota(jnp.int32, (L, 2 * L), 1) & (L - 1)
    dt = sign * (s_i - t_i)
    strict_c = dt < 0
    incl_c = dt <= 0
    tri2 = incl_c.astype(F32)
    eye_c = (dt == 0).astype(F32)
    r2 = lax.broadcasted_iota(jnp.int32, (LANES, LANES), 0)
    c2 = lax.broadcasted_iota(jnp.int32, (LANES, LANES), 1)
    eye2 = (r2 == c2).astype(F32)
    same_head = _head_ones()
    first = lax.broadcasted_iota(jnp.int32, (L, LANES), 1) < H
    zero = jnp.zeros((L, LANES), F32)

    head0 = lambda u: jnp.where(first, u, zero)
    head1 = lambda u: jnp.where(first, zero, u)
    bd = lambda u: jnp.concatenate([head0(u), head1(u)], axis=0)

    each = lambda fn, *lists: [fn(*args) for args in zip(*lists)]
    sls = [slice(p * LANES, (p + 1) * LANES) for p in range(pairs)]
    cat0 = lambda *u: jnp.concatenate(u, axis=0)
    cat1 = lambda *u: jnp.concatenate(u, axis=1)

    mm_each = lambda lhs, w: [_mm(a, b) for a, b in zip(lhs, w)]

    def chunk_terms(rw):
        lw = [lw_ref[0, 0, rw, sl] for sl in sls]
        r = [r_ref[0, rw, sl].astype(F32) for sl in sls]
        k = [k_ref[0, rw, sl].astype(F32) for sl in sls]
        v = [v_ref[0, rw, sl].astype(F32) for sl in sls]
        a_s = [as_ref[0, 0, rw, sl].astype(F32) for sl in sls]

        c_in = each(lambda u: _mm(tri2, cat0(*_split(u, 2))), lw)
        c_all = each(lambda u: jnp.sum(u, axis=0, keepdims=True), lw)
        kk = each(lambda u, sl: u * kk_ref[:, sl], k, sls)
        ss = each(lambda u: _mm(u * u, same_head), kk)
        kk = each(lambda u, s: u * lax.rsqrt(jnp.maximum(s, 1e-12)), kk, ss)
        b_v = each(lambda u, a: u * a, kk, a_s)
        k_d = each(lambda u, a, sl: u * (1.0 + (a - 1.0) * ka_ref[:, sl]), k, a_s, sls)
        e_neg = each(lambda ci: jnp.exp(-ci), c_in)
        e_rem = each(lambda ci, ca: jnp.exp(ca - ci), c_in, c_all)
        a_t = each(lambda u, ci, l: -u * jnp.exp(ci - l), kk, c_in, lw)
        r_t = each(lambda u, ci: u * jnp.exp(ci), r, c_in)
        b_t = each(lambda u, e: u * e, b_v, e_neg)
        k_t = each(lambda u, e: u * e, k_d, e_neg)
        b_h = each(lambda u, e: u * e, b_v, e_rem)
        k_h = each(lambda u, e: u * e, k_d, e_rem)
        yield None

        x = each(lambda a, rr, b, kt: _mm_nt(cat0(a, rr), cat0(bd(b), bd(kt))), a_t, r_t, b_t, k_t)
        m_ab = each(lambda u: jnp.where(strict_c, u[:L, :2 * L], 0.0), x)
        m_ak = each(lambda u: jnp.where(strict_c, u[:L, 2 * L:], 0.0), x)
        n_rb = each(lambda u: jnp.where(incl_c, u[L:, :2 * L], 0.0), x)
        n_rk = each(lambda u: jnp.where(incl_c, u[L:, 2 * L:], 0.0), x)

        inv = each(lambda m: eye_c + m, m_ab)
        yield None
        pw = mm_each(m_ab, each(bd, m_ab))
        for _ in range(int(np.log2(L)) - 2):
            yield None
            st = mm_each(each(cat0, pw, inv), each(bd, pw))
            pw = each(lambda s: s[:L], st)
            inv = each(lambda b, s: b + s[L:], inv, st)
        yield None
        inv = each(lambda b, s: b + s, inv, mm_each(inv, each(bd, pw)))
        mv = mm_each(each(cat0, m_ak, n_rk), each(bd, v))
        yield None
        tw = each(lambda i, a, m: _mm(i, cat1(bd(a), bd(m[:L]))), inv, a_t, mv)
        p1 = each(lambda u: u[:, :LANES], tw)
        p2 = each(lambda u: u[:, LANES:], tw)
        yield None
        nw = each(lambda m, a, b: _mm(m, cat1(bd(a), bd(b))), n_rb, p1, p2)
        q1 = each(lambda rr, u: rr + u[:, :LANES], r_t, nw)
        q2 = each(lambda u, m: u[:, LANES:] + m[L:], nw, mv)
        gh = each(lambda b, kh, a, c_, u: _mm_tn(cat0(b, kh), cat0(cat1(a, c_), cat1(zero, u))),
                  b_h, k_h, p1, p2, v)
        g_t = each(lambda ca, u: eye2 * jnp.exp(ca) + same_head * u[:, :LANES], c_all, gh)
        h_t = each(lambda u: same_head * u[:, LANES:], gh)
        yield q1, q2, g_t, h_t

    chunk_rows = [slice(u * L, (u + 1) * L) for u in (range(sub - 1, -1, -1) if reverse else range(sub))]
    gens = [chunk_terms(rw) for rw in chunk_rows]
    terms = [None] * sub
    while any(t is None for t in terms):
        terms = [next(gen) for gen in gens]
    z = [z_scr[p] for p in range(pairs)]
    for rw, (q1, q2, g_t, h_t) in zip(chunk_rows, terms):
        yz = mm_each(each(cat0, q1, g_t), z)
        for p in range(pairs):
            y_ref[0, rw, sls[p]] = (yz[p][:L] + q2[p]).astype(y_ref.dtype)
            z[p] = yz[p][L:] + h_t[p]
    for p in range(pairs):
        z_scr[p] = z[p]

    @pl.when(c == pl.num_programs(1) - 1)
    def _():
        zf_ref[0] = z_scr[...]


def _rw_scan(r, k, v, lw, a_s, kk_w, ka_w, z0, reverse):
    bsz, t, d = r.shape
    sub = min(RW_SCAN_CHUNKS, t // RW_CHUNK)
    rows = sub * RW_CHUNK
    assert t % rows == 0
    n_c = t // rows
    n = int(reverse)
    pairs = d // LANES
    cidx = (lambda c: n_c - 1 - c) if reverse else (lambda c: c)
    tok = pl.BlockSpec((1, rows, d), lambda b, c: (b, cidx(c), 0))
    dtok = pl.BlockSpec((1, 1, rows, d), lambda b, c: (n, b, cidx(c), 0))
    vec = pl.BlockSpec((1, d), lambda b, c: (0, 0))
    return pl.pallas_call(
        functools.partial(_rw_scan_kernel, pairs=pairs, sub=sub, reverse=reverse),
        grid=(bsz, n_c),
        in_specs=[tok, tok, tok, dtok, dtok, vec, vec,
                  pl.BlockSpec((1, 1, pairs, LANES, LANES), lambda b, c: (n, b, 0, 0, 0))],
        out_specs=[tok, pl.BlockSpec((1, pairs, LANES, LANES), lambda b, c: (b, 0, 0, 0))],
        out_shape=[jax.ShapeDtypeStruct((bsz, t, d), MXU_DTYPE),
                   jax.ShapeDtypeStruct((bsz, pairs, LANES, LANES), F32)],
        scratch_shapes=[pltpu.VMEM((pairs, LANES, LANES), F32)],
        compiler_params=_cparams("parallel", "arbitrary"),
        name="rw_scan",
    )(r, k, v, lw, a_s, kk_w, ka_w, z0)


def _rw_out_kernel(yf_ref, yb_ref, r_ref, k_ref, v_ref, g_ref, as_ref, ka_ref, rk_ref, lg_ref, lb_ref, wo_ref, x_ref,
                   mod_ref, g_ln_ref, b_ln_ref, o_ref, u_scr, *, d, alpha):
    width = 2 * LANES
    same_head = _head_ones(width)
    inv_n = 1.0 / RW_HEAD
    for p in range(d // width):
        sl = slice(p * width, (p + 1) * width)
        y = yf_ref[0, :, sl].astype(F32) + yb_ref[0, :, sl].astype(F32)
        mu = _mm_rhs_exact(y, same_head) * inv_n
        dy = y - mu
        var = _mm(dy * dy, same_head) * inv_n
        yn = dy * lax.rsqrt(var + RW_GN_EPS) * lg_ref[:, sl] + lb_ref[:, sl]
        k = k_ref[0, :, sl].astype(F32)
        ka = ka_ref[:, sl]
        k_sum = k * ((1.0 + (as_ref[0, 0, :, sl].astype(F32) - 1.0) * ka)
                     + (1.0 + (as_ref[1, 0, :, sl].astype(F32) - 1.0) * ka))
        bonus = _mm(r_ref[0, :, sl].astype(F32) * k_sum * rk_ref[:, sl], same_head)
        u = (yn + bonus * v_ref[0, :, sl].astype(F32)) * g_ref[0, :, sl].astype(F32)
        u_scr[:, sl] = u.astype(u_scr.dtype)
    z = alpha * x_ref[0] + _mod_part(mod_ref, 2, d) * _mm(u_scr[...], wo_ref[...])
    o_ref[0] = _layernorm(z, g_ln_ref[...], b_ln_ref[...])


def _rw_out(y_f, y_b, r, k, v, g, a_s, p, x, mod, per_batch, ln_g, ln_b, alpha):
    bsz, t, d = x.shape
    tm = min(RW_OUT_ROWS, t)
    tok = pl.BlockSpec((1, tm, d), lambda b, i: (b, i, 0))
    dtok = pl.BlockSpec((2, 1, tm, d), lambda b, i: (0, b, i, 0))
    vec = pl.BlockSpec((1, d), lambda b, i: (0, 0))
    return pl.pallas_call(
        functools.partial(_rw_out_kernel, d=d, alpha=alpha),
        grid=(bsz, t // tm),
        in_specs=[tok, tok, tok, tok, tok, tok, dtok, vec, vec, vec, vec, pl.BlockSpec((d, d), lambda b, i: (0, 0)),
                  tok, _mod_spec(d, per_batch), vec, vec],
        out_specs=tok,
        out_shape=jax.ShapeDtypeStruct((bsz, t, d), F32),
        scratch_shapes=[pltpu.VMEM((tm, d), MXU_DTYPE)],
        compiler_params=_cparams("parallel", "parallel"),
        name="rw_out",
    )(y_f, y_b, r, k, v, g, a_s, p['k_a'], p['r_k'], p['lnx_g'], p['lnx_b'], p['wo'], x, mod, ln_g, ln_b)


def _rw_layer(x, xc, mod, mod_c, p, want_ctx, ln_g, ln_b, alpha):
    bsz, _, d = x.shape
    rc, kc, vc, gc, lwc, asc = _rw_proj(xc, mod_c, False, p)
    r, k, v, g, lw, a_s = _rw_proj(x, mod, True, p)
    z0 = jnp.zeros((2, bsz, d // LANES, LANES, LANES), F32)
    scan = lambda args, z, rev: _rw_scan(*args, p['k_k'], p['k_a'], z, rev)
    yc_f, zc_f = scan((rc, kc, vc, lwc, asc), z0, False)
    yc_b, zc_b = scan((rc, kc, vc, lwc, asc), z0, True)
    zc = jnp.stack([zc_f, zc_b])
    y_f, _ = scan((r, k, v, lw, a_s), zc, False)
    y_b, _ = scan((r, k, v, lw, a_s), zc, True)
    x_new = _rw_out(y_f, y_b, r, k, v, g, a_s, p, x, mod, True, ln_g, ln_b, alpha)
    xc_new = (_rw_out(yc_f, yc_b, rc, kc, vc, gc, asc, p, xc, mod_c, False, ln_g, ln_b, alpha)
              if want_ctx else None)
    return x_new, xc_new


def kernel(x, c, ctx, c_ctx, mod_w, mod_b, post_ln_g, post_ln_b, mlp_w1, mlp_w2, att_wqkv, att_wo, att_q_norm, att_k_norm, na_wqkv, na_wo, na_rpb, rw_mu, rw_wr, rw_wk, rw_wv, rw_wo, rw_w0, rw_w1, rw_w2, rw_a0, rw_a1, rw_a2, rw_g1, rw_g2, rw_k_k, rw_k_a, rw_r_k, rw_lnx_g, rw_lnx_b):
    depth = mod_w.shape[0]
    bsz, _, d = x.shape
    alpha = (2.0 * depth) ** 0.25
    cast = lambda a: a.astype(MXU_DTYPE)

    cond_rows = -(-(bsz + 1) // 8) * 8
    cond = jnp.zeros((cond_rows, d), F32).at[:bsz].set(c).at[bsz].set(c_ctx)
    mods = _modulation(cond, mod_w, mod_b)

    xc = ctx
    for i in range(depth):
        kind, slot = i % N_MIXERS, i // N_MIXERS
        want_ctx = i < depth - 1
        mod = mods[i, :bsz].reshape(bsz, 1, N_MOD * d)
        mod_c = mods[i, bsz:bsz + 1].reshape(1, 1, N_MOD * d)
        g1, b1 = post_ln_g[i, 0:1], post_ln_b[i, 0:1]
        g2, b2 = post_ln_g[i, 1:2], post_ln_b[i, 1:2]
        if kind == 2:
            cat = lambda a: jnp.concatenate([a[0], a[1]], axis=-1)
            p = {'mu': rw_mu[slot], 'wr': cast(rw_wr[slot]), 'wk': cast(rw_wk[slot]), 'wv': cast(rw_wv[slot]),
                 'wo': cast(rw_wo[slot]), 'w0': rw_w0[slot], 'a0': rw_a0[slot],
                 'w1': cast(cat(rw_w1[slot])), 'a1': cast(cat(rw_a1[slot])),
                 'w2': cast(rw_w2[slot].reshape(-1, d)), 'a2': cast(rw_a2[slot].reshape(-1, d)),
                 'g1': cast(rw_g1[slot]), 'g2': cast(rw_g2[slot]),
                 'k_k': rw_k_k[slot].reshape(1, d), 'k_a': rw_k_a[slot].reshape(1, d),
                 'r_k': rw_r_k[slot].reshape(1, d), 'lnx_g': rw_lnx_g[slot].reshape(1, d),
                 'lnx_b': rw_lnx_b[slot].reshape(1, d)}
            x, xc_new = _rw_layer(x, xc, mod, mod_c, p, want_ctx, g1, b1, alpha)
        else:
            if kind == 0:
                o, oc, wo = _gqa_layer(x, xc, mod, mod_c, cast(att_wqkv[slot]), cast(att_wo[slot]),
                                       att_q_norm[slot], att_k_norm[slot], want_ctx)
            else:
                o, oc, wo = _na_layer(x, xc, mod, mod_c, cast(na_wqkv[slot]), cast(na_wo[slot]), na_rpb[slot],
                                      want_ctx)
            x = _out_ln(o, wo, x, mod, True, g1, b1, alpha)
            xc_new = _out_ln(oc, wo, xc, mod_c, False, g1, b1, alpha) if want_ctx else None
        w1, w2 = cast(mlp_w1[i]), cast(mlp_w2[i])
        x = _mlp(x, mod, True, w1, w2, g2, b2, alpha)
        if want_ctx:
            xc = _mlp(xc_new, mod_c, False, w1, w2, g2, b2, alpha)
    return x
```

```python
import functools

import numpy as np
import jax
import jax.numpy as jnp
from jax import lax
from jax.experimental import pallas as pl
from jax.experimental.pallas import tpu as pltpu

F32 = jnp.float32
MXU_DTYPE = jnp.bfloat16

GRID_W = 64
N_MOD = 6
N_MIXERS = 3
LN_EPS = 1e-6
ATT_HEAD_DIM = 128
ATT_KV_HEADS = 2
ROPE_BASE = 10000.0
GQA_KEY_CHUNK = 512
NA_HEAD_DIM = 64
NA_WIN_H = 8
NA_WIN_W = 16
NA_Q_ROWS = 64
RW_HEAD = 64
RW_GN_EPS = 64e-5
RW_CHUNK = 64
LANES = 128
NEG_BIG = -1e30
LOG2E = 1.4426950408889634

VMEM_LIMIT = 56 * 1024 * 1024

PROJ_ROWS = 512
MLP_ROWS = 1024
MLP_FF_TILE = 2048
GQA_Q_ROWS = 512
RW_OUT_ROWS = 512
RW_SCAN_CHUNKS = 8
MOD_COL_TILES = 4


def _cparams(*sem):
    return pltpu.CompilerParams(dimension_semantics=sem, vmem_limit_bytes=VMEM_LIMIT)


def _mm(a, b):
    return jnp.dot(a.astype(MXU_DTYPE), b.astype(MXU_DTYPE), preferred_element_type=F32)


def _mm_nt(a, b):
    return lax.dot_general(a.astype(MXU_DTYPE), b.astype(MXU_DTYPE), (((1,), (1,)), ((), ())),
                           preferred_element_type=F32)


def _mm_tn(a, b):
    return lax.dot_general(a.astype(MXU_DTYPE), b.astype(MXU_DTYPE), (((0,), (0,)), ((), ())),
                           preferred_element_type=F32)


def _split(a, n):
    parts = []
    for _ in range(n - 1):
        p = a.astype(MXU_DTYPE)
        parts.append(p)
        a = a - p.astype(F32)
    parts.append(a.astype(MXU_DTYPE))
    return parts


def _mm_rhs_exact(a, m01, n=2):
    m01 = m01.astype(MXU_DTYPE)
    return sum(jnp.dot(p, m01, preferred_element_type=F32) for p in _split(a, n))


def _mm_hi(a, b):
    a1, a2 = _split(a, 2)
    b1, b2 = _split(b, 2)
    d = functools.partial(jnp.dot, preferred_element_type=F32)
    return d(a1, b1) + d(a1, b2) + d(a2, b1)


def _sigmoid(x):
    return 1.0 / (1.0 + jnp.exp(-x))


def _layernorm(z, g, b):
    mu = jnp.mean(z, -1, keepdims=True)
    dz = z - mu
    var = jnp.mean(dz * dz, -1, keepdims=True)
    return dz * lax.rsqrt(var + LN_EPS) * g + b


def _mod_part(mod_ref, j, d):
    return mod_ref[0, :, j * d:(j + 1) * d]


def _head_ones(n=LANES):
    r = lax.broadcasted_iota(jnp.int32, (n, n), 0)
    c = lax.broadcasted_iota(jnp.int32, (n, n), 1)
    return ((r // RW_HEAD) == (c // RW_HEAD)).astype(F32)


def _modulation_kernel(c_ref, w_ref, b_ref, o_ref):
    c = c_ref[...]
    o_ref[0] = _mm_hi(c * _sigmoid(c), w_ref[0]) + b_ref[0]


def _modulation(cond, mod_w, mod_b):
    depth, d, n = mod_w.shape
    rows = cond.shape[0]
    tn = n // MOD_COL_TILES
    return pl.pallas_call(
        _modulation_kernel,
        grid=(depth, n // tn),
        in_specs=[pl.BlockSpec((rows, d), lambda i, j: (0, 0)),
                  pl.BlockSpec((1, d, tn), lambda i, j: (i, 0, j)),
                  pl.BlockSpec((1, 1, tn), lambda i, j: (i, 0, j))],
        out_specs=pl.BlockSpec((1, rows, tn), lambda i, j: (i, 0, j)),
        out_shape=jax.ShapeDtypeStruct((depth, rows, n), F32),
        compiler_params=_cparams("parallel", "parallel"),
        name="modulation",
    )(cond, mod_w, mod_b.reshape(depth, 1, n))


def _out_ln_kernel(o_ref, w_ref, x_ref, mod_ref, g_ref, b_ref, y_ref, *, d, alpha):
    y = _mm(o_ref[0], w_ref[...])
    z = alpha * x_ref[0] + _mod_part(mod_ref, 2, d) * y
    y_ref[0] = _layernorm(z, g_ref[...], b_ref[...])


def _mod_spec(d, per_batch):
    if per_batch:
        return pl.BlockSpec((1, 1, N_MOD * d), lambda b, *_: (b, 0, 0))
    return pl.BlockSpec((1, 1, N_MOD * d), lambda b, *_: (0, 0, 0))


def _out_ln(o, wo, x, mod, per_batch, ln_g, ln_b, alpha):
    bsz, t, d = x.shape
    tm = min(PROJ_ROWS, t)
    tok = lambda b, i: (b, i, 0)
    const = lambda b, i: (0, 0)
    return pl.pallas_call(
        functools.partial(_out_ln_kernel, d=d, alpha=alpha),
        grid=(bsz, t // tm),
        in_specs=[pl.BlockSpec((1, tm, d), tok), pl.BlockSpec((d, d), const), pl.BlockSpec((1, tm, d), tok),
                  _mod_spec(d, per_batch), pl.BlockSpec((1, d), const), pl.BlockSpec((1, d), const)],
        out_specs=pl.BlockSpec((1, tm, d), tok),
        out_shape=jax.ShapeDtypeStruct((bsz, t, d), F32),
        compiler_params=_cparams("parallel", "parallel"),
        name="out_ln",
    )(o, wo, x, mod, ln_g, ln_b)


def _mlp_kernel(x_ref, mod_ref, w1_ref, w2_ref, g_ref, b_ref, y_ref, h_scr, acc_scr, *, d, alpha):
    j = pl.program_id(2)

    @pl.when(j == 0)
    def _():
        h = x_ref[0] * (1.0 + _mod_part(mod_ref, 4, d)) + _mod_part(mod_ref, 3, d)
        h_scr[...] = h.astype(h_scr.dtype)
        acc_scr[...] = jnp.zeros_like(acc_scr)

    a = jnp.square(jnp.maximum(_mm(h_scr[...], w1_ref[...]), 0.0))
    acc_scr[...] += _mm(a, w2_ref[...])

    @pl.when(j == pl.num_programs(2) - 1)
    def _():
        z = alpha * x_ref[0] + _mod_part(mod_ref, 5, d) * acc_scr[...]
        y_ref[0] = _layernorm(z, g_ref[...], b_ref[...])


def _mlp(x, mod, per_batch, w1, w2, ln_g, ln_b, alpha):
    bsz, t, d = x.shape
    ff = w1.shape[1]
    tm = min(MLP_ROWS, t)
    tf = min(MLP_FF_TILE, ff)
    tok = lambda b, i, j: (b, i, 0)
    const = lambda b, i, j: (0, 0)
    return pl.pallas_call(
        functools.partial(_mlp_kernel, d=d, alpha=alpha),
        grid=(bsz, t // tm, ff // tf),
        in_specs=[pl.BlockSpec((1, tm, d), tok), _mod_spec(d, per_batch),
                  pl.BlockSpec((d, tf), lambda b, i, j: (0, j)), pl.BlockSpec((tf, d), lambda b, i, j: (j, 0)),
                  pl.BlockSpec((1, d), const), pl.BlockSpec((1, d), const)],
        out_specs=pl.BlockSpec((1, tm, d), tok),
        out_shape=jax.ShapeDtypeStruct((bsz, t, d), F32),
        scratch_shapes=[pltpu.VMEM((tm, d), MXU_DTYPE), pltpu.VMEM((tm, d), F32)],
        compiler_params=_cparams("parallel", "parallel", "arbitrary"),
        name="mlp",
    )(x, mod, w1, w2, ln_g, ln_b)


def _gqa_qkv_kernel(x_ref, mod_ref, w_ref, gq_ref, gk_ref, cos_ref, sin_ref, q_ref, k_ref, v_ref,
                    *, d, n_q, n_kv):
    h = x_ref[0] * (1.0 + _mod_part(mod_ref, 1, d)) + _mod_part(mod_ref, 0, d)
    acc = _mm(h, w_ref[...])
    cos, sin = cos_ref[...], sin_ref[...]
    hd = ATT_HEAD_DIM
    src = lax.broadcasted_iota(jnp.int32, (hd, hd), 0)
    dst = lax.broadcasted_iota(jnp.int32, (hd, hd), 1)
    first = (dst & (hd // 2 - 1)) < hd // 4
    rot = (jnp.where(first & (src == dst + hd // 4), -1.0, 0.0)
           + jnp.where(jnp.logical_not(first) & (src == dst - hd // 4), 1.0, 0.0))

    def norm_rope(u, g):
        u = u * lax.rsqrt(jnp.mean(u * u, -1, keepdims=True) + LN_EPS) * g
        return u * cos + _mm(u, rot) * sin

    for i in range(n_q):
        q_ref[0, :, i * hd:(i + 1) * hd] = norm_rope(acc[:, i * hd:(i + 1) * hd], gq_ref[...]).astype(q_ref.dtype)
    for i in range(n_kv):
        c0 = (n_q + i) * hd
        k_ref[0, :, i * hd:(i + 1) * hd] = norm_rope(acc[:, c0:c0 + hd], gk_ref[...]).astype(k_ref.dtype)
    ones = jnp.ones((acc.shape[0], hd), v_ref.dtype)
    for i in range(n_kv):
        c0 = (n_q + n_kv + i) * hd
        v_ref[0, :, 2 * i * hd:(2 * i + 1) * hd] = acc[:, c0:c0 + hd].astype(v_ref.dtype)
        v_ref[0, :, (2 * i + 1) * hd:(2 * i + 2) * hd] = ones


def _gqa_qkv(x, mod, per_batch, w, gq, gk, cos, sin):
    bsz, t, d = x.shape
    hd = ATT_HEAD_DIM
    n_kv = ATT_KV_HEADS
    n_q = w.shape[1] // hd - 2 * n_kv
    tm = min(PROJ_ROWS, t)
    tok = lambda b, i: (b, i, 0)
    const = lambda b, i: (0, 0)
    tab = pl.BlockSpec((tm, hd), lambda b, i: (i, 0))
    return pl.pallas_call(
        functools.partial(_gqa_qkv_kernel, d=d, n_q=n_q, n_kv=n_kv),
        grid=(bsz, t // tm),
        in_specs=[pl.BlockSpec((1, tm, d), tok), _mod_spec(d, per_batch), pl.BlockSpec(w.shape, const),
                  pl.BlockSpec((1, hd), const), pl.BlockSpec((1, hd), const), tab, tab],
        out_specs=[pl.BlockSpec((1, tm, n_q * hd), tok), pl.BlockSpec((1, tm, n_kv * hd), tok),
                   pl.BlockSpec((1, tm, 2 * n_kv * hd), tok)],
        out_shape=[jax.ShapeDtypeStruct((bsz, t, n_q * hd), MXU_DTYPE),
                   jax.ShapeDtypeStruct((bsz, t, n_kv * hd), MXU_DTYPE),
                   jax.ShapeDtypeStruct((bsz, t, 2 * n_kv * hd), MXU_DTYPE)],
        compiler_params=_cparams("parallel", "parallel"),
        name="gqa_qkv",
    )(x, mod, w, gq, gk, cos, sin)


def _gqa_att_kernel(*refs, n_sets, groups, tk):
    q_ref, o_ref = refs[0], refs[-1]
    kv = refs[1:-1]
    hd = ATT_HEAD_DIM
    tq = q_ref.shape[1]
    q = jnp.concatenate([q_ref[0, :, g * hd:(g + 1) * hd] for g in range(groups)], axis=0)
    m = acc = None
    for j in range(n_sets):
        k_ref, v_ref = kv[2 * j], kv[2 * j + 1]
        t = k_ref.shape[1]
        for c0 in range(0, t, tk):
            c1 = min(c0 + tk, t)
            s = _mm_nt(q, k_ref[0, c0:c1, :])
            m_c = jnp.max(s, -1, keepdims=True)
            if m is None:
                m = m_c
                acc = _mm(jnp.exp2(s - m), v_ref[0, c0:c1, :])
            else:
                m_new = jnp.maximum(m, m_c)
                acc = acc * jnp.exp2(m - m_new) + _mm(jnp.exp2(s - m_new), v_ref[0, c0:c1, :])
                m = m_new
    o = acc[:, :hd] / acc[:, hd:]
    for g in range(groups):
        o_ref[0, :, g * hd:(g + 1) * hd] = o[g * tq:(g + 1) * tq].astype(o_ref.dtype)


def _gqa_att(q, kv_sets):
    bsz, s, dq = q.shape
    hd = ATT_HEAD_DIM
    n_kv = ATT_KV_HEADS
    groups = dq // hd // n_kv
    tq = min(GQA_Q_ROWS, s)
    in_specs = [pl.BlockSpec((1, tq, groups * hd), lambda b, kh, i: (b, i, kh))]
    args = [q]
    for k, v in kv_sets:
        t = k.shape[1]
        in_specs += [pl.BlockSpec((1, t, hd), lambda b, kh, i: (b, 0, kh)),
                     pl.BlockSpec((1, t, 2 * hd), lambda b, kh, i: (b, 0, kh))]
        args += [k, v]
    return pl.pallas_call(
        functools.partial(_gqa_att_kernel, n_sets=len(kv_sets), groups=groups, tk=GQA_KEY_CHUNK),
        grid=(bsz, n_kv, s // tq),
        in_specs=in_specs,
        out_specs=pl.BlockSpec((1, tq, groups * hd), lambda b, kh, i: (b, i, kh)),
        out_shape=jax.ShapeDtypeStruct((bsz, s, dq), MXU_DTYPE),
        compiler_params=_cparams("parallel", "parallel", "parallel"),
        name="gqa_att",
    )(*args)


def _rope_tables(n_tokens, head_dim):
    t = np.arange(n_tokens)
    row = (t // GRID_W).astype(np.float32)
    col = (t % GRID_W).astype(np.float32)
    half = head_dim // 2
    freqs = jnp.asarray(ROPE_BASE, F32) ** (-jnp.arange(0, half, 2, dtype=F32) / half)
    ang_r = jnp.asarray(row)[:, None] * freqs[None, :]
    ang_c = jnp.asarray(col)[:, None] * freqs[None, :]
    ang = jnp.concatenate([ang_r, ang_r, ang_c, ang_c], axis=-1)
    return jnp.cos(ang), jnp.sin(ang)


def _gqa_layer(x, xc, mod, mod_c, wqkv, wo, q_norm, k_norm, want_ctx):
    s, tc = x.shape[1], xc.shape[1]
    hd = ATT_HEAD_DIM
    gq = (q_norm * (hd ** -0.5 * LOG2E)).reshape(1, hd)
    gk = k_norm.reshape(1, hd)
    cos, sin = _rope_tables(s, hd)
    one, zero = jnp.ones((tc, hd), F32), jnp.zeros((tc, hd), F32)
    q, k, v = _gqa_qkv(x, mod, True, wqkv, gq, gk, cos, sin)
    qc, kc, vc = _gqa_qkv(xc, mod_c, False, wqkv, gq, gk, one, zero)
    o = _gqa_att(q, [(k, v), (kc, vc)])
    oc = _gqa_att(qc, [(kc, vc)]) if want_ctx else None
    return o, oc, wo


def _na_qkv_kernel(x_ref, mod_ref, w_ref, q_ref, k_ref, v_ref, *, d, scale):
    h = x_ref[0] * (1.0 + _mod_part(mod_ref, 1, d)) + _mod_part(mod_ref, 0, d)
    acc = _mm(h, w_ref[...])
    q_ref[0] = (acc[:, :d] * scale).astype(q_ref.dtype)
    k_ref[0] = acc[:, d:2 * d].astype(k_ref.dtype)
    ones = jnp.ones((acc.shape[0], LANES), v_ref.dtype)
    for p in range(d // LANES):
        c0 = 2 * d + p * LANES
        v_ref[0, :, 2 * p * LANES:(2 * p + 1) * LANES] = acc[:, c0:c0 + LANES].astype(v_ref.dtype)
        v_ref[0, :, (2 * p + 1) * LANES:(2 * p + 2) * LANES] = ones


def _na_qkv(x, mod, per_batch, w):
    bsz, t, d = x.shape
    tm = min(PROJ_ROWS, t)
    tok = lambda b, i: (b, i, 0)
    out = jax.ShapeDtypeStruct((bsz, t, d), MXU_DTYPE)
    return pl.pallas_call(
        functools.partial(_na_qkv_kernel, d=d, scale=NA_HEAD_DIM ** -0.5 * LOG2E),
        grid=(bsz, t // tm),
        in_specs=[pl.BlockSpec((1, tm, d), tok), _mod_spec(d, per_batch), pl.BlockSpec(w.shape, lambda b, i: (0, 0))],
        out_specs=[pl.BlockSpec((1, tm, d), tok)] * 2 + [pl.BlockSpec((1, tm, 2 * d), tok)],
        out_shape=[out, out, jax.ShapeDtypeStruct((bsz, t, 2 * d), MXU_DTYPE)],
        compiler_params=_cparams("parallel", "parallel"),
        name="na_qkv",
    )(x, mod, w)


def _pair_softmax_att(q, ks, vs, biases):
    lane = lax.broadcasted_iota(jnp.int32, q.shape, 1)
    first = lane < NA_HEAD_DIM
    zero = jnp.zeros_like(q)
    qm = [jnp.where(first, q, zero), jnp.where(first, zero, q)]
    m, acc = [None, None], [None, None]
    for j, (k, v) in enumerate(zip(ks, vs)):
        for h in range(2):
            s = _mm_nt(qm[h], k)
            if biases[h][j] is not None:
                s = s + biases[h][j]
            m_c = jnp.max(s, -1, keepdims=True)
            if m[h] is None:
                m[h] = m_c
                acc[h] = _mm(jnp.exp2(s - m_c), v)
            else:
                m_new = jnp.maximum(m[h], m_c)
                acc[h] = acc[h] * jnp.exp2(m[h] - m_new) + _mm(jnp.exp2(s - m_new), v)
                m[h] = m_new
    outs = [a[:, :LANES] / a[:, LANES:] for a in acc]
    return jnp.where(first, outs[0], outs[1])


def _na_att_kernel(q_ref, k_ref, v_ref, kc_ref, vc_ref, bias_ref, o_ref, *, rows):
    i = pl.program_id(2)
    q_rows = q_ref.shape[1] // GRID_W
    win = NA_WIN_H * GRID_W
    w = GRID_W
    first = lax.broadcasted_iota(jnp.int32, (w, LANES), 1) < NA_HEAD_DIM
    qm, kw, vw, bias = [], [], [], []
    for rq in range(q_rows):
        row_q = i * q_rows + rq
        r0 = jnp.clip(row_q - NA_WIN_H // 2, 0, rows - NA_WIN_H)
        start = pl.multiple_of(r0 * w, w)
        q = q_ref[0, rq * w:(rq + 1) * w, :]
        zero = jnp.zeros_like(q)
        qm.append(jnp.concatenate([jnp.where(first, q, zero), jnp.where(first, zero, q)], axis=0))
        kw.append(k_ref[0, pl.ds(start, win), :])
        vw.append(v_ref[0, pl.ds(start, win), :])
        e = r0 - row_q + NA_WIN_H - 1
        bias.append(jnp.concatenate([bias_ref[0, 0, e], bias_ref[0, 1, e]], axis=0))
    each = lambda fn, *lists: [fn(*args) for args in zip(*lists)]
    s_ctx = _mm_nt(jnp.concatenate(qm, axis=0), kc_ref[0])
    s_ctx = [s_ctx[2 * w * rq:2 * w * (rq + 1)] for rq in range(q_rows)]
    s_loc = each(lambda a, b, c: _mm_nt(a, b) + c, qm, kw, bias)
    m = each(lambda a, b: jnp.maximum(jnp.max(a, -1, keepdims=True), jnp.max(b, -1, keepdims=True)), s_loc, s_ctx)
    acc_ctx = _mm(jnp.concatenate(each(lambda a, mm: jnp.exp2(a - mm), s_ctx, m), axis=0), vc_ref[0])
    acc = each(lambda a, mm, v: _mm(jnp.exp2(a - mm), v), s_loc, m, vw)
    for rq in range(q_rows):
        a = acc[rq] + acc_ctx[2 * w * rq:2 * w * (rq + 1)]
        o = a[:, :LANES] / a[:, LANES:]
        o_ref[0, rq * w:(rq + 1) * w, :] = jnp.where(first, o[:w], o[w:]).astype(o_ref.dtype)


def _na_bias_tables(rpb, scale):
    pad = GRID_W
    rpb_p = jnp.pad(rpb, ((0, 0), (0, 0), (pad, pad)))
    cmat = jnp.stack([rpb_p[:, :, pad + NA_WIN_W - 1 - cq: pad + NA_WIN_W - 1 - cq + GRID_W]
                      for cq in range(GRID_W)], axis=2)
    cq, ck = np.arange(GRID_W)[:, None], np.arange(GRID_W)[None, :]
    c0 = np.clip(cq - NA_WIN_W // 2, 0, GRID_W - NA_WIN_W)
    vcol = (ck >= c0) & (ck < c0 + NA_WIN_W)
    cmat = jnp.where(vcol[None, None], cmat * scale, NEG_BIG)
    return jnp.stack([jnp.concatenate([cmat[:, e + j] for j in range(NA_WIN_H)], axis=-1)
                      for e in range(NA_WIN_H)], axis=1)


def _na_att(q, k, v, kc, vc, bias):
    bsz, s, d = q.shape
    tc = kc.shape[1]
    rows = s // GRID_W
    q_rows = min(NA_Q_ROWS, rows)
    assert rows % q_rows == 0 and rows >= NA_WIN_H
    tq = q_rows * GRID_W
    pairs = d // LANES
    q_spec = pl.BlockSpec((1, tq, LANES), lambda hp, b, i: (b, i, hp))
    whole = lambda t, width: pl.BlockSpec((1, t, width), lambda hp, b, i: (b, 0, hp))
    bias_spec = pl.BlockSpec((1, 2) + bias.shape[1:], lambda hp, b, i: (0, hp, 0, 0, 0))
    return pl.pallas_call(
        functools.partial(_na_att_kernel, rows=rows),
        grid=(pairs, bsz, rows // q_rows),
        in_specs=[q_spec, whole(s, LANES), whole(s, 2 * LANES), whole(tc, LANES), whole(tc, 2 * LANES), bias_spec],
        out_specs=q_spec,
        out_shape=jax.ShapeDtypeStruct((bsz, s, d), MXU_DTYPE),
        compiler_params=_cparams("parallel", "parallel", "parallel"),
        name="na_att",
    )(q, k, v, kc, vc, bias[None])


def _pair_att_kernel(q_ref, k_ref, v_ref, o_ref):
    o_ref[0] = _pair_softmax_att(q_ref[0], [k_ref[0]], [v_ref[0]], [[None], [None]]).astype(o_ref.dtype)


def _pair_att(q, k, v):
    bsz, t, d = q.shape
    spec = pl.BlockSpec((1, t, LANES), lambda b, hp: (b, 0, hp))
    return pl.pallas_call(
        _pair_att_kernel,
        grid=(bsz, d // LANES),
        in_specs=[spec, spec, pl.BlockSpec((1, t, 2 * LANES), lambda b, hp: (b, 0, hp))],
        out_specs=spec,
        out_shape=jax.ShapeDtypeStruct((bsz, t, d), MXU_DTYPE),
        compiler_params=_cparams("parallel", "parallel"),
        name="na_ctx_att",
    )(q, k, v)


def _na_layer(x, xc, mod, mod_c, wqkv, wo, rpb, want_ctx):
    q, k, v = _na_qkv(x, mod, True, wqkv)
    qc, kc, vc = _na_qkv(xc, mod_c, False, wqkv)
    o = _na_att(q, k, v, kc, vc, _na_bias_tables(rpb, LOG2E))
    oc = _pair_att(qc, kc, vc) if want_ctx else None
    return o, oc, wo


def _rw_proj_kernel(x_ref, xp_ref, xn_ref, mod_ref, mu_ref, wr_ref, wk_ref, wv_ref, w1_ref, w2_ref, a1_ref, a2_ref,
                    g1_ref, g2_ref, w0_ref, a0_ref, r_ref, k_ref, v_ref, g_ref, lw_ref, as_ref, *, d):
    t = pl.program_id(1)
    nt = pl.num_programs(1)
    scale = 1.0 + _mod_part(mod_ref, 1, d)
    shift = _mod_part(mod_ref, 0, d)
    h = x_ref[0] * scale + shift
    tm = h.shape[0]
    h_prev = (xp_ref[0, 7:8, :] * scale + shift) * (t > 0).astype(F32)
    h_next = (xn_ref[0, 0:1, :] * scale + shift) * (t < nt - 1).astype(F32)
    row = lax.broadcasted_iota(jnp.int32, h.shape, 0)
    prev = jnp.where(row == 0, h_prev, pltpu.roll(h, 1, 0))
    nxt = jnp.where(row == tm - 1, h_next, pltpu.roll(h, tm - 1, 0))
    xx = 0.5 * (prev + nxt) - h
    mix = lambda j: h + xx * mu_ref[j:j + 1, :]

    r_ref[0] = _mm(mix(0), wr_ref[...]).astype(r_ref.dtype)
    k_ref[0] = _mm(mix(2), wk_ref[...]).astype(k_ref.dtype)
    v_ref[0] = _mm(mix(3), wv_ref[...]).astype(v_ref.dtype)
    g_ref[0] = _mm(_sigmoid(_mm(mix(5), g1_ref[...])), g2_ref[...]).astype(g_ref.dtype)

    tw = jnp.tanh(_mm(mix(1), w1_ref[...]))
    al = _mm(mix(4), a1_ref[...])
    first = lax.broadcasted_iota(jnp.int32, tw.shape, 1) < tw.shape[1] // 2
    zero = jnp.zeros_like(tw)
    for n in range(2):
        pick = lambda u: jnp.where(first, u, zero) if n == 0 else jnp.where(first, zero, u)
        z = -(w0_ref[n:n + 1, :] + _mm(pick(tw), w2_ref[...]))
        softplus = jnp.maximum(z, 0.0) + jnp.log(1.0 + jnp.exp(-jnp.abs(z)))
        lw_ref[n, 0] = -jnp.exp(-softplus - 0.5)
        as_ref[n, 0] = _sigmoid(a0_ref[n:n + 1, :] + _mm(pick(al), a2_ref[...])).astype(as_ref.dtype)


def _rw_proj(x, mod, per_batch, p):
    bsz, t, d = x.shape
    tm = min(PROJ_ROWS, t)
    n8 = t // 8
    tok = lambda b, i: (b, i, 0)
    dtok = lambda b, i: (0, b, i, 0)
    full = lambda a: pl.BlockSpec(a.shape, lambda b, i: (0,) * a.ndim)
    halo_prev = pl.BlockSpec((1, 8, d), lambda b, i: (b, jnp.maximum(i * (tm // 8) - 1, 0), 0))
    halo_next = pl.BlockSpec((1, 8, d), lambda b, i: (b, jnp.minimum((i + 1) * (tm // 8), n8 - 1), 0))
    weights = [p['mu'], p['wr'], p['wk'], p['wv'], p['w1'], p['w2'], p['a1'], p['a2'], p['g1'], p['g2'],
               p['w0'], p['a0']]
    one = jax.ShapeDtypeStruct((bsz, t, d), MXU_DTYPE)
    two = lambda dtype: jax.ShapeDtypeStruct((2, bsz, t, d), dtype)
    return pl.pallas_call(
        functools.partial(_rw_proj_kernel, d=d),
        grid=(bsz, t // tm),
        in_specs=[pl.BlockSpec((1, tm, d), tok), halo_prev, halo_next, _mod_spec(d, per_batch)]
                 + [full(a) for a in weights],
        out_specs=[pl.BlockSpec((1, tm, d), tok)] * 4 + [pl.BlockSpec((2, 1, tm, d), dtok)] * 2,
        out_shape=[one, one, one, one, two(F32), two(MXU_DTYPE)],
        compiler_params=_cparams("parallel", "parallel"),
        name="rw_proj",
    )(x, x, x, mod, *weights)


def _rw_scan_kernel(r_ref, k_ref, v_ref, lw_ref, as_ref, kk_ref, ka_ref, z0_ref, y_ref, zf_ref, z_scr,
                    *, pairs, sub, reverse):
    c = pl.program_id(1)
    L = RW_CHUNK
    H = RW_HEAD
    assert 2 * L == LANES and 2 * H == LANES

    @pl.when(c == 0)
    def _():
        z_scr[...] = z0_ref[0, 0]

    sign = -1 if reverse else 1
    t_i = lax.broadcasted_iota(jnp.int32, (L, 2 * L), 0)
    s_i = lax.broadcasted_iota(jnp.int32, (L, 2 * L), 1) & (L - 1)
    dt = sign * (s_i - t_i)
    strict_c = dt < 0
    incl_c = dt <= 0
    tri2 = incl_c.astype(F32)
    eye_c = (dt == 0).astype(F32)
    r2 = lax.broadcasted_iota(jnp.int32, (LANES, LANES), 0)
    c2 = lax.broadcasted_iota(jnp.int32, (LANES, LANES), 1)
    eye2 = (r2 == c2).astype(F32)
    same_head = _head_ones()
    first = lax.broadcasted_iota(jnp.int32, (L, LANES), 1) < H
    zero = jnp.zeros((L, LANES), F32)

    head0 = lambda u: jnp.where(first, u, zero)
    head1 = lambda u: jnp.where(first, zero, u)
    bd = lambda u: jnp.concatenate([head0(u), head1(u)], axis=0)

    each = lambda fn, *lists: [fn(*args) for args in zip(*lists)]
    sls = [slice(p * LANES, (p + 1) * LANES) for p in range(pairs)]
    cat0 = lambda *u: jnp.concatenate(u, axis=0)
    cat1 = lambda *u: jnp.concatenate(u, axis=1)

    mm_each = lambda lhs, w: [_mm(a, b) for a, b in zip(lhs, w)]

    def chunk_terms(rw):
        lw = [lw_ref[0, 0, rw, sl] for sl in sls]
        r = [r_ref[0, rw, sl].astype(F32) for sl in sls]
        k = [k_ref[0, rw, sl].astype(F32) for sl in sls]
        v = [v_ref[0, rw, sl].astype(F32) for sl in sls]
        a_s = [as_ref[0, 0, rw, sl].astype(F32) for sl in sls]

        c_in = each(lambda u: _mm(tri2, cat0(*_split(u, 2))), lw)
        c_all = each(lambda u: jnp.sum(u, axis=0, keepdims=True), lw)
        kk = each(lambda u, sl: u * kk_ref[:, sl], k, sls)
        ss = each(lambda u: _mm(u * u, same_head), kk)
        kk = each(lambda u, s: u * lax.rsqrt(jnp.maximum(s, 1e-12)), kk, ss)
        b_v = each(lambda u, a: u * a, kk, a_s)
        k_d = each(lambda u, a, sl: u * (1.0 + (a - 1.0) * ka_ref[:, sl]), k, a_s, sls)
        e_neg = each(lambda ci: jnp.exp(-ci), c_in)
        e_rem = each(lambda ci, ca: jnp.exp(ca - ci), c_in, c_all)
        a_t = each(lambda u, ci, l: -u * jnp.exp(ci - l), kk, c_in, lw)
        r_t = each(lambda u, ci: u * jnp.exp(ci), r, c_in)
        b_t = each(lambda u, e: u * e, b_v, e_neg)
        k_t = each(lambda u, e: u * e, k_d, e_neg)
        b_h = each(lambda u, e: u * e, b_v, e_rem)
        k_h = each(lambda u, e: u * e, k_d, e_rem)
        yield None

        x = each(lambda a, rr, b, kt: _mm_nt(cat0(a, rr), cat0(bd(b), bd(kt))), a_t, r_t, b_t, k_t)
        m_ab = each(lambda u: jnp.where(strict_c, u[:L, :2 * L], 0.0), x)
        m_ak = each(lambda u: jnp.where(strict_c, u[:L, 2 * L:], 0.0), x)
        n_rb = each(lambda u: jnp.where(incl_c, u[L:, :2 * L], 0.0), x)
        n_rk = each(lambda u: jnp.where(incl_c, u[L:, 2 * L:], 0.0), x)

        inv = each(lambda m: eye_c + m, m_ab)
        yield None
        pw = mm_each(m_ab, each(bd, m_ab))
        for _ in range(int(np.log2(L)) - 2):
            yield None
            st = mm_each(each(cat0, pw, inv), each(bd, pw))
            pw = each(lambda s: s[:L], st)
            inv = each(lambda b, s: b + s[L:], inv, st)
        yield None
        inv = each(lambda b, s: b + s, inv, mm_each(inv, each(bd, pw)))
        mv = mm_each(each(cat0, m_ak, n_rk), each(bd, v))
        yield None
        tw = each(lambda i, a, m: _mm(i, cat1(bd(a), bd(m[:L]))), inv, a_t, mv)
        p1 = each(lambda u: u[:, :LANES], tw)
        p2 = each(lambda u: u[:, LANES:], tw)
        yield None
        nw = each(lambda m, a, b: _mm(m, cat1(bd(a), bd(b))), n_rb, p1, p2)
        q1 = each(lambda rr, u: rr + u[:, :LANES], r_t, nw)
        q2 = each(lambda u, m: u[:, LANES:] + m[L:], nw, mv)
        gh = each(lambda b, kh, a, c_, u: _mm_tn(cat0(b, kh), cat0(cat1(a, c_), cat1(zero, u))),
                  b_h, k_h, p1, p2, v)
        g_t = each(lambda ca, u: eye2 * jnp.exp(ca) + same_head * u[:, :LANES], c_all, gh)
        h_t = each(lambda u: same_head * u[:, LANES:], gh)
        yield q1, q2, g_t, h_t

    chunk_rows = [slice(u * L, (u + 1) * L) for u in (range(sub - 1, -1, -1) if reverse else range(sub))]
    gens = [chunk_terms(rw) for rw in chunk_rows]
    terms = [None] * sub
    while any(t is None for t in terms):
        terms = [next(gen) for gen in gens]
    z = [z_scr[p] for p in range(pairs)]
    for rw, (q1, q2, g_t, h_t) in zip(chunk_rows, terms):
        yz = mm_each(each(cat0, q1, g_t), z)
        for p in range(pairs):
            y_ref[0, rw, sls[p]] = (yz[p][:L] + q2[p]).astype(y_ref.dtype)
            z[p] = yz[p][L:] + h_t[p]
    for p in range(pairs):
        z_scr[p] = z[p]

    @pl.when(c == pl.num_programs(1) - 1)
    def _():
        zf_ref[0] = z_scr[...]


def _rw_scan(r, k, v, lw, a_s, kk_w, ka_w, z0, reverse):
    bsz, t, d = r.shape
    sub = min(RW_SCAN_CHUNKS, t // RW_CHUNK)
    rows = sub * RW_CHUNK
    assert t % rows == 0
    n_c = t // rows
    n = int(reverse)
    pairs = d // LANES
    cidx = (lambda c: n_c - 1 - c) if reverse else (lambda c: c)
    tok = pl.BlockSpec((1, rows, d), lambda b, c: (b, cidx(c), 0))
    dtok = pl.BlockSpec((1, 1, rows, d), lambda b, c: (n, b, cidx(c), 0))
    vec = pl.BlockSpec((1, d), lambda b, c: (0, 0))
    return pl.pallas_call(
        functools.partial(_rw_scan_kernel, pairs=pairs, sub=sub, reverse=reverse),
        grid=(bsz, n_c),
        in_specs=[tok, tok, tok, dtok, dtok, vec, vec,
                  pl.BlockSpec((1, 1, pairs, LANES, LANES), lambda b, c: (n, b, 0, 0, 0))],
        out_specs=[tok, pl.BlockSpec((1, pairs, LANES, LANES), lambda b, c: (b, 0, 0, 0))],
        out_shape=[jax.ShapeDtypeStruct((bsz, t, d), MXU_DTYPE),
                   jax.ShapeDtypeStruct((bsz, pairs, LANES, LANES), F32)],
        scratch_shapes=[pltpu.VMEM((pairs, LANES, LANES), F32)],
        compiler_params=_cparams("parallel", "arbitrary"),
        name="rw_scan",
    )(r, k, v, lw, a_s, kk_w, ka_w, z0)


def _rw_out_kernel(yf_ref, yb_ref, r_ref, k_ref, v_ref, g_ref, as_ref, ka_ref, rk_ref, lg_ref, lb_ref, wo_ref, x_ref,
                   mod_ref, g_ln_ref, b_ln_ref, o_ref, u_scr, *, d, alpha):
    width = 2 * LANES
    same_head = _head_ones(width)
    inv_n = 1.0 / RW_HEAD
    for p in range(d // width):
        sl = slice(p * width, (p + 1) * width)
        y = yf_ref[0, :, sl].astype(F32) + yb_ref[0, :, sl].astype(F32)
        mu = _mm_rhs_exact(y, same_head) * inv_n
        dy = y - mu
        var = _mm(dy * dy, same_head) * inv_n
        yn = dy * lax.rsqrt(var + RW_GN_EPS) * lg_ref[:, sl] + lb_ref[:, sl]
        k = k_ref[0, :, sl].astype(F32)
        ka = ka_ref[:, sl]
        k_sum = k * ((1.0 + (as_ref[0, 0, :, sl].astype(F32) - 1.0) * ka)
                     + (1.0 + (as_ref[1, 0, :, sl].astype(F32) - 1.0) * ka))
        bonus = _mm(r_ref[0, :, sl].astype(F32) * k_sum * rk_ref[:, sl], same_head)
        u = (yn + bonus * v_ref[0, :, sl].astype(F32)) * g_ref[0, :, sl].astype(F32)
        u_scr[:, sl] = u.astype(u_scr.dtype)
    z = alpha * x_ref[0] + _mod_part(mod_ref, 2, d) * _mm(u_scr[...], wo_ref[...])
    o_ref[0] = _layernorm(z, g_ln_ref[...], b_ln_ref[...])


def _rw_out(y_f, y_b, r, k, v, g, a_s, p, x, mod, per_batch, ln_g, ln_b, alpha):
    bsz, t, d = x.shape
    tm = min(RW_OUT_ROWS, t)
    tok = pl.BlockSpec((1, tm, d), lambda b, i: (b, i, 0))
    dtok = pl.BlockSpec((2, 1, tm, d), lambda b, i: (0, b, i, 0))
    vec = pl.BlockSpec((1, d), lambda b, i: (0, 0))
    return pl.pallas_call(
        functools.partial(_rw_out_kernel, d=d, alpha=alpha),
        grid=(bsz, t // tm),
        in_specs=[tok, tok, tok, tok, tok, tok, dtok, vec, vec, vec, vec, pl.BlockSpec((d, d), lambda b, i: (0, 0)),
                  tok, _mod_spec(d, per_batch), vec, vec],
        out_specs=tok,
        out_shape=jax.ShapeDtypeStruct((bsz, t, d), F32),
        scratch_shapes=[pltpu.VMEM((tm, d), MXU_DTYPE)],
        compiler_params=_cparams("parallel", "parallel"),
        name="rw_out",
    )(y_f, y_b, r, k, v, g, a_s, p['k_a'], p['r_k'], p['lnx_g'], p['lnx_b'], p['wo'], x, mod, ln_g, ln_b)


def _rw_layer(x, xc, mod, mod_c, p, want_ctx, ln_g, ln_b, alpha):
    bsz, _, d = x.shape
    rc, kc, vc, gc, lwc, asc = _rw_proj(xc, mod_c, False, p)
    r, k, v, g, lw, a_s = _rw_proj(x, mod, True, p)
    z0 = jnp.zeros((2, bsz, d // LANES, LANES, LANES), F32)
    scan = lambda args, z, rev: _rw_scan(*args, p['k_k'], p['k_a'], z, rev)
    yc_f, zc_f = scan((rc, kc, vc, lwc, asc), z0, False)
    yc_b, zc_b = scan((rc, kc, vc, lwc, asc), z0, True)
    zc = jnp.stack([zc_f, zc_b])
    y_f, _ = scan((r, k, v, lw, a_s), zc, False)
    y_b, _ = scan((r, k, v, lw, a_s), zc, True)
    x_new = _rw_out(y_f, y_b, r, k, v, g, a_s, p, x, mod, True, ln_g, ln_b, alpha)
    xc_new = (_rw_out(yc_f, yc_b, rc, kc, vc, gc, asc, p, xc, mod_c, False, ln_g, ln_b, alpha)
              if want_ctx else None)
    return x_new, xc_new


def kernel(x, c, ctx, c_ctx, mod_w, mod_b, post_ln_g, post_ln_b, mlp_w1, mlp_w2, att_wqkv, att_wo, att_q_norm, att_k_norm, na_wqkv, na_wo, na_rpb, rw_mu, rw_wr, rw_wk, rw_wv, rw_wo, rw_w0, rw_w1, rw_w2, rw_a0, rw_a1, rw_a2, rw_g1, rw_g2, rw_k_k, rw_k_a, rw_r_k, rw_lnx_g, rw_lnx_b):
    depth = mod_w.shape[0]
    bsz, _, d = x.shape
    alpha = (2.0 * depth) ** 0.25
    cast = lambda a: a.astype(MXU_DTYPE)

    cond_rows = -(-(bsz + 1) // 8) * 8
    cond = jnp.zeros((cond_rows, d), F32).at[:bsz].set(c).at[bsz].set(c_ctx)
    mods = _modulation(cond, mod_w, mod_b)

    xc = ctx
    for i in range(depth):
        kind, slot = i % N_MIXERS, i // N_MIXERS
        want_ctx = i < depth - 1
        mod = mods[i, :bsz].reshape(bsz, 1, N_MOD * d)
        mod_c = mods[i, bsz:bsz + 1].reshape(1, 1, N_MOD * d)
        g1, b1 = post_ln_g[i, 0:1], post_ln_b[i, 0:1]
        g2, b2 = post_ln_g[i, 1:2], post_ln_b[i, 1:2]
        if kind == 2:
            cat = lambda a: jnp.concatenate([a[0], a[1]], axis=-1)
            p = {'mu': rw_mu[slot], 'wr': cast(rw_wr[slot]), 'wk': cast(rw_wk[slot]), 'wv': cast(rw_wv[slot]),
                 'wo': cast(rw_wo[slot]), 'w0': rw_w0[slot], 'a0': rw_a0[slot],
                 'w1': cast(cat(rw_w1[slot])), 'a1': cast(cat(rw_a1[slot])),
                 'w2': cast(rw_w2[slot].reshape(-1, d)), 'a2': cast(rw_a2[slot].reshape(-1, d)),
                 'g1': cast(rw_g1[slot]), 'g2': cast(rw_g2[slot]),
                 'k_k': rw_k_k[slot].reshape(1, d), 'k_a': rw_k_a[slot].reshape(1, d),
                 'r_k': rw_r_k[slot].reshape(1, d), 'lnx_g': rw_lnx_g[slot].reshape(1, d),
                 'lnx_b': rw_lnx_b[slot].reshape(1, d)}
            x, xc_new = _rw_layer(x, xc, mod, mod_c, p, want_ctx, g1, b1, alpha)
        else:
            if kind == 0:
                o, oc, wo = _gqa_layer(x, xc, mod, mod_c, cast(att_wqkv[slot]), cast(att_wo[slot]),
                                       att_q_norm[slot], att_k_norm[slot], want_ctx)
            else:
                o, oc, wo = _na_layer(x, xc, mod, mod_c, cast(na_wqkv[slot]), cast(na_wo[slot]), na_rpb[slot],
                                      want_ctx)
            x = _out_ln(o, wo, x, mod, True, g1, b1, alpha)
            xc_new = _out_ln(oc, wo, xc, mod_c, False, g1, b1, alpha) if want_ctx else None
        w1, w2 = cast(mlp_w1[i]), cast(mlp_w2[i])
        x = _mlp(x, mod, True, w1, w2, g2, b2, alpha)
        if want_ctx:
            xc = _mlp(xc_new, mod_c, False, w1, w2, g2, b2, alpha)
    return x
```

```python
import functools

import numpy as np
import jax
import jax.numpy as jnp
from jax import lax
from jax.experimental import pallas as pl
from jax.experimental.pallas import tpu as pltpu

F32 = jnp.float32
MXU_DTYPE = jnp.bfloat16

GRID_W = 64
N_MOD = 6
N_MIXERS = 3
LN_EPS = 1e-6
ATT_HEAD_DIM = 128
ATT_KV_HEADS = 2
ROPE_BASE = 10000.0
GQA_KEY_CHUNK = 512
NA_HEAD_DIM = 64
NA_WIN_H = 8
NA_WIN_W = 16
NA_Q_ROWS = 64
RW_HEAD = 64
RW_GN_EPS = 64e-5
RW_CHUNK = 64
LANES = 128
NEG_BIG = -1e30
LOG2E = 1.4426950408889634

VMEM_LIMIT = 56 * 1024 * 1024

PROJ_ROWS = 512
MLP_ROWS = 1024
MLP_FF_TILE = 2048
GQA_Q_ROWS = 512
RW_OUT_ROWS = 512
RW_SCAN_CHUNKS = 8
MOD_COL_TILES = 4


def _cparams(*sem):
    return pltpu.CompilerParams(dimension_semantics=sem, vmem_limit_bytes=VMEM_LIMIT)


def _mm(a, b):
    return jnp.dot(a.astype(MXU_DTYPE), b.astype(MXU_DTYPE), preferred_element_type=F32)


def _mm_nt(a, b):
    return lax.dot_general(a.astype(MXU_DTYPE), b.astype(MXU_DTYPE), (((1,), (1,)), ((), ())),
                           preferred_element_type=F32)


def _mm_tn(a, b):
    return lax.dot_general(a.astype(MXU_DTYPE), b.astype(MXU_DTYPE), (((0,), (0,)), ((), ())),
                           preferred_element_type=F32)


def _split(a, n):
    parts = []
    for _ in range(n - 1):
        p = a.astype(MXU_DTYPE)
        parts.append(p)
        a = a - p.astype(F32)
    parts.append(a.astype(MXU_DTYPE))
    return parts


def _mm_rhs_exact(a, m01, n=2):
    m01 = m01.astype(MXU_DTYPE)
    return sum(jnp.dot(p, m01, preferred_element_type=F32) for p in _split(a, n))


def _mm_hi(a, b):
    a1, a2 = _split(a, 2)
    b1, b2 = _split(b, 2)
    d = functools.partial(jnp.dot, preferred_element_type=F32)
    return d(a1, b1) + d(a1, b2) + d(a2, b1)


def _sigmoid(x):
    return 1.0 / (1.0 + jnp.exp(-x))


def _layernorm(z, g, b):
    mu = jnp.mean(z, -1, keepdims=True)
    dz = z - mu
    var = jnp.mean(dz * dz, -1, keepdims=True)
    return dz * lax.rsqrt(var + LN_EPS) * g + b


def _mod_part(mod_ref, j, d):
    return mod_ref[0, :, j * d:(j + 1) * d]


def _head_ones(n=LANES):
    r = lax.broadcasted_iota(jnp.int32, (n, n), 0)
    c = lax.broadcasted_iota(jnp.int32, (n, n), 1)
    return ((r // RW_HEAD) == (c // RW_HEAD)).astype(F32)


def _modulation_kernel(c_ref, w_ref, b_ref, o_ref):
    c = c_ref[...]
    o_ref[0] = _mm_hi(c * _sigmoid(c), w_ref[0]) + b_ref[0]


def _modulation(cond, mod_w, mod_b):
    depth, d, n = mod_w.shape
    rows = cond.shape[0]
    tn = n // MOD_COL_TILES
    return pl.pallas_call(
        _modulation_kernel,
        grid=(depth, n // tn),
        in_specs=[pl.BlockSpec((rows, d), lambda i, j: (0, 0)),
                  pl.BlockSpec((1, d, tn), lambda i, j: (i, 0, j)),
                  pl.BlockSpec((1, 1, tn), lambda i, j: (i, 0, j))],
        out_specs=pl.BlockSpec((1, rows, tn), lambda i, j: (i, 0, j)),
        out_shape=jax.ShapeDtypeStruct((depth, rows, n), F32),
        compiler_params=_cparams("parallel", "parallel"),
        name="modulation",
    )(cond, mod_w, mod_b.reshape(depth, 1, n))


def _out_ln_kernel(o_ref, w_ref, x_ref, mod_ref, g_ref, b_ref, y_ref, *, d, alpha):
    y = _mm(o_ref[0], w_ref[...])
    z = alpha * x_ref[0] + _mod_part(mod_ref, 2, d) * y
    y_ref[0] = _layernorm(z, g_ref[...], b_ref[...])


def _fold(a):
    return a.reshape(1, -1, a.shape[-1])


def _unfold(a, bsz):
    return a.reshape(bsz, -1, a.shape[-1])


def _mod_spec(d, per_batch):
    if per_batch:
        return pl.BlockSpec((1, 1, N_MOD * d), lambda b, *_: (b, 0, 0))
    return pl.BlockSpec((1, 1, N_MOD * d), lambda b, *_: (0, 0, 0))


def _out_ln(o, wo, x, mod, per_batch, ln_g, ln_b, alpha):
    bsz, t, d = x.shape
    tm = min(PROJ_ROWS, t)
    tok = lambda b, i: (b, i, 0)
    const = lambda b, i: (0, 0)
    return pl.pallas_call(
        functools.partial(_out_ln_kernel, d=d, alpha=alpha),
        grid=(bsz, t // tm),
        in_specs=[pl.BlockSpec((1, tm, d), tok), pl.BlockSpec((d, d), const), pl.BlockSpec((1, tm, d), tok),
                  _mod_spec(d, per_batch), pl.BlockSpec((1, d), const), pl.BlockSpec((1, d), const)],
        out_specs=pl.BlockSpec((1, tm, d), tok),
        out_shape=jax.ShapeDtypeStruct((bsz, t, d), F32),
        compiler_params=_cparams("parallel", "parallel"),
        name="out_ln",
    )(o, wo, x, mod, ln_g, ln_b)


def _mlp_kernel(x_ref, mod_ref, w1_ref, w2_ref, g_ref, b_ref, y_ref, h_scr, acc_scr, *, d, alpha):
    j = pl.program_id(2)

    @pl.when(j == 0)
    def _():
        h = x_ref[0] * (1.0 + _mod_part(mod_ref, 4, d)) + _mod_part(mod_ref, 3, d)
        h_scr[...] = h.astype(h_scr.dtype)
        acc_scr[...] = jnp.zeros_like(acc_scr)

    a = jnp.square(jnp.maximum(_mm(h_scr[...], w1_ref[...]), 0.0))
    acc_scr[...] += _mm(a, w2_ref[...])

    @pl.when(j == pl.num_programs(2) - 1)
    def _():
        z = alpha * x_ref[0] + _mod_part(mod_ref, 5, d) * acc_scr[...]
        y_ref[0] = _layernorm(z, g_ref[...], b_ref[...])


def _mlp(x, mod, per_batch, w1, w2, ln_g, ln_b, alpha):
    bsz, t, d = x.shape
    ff = w1.shape[1]
    tm = min(MLP_ROWS, t)
    tf = min(MLP_FF_TILE, ff)
    tok = lambda b, i, j: (b, i, 0)
    const = lambda b, i, j: (0, 0)
    return pl.pallas_call(
        functools.partial(_mlp_kernel, d=d, alpha=alpha),
        grid=(bsz, t // tm, ff // tf),
        in_specs=[pl.BlockSpec((1, tm, d), tok), _mod_spec(d, per_batch),
                  pl.BlockSpec((d, tf), lambda b, i, j: (0, j)), pl.BlockSpec((tf, d), lambda b, i, j: (j, 0)),
                  pl.BlockSpec((1, d), const), pl.BlockSpec((1, d), const)],
        out_specs=pl.BlockSpec((1, tm, d), tok),
        out_shape=jax.ShapeDtypeStruct((bsz, t, d), F32),
        scratch_shapes=[pltpu.VMEM((tm, d), MXU_DTYPE), pltpu.VMEM((tm, d), F32)],
        compiler_params=_cparams("parallel", "parallel", "arbitrary"),
        name="mlp",
    )(x, mod, w1, w2, ln_g, ln_b)


def _gqa_qkv_kernel(x_ref, mod_ref, w_ref, gq_ref, gk_ref, cos_ref, sin_ref, q_ref, k_ref, v_ref,
                    *, d, n_q, n_kv):
    h = x_ref[0] * (1.0 + _mod_part(mod_ref, 1, d)) + _mod_part(mod_ref, 0, d)
    acc = _mm(h, w_ref[...])
    cos, sin = cos_ref[...], sin_ref[...]
    hd = ATT_HEAD_DIM
    src = lax.broadcasted_iota(jnp.int32, (hd, hd), 0)
    dst = lax.broadcasted_iota(jnp.int32, (hd, hd), 1)
    first = (dst & (hd // 2 - 1)) < hd // 4
    rot = (jnp.where(first & (src == dst + hd // 4), -1.0, 0.0)
           + jnp.where(jnp.logical_not(first) & (src == dst - hd // 4), 1.0, 0.0))

    def norm_rope(u, g):
        u = u * lax.rsqrt(jnp.mean(u * u, -1, keepdims=True) + LN_EPS) * g
        return u * cos + _mm(u, rot) * sin

    for i in range(n_q):
        q_ref[0, :, i * hd:(i + 1) * hd] = norm_rope(acc[:, i * hd:(i + 1) * hd], gq_ref[...]).astype(q_ref.dtype)
    for i in range(n_kv):
        c0 = (n_q + i) * hd
        k_ref[0, :, i * hd:(i + 1) * hd] = norm_rope(acc[:, c0:c0 + hd], gk_ref[...]).astype(k_ref.dtype)
    ones = jnp.ones((acc.shape[0], hd), v_ref.dtype)
    for i in range(n_kv):
        c0 = (n_q + n_kv + i) * hd
        v_ref[0, :, 2 * i * hd:(2 * i + 1) * hd] = acc[:, c0:c0 + hd].astype(v_ref.dtype)
        v_ref[0, :, (2 * i + 1) * hd:(2 * i + 2) * hd] = ones


def _gqa_qkv(x, mod, per_batch, w, gq, gk, cos, sin):
    bsz, t, d = x.shape
    hd = ATT_HEAD_DIM
    n_kv = ATT_KV_HEADS
    n_q = w.shape[1] // hd - 2 * n_kv
    tm = min(PROJ_ROWS, t)
    tok = lambda b, i: (b, i, 0)
    const = lambda b, i: (0, 0)
    tab = pl.BlockSpec((tm, hd), lambda b, i: (i, 0))
    return pl.pallas_call(
        functools.partial(_gqa_qkv_kernel, d=d, n_q=n_q, n_kv=n_kv),
        grid=(bsz, t // tm),
        in_specs=[pl.BlockSpec((1, tm, d), tok), _mod_spec(d, per_batch), pl.BlockSpec(w.shape, const),
                  pl.BlockSpec((1, hd), const), pl.BlockSpec((1, hd), const), tab, tab],
        out_specs=[pl.BlockSpec((1, tm, n_q * hd), tok), pl.BlockSpec((1, tm, n_kv * hd), tok),
                   pl.BlockSpec((1, tm, 2 * n_kv * hd), tok)],
        out_shape=[jax.ShapeDtypeStruct((bsz, t, n_q * hd), MXU_DTYPE),
                   jax.ShapeDtypeStruct((bsz, t, n_kv * hd), MXU_DTYPE),
                   jax.ShapeDtypeStruct((bsz, t, 2 * n_kv * hd), MXU_DTYPE)],
        compiler_params=_cparams("parallel", "parallel"),
        name="gqa_qkv",
    )(x, mod, w, gq, gk, cos, sin)


def _gqa_att_kernel(*refs, n_sets, groups, tk):
    q_ref, o_ref = refs[0], refs[-1]
    kv = refs[1:-1]
    hd = ATT_HEAD_DIM
    tq = q_ref.shape[1]
    q = jnp.concatenate([q_ref[0, :, g * hd:(g + 1) * hd] for g in range(groups)], axis=0)
    m = acc = None
    for j in range(n_sets):
        k_ref, v_ref = kv[2 * j], kv[2 * j + 1]
        t = k_ref.shape[1]
        for c0 in range(0, t, tk):
            c1 = min(c0 + tk, t)
            s = _mm_nt(q, k_ref[0, c0:c1, :])
            m_c = jnp.max(s, -1, keepdims=True)
            if m is None:
                m = m_c
                acc = _mm(jnp.exp2(s - m), v_ref[0, c0:c1, :])
            else:
                m_new = jnp.maximum(m, m_c)
                acc = acc * jnp.exp2(m - m_new) + _mm(jnp.exp2(s - m_new), v_ref[0, c0:c1, :])
                m = m_new
    o = acc[:, :hd] / acc[:, hd:]
    for g in range(groups):
        o_ref[0, :, g * hd:(g + 1) * hd] = o[g * tq:(g + 1) * tq].astype(o_ref.dtype)


def _gqa_att(q, kv_sets):
    bsz, s, dq = q.shape
    hd = ATT_HEAD_DIM
    n_kv = ATT_KV_HEADS
    groups = dq // hd // n_kv
    tq = min(GQA_Q_ROWS, s)
    in_specs = [pl.BlockSpec((1, tq, groups * hd), lambda b, kh, i: (b, i, kh))]
    args = [q]
    for k, v in kv_sets:
        t = k.shape[1]
        in_specs += [pl.BlockSpec((1, t, hd), lambda b, kh, i: (b, 0, kh)),
                     pl.BlockSpec((1, t, 2 * hd), lambda b, kh, i: (b, 0, kh))]
        args += [k, v]
    return pl.pallas_call(
        functools.partial(_gqa_att_kernel, n_sets=len(kv_sets), groups=groups, tk=GQA_KEY_CHUNK),
        grid=(bsz, n_kv, s // tq),
        in_specs=in_specs,
        out_specs=pl.BlockSpec((1, tq, groups * hd), lambda b, kh, i: (b, i, kh)),
        out_shape=jax.ShapeDtypeStruct((bsz, s, dq), MXU_DTYPE),
        compiler_params=_cparams("parallel", "parallel", "parallel"),
        name="gqa_att",
    )(*args)


def _rope_tables(n_tokens, head_dim):
    t = np.arange(n_tokens)
    row = (t // GRID_W).astype(np.float32)
    col = (t % GRID_W).astype(np.float32)
    half = head_dim // 2
    freqs = jnp.asarray(ROPE_BASE, F32) ** (-jnp.arange(0, half, 2, dtype=F32) / half)
    ang_r = jnp.asarray(row)[:, None] * freqs[None, :]
    ang_c = jnp.asarray(col)[:, None] * freqs[None, :]
    ang = jnp.concatenate([ang_r, ang_r, ang_c, ang_c], axis=-1)
    return jnp.cos(ang), jnp.sin(ang)


def _gqa_layer(x, xc, mod, mod_c, wqkv, wo, q_norm, k_norm, want_ctx):
    s, tc = x.shape[1], xc.shape[1]
    hd = ATT_HEAD_DIM
    gq = (q_norm * (hd ** -0.5 * LOG2E)).reshape(1, hd)
    gk = k_norm.reshape(1, hd)
    cos, sin = _rope_tables(s, hd)
    bsz = x.shape[0]
    one, zero = jnp.ones((bsz * tc, hd), F32), jnp.zeros((bsz * tc, hd), F32)
    q, k, v = _gqa_qkv(x, mod, True, wqkv, gq, gk, cos, sin)
    qc, kc, vc = [_unfold(a, bsz) for a in _gqa_qkv(_fold(xc), mod_c, False, wqkv, gq, gk, one, zero)]
    o = _gqa_att(q, [(k, v), (kc, vc)])
    oc = _gqa_att(qc, [(kc, vc)]) if want_ctx else None
    return o, oc, wo


def _na_qkv_kernel(x_ref, mod_ref, w_ref, q_ref, k_ref, v_ref, *, d, scale):
    h = x_ref[0] * (1.0 + _mod_part(mod_ref, 1, d)) + _mod_part(mod_ref, 0, d)
    acc = _mm(h, w_ref[...])
    q_ref[0] = (acc[:, :d] * scale).astype(q_ref.dtype)
    k_ref[0] = acc[:, d:2 * d].astype(k_ref.dtype)
    ones = jnp.ones((acc.shape[0], LANES), v_ref.dtype)
    for p in range(d // LANES):
        c0 = 2 * d + p * LANES
        v_ref[0, :, 2 * p * LANES:(2 * p + 1) * LANES] = acc[:, c0:c0 + LANES].astype(v_ref.dtype)
        v_ref[0, :, (2 * p + 1) * LANES:(2 * p + 2) * LANES] = ones


def _na_qkv(x, mod, per_batch, w):
    bsz, t, d = x.shape
    tm = min(PROJ_ROWS, t)
    tok = lambda b, i: (b, i, 0)
    out = jax.ShapeDtypeStruct((bsz, t, d), MXU_DTYPE)
    return pl.pallas_call(
        functools.partial(_na_qkv_kernel, d=d, scale=NA_HEAD_DIM ** -0.5 * LOG2E),
        grid=(bsz, t // tm),
        in_specs=[pl.BlockSpec((1, tm, d), tok), _mod_spec(d, per_batch), pl.BlockSpec(w.shape, lambda b, i: (0, 0))],
        out_specs=[pl.BlockSpec((1, tm, d), tok)] * 2 + [pl.BlockSpec((1, tm, 2 * d), tok)],
        out_shape=[out, out, jax.ShapeDtypeStruct((bsz, t, 2 * d), MXU_DTYPE)],
        compiler_params=_cparams("parallel", "parallel"),
        name="na_qkv",
    )(x, mod, w)


def _pair_softmax_att(q, ks, vs, biases):
    lane = lax.broadcasted_iota(jnp.int32, q.shape, 1)
    first = lane < NA_HEAD_DIM
    zero = jnp.zeros_like(q)
    qm = [jnp.where(first, q, zero), jnp.where(first, zero, q)]
    m, acc = [None, None], [None, None]
    for j, (k, v) in enumerate(zip(ks, vs)):
        for h in range(2):
            s = _mm_nt(qm[h], k)
            if biases[h][j] is not None:
                s = s + biases[h][j]
            m_c = jnp.max(s, -1, keepdims=True)
            if m[h] is None:
                m[h] = m_c
                acc[h] = _mm(jnp.exp2(s - m_c), v)
            else:
                m_new = jnp.maximum(m[h], m_c)
                acc[h] = acc[h] * jnp.exp2(m[h] - m_new) + _mm(jnp.exp2(s - m_new), v)
                m[h] = m_new
    outs = [a[:, :LANES] / a[:, LANES:] for a in acc]
    return jnp.where(first, outs[0], outs[1])


def _na_att_kernel(q_ref, k_ref, v_ref, kc_ref, vc_ref, bias_ref, o_ref, *, rows):
    i = pl.program_id(2)
    q_rows = q_ref.shape[1] // GRID_W
    win = NA_WIN_H * GRID_W
    w = GRID_W
    first = lax.broadcasted_iota(jnp.int32, (w, LANES), 1) < NA_HEAD_DIM
    qm, kw, vw, bias = [], [], [], []
    for rq in range(q_rows):
        row_q = i * q_rows + rq
        r0 = jnp.clip(row_q - NA_WIN_H // 2, 0, rows - NA_WIN_H)
        start = pl.multiple_of(r0 * w, w)
        q = q_ref[0, rq * w:(rq + 1) * w, :]
        zero = jnp.zeros_like(q)
        qm.append(jnp.concatenate([jnp.where(first, q, zero), jnp.where(first, zero, q)], axis=0))
        kw.append(k_ref[0, pl.ds(start, win), :])
        vw.append(v_ref[0, pl.ds(start, win), :])
        e = r0 - row_q + NA_WIN_H - 1
        bias.append(jnp.concatenate([bias_ref[0, 0, e], bias_ref[0, 1, e]], axis=0))
    each = lambda fn, *lists: [fn(*args) for args in zip(*lists)]
    s_ctx = _mm_nt(jnp.concatenate(qm, axis=0), kc_ref[0])
    s_ctx = [s_ctx[2 * w * rq:2 * w * (rq + 1)] for rq in range(q_rows)]
    s_loc = each(lambda a, b, c: _mm_nt(a, b) + c, qm, kw, bias)
    m = each(lambda a, b: jnp.maximum(jnp.max(a, -1, keepdims=True), jnp.max(b, -1, keepdims=True)), s_loc, s_ctx)
    acc_ctx = _mm(jnp.concatenate(each(lambda a, mm: jnp.exp2(a - mm), s_ctx, m), axis=0), vc_ref[0])
    acc = each(lambda a, mm, v: _mm(jnp.exp2(a - mm), v), s_loc, m, vw)
    for rq in range(q_rows):
        a = acc[rq] + acc_ctx[2 * w * rq:2 * w * (rq + 1)]
        o = a[:, :LANES] / a[:, LANES:]
        o_ref[0, rq * w:(rq + 1) * w, :] = jnp.where(first, o[:w], o[w:]).astype(o_ref.dtype)


def _na_bias_tables(rpb, scale):
    pad = GRID_W
    rpb_p = jnp.pad(rpb, ((0, 0), (0, 0), (pad, pad)))
    cmat = jnp.stack([rpb_p[:, :, pad + NA_WIN_W - 1 - cq: pad + NA_WIN_W - 1 - cq + GRID_W]
                      for cq in range(GRID_W)], axis=2)
    cq, ck = np.arange(GRID_W)[:, None], np.arange(GRID_W)[None, :]
    c0 = np.clip(cq - NA_WIN_W // 2, 0, GRID_W - NA_WIN_W)
    vcol = (ck >= c0) & (ck < c0 + NA_WIN_W)
    cmat = jnp.where(vcol[None, None], cmat * scale, NEG_BIG)
    return jnp.stack([jnp.concatenate([cmat[:, e + j] for j in range(NA_WIN_H)], axis=-1)
                      for e in range(NA_WIN_H)], axis=1)


def _na_att(q, k, v, kc, vc, bias):
    bsz, s, d = q.shape
    tc = kc.shape[1]
    rows = s // GRID_W
    q_rows = min(NA_Q_ROWS, rows)
    assert rows % q_rows == 0 and rows >= NA_WIN_H
    tq = q_rows * GRID_W
    pairs = d // LANES
    q_spec = pl.BlockSpec((1, tq, LANES), lambda hp, b, i: (b, i, hp))
    whole = lambda t, width: pl.BlockSpec((1, t, width), lambda hp, b, i: (b, 0, hp))
    bias_spec = pl.BlockSpec((1, 2) + bias.shape[1:], lambda hp, b, i: (0, hp, 0, 0, 0))
    return pl.pallas_call(
        functools.partial(_na_att_kernel, rows=rows),
        grid=(pairs, bsz, rows // q_rows),
        in_specs=[q_spec, whole(s, LANES), whole(s, 2 * LANES), whole(tc, LANES), whole(tc, 2 * LANES), bias_spec],
        out_specs=q_spec,
        out_shape=jax.ShapeDtypeStruct((bsz, s, d), MXU_DTYPE),
        compiler_params=_cparams("parallel", "parallel", "parallel"),
        name="na_att",
    )(q, k, v, kc, vc, bias[None])


def _pair_att_kernel(q_ref, k_ref, v_ref, o_ref):
    o_ref[0] = _pair_softmax_att(q_ref[0], [k_ref[0]], [v_ref[0]], [[None], [None]]).astype(o_ref.dtype)


def _pair_att(q, k, v):
    bsz, t, d = q.shape
    spec = pl.BlockSpec((1, t, LANES), lambda b, hp: (b, 0, hp))
    return pl.pallas_call(
        _pair_att_kernel,
        grid=(bsz, d // LANES),
        in_specs=[spec, spec, pl.BlockSpec((1, t, 2 * LANES), lambda b, hp: (b, 0, hp))],
        out_specs=spec,
        out_shape=jax.ShapeDtypeStruct((bsz, t, d), MXU_DTYPE),
        compiler_params=_cparams("parallel", "parallel"),
        name="na_ctx_att",
    )(q, k, v)


def _na_layer(x, xc, mod, mod_c, wqkv, wo, rpb, want_ctx):
    q, k, v = _na_qkv(x, mod, True, wqkv)
    qc, kc, vc = [_unfold(a, x.shape[0]) for a in _na_qkv(_fold(xc), mod_c, False, wqkv)]
    o = _na_att(q, k, v, kc, vc, _na_bias_tables(rpb, LOG2E))
    oc = _pair_att(qc, kc, vc) if want_ctx else None
    return o, oc, wo


def _rw_proj_kernel(x_ref, xp_ref, xn_ref, mod_ref, mu_ref, wr_ref, wk_ref, wv_ref, w1_ref, w2_ref, a1_ref, a2_ref,
                    g1_ref, g2_ref, w0_ref, a0_ref, r_ref, k_ref, v_ref, g_ref, lw_ref, as_ref, *, d):
    t = pl.program_id(1)
    nt = pl.num_programs(1)
    scale = 1.0 + _mod_part(mod_ref, 1, d)
    shift = _mod_part(mod_ref, 0, d)
    h = x_ref[0] * scale + shift
    tm = h.shape[0]
    h_prev = (xp_ref[0, 7:8, :] * scale + shift) * (t > 0).astype(F32)
    h_next = (xn_ref[0, 0:1, :] * scale + shift) * (t < nt - 1).astype(F32)
    row = lax.broadcasted_iota(jnp.int32, h.shape, 0)
    prev = jnp.where(row == 0, h_prev, pltpu.roll(h, 1, 0))
    nxt = jnp.where(row == tm - 1, h_next, pltpu.roll(h, tm - 1, 0))
    xx = 0.5 * (prev + nxt) - h
    mix = lambda j: h + xx * mu_ref[j:j + 1, :]

    r_ref[0] = _mm(mix(0), wr_ref[...]).astype(r_ref.dtype)
    k_ref[0] = _mm(mix(2), wk_ref[...]).astype(k_ref.dtype)
    v_ref[0] = _mm(mix(3), wv_ref[...]).astype(v_ref.dtype)
    g_ref[0] = _mm(_sigmoid(_mm(mix(5), g1_ref[...])), g2_ref[...]).astype(g_ref.dtype)

    tw = jnp.tanh(_mm(mix(1), w1_ref[...]))
    al = _mm(mix(4), a1_ref[...])
    first = lax.broadcasted_iota(jnp.int32, tw.shape, 1) < tw.shape[1] // 2
    zero = jnp.zeros_like(tw)
    for n in range(2):
        pick = lambda u: jnp.where(first, u, zero) if n == 0 else jnp.where(first, zero, u)
        z = -(w0_ref[n:n + 1, :] + _mm(pick(tw), w2_ref[...]))
        softplus = jnp.maximum(z, 0.0) + jnp.log(1.0 + jnp.exp(-jnp.abs(z)))
        lw_ref[n, 0] = -jnp.exp(-softplus - 0.5)
        as_ref[n, 0] = _sigmoid(a0_ref[n:n + 1, :] + _mm(pick(al), a2_ref[...])).astype(as_ref.dtype)


def _rw_proj(x, mod, per_batch, p):
    bsz, t, d = x.shape
    tm = min(PROJ_ROWS, t)
    n8 = t // 8
    tok = lambda b, i: (b, i, 0)
    dtok = lambda b, i: (0, b, i, 0)
    full = lambda a: pl.BlockSpec(a.shape, lambda b, i: (0,) * a.ndim)
    halo_prev = pl.BlockSpec((1, 8, d), lambda b, i: (b, jnp.maximum(i * (tm // 8) - 1, 0), 0))
    halo_next = pl.BlockSpec((1, 8, d), lambda b, i: (b, jnp.minimum((i + 1) * (tm // 8), n8 - 1), 0))
    weights = [p['mu'], p['wr'], p['wk'], p['wv'], p['w1'], p['w2'], p['a1'], p['a2'], p['g1'], p['g2'],
               p['w0'], p['a0']]
    one = jax.ShapeDtypeStruct((bsz, t, d), MXU_DTYPE)
    two = lambda dtype: jax.ShapeDtypeStruct((2, bsz, t, d), dtype)
    return pl.pallas_call(
        functools.partial(_rw_proj_kernel, d=d),
        grid=(bsz, t // tm),
        in_specs=[pl.BlockSpec((1, tm, d), tok), halo_prev, halo_next, _mod_spec(d, per_batch)]
                 + [full(a) for a in weights],
        out_specs=[pl.BlockSpec((1, tm, d), tok)] * 4 + [pl.BlockSpec((2, 1, tm, d), dtok)] * 2,
        out_shape=[one, one, one, one, two(F32), two(MXU_DTYPE)],
        compiler_params=_cparams("parallel", "parallel"),
        name="rw_proj",
    )(x, x, x, mod, *weights)


def _rw_scan_kernel(r_ref, k_ref, v_ref, lw_ref, as_ref, kk_ref, ka_ref, z0_ref, y_ref, zf_ref, z_scr,
                    *, pairs, sub, reverse):
    c = pl.program_id(1)
    L = RW_CHUNK
    H = RW_HEAD
    assert 2 * L == LANES and 2 * H == LANES

    @pl.when(c == 0)
    def _():
        z_scr[...] = z0_ref[0, 0]

    sign = -1 if reverse else 1
    t_i = lax.broadcasted_iota(jnp.int32, (L, 2 * L), 0)
    s_i = lax.broadcasted_iota(jnp.int32, (L, 2 * L), 1) & (L - 1)
    dt = sign * (s_i - t_i)
    strict_c = dt < 0
    incl_c = dt <= 0
    tri2 = incl_c.astype(F32)
    eye_c = (dt == 0).astype(F32)
    r2 = lax.broadcasted_iota(jnp.int32, (LANES, LANES), 0)
    c2 = lax.broadcasted_iota(jnp.int32, (LANES, LANES), 1)
    eye2 = (r2 == c2).astype(F32)
    same_head = _head_ones()
    first = lax.broadcasted_iota(jnp.int32, (L, LANES), 1) < H
    zero = jnp.zeros((L, LANES), F32)

    head0 = lambda u: jnp.where(first, u, zero)
    head1 = lambda u: jnp.where(first, zero, u)
    bd = lambda u: jnp.concatenate([head0(u), head1(u)], axis=0)

    each = lambda fn, *lists: [fn(*args) for args in zip(*lists)]
    sls = [slice(p * LANES, (p + 1) * LANES) for p in range(pairs)]
    cat0 = lambda *u: jnp.concatenate(u, axis=0)
    cat1 = lambda *u: jnp.concatenate(u, axis=1)

    mm_each = lambda lhs, w: [_mm(a, b) for a, b in zip(lhs, w)]

    def chunk_terms(rw):
        lw = [lw_ref[0, 0, rw, sl] for sl in sls]
        r = [r_ref[0, rw, sl].astype(F32) for sl in sls]
        k = [k_ref[0, rw, sl].astype(F32) for sl in sls]
        v = [v_ref[0, rw, sl].astype(F32) for sl in sls]
        a_s = [as_ref[0, 0, rw, sl].astype(F32) for sl in sls]

        c_in = each(lambda u: _mm(tri2, cat0(*_split(u, 2))), lw)
        c_all = each(lambda u: jnp.sum(u, axis=0, keepdims=True), lw)
        kk = each(lambda u, sl: u * kk_ref[:, sl], k, sls)
        ss = each(lambda u: _mm(u * u, same_head), kk)
        kk = each(lambda u, s: u * lax.rsqrt(jnp.maximum(s, 1e-12)), kk, ss)
        b_v = each(lambda u, a: u * a, kk, a_s)
        k_d = each(lambda u, a, sl: u * (1.0 + (a - 1.0) * ka_ref[:, sl]), k, a_s, sls)
        e_neg = each(lambda ci: jnp.exp(-ci), c_in)
        e_rem = each(lambda ci, ca: jnp.exp(ca - ci), c_in, c_all)
        a_t = each(lambda u, ci, l: -u * jnp.exp(ci - l), kk, c_in, lw)
        r_t = each(lambda u, ci: u * jnp.exp(ci), r, c_in)
        b_t = each(lambda u, e: u * e, b_v, e_neg)
        k_t = each(lambda u, e: u * e, k_d, e_neg)
        b_h = each(lambda u, e: u * e, b_v, e_rem)
        k_h = each(lambda u, e: u * e, k_d, e_rem)
        yield None

        x = each(lambda a, rr, b, kt: _mm_nt(cat0(a, rr), cat0(bd(b), bd(kt))), a_t, r_t, b_t, k_t)
        m_ab = each(lambda u: jnp.where(strict_c, u[:L, :2 * L], 0.0), x)
        m_ak = each(lambda u: jnp.where(strict_c, u[:L, 2 * L:], 0.0), x)
        n_rb = each(lambda u: jnp.where(incl_c, u[L:, :2 * L], 0.0), x)
        n_rk = each(lambda u: jnp.where(incl_c, u[L:, 2 * L:], 0.0), x)

        inv = each(lambda m: eye_c + m, m_ab)
        yield None
        pw = mm_each(m_ab, each(bd, m_ab))
        for _ in range(int(np.log2(L)) - 2):
            yield None
            st = mm_each(each(cat0, pw, inv), each(bd, pw))
            pw = each(lambda s: s[:L], st)
            inv = each(lambda b, s: b + s[L:], inv, st)
        yield None
        inv = each(lambda b, s: b + s, inv, mm_each(inv, each(bd, pw)))
        mv = mm_each(each(cat0, m_ak, n_rk), each(bd, v))
        yield None
        tw = each(lambda i, a, m: _mm(i, cat1(bd(a), bd(m[:L]))), inv, a_t, mv)
        p1 = each(lambda u: u[:, :LANES], tw)
        p2 = each(lambda u: u[:, LANES:], tw)
        yield None
        nw = each(lambda m, a, b: _mm(m, cat1(bd(a), bd(b))), n_rb, p1, p2)
        q1 = each(lambda rr, u: rr + u[:, :LANES], r_t, nw)
        q2 = each(lambda u, m: u[:, LANES:] + m[L:], nw, mv)
        gh = each(lambda b, kh, a, c_, u: _mm_tn(cat0(b, kh), cat0(cat1(a, c_), cat1(zero, u))),
                  b_h, k_h, p1, p2, v)
        g_t = each(lambda ca, u: eye2 * jnp.exp(ca) + same_head * u[:, :LANES], c_all, gh)
        h_t = each(lambda u: same_head * u[:, LANES:], gh)
        yield q1, q2, g_t, h_t

    chunk_rows = [slice(u * L, (u + 1) * L) for u in (range(sub - 1, -1, -1) if reverse else range(sub))]
    gens = [chunk_terms(rw) for rw in chunk_rows]
    terms = [None] * sub
    while any(t is None for t in terms):
        terms = [next(gen) for gen in gens]
    z = [z_scr[p] for p in range(pairs)]
    for rw, (q1, q2, g_t, h_t) in zip(chunk_rows, terms):
        yz = mm_each(each(cat0, q1, g_t), z)
        for p in range(pairs):
            y_ref[0, rw, sls[p]] = (yz[p][:L] + q2[p]).astype(y_ref.dtype)
            z[p] = yz[p][L:] + h_t[p]
    for p in range(pairs):
        z_scr[p] = z[p]

    @pl.when(c == pl.num_programs(1) - 1)
    def _():
        zf_ref[0] = z_scr[...]


def _rw_scan(r, k, v, lw, a_s, kk_w, ka_w, z0, reverse):
    bsz, t, d = r.shape
    sub = min(RW_SCAN_CHUNKS, t // RW_CHUNK)
    rows = sub * RW_CHUNK
    assert t % rows == 0
    n_c = t // rows
    n = int(reverse)
    pairs = d // LANES
    cidx = (lambda c: n_c - 1 - c) if reverse else (lambda c: c)
    tok = pl.BlockSpec((1, rows, d), lambda b, c: (b, cidx(c), 0))
    dtok = pl.BlockSpec((1, 1, rows, d), lambda b, c: (n, b, cidx(c), 0))
    vec = pl.BlockSpec((1, d), lambda b, c: (0, 0))
    return pl.pallas_call(
        functools.partial(_rw_scan_kernel, pairs=pairs, sub=sub, reverse=reverse),
        grid=(bsz, n_c),
        in_specs=[tok, tok, tok, dtok, dtok, vec, vec,
                  pl.BlockSpec((1, 1, pairs, LANES, LANES), lambda b, c: (n, b, 0, 0, 0))],
        out_specs=[tok, pl.BlockSpec((1, pairs, LANES, LANES), lambda b, c: (b, 0, 0, 0))],
        out_shape=[jax.ShapeDtypeStruct((bsz, t, d), MXU_DTYPE),
                   jax.ShapeDtypeStruct((bsz, pairs, LANES, LANES), F32)],
        scratch_shapes=[pltpu.VMEM((pairs, LANES, LANES), F32)],
        compiler_params=_cparams("parallel", "arbitrary"),
        name="rw_scan",
    )(r, k, v, lw, a_s, kk_w, ka_w, z0)


def _rw_out_kernel(yf_ref, yb_ref, r_ref, k_ref, v_ref, g_ref, as_ref, ka_ref, rk_ref, lg_ref, lb_ref, wo_ref, x_ref,
                   mod_ref, g_ln_ref, b_ln_ref, o_ref, u_scr, *, d, alpha):
    width = 2 * LANES
    same_head = _head_ones(width)
    inv_n = 1.0 / RW_HEAD
    for p in range(d // width):
        sl = slice(p * width, (p + 1) * width)
        y = yf_ref[0, :, sl].astype(F32) + yb_ref[0, :, sl].astype(F32)
        mu = _mm_rhs_exact(y, same_head) * inv_n
        dy = y - mu
        var = _mm(dy * dy, same_head) * inv_n
        yn = dy * lax.rsqrt(var + RW_GN_EPS) * lg_ref[:, sl] + lb_ref[:, sl]
        k = k_ref[0, :, sl].astype(F32)
        ka = ka_ref[:, sl]
        k_sum = k * ((1.0 + (as_ref[0, 0, :, sl].astype(F32) - 1.0) * ka)
                     + (1.0 + (as_ref[1, 0, :, sl].astype(F32) - 1.0) * ka))
        bonus = _mm(r_ref[0, :, sl].astype(F32) * k_sum * rk_ref[:, sl], same_head)
        u = (yn + bonus * v_ref[0, :, sl].astype(F32)) * g_ref[0, :, sl].astype(F32)
        u_scr[:, sl] = u.astype(u_scr.dtype)
    z = alpha * x_ref[0] + _mod_part(mod_ref, 2, d) * _mm(u_scr[...], wo_ref[...])
    o_ref[0] = _layernorm(z, g_ln_ref[...], b_ln_ref[...])


def _rw_out(y_f, y_b, r, k, v, g, a_s, p, x, mod, per_batch, ln_g, ln_b, alpha):
    bsz, t, d = x.shape
    tm = min(RW_OUT_ROWS, t)
    tok = pl.BlockSpec((1, tm, d), lambda b, i: (b, i, 0))
    dtok = pl.BlockSpec((2, 1, tm, d), lambda b, i: (0, b, i, 0))
    vec = pl.BlockSpec((1, d), lambda b, i: (0, 0))
    return pl.pallas_call(
        functools.partial(_rw_out_kernel, d=d, alpha=alpha),
        grid=(bsz, t // tm),
        in_specs=[tok, tok, tok, tok, tok, tok, dtok, vec, vec, vec, vec, pl.BlockSpec((d, d), lambda b, i: (0, 0)),
                  tok, _mod_spec(d, per_batch), vec, vec],
        out_specs=tok,
        out_shape=jax.ShapeDtypeStruct((bsz, t, d), F32),
        scratch_shapes=[pltpu.VMEM((tm, d), MXU_DTYPE)],
        compiler_params=_cparams("parallel", "parallel"),
        name="rw_out",
    )(y_f, y_b, r, k, v, g, a_s, p['k_a'], p['r_k'], p['lnx_g'], p['lnx_b'], p['wo'], x, mod, ln_g, ln_b)


def _rw_layer(x, xc, mod, mod_c, p, want_ctx, ln_g, ln_b, alpha):
    bsz, _, d = x.shape
    rc, kc, vc, gc, lwc, asc = _rw_proj(xc, mod_c, False, p)
    r, k, v, g, lw, a_s = _rw_proj(x, mod, True, p)
    z0 = jnp.zeros((2, bsz, d // LANES, LANES, LANES), F32)
    scan = lambda args, z, rev: _rw_scan(*args, p['k_k'], p['k_a'], z, rev)
    yc_f, zc_f = scan((rc, kc, vc, lwc, asc), z0, False)
    yc_b, zc_b = scan((rc, kc, vc, lwc, asc), z0, True)
    zc = jnp.stack([zc_f, zc_b])
    y_f, _ = scan((r, k, v, lw, a_s), zc, False)
    y_b, _ = scan((r, k, v, lw, a_s), zc, True)
    x_new = _rw_out(y_f, y_b, r, k, v, g, a_s, p, x, mod, True, ln_g, ln_b, alpha)
    xc_new = (_rw_out(yc_f, yc_b, rc, kc, vc, gc, asc, p, xc, mod_c, False, ln_g, ln_b, alpha)
              if want_ctx else None)
    return x_new, xc_new


def kernel(x, c, ctx, c_ctx, mod_w, mod_b, post_ln_g, post_ln_b, mlp_w1, mlp_w2, att_wqkv, att_wo, att_q_norm, att_k_norm, na_wqkv, na_wo, na_rpb, rw_mu, rw_wr, rw_wk, rw_wv, rw_wo, rw_w0, rw_w1, rw_w2, rw_a0, rw_a1, rw_a2, rw_g1, rw_g2, rw_k_k, rw_k_a, rw_r_k, rw_lnx_g, rw_lnx_b):
    depth = mod_w.shape[0]
    bsz, _, d = x.shape
    alpha = (2.0 * depth) ** 0.25
    cast = lambda a: a.astype(MXU_DTYPE)

    cond_rows = -(-(bsz + 1) // 8) * 8
    cond = jnp.zeros((cond_rows, d), F32).at[:bsz].set(c).at[bsz].set(c_ctx)
    mods = _modulation(cond, mod_w, mod_b)

    xc = ctx
    for i in range(depth):
        kind, slot = i % N_MIXERS, i // N_MIXERS
        want_ctx = i < depth - 1
        mod = mods[i, :bsz].reshape(bsz, 1, N_MOD * d)
        mod_c = mods[i, bsz:bsz + 1].reshape(1, 1, N_MOD * d)
        g1, b1 = post_ln_g[i, 0:1], post_ln_b[i, 0:1]
        g2, b2 = post_ln_g[i, 1:2], post_ln_b[i, 1:2]
        if kind == 2:
            cat = lambda a: jnp.concatenate([a[0], a[1]], axis=-1)
            p = {'mu': rw_mu[slot], 'wr': cast(rw_wr[slot]), 'wk': cast(rw_wk[slot]), 'wv': cast(rw_wv[slot]),
                 'wo': cast(rw_wo[slot]), 'w0': rw_w0[slot], 'a0': rw_a0[slot],
                 'w1': cast(cat(rw_w1[slot])), 'a1': cast(cat(rw_a1[slot])),
                 'w2': cast(rw_w2[slot].reshape(-1, d)), 'a2': cast(rw_a2[slot].reshape(-1, d)),
                 'g1': cast(rw_g1[slot]), 'g2': cast(rw_g2[slot]),
                 'k_k': rw_k_k[slot].reshape(1, d), 'k_a': rw_k_a[slot].reshape(1, d),
                 'r_k': rw_r_k[slot].reshape(1, d), 'lnx_g': rw_lnx_g[slot].reshape(1, d),
                 'lnx_b': rw_lnx_b[slot].reshape(1, d)}
            x, xc_new = _rw_layer(x, xc, mod, mod_c, p, want_ctx, g1, b1, alpha)
        else:
            if kind == 0:
                o, oc, wo = _gqa_layer(x, xc, mod, mod_c, cast(att_wqkv[slot]), cast(att_wo[slot]),
                                       att_q_norm[slot], att_k_norm[slot], want_ctx)
            else:
                o, oc, wo = _na_layer(x, xc, mod, mod_c, cast(na_wqkv[slot]), cast(na_wo[slot]), na_rpb[slot],
                                      want_ctx)
            x = _out_ln(o, wo, x, mod, True, g1, b1, alpha)
            xc_new = (_unfold(_out_ln(_fold(oc), wo, _fold(xc), mod_c, False, g1, b1, alpha), bsz)
                      if want_ctx else None)
        w1, w2 = cast(mlp_w1[i]), cast(mlp_w2[i])
        x = _mlp(x, mod, True, w1, w2, g2, b2, alpha)
        if want_ctx:
            xc = _unfold(_mlp(_fold(xc_new), mod_c, False, w1, w2, g2, b2, alpha), bsz)
    return x
```

```python
import functools

import numpy as np
import jax
import jax.numpy as jnp
from jax import lax
from jax.experimental import pallas as pl
from jax.experimental.pallas import tpu as pltpu

F32 = jnp.float32
MXU_DTYPE = jnp.bfloat16

GRID_W = 64
N_MOD = 6
N_MIXERS = 3
LN_EPS = 1e-6
ATT_HEAD_DIM = 128
ATT_KV_HEADS = 2
ROPE_BASE = 10000.0
GQA_KEY_CHUNK = 512
NA_HEAD_DIM = 64
NA_WIN_H = 8
NA_WIN_W = 16
NA_Q_ROWS = 64
RW_HEAD = 64
RW_GN_EPS = 64e-5
RW_DECAY_FLOOR_RATE = float(np.exp(-0.5))
RW_CHUNK = 64
LANES = 128
NEG_BIG = -1e30
LOG2E = 1.4426950408889634

VMEM_LIMIT = 56 * 1024 * 1024

PROJ_ROWS = 512
MLP_ROWS = 1024
MLP_FF_TILE = 2048
GQA_Q_ROWS = 512
RW_OUT_ROWS = 512
RW_SCAN_CHUNKS = 8
MOD_COL_TILES = 4


def _cparams(*sem):
    return pltpu.CompilerParams(dimension_semantics=sem, vmem_limit_bytes=VMEM_LIMIT)


def _mm(a, b):
    return jnp.dot(a.astype(MXU_DTYPE), b.astype(MXU_DTYPE), preferred_element_type=F32)


def _mm_nt(a, b):
    return lax.dot_general(a.astype(MXU_DTYPE), b.astype(MXU_DTYPE), (((1,), (1,)), ((), ())),
                           preferred_element_type=F32)


def _mm_tn(a, b):
    return lax.dot_general(a.astype(MXU_DTYPE), b.astype(MXU_DTYPE), (((0,), (0,)), ((), ())),
                           preferred_element_type=F32)


def _split(a, n):
    parts = []
    for _ in range(n - 1):
        p = a.astype(MXU_DTYPE)
        parts.append(p)
        a = a - p.astype(F32)
    parts.append(a.astype(MXU_DTYPE))
    return parts


def _mm_rhs_exact(a, m01, n=2):
    m01 = m01.astype(MXU_DTYPE)
    return sum(jnp.dot(p, m01, preferred_element_type=F32) for p in _split(a, n))


def _mm_hi(a, b):
    a1, a2 = _split(a, 2)
    b1, b2 = _split(b, 2)
    d = functools.partial(jnp.dot, preferred_element_type=F32)
    return d(a1, b1) + d(a1, b2) + d(a2, b1)


def _sigmoid(x):
    return 1.0 / (1.0 + jnp.exp(-x))


def _layernorm(z, g, b):
    mu = jnp.mean(z, -1, keepdims=True)
    dz = z - mu
    var = jnp.mean(dz * dz, -1, keepdims=True)
    return dz * lax.rsqrt(var + LN_EPS) * g + b


def _mod_part(mod_ref, j, d):
    return mod_ref[0, :, j * d:(j + 1) * d]


def _head_ones(n=LANES):
    r = lax.broadcasted_iota(jnp.int32, (n, n), 0)
    c = lax.broadcasted_iota(jnp.int32, (n, n), 1)
    return ((r // RW_HEAD) == (c // RW_HEAD)).astype(F32)


def _modulation_kernel(c_ref, w_ref, b_ref, o_ref):
    c = c_ref[...]
    o_ref[0] = _mm_hi(c * _sigmoid(c), w_ref[0]) + b_ref[0]


def _modulation(cond, mod_w, mod_b):
    depth, d, n = mod_w.shape
    rows = cond.shape[0]
    tn = n // MOD_COL_TILES
    return pl.pallas_call(
        _modulation_kernel,
        grid=(depth, n // tn),
        in_specs=[pl.BlockSpec((rows, d), lambda i, j: (0, 0)),
                  pl.BlockSpec((1, d, tn), lambda i, j: (i, 0, j)),
                  pl.BlockSpec((1, 1, tn), lambda i, j: (i, 0, j))],
        out_specs=pl.BlockSpec((1, rows, tn), lambda i, j: (i, 0, j)),
        out_shape=jax.ShapeDtypeStruct((depth, rows, n), F32),
        compiler_params=_cparams("parallel", "parallel"),
        name="modulation",
    )(cond, mod_w, mod_b.reshape(depth, 1, n))


def _out_ln_kernel(o_ref, w_ref, x_ref, mod_ref, g_ref, b_ref, y_ref, *, d, alpha):
    y = _mm(o_ref[0], w_ref[...])
    z = alpha * x_ref[0] + _mod_part(mod_ref, 2, d) * y
    y_ref[0] = _layernorm(z, g_ref[...], b_ref[...])


def _fold(a):
    return a.reshape(1, -1, a.shape[-1])


def _unfold(a, bsz):
    return a.reshape(bsz, -1, a.shape[-1])


def _mod_spec(d, per_batch):
    if per_batch:
        return pl.BlockSpec((1, 1, N_MOD * d), lambda b, *_: (b, 0, 0))
    return pl.BlockSpec((1, 1, N_MOD * d), lambda b, *_: (0, 0, 0))


def _out_ln(o, wo, x, mod, per_batch, ln_g, ln_b, alpha):
    bsz, t, d = x.shape
    tm = min(PROJ_ROWS, t)
    tok = lambda b, i: (b, i, 0)
    const = lambda b, i: (0, 0)
    return pl.pallas_call(
        functools.partial(_out_ln_kernel, d=d, alpha=alpha),
        grid=(bsz, t // tm),
        in_specs=[pl.BlockSpec((1, tm, d), tok), pl.BlockSpec((d, d), const), pl.BlockSpec((1, tm, d), tok),
                  _mod_spec(d, per_batch), pl.BlockSpec((1, d), const), pl.BlockSpec((1, d), const)],
        out_specs=pl.BlockSpec((1, tm, d), tok),
        out_shape=jax.ShapeDtypeStruct((bsz, t, d), F32),
        compiler_params=_cparams("parallel", "parallel"),
        name="out_ln",
    )(o, wo, x, mod, ln_g, ln_b)


def _mlp_kernel(x_ref, mod_ref, w1_ref, w2_ref, g_ref, b_ref, y_ref, h_scr, acc_scr, *, d, alpha):
    j = pl.program_id(2)

    @pl.when(j == 0)
    def _():
        h = x_ref[0] * (1.0 + _mod_part(mod_ref, 4, d)) + _mod_part(mod_ref, 3, d)
        h_scr[...] = h.astype(h_scr.dtype)
        acc_scr[...] = jnp.zeros_like(acc_scr)

    a = jnp.square(jnp.maximum(_mm(h_scr[...], w1_ref[...]), 0.0))
    acc_scr[...] += _mm(a, w2_ref[...])

    @pl.when(j == pl.num_programs(2) - 1)
    def _():
        z = alpha * x_ref[0] + _mod_part(mod_ref, 5, d) * acc_scr[...]
        y_ref[0] = _layernorm(z, g_ref[...], b_ref[...])


def _mlp(x, mod, per_batch, w1, w2, ln_g, ln_b, alpha):
    bsz, t, d = x.shape
    ff = w1.shape[1]
    tm = min(MLP_ROWS, t)
    tf = min(MLP_FF_TILE, ff)
    tok = lambda b, i, j: (b, i, 0)
    const = lambda b, i, j: (0, 0)
    return pl.pallas_call(
        functools.partial(_mlp_kernel, d=d, alpha=alpha),
        grid=(bsz, t // tm, ff // tf),
        in_specs=[pl.BlockSpec((1, tm, d), tok), _mod_spec(d, per_batch),
                  pl.BlockSpec((d, tf), lambda b, i, j: (0, j)), pl.BlockSpec((tf, d), lambda b, i, j: (j, 0)),
                  pl.BlockSpec((1, d), const), pl.BlockSpec((1, d), const)],
        out_specs=pl.BlockSpec((1, tm, d), tok),
        out_shape=jax.ShapeDtypeStruct((bsz, t, d), F32),
        scratch_shapes=[pltpu.VMEM((tm, d), MXU_DTYPE), pltpu.VMEM((tm, d), F32)],
        compiler_params=_cparams("parallel", "parallel", "arbitrary"),
        name="mlp",
    )(x, mod, w1, w2, ln_g, ln_b)


def _gqa_qkv_kernel(x_ref, mod_ref, w_ref, gq_ref, gk_ref, cos_ref, sin_ref, q_ref, k_ref, v_ref,
                    *, d, n_q, n_kv):
    h = x_ref[0] * (1.0 + _mod_part(mod_ref, 1, d)) + _mod_part(mod_ref, 0, d)
    acc = _mm(h, w_ref[...])
    cos, sin = cos_ref[...], sin_ref[...]
    hd = ATT_HEAD_DIM
    src = lax.broadcasted_iota(jnp.int32, (hd, hd), 0)
    dst = lax.broadcasted_iota(jnp.int32, (hd, hd), 1)
    first = (dst & (hd // 2 - 1)) < hd // 4
    rot = (jnp.where(first & (src == dst + hd // 4), -1.0, 0.0)
           + jnp.where(jnp.logical_not(first) & (src == dst - hd // 4), 1.0, 0.0))

    def norm_rope(u, g):
        u = u * lax.rsqrt(jnp.mean(u * u, -1, keepdims=True) + LN_EPS) * g
        return u * cos + _mm(u, rot) * sin

    for i in range(n_q):
        q_ref[0, :, i * hd:(i + 1) * hd] = norm_rope(acc[:, i * hd:(i + 1) * hd], gq_ref[...]).astype(q_ref.dtype)
    for i in range(n_kv):
        c0 = (n_q + i) * hd
        k_ref[0, :, i * hd:(i + 1) * hd] = norm_rope(acc[:, c0:c0 + hd], gk_ref[...]).astype(k_ref.dtype)
    ones = jnp.ones((acc.shape[0], hd), v_ref.dtype)
    for i in range(n_kv):
        c0 = (n_q + n_kv + i) * hd
        v_ref[0, :, 2 * i * hd:(2 * i + 1) * hd] = acc[:, c0:c0 + hd].astype(v_ref.dtype)
        v_ref[0, :, (2 * i + 1) * hd:(2 * i + 2) * hd] = ones


def _gqa_qkv(x, mod, per_batch, w, gq, gk, cos, sin):
    bsz, t, d = x.shape
    hd = ATT_HEAD_DIM
    n_kv = ATT_KV_HEADS
    n_q = w.shape[1] // hd - 2 * n_kv
    tm = min(PROJ_ROWS, t)
    tok = lambda b, i: (b, i, 0)
    const = lambda b, i: (0, 0)
    tab = pl.BlockSpec((tm, hd), lambda b, i: (i, 0))
    return pl.pallas_call(
        functools.partial(_gqa_qkv_kernel, d=d, n_q=n_q, n_kv=n_kv),
        grid=(bsz, t // tm),
        in_specs=[pl.BlockSpec((1, tm, d), tok), _mod_spec(d, per_batch), pl.BlockSpec(w.shape, const),
                  pl.BlockSpec((1, hd), const), pl.BlockSpec((1, hd), const), tab, tab],
        out_specs=[pl.BlockSpec((1, tm, n_q * hd), tok), pl.BlockSpec((1, tm, n_kv * hd), tok),
                   pl.BlockSpec((1, tm, 2 * n_kv * hd), tok)],
        out_shape=[jax.ShapeDtypeStruct((bsz, t, n_q * hd), MXU_DTYPE),
                   jax.ShapeDtypeStruct((bsz, t, n_kv * hd), MXU_DTYPE),
                   jax.ShapeDtypeStruct((bsz, t, 2 * n_kv * hd), MXU_DTYPE)],
        compiler_params=_cparams("parallel", "parallel"),
        name="gqa_qkv",
    )(x, mod, w, gq, gk, cos, sin)


def _gqa_att_kernel(*refs, n_sets, groups, tk):
    q_ref, o_ref = refs[0], refs[-1]
    kv = refs[1:-1]
    hd = ATT_HEAD_DIM
    tq = q_ref.shape[1]
    q = jnp.concatenate([q_ref[0, :, g * hd:(g + 1) * hd] for g in range(groups)], axis=0)
    m = acc = None
    for j in range(n_sets):
        k_ref, v_ref = kv[2 * j], kv[2 * j + 1]
        t = k_ref.shape[1]
        for c0 in range(0, t, tk):
            c1 = min(c0 + tk, t)
            s = _mm_nt(q, k_ref[0, c0:c1, :])
            m_c = jnp.max(s, -1, keepdims=True)
            if m is None:
                m = m_c
                acc = _mm(jnp.exp2(s - m), v_ref[0, c0:c1, :])
            else:
                m_new = jnp.maximum(m, m_c)
                acc = acc * jnp.exp2(m - m_new) + _mm(jnp.exp2(s - m_new), v_ref[0, c0:c1, :])
                m = m_new
    o = acc[:, :hd] / acc[:, hd:]
    for g in range(groups):
        o_ref[0, :, g * hd:(g + 1) * hd] = o[g * tq:(g + 1) * tq].astype(o_ref.dtype)


def _gqa_att(q, kv_sets):
    bsz, s, dq = q.shape
    hd = ATT_HEAD_DIM
    n_kv = ATT_KV_HEADS
    groups = dq // hd // n_kv
    tq = min(GQA_Q_ROWS, s)
    in_specs = [pl.BlockSpec((1, tq, groups * hd), lambda b, kh, i: (b, i, kh))]
    args = [q]
    for k, v in kv_sets:
        t = k.shape[1]
        in_specs += [pl.BlockSpec((1, t, hd), lambda b, kh, i: (b, 0, kh)),
                     pl.BlockSpec((1, t, 2 * hd), lambda b, kh, i: (b, 0, kh))]
        args += [k, v]
    return pl.pallas_call(
        functools.partial(_gqa_att_kernel, n_sets=len(kv_sets), groups=groups, tk=GQA_KEY_CHUNK),
        grid=(bsz, n_kv, s // tq),
        in_specs=in_specs,
        out_specs=pl.BlockSpec((1, tq, groups * hd), lambda b, kh, i: (b, i, kh)),
        out_shape=jax.ShapeDtypeStruct((bsz, s, dq), MXU_DTYPE),
        compiler_params=_cparams("parallel", "parallel", "parallel"),
        name="gqa_att",
    )(*args)


def _rope_tables(n_tokens, head_dim):
    t = np.arange(n_tokens)
    row = (t // GRID_W).astype(np.float32)
    col = (t % GRID_W).astype(np.float32)
    half = head_dim // 2
    freqs = jnp.asarray(ROPE_BASE, F32) ** (-jnp.arange(0, half, 2, dtype=F32) / half)
    ang_r = jnp.asarray(row)[:, None] * freqs[None, :]
    ang_c = jnp.asarray(col)[:, None] * freqs[None, :]
    ang = jnp.concatenate([ang_r, ang_r, ang_c, ang_c], axis=-1)
    return jnp.cos(ang), jnp.sin(ang)


def _gqa_layer(x, xc, mod, mod_c, wqkv, wo, q_norm, k_norm, want_ctx):
    s, tc = x.shape[1], xc.shape[1]
    hd = ATT_HEAD_DIM
    gq = (q_norm * (hd ** -0.5 * LOG2E)).reshape(1, hd)
    gk = k_norm.reshape(1, hd)
    cos, sin = _rope_tables(s, hd)
    bsz = x.shape[0]
    one, zero = jnp.ones((bsz * tc, hd), F32), jnp.zeros((bsz * tc, hd), F32)
    q, k, v = _gqa_qkv(x, mod, True, wqkv, gq, gk, cos, sin)
    qc, kc, vc = [_unfold(a, bsz) for a in _gqa_qkv(_fold(xc), mod_c, False, wqkv, gq, gk, one, zero)]
    o = _gqa_att(q, [(k, v), (kc, vc)])
    oc = _gqa_att(qc, [(kc, vc)]) if want_ctx else None
    return o, oc, wo


def _na_qkv_kernel(x_ref, mod_ref, w_ref, q_ref, k_ref, v_ref, *, d, scale):
    h = x_ref[0] * (1.0 + _mod_part(mod_ref, 1, d)) + _mod_part(mod_ref, 0, d)
    acc = _mm(h, w_ref[...])
    q_ref[0] = (acc[:, :d] * scale).astype(q_ref.dtype)
    k_ref[0] = acc[:, d:2 * d].astype(k_ref.dtype)
    ones = jnp.ones((acc.shape[0], LANES), v_ref.dtype)
    for p in range(d // LANES):
        c0 = 2 * d + p * LANES
        v_ref[0, :, 2 * p * LANES:(2 * p + 1) * LANES] = acc[:, c0:c0 + LANES].astype(v_ref.dtype)
        v_ref[0, :, (2 * p + 1) * LANES:(2 * p + 2) * LANES] = ones


def _na_qkv(x, mod, per_batch, w):
    bsz, t, d = x.shape
    tm = min(PROJ_ROWS, t)
    tok = lambda b, i: (b, i, 0)
    out = jax.ShapeDtypeStruct((bsz, t, d), MXU_DTYPE)
    return pl.pallas_call(
        functools.partial(_na_qkv_kernel, d=d, scale=NA_HEAD_DIM ** -0.5 * LOG2E),
        grid=(bsz, t // tm),
        in_specs=[pl.BlockSpec((1, tm, d), tok), _mod_spec(d, per_batch), pl.BlockSpec(w.shape, lambda b, i: (0, 0))],
        out_specs=[pl.BlockSpec((1, tm, d), tok)] * 2 + [pl.BlockSpec((1, tm, 2 * d), tok)],
        out_shape=[out, out, jax.ShapeDtypeStruct((bsz, t, 2 * d), MXU_DTYPE)],
        compiler_params=_cparams("parallel", "parallel"),
        name="na_qkv",
    )(x, mod, w)


def _pair_softmax_att(q, ks, vs, biases):
    lane = lax.broadcasted_iota(jnp.int32, q.shape, 1)
    first = lane < NA_HEAD_DIM
    zero = jnp.zeros_like(q)
    qm = [jnp.where(first, q, zero), jnp.where(first, zero, q)]
    m, acc = [None, None], [None, None]
    for j, (k, v) in enumerate(zip(ks, vs)):
        for h in range(2):
            s = _mm_nt(qm[h], k)
            if biases[h][j] is not None:
                s = s + biases[h][j]
            m_c = jnp.max(s, -1, keepdims=True)
            if m[h] is None:
                m[h] = m_c
                acc[h] = _mm(jnp.exp2(s - m_c), v)
            else:
                m_new = jnp.maximum(m[h], m_c)
                acc[h] = acc[h] * jnp.exp2(m[h] - m_new) + _mm(jnp.exp2(s - m_new), v)
                m[h] = m_new
    outs = [a[:, :LANES] / a[:, LANES:] for a in acc]
    return jnp.where(first, outs[0], outs[1])


def _na_att_kernel(q_ref, k_ref, v_ref, kc_ref, vc_ref, bias_ref, o_ref, *, rows):
    i = pl.program_id(2)
    q_rows = q_ref.shape[1] // GRID_W
    win = NA_WIN_H * GRID_W
    w = GRID_W
    first = lax.broadcasted_iota(jnp.int32, (w, LANES), 1) < NA_HEAD_DIM
    qm, kw, vw, bias = [], [], [], []
    for rq in range(q_rows):
        row_q = i * q_rows + rq
        r0 = jnp.clip(row_q - NA_WIN_H // 2, 0, rows - NA_WIN_H)
        start = pl.multiple_of(r0 * w, w)
        q = q_ref[0, rq * w:(rq + 1) * w, :]
        zero = jnp.zeros_like(q)
        qm.append(jnp.concatenate([jnp.where(first, q, zero), jnp.where(first, zero, q)], axis=0))
        kw.append(k_ref[0, pl.ds(start, win), :])
        vw.append(v_ref[0, pl.ds(start, win), :])
        e = r0 - row_q + NA_WIN_H - 1
        bias.append(jnp.concatenate([bias_ref[0, 0, e], bias_ref[0, 1, e]], axis=0))
    each = lambda fn, *lists: [fn(*args) for args in zip(*lists)]
    s_ctx = _mm_nt(jnp.concatenate(qm, axis=0), kc_ref[0])
    s_ctx = [s_ctx[2 * w * rq:2 * w * (rq + 1)] for rq in range(q_rows)]
    s_loc = each(lambda a, b, c: _mm_nt(a, b) + c, qm, kw, bias)
    m = each(lambda a, b: jnp.maximum(jnp.max(a, -1, keepdims=True), jnp.max(b, -1, keepdims=True)), s_loc, s_ctx)
    acc_ctx = _mm(jnp.concatenate(each(lambda a, mm: jnp.exp2(a - mm), s_ctx, m), axis=0), vc_ref[0])
    acc = each(lambda a, mm, v: _mm(jnp.exp2(a - mm), v), s_loc, m, vw)
    for rq in range(q_rows):
        a = acc[rq] + acc_ctx[2 * w * rq:2 * w * (rq + 1)]
        o = a[:, :LANES] / a[:, LANES:]
        o_ref[0, rq * w:(rq + 1) * w, :] = jnp.where(first, o[:w], o[w:]).astype(o_ref.dtype)


def _na_bias_tables(rpb, scale):
    pad = GRID_W
    rpb_p = jnp.pad(rpb, ((0, 0), (0, 0), (pad, pad)))
    cmat = jnp.stack([rpb_p[:, :, pad + NA_WIN_W - 1 - cq: pad + NA_WIN_W - 1 - cq + GRID_W]
                      for cq in range(GRID_W)], axis=2)
    cq, ck = np.arange(GRID_W)[:, None], np.arange(GRID_W)[None, :]
    c0 = np.clip(cq - NA_WIN_W // 2, 0, GRID_W - NA_WIN_W)
    vcol = (ck >= c0) & (ck < c0 + NA_WIN_W)
    cmat = jnp.where(vcol[None, None], cmat * scale, NEG_BIG)
    return jnp.stack([jnp.concatenate([cmat[:, e + j] for j in range(NA_WIN_H)], axis=-1)
                      for e in range(NA_WIN_H)], axis=1)


def _na_att(q, k, v, kc, vc, bias):
    bsz, s, d = q.shape
    tc = kc.shape[1]
    rows = s // GRID_W
    q_rows = min(NA_Q_ROWS, rows)
    assert rows % q_rows == 0 and rows >= NA_WIN_H
    tq = q_rows * GRID_W
    pairs = d // LANES
    q_spec = pl.BlockSpec((1, tq, LANES), lambda hp, b, i: (b, i, hp))
    whole = lambda t, width: pl.BlockSpec((1, t, width), lambda hp, b, i: (b, 0, hp))
    bias_spec = pl.BlockSpec((1, 2) + bias.shape[1:], lambda hp, b, i: (0, hp, 0, 0, 0))
    return pl.pallas_call(
        functools.partial(_na_att_kernel, rows=rows),
        grid=(pairs, bsz, rows // q_rows),
        in_specs=[q_spec, whole(s, LANES), whole(s, 2 * LANES), whole(tc, LANES), whole(tc, 2 * LANES), bias_spec],
        out_specs=q_spec,
        out_shape=jax.ShapeDtypeStruct((bsz, s, d), MXU_DTYPE),
        compiler_params=_cparams("parallel", "parallel", "parallel"),
        name="na_att",
    )(q, k, v, kc, vc, bias[None])


def _pair_att_kernel(q_ref, k_ref, v_ref, o_ref):
    o_ref[0] = _pair_softmax_att(q_ref[0], [k_ref[0]], [v_ref[0]], [[None], [None]]).astype(o_ref.dtype)


def _pair_att(q, k, v):
    bsz, t, d = q.shape
    spec = pl.BlockSpec((1, t, LANES), lambda b, hp: (b, 0, hp))
    return pl.pallas_call(
        _pair_att_kernel,
        grid=(bsz, d // LANES),
        in_specs=[spec, spec, pl.BlockSpec((1, t, 2 * LANES), lambda b, hp: (b, 0, hp))],
        out_specs=spec,
        out_shape=jax.ShapeDtypeStruct((bsz, t, d), MXU_DTYPE),
        compiler_params=_cparams("parallel", "parallel"),
        name="na_ctx_att",
    )(q, k, v)


def _na_layer(x, xc, mod, mod_c, wqkv, wo, rpb, want_ctx):
    q, k, v = _na_qkv(x, mod, True, wqkv)
    qc, kc, vc = [_unfold(a, x.shape[0]) for a in _na_qkv(_fold(xc), mod_c, False, wqkv)]
    o = _na_att(q, k, v, kc, vc, _na_bias_tables(rpb, LOG2E))
    oc = _pair_att(qc, kc, vc) if want_ctx else None
    return o, oc, wo


def _rw_proj_kernel(x_ref, xp_ref, xn_ref, mod_ref, mu_ref, wr_ref, wk_ref, wv_ref, w1_ref, w2_ref, a1_ref, a2_ref,
                    g1_ref, g2_ref, w0_ref, a0_ref, r_ref, k_ref, v_ref, g_ref, lw_ref, as_ref, *, d):
    t = pl.program_id(1)
    nt = pl.num_programs(1)
    scale = 1.0 + _mod_part(mod_ref, 1, d)
    shift = _mod_part(mod_ref, 0, d)
    h = x_ref[0] * scale + shift
    tm = h.shape[0]
    h_prev = (xp_ref[0, 7:8, :] * scale + shift) * (t > 0).astype(F32)
    h_next = (xn_ref[0, 0:1, :] * scale + shift) * (t < nt - 1).astype(F32)
    row = lax.broadcasted_iota(jnp.int32, h.shape, 0)
    prev = jnp.where(row == 0, h_prev, pltpu.roll(h, 1, 0))
    nxt = jnp.where(row == tm - 1, h_next, pltpu.roll(h, tm - 1, 0))
    xx = 0.5 * (prev + nxt) - h
    mix = lambda j: h + xx * mu_ref[j:j + 1, :]

    r_ref[0] = _mm(mix(0), wr_ref[...]).astype(r_ref.dtype)
    k_ref[0] = _mm(mix(2), wk_ref[...]).astype(k_ref.dtype)
    v_ref[0] = _mm(mix(3), wv_ref[...]).astype(v_ref.dtype)
    g_ref[0] = _mm(_sigmoid(_mm(mix(5), g1_ref[...])), g2_ref[...]).astype(g_ref.dtype)

    tw = jnp.tanh(_mm(mix(1), w1_ref[...]))
    al = _mm(mix(4), a1_ref[...])
    first = lax.broadcasted_iota(jnp.int32, tw.shape, 1) < tw.shape[1] // 2
    zero = jnp.zeros_like(tw)
    for n in range(2):
        pick = lambda u: jnp.where(first, u, zero) if n == 0 else jnp.where(first, zero, u)
        u = w0_ref[n:n + 1, :] + _mm(pick(tw), w2_ref[...])
        lw_ref[n, 0] = -RW_DECAY_FLOOR_RATE * _sigmoid(u)
        as_ref[n, 0] = _sigmoid(a0_ref[n:n + 1, :] + _mm(pick(al), a2_ref[...])).astype(as_ref.dtype)


def _rw_proj(x, mod, per_batch, p):
    bsz, t, d = x.shape
    tm = min(PROJ_ROWS, t)
    n8 = t // 8
    tok = lambda b, i: (b, i, 0)
    dtok = lambda b, i: (0, b, i, 0)
    full = lambda a: pl.BlockSpec(a.shape, lambda b, i: (0,) * a.ndim)
    halo_prev = pl.BlockSpec((1, 8, d), lambda b, i: (b, jnp.maximum(i * (tm // 8) - 1, 0), 0))
    halo_next = pl.BlockSpec((1, 8, d), lambda b, i: (b, jnp.minimum((i + 1) * (tm // 8), n8 - 1), 0))
    weights = [p['mu'], p['wr'], p['wk'], p['wv'], p['w1'], p['w2'], p['a1'], p['a2'], p['g1'], p['g2'],
               p['w0'], p['a0']]
    one = jax.ShapeDtypeStruct((bsz, t, d), MXU_DTYPE)
    two = lambda dtype: jax.ShapeDtypeStruct((2, bsz, t, d), dtype)
    return pl.pallas_call(
        functools.partial(_rw_proj_kernel, d=d),
        grid=(bsz, t // tm),
        in_specs=[pl.BlockSpec((1, tm, d), tok), halo_prev, halo_next, _mod_spec(d, per_batch)]
                 + [full(a) for a in weights],
        out_specs=[pl.BlockSpec((1, tm, d), tok)] * 4 + [pl.BlockSpec((2, 1, tm, d), dtok)] * 2,
        out_shape=[one, one, one, one, two(F32), two(MXU_DTYPE)],
        compiler_params=_cparams("parallel", "parallel"),
        name="rw_proj",
    )(x, x, x, mod, *weights)


def _rw_scan_kernel(r_ref, k_ref, v_ref, lw_ref, as_ref, kk_ref, ka_ref, z0_ref, y_ref, zf_ref, z_scr,
                    *, pairs, sub, reverse):
    c = pl.program_id(1)
    L = RW_CHUNK
    H = RW_HEAD
    assert 2 * L == LANES and 2 * H == LANES

    @pl.when(c == 0)
    def _():
        z_scr[...] = z0_ref[0, 0]

    sign = -1 if reverse else 1
    t_i = lax.broadcasted_iota(jnp.int32, (L, 2 * L), 0)
    s_i = lax.broadcasted_iota(jnp.int32, (L, 2 * L), 1) & (L - 1)
    dt = sign * (s_i - t_i)
    strict_c = dt < 0
    incl_c = dt <= 0
    tri2 = incl_c.astype(F32)
    eye_c = (dt == 0).astype(F32)
    r2 = lax.broadcasted_iota(jnp.int32, (LANES, LANES), 0)
    c2 = lax.broadcasted_iota(jnp.int32, (LANES, LANES), 1)
    eye2 = (r2 == c2).astype(F32)
    same_head = _head_ones()
    first = lax.broadcasted_iota(jnp.int32, (L, LANES), 1) < H
    zero = jnp.zeros((L, LANES), F32)

    head0 = lambda u: jnp.where(first, u, zero)
    head1 = lambda u: jnp.where(first, zero, u)
    bd = lambda u: jnp.concatenate([head0(u), head1(u)], axis=0)

    each = lambda fn, *lists: [fn(*args) for args in zip(*lists)]
    sls = [slice(p * LANES, (p + 1) * LANES) for p in range(pairs)]
    cat0 = lambda *u: jnp.concatenate(u, axis=0)
    cat1 = lambda *u: jnp.concatenate(u, axis=1)

    mm_each = lambda lhs, w: [_mm(a, b) for a, b in zip(lhs, w)]

    def chunk_terms(rw):
        lw = [lw_ref[0, 0, rw, sl] for sl in sls]
        r = [r_ref[0, rw, sl].astype(F32) for sl in sls]
        k = [k_ref[0, rw, sl].astype(F32) for sl in sls]
        v = [v_ref[0, rw, sl].astype(F32) for sl in sls]
        a_s = [as_ref[0, 0, rw, sl].astype(F32) for sl in sls]

        c_in = each(lambda u: _mm(tri2, cat0(*_split(u, 2))), lw)
        c_all = each(lambda u: jnp.sum(u, axis=0, keepdims=True), lw)
        kk = each(lambda u, sl: u * kk_ref[:, sl], k, sls)
        ss = each(lambda u: _mm(u * u, same_head), kk)
        kk = each(lambda u, s: u * lax.rsqrt(jnp.maximum(s, 1e-12)), kk, ss)
        b_v = each(lambda u, a: u * a, kk, a_s)
        k_d = each(lambda u, a, sl: u * (1.0 + (a - 1.0) * ka_ref[:, sl]), k, a_s, sls)
        e_neg = each(lambda ci: jnp.exp(-ci), c_in)
        e_all = each(jnp.exp, c_all)
        a_t = each(lambda u, ci, l: -u * jnp.exp(ci - l), kk, c_in, lw)
        r_t = each(lambda u, ci: u * jnp.exp(ci), r, c_in)
        b_t = each(lambda u, e: u * e, b_v, e_neg)
        k_t = each(lambda u, e: u * e, k_d, e_neg)
        b_h = each(lambda u, e: u * e, b_t, e_all)
        k_h = each(lambda u, e: u * e, k_t, e_all)
        yield None

        x = each(lambda a, rr, b, kt: _mm_nt(cat0(a, rr), cat0(bd(b), bd(kt))), a_t, r_t, b_t, k_t)
        m_ab = each(lambda u: jnp.where(strict_c, u[:L, :2 * L], 0.0), x)
        m_ak = each(lambda u: jnp.where(strict_c, u[:L, 2 * L:], 0.0), x)
        n_rb = each(lambda u: jnp.where(incl_c, u[L:, :2 * L], 0.0), x)
        n_rk = each(lambda u: jnp.where(incl_c, u[L:, 2 * L:], 0.0), x)

        inv = each(lambda m: eye_c + m, m_ab)
        yield None
        pw = mm_each(m_ab, each(bd, m_ab))
        for _ in range(int(np.log2(L)) - 2):
            yield None
            st = mm_each(each(cat0, pw, inv), each(bd, pw))
            pw = each(lambda s: s[:L], st)
            inv = each(lambda b, s: b + s[L:], inv, st)
        yield None
        inv = each(lambda b, s: b + s, inv, mm_each(inv, each(bd, pw)))
        mv = mm_each(each(cat0, m_ak, n_rk), each(bd, v))
        yield None
        tw = each(lambda i, a, m: _mm(i, cat1(bd(a), bd(m[:L]))), inv, a_t, mv)
        p1 = each(lambda u: u[:, :LANES], tw)
        p2 = each(lambda u: u[:, LANES:], tw)
        yield None
        nw = each(lambda m, a, b: _mm(m, cat1(bd(a), bd(b))), n_rb, p1, p2)
        q1 = each(lambda rr, u: rr + u[:, :LANES], r_t, nw)
        q2 = each(lambda u, m: u[:, LANES:] + m[L:], nw, mv)
        gh = each(lambda b, kh, a, c_, u: _mm_tn(cat0(b, kh), cat0(cat1(a, c_), cat1(zero, u))),
                  b_h, k_h, p1, p2, v)
        g_t = each(lambda ea, u: eye2 * ea + same_head * u[:, :LANES], e_all, gh)
        h_t = each(lambda u: same_head * u[:, LANES:], gh)
        yield q1, q2, g_t, h_t

    chunk_rows = [slice(u * L, (u + 1) * L) for u in (range(sub - 1, -1, -1) if reverse else range(sub))]
    gens = [chunk_terms(rw) for rw in chunk_rows]
    terms = [None] * sub
    while any(t is None for t in terms):
        terms = [next(gen) for gen in gens]
    z = [z_scr[p] for p in range(pairs)]
    for rw, (q1, q2, g_t, h_t) in zip(chunk_rows, terms):
        yz = mm_each(each(cat0, q1, g_t), z)
        for p in range(pairs):
            y_ref[0, rw, sls[p]] = (yz[p][:L] + q2[p]).astype(y_ref.dtype)
            z[p] = yz[p][L:] + h_t[p]
    for p in range(pairs):
        z_scr[p] = z[p]

    @pl.when(c == pl.num_programs(1) - 1)
    def _():
        zf_ref[0] = z_scr[...]


def _rw_scan(r, k, v, lw, a_s, kk_w, ka_w, z0, reverse):
    bsz, t, d = r.shape
    sub = min(RW_SCAN_CHUNKS, t // RW_CHUNK)
    rows = sub * RW_CHUNK
    assert t % rows == 0
    n_c = t // rows
    n = int(reverse)
    pairs = d // LANES
    cidx = (lambda c: n_c - 1 - c) if reverse else (lambda c: c)
    tok = pl.BlockSpec((1, rows, d), lambda b, c: (b, cidx(c), 0))
    dtok = pl.BlockSpec((1, 1, rows, d), lambda b, c: (n, b, cidx(c), 0))
    vec = pl.BlockSpec((1, d), lambda b, c: (0, 0))
    return pl.pallas_call(
        functools.partial(_rw_scan_kernel, pairs=pairs, sub=sub, reverse=reverse),
        grid=(bsz, n_c),
        in_specs=[tok, tok, tok, dtok, dtok, vec, vec,
                  pl.BlockSpec((1, 1, pairs, LANES, LANES), lambda b, c: (n, b, 0, 0, 0))],
        out_specs=[tok, pl.BlockSpec((1, pairs, LANES, LANES), lambda b, c: (b, 0, 0, 0))],
        out_shape=[jax.ShapeDtypeStruct((bsz, t, d), MXU_DTYPE),
                   jax.ShapeDtypeStruct((bsz, pairs, LANES, LANES), F32)],
        scratch_shapes=[pltpu.VMEM((pairs, LANES, LANES), F32)],
        compiler_params=_cparams("parallel", "arbitrary"),
        name="rw_scan",
    )(r, k, v, lw, a_s, kk_w, ka_w, z0)


def _rw_out_kernel(yf_ref, yb_ref, r_ref, k_ref, v_ref, g_ref, as_ref, ka_ref, rk_ref, lg_ref, lb_ref, wo_ref, x_ref,
                   mod_ref, g_ln_ref, b_ln_ref, o_ref, u_scr, *, d, alpha):
    width = 2 * LANES
    same_head = _head_ones(width)
    inv_n = 1.0 / RW_HEAD
    for p in range(d // width):
        sl = slice(p * width, (p + 1) * width)
        y = yf_ref[0, :, sl].astype(F32) + yb_ref[0, :, sl].astype(F32)
        mu = _mm_rhs_exact(y, same_head) * inv_n
        dy = y - mu
        var = _mm(dy * dy, same_head) * inv_n
        yn = dy * lax.rsqrt(var + RW_GN_EPS) * lg_ref[:, sl] + lb_ref[:, sl]
        k = k_ref[0, :, sl].astype(F32)
        ka = ka_ref[:, sl]
        k_sum = k * (2.0 + (as_ref[0, 0, :, sl].astype(F32) + as_ref[1, 0, :, sl].astype(F32) - 2.0) * ka)
        bonus = _mm(r_ref[0, :, sl].astype(F32) * k_sum * rk_ref[:, sl], same_head)
        u = (yn + bonus * v_ref[0, :, sl].astype(F32)) * g_ref[0, :, sl].astype(F32)
        u_scr[:, sl] = u.astype(u_scr.dtype)
    z = alpha * x_ref[0] + _mod_part(mod_ref, 2, d) * _mm(u_scr[...], wo_ref[...])
    o_ref[0] = _layernorm(z, g_ln_ref[...], b_ln_ref[...])


def _rw_out(y_f, y_b, r, k, v, g, a_s, p, x, mod, per_batch, ln_g, ln_b, alpha):
    bsz, t, d = x.shape
    tm = min(RW_OUT_ROWS, t)
    tok = pl.BlockSpec((1, tm, d), lambda b, i: (b, i, 0))
    dtok = pl.BlockSpec((2, 1, tm, d), lambda b, i: (0, b, i, 0))
    vec = pl.BlockSpec((1, d), lambda b, i: (0, 0))
    return pl.pallas_call(
        functools.partial(_rw_out_kernel, d=d, alpha=alpha),
        grid=(bsz, t // tm),
        in_specs=[tok, tok, tok, tok, tok, tok, dtok, vec, vec, vec, vec, pl.BlockSpec((d, d), lambda b, i: (0, 0)),
                  tok, _mod_spec(d, per_batch), vec, vec],
        out_specs=tok,
        out_shape=jax.ShapeDtypeStruct((bsz, t, d), F32),
        scratch_shapes=[pltpu.VMEM((tm, d), MXU_DTYPE)],
        compiler_params=_cparams("parallel", "parallel"),
        name="rw_out",
    )(y_f, y_b, r, k, v, g, a_s, p['k_a'], p['r_k'], p['lnx_g'], p['lnx_b'], p['wo'], x, mod, ln_g, ln_b)


def _rw_layer(x, xc, mod, mod_c, p, want_ctx, ln_g, ln_b, alpha):
    bsz, _, d = x.shape
    rc, kc, vc, gc, lwc, asc = _rw_proj(xc, mod_c, False, p)
    r, k, v, g, lw, a_s = _rw_proj(x, mod, True, p)
    z0 = jnp.zeros((2, bsz, d // LANES, LANES, LANES), F32)
    scan = lambda args, z, rev: _rw_scan(*args, p['k_k'], p['k_a'], z, rev)
    yc_f, zc_f = scan((rc, kc, vc, lwc, asc), z0, False)
    yc_b, zc_b = scan((rc, kc, vc, lwc, asc), z0, True)
    zc = jnp.stack([zc_f, zc_b])
    y_f, _ = scan((r, k, v, lw, a_s), zc, False)
    y_b, _ = scan((r, k, v, lw, a_s), zc, True)
    x_new = _rw_out(y_f, y_b, r, k, v, g, a_s, p, x, mod, True, ln_g, ln_b, alpha)
    xc_new = (_rw_out(yc_f, yc_b, rc, kc, vc, gc, asc, p, xc, mod_c, False, ln_g, ln_b, alpha)
              if want_ctx else None)
    return x_new, xc_new


def kernel(x, c, ctx, c_ctx, mod_w, mod_b, post_ln_g, post_ln_b, mlp_w1, mlp_w2, att_wqkv, att_wo, att_q_norm, att_k_norm, na_wqkv, na_wo, na_rpb, rw_mu, rw_wr, rw_wk, rw_wv, rw_wo, rw_w0, rw_w1, rw_w2, rw_a0, rw_a1, rw_a2, rw_g1, rw_g2, rw_k_k, rw_k_a, rw_r_k, rw_lnx_g, rw_lnx_b):
    depth = mod_w.shape[0]
    bsz, _, d = x.shape
    alpha = (2.0 * depth) ** 0.25
    cast = lambda a: a.astype(MXU_DTYPE)

    cond_rows = -(-(bsz + 1) // 8) * 8
    cond = jnp.zeros((cond_rows, d), F32).at[:bsz].set(c).at[bsz].set(c_ctx)
    mods = _modulation(cond, mod_w, mod_b)

    xc = ctx
    for i in range(depth):
        kind, slot = i % N_MIXERS, i // N_MIXERS
        want_ctx = i < depth - 1
        mod = mods[i, :bsz].reshape(bsz, 1, N_MOD * d)
        mod_c = mods[i, bsz:bsz + 1].reshape(1, 1, N_MOD * d)
        g1, b1 = post_ln_g[i, 0:1], post_ln_b[i, 0:1]
        g2, b2 = post_ln_g[i, 1:2], post_ln_b[i, 1:2]
        if kind == 2:
            cat = lambda a: jnp.concatenate([a[0], a[1]], axis=-1)
            p = {'mu': rw_mu[slot], 'wr': cast(rw_wr[slot]), 'wk': cast(rw_wk[slot]), 'wv': cast(rw_wv[slot]),
                 'wo': cast(rw_wo[slot]), 'w0': rw_w0[slot], 'a0': rw_a0[slot],
                 'w1': cast(cat(rw_w1[slot])), 'a1': cast(cat(rw_a1[slot])),
                 'w2': cast(rw_w2[slot].reshape(-1, d)), 'a2': cast(rw_a2[slot].reshape(-1, d)),
                 'g1': cast(rw_g1[slot]), 'g2': cast(rw_g2[slot]),
                 'k_k': rw_k_k[slot].reshape(1, d), 'k_a': rw_k_a[slot].reshape(1, d),
                 'r_k': rw_r_k[slot].reshape(1, d), 'lnx_g': rw_lnx_g[slot].reshape(1, d),
                 'lnx_b': rw_lnx_b[slot].reshape(1, d)}
            x, xc_new = _rw_layer(x, xc, mod, mod_c, p, want_ctx, g1, b1, alpha)
        else:
            if kind == 0:
                o, oc, wo = _gqa_layer(x, xc, mod, mod_c, cast(att_wqkv[slot]), cast(att_wo[slot]),
                                       att_q_norm[slot], att_k_norm[slot], want_ctx)
            else:
                o, oc, wo = _na_layer(x, xc, mod, mod_c, cast(na_wqkv[slot]), cast(na_wo[slot]), na_rpb[slot],
                                      want_ctx)
            x = _out_ln(o, wo, x, mod, True, g1, b1, alpha)
            xc_new = (_unfold(_out_ln(_fold(oc), wo, _fold(xc), mod_c, False, g1, b1, alpha), bsz)
                      if want_ctx else None)
        w1, w2 = cast(mlp_w1[i]), cast(mlp_w2[i])
        x = _mlp(x, mod, True, w1, w2, g2, b2, alpha)
        if want_ctx:
            xc = _unfold(_mlp(_fold(xc_new), mod_c, False, w1, w2, g2, b2, alpha), bsz)
    return x
```

```python
import functools

import numpy as np
import jax
import jax.numpy as jnp
from jax import lax
from jax.experimental import pallas as pl
from jax.experimental.pallas import tpu as pltpu

F32 = jnp.float32
MXU_DTYPE = jnp.bfloat16

GRID_W = 64
N_MOD = 6
N_MIXERS = 3
LN_EPS = 1e-6
ATT_HEAD_DIM = 128
ATT_KV_HEADS = 2
ROPE_BASE = 10000.0
GQA_KEY_CHUNK = 512
NA_HEAD_DIM = 64
NA_WIN_H = 8
NA_WIN_W = 16
NA_Q_ROWS = 64
RW_HEAD = 64
RW_GN_EPS = 64e-5
RW_DECAY_FLOOR_RATE = float(np.exp(-0.5))
RW_CHUNK = 64
LANES = 128
NEG_BIG = -1e30
LOG2E = 1.4426950408889634

VMEM_LIMIT = 56 * 1024 * 1024

PROJ_ROWS = 512
MLP_ROWS = 1024
MLP_FF_TILE = 2048
MLP_FUSED_FF_TILE = 1024
GQA_Q_ROWS = 512
RW_OUT_ROWS = 512
RW_SCAN_CHUNKS = 8
MOD_COL_TILES = 4


def _cparams(*sem):
    return pltpu.CompilerParams(dimension_semantics=sem, vmem_limit_bytes=VMEM_LIMIT)


def _mm(a, b):
    return jnp.dot(a.astype(MXU_DTYPE), b.astype(MXU_DTYPE), preferred_element_type=F32)


def _mm_nt(a, b):
    return lax.dot_general(a.astype(MXU_DTYPE), b.astype(MXU_DTYPE), (((1,), (1,)), ((), ())),
                           preferred_element_type=F32)


def _mm_tn(a, b):
    return lax.dot_general(a.astype(MXU_DTYPE), b.astype(MXU_DTYPE), (((0,), (0,)), ((), ())),
                           preferred_element_type=F32)


def _split(a, n):
    parts = []
    for _ in range(n - 1):
        p = a.astype(MXU_DTYPE)
        parts.append(p)
        a = a - p.astype(F32)
    parts.append(a.astype(MXU_DTYPE))
    return parts


def _mm_rhs_exact(a, m01, n=2):
    m01 = m01.astype(MXU_DTYPE)
    return sum(jnp.dot(p, m01, preferred_element_type=F32) for p in _split(a, n))


def _mm_hi(a, b):
    a1, a2 = _split(a, 2)
    b1, b2 = _split(b, 2)
    d = functools.partial(jnp.dot, preferred_element_type=F32)
    return d(a1, b1) + d(a1, b2) + d(a2, b1)


def _sigmoid(x):
    return 1.0 / (1.0 + jnp.exp(-x))


def _layernorm(z, g, b):
    mu = jnp.mean(z, -1, keepdims=True)
    dz = z - mu
    var = jnp.mean(dz * dz, -1, keepdims=True)
    return dz * lax.rsqrt(var + LN_EPS) * g + b


def _mod_part(mod_ref, j, d):
    return mod_ref[0, :, j * d:(j + 1) * d]


def _head_ones(n=LANES):
    r = lax.broadcasted_iota(jnp.int32, (n, n), 0)
    c = lax.broadcasted_iota(jnp.int32, (n, n), 1)
    return ((r // RW_HEAD) == (c // RW_HEAD)).astype(F32)


def _modulation_kernel(c_ref, w_ref, b_ref, o_ref):
    c = c_ref[...]
    o_ref[0] = _mm_hi(c * _sigmoid(c), w_ref[0]) + b_ref[0]


def _modulation(cond, mod_w, mod_b):
    depth, d, n = mod_w.shape
    rows = cond.shape[0]
    tn = n // MOD_COL_TILES
    return pl.pallas_call(
        _modulation_kernel,
        grid=(depth, n // tn),
        in_specs=[pl.BlockSpec((rows, d), lambda i, j: (0, 0)),
                  pl.BlockSpec((1, d, tn), lambda i, j: (i, 0, j)),
                  pl.BlockSpec((1, 1, tn), lambda i, j: (i, 0, j))],
        out_specs=pl.BlockSpec((1, rows, tn), lambda i, j: (i, 0, j)),
        out_shape=jax.ShapeDtypeStruct((depth, rows, n), F32),
        compiler_params=_cparams("parallel", "parallel"),
        name="modulation",
    )(cond, mod_w, mod_b.reshape(depth, 1, n))


def _out_ln_kernel(o_ref, w_ref, x_ref, mod_ref, g_ref, b_ref, y_ref, *, d, alpha):
    y = _mm(o_ref[0], w_ref[...])
    z = alpha * x_ref[0] + _mod_part(mod_ref, 2, d) * y
    y_ref[0] = _layernorm(z, g_ref[...], b_ref[...])


def _fold(a):
    return a.reshape(1, -1, a.shape[-1])


def _unfold(a, bsz):
    return a.reshape(bsz, -1, a.shape[-1])


def _mod_spec(d, per_batch):
    if per_batch:
        return pl.BlockSpec((1, 1, N_MOD * d), lambda b, *_: (b, 0, 0))
    return pl.BlockSpec((1, 1, N_MOD * d), lambda b, *_: (0, 0, 0))


def _out_ln(o, wo, x, mod, per_batch, ln_g, ln_b, alpha):
    bsz, t, d = x.shape
    tm = min(PROJ_ROWS, t)
    tok = lambda b, i: (b, i, 0)
    const = lambda b, i: (0, 0)
    return pl.pallas_call(
        functools.partial(_out_ln_kernel, d=d, alpha=alpha),
        grid=(bsz, t // tm),
        in_specs=[pl.BlockSpec((1, tm, d), tok), pl.BlockSpec((d, d), const), pl.BlockSpec((1, tm, d), tok),
                  _mod_spec(d, per_batch), pl.BlockSpec((1, d), const), pl.BlockSpec((1, d), const)],
        out_specs=pl.BlockSpec((1, tm, d), tok),
        out_shape=jax.ShapeDtypeStruct((bsz, t, d), F32),
        compiler_params=_cparams("parallel", "parallel"),
        name="out_ln",
    )(o, wo, x, mod, ln_g, ln_b)


def _mlp_kernel(x_ref, mod_ref, w1_ref, w2_ref, g_ref, b_ref, y_ref, h_scr, acc_scr, *, d, alpha):
    j = pl.program_id(2)

    @pl.when(j == 0)
    def _():
        h = x_ref[0] * (1.0 + _mod_part(mod_ref, 4, d)) + _mod_part(mod_ref, 3, d)
        h_scr[...] = h.astype(h_scr.dtype)
        acc_scr[...] = jnp.zeros_like(acc_scr)

    a = jnp.square(jnp.maximum(_mm(h_scr[...], w1_ref[...]), 0.0))
    acc_scr[...] += _mm(a, w2_ref[...])

    @pl.when(j == pl.num_programs(2) - 1)
    def _():
        z = alpha * x_ref[0] + _mod_part(mod_ref, 5, d) * acc_scr[...]
        y_ref[0] = _layernorm(z, g_ref[...], b_ref[...])


def _mlp(x, mod, per_batch, w1, w2, ln_g, ln_b, alpha):
    bsz, t, d = x.shape
    ff = w1.shape[1]
    tm = min(MLP_ROWS, t)
    tf = min(MLP_FF_TILE, ff)
    tok = lambda b, i, j: (b, i, 0)
    const = lambda b, i, j: (0, 0)
    return pl.pallas_call(
        functools.partial(_mlp_kernel, d=d, alpha=alpha),
        grid=(bsz, t // tm, ff // tf),
        in_specs=[pl.BlockSpec((1, tm, d), tok), _mod_spec(d, per_batch),
                  pl.BlockSpec((d, tf), lambda b, i, j: (0, j)), pl.BlockSpec((tf, d), lambda b, i, j: (j, 0)),
                  pl.BlockSpec((1, d), const), pl.BlockSpec((1, d), const)],
        out_specs=pl.BlockSpec((1, tm, d), tok),
        out_shape=jax.ShapeDtypeStruct((bsz, t, d), F32),
        scratch_shapes=[pltpu.VMEM((tm, d), MXU_DTYPE), pltpu.VMEM((tm, d), F32)],
        compiler_params=_cparams("parallel", "parallel", "arbitrary"),
        name="mlp",
    )(x, mod, w1, w2, ln_g, ln_b)


def _out_mlp_kernel(o_ref, wo_ref, x_ref, mod_ref, g1_ref, b1_ref, w1_ref, w2_ref, g2_ref, b2_ref, y_ref,
                    x1_scr, h_scr, acc_scr, *, d, alpha):
    j = pl.program_id(2)

    @pl.when(j == 0)
    def _():
        z = alpha * x_ref[0] + _mod_part(mod_ref, 2, d) * _mm(o_ref[0], wo_ref[...])
        x1 = _layernorm(z, g1_ref[...], b1_ref[...])
        x1_scr[...] = x1
        h_scr[...] = (x1 * (1.0 + _mod_part(mod_ref, 4, d)) + _mod_part(mod_ref, 3, d)).astype(h_scr.dtype)
        acc_scr[...] = jnp.zeros_like(acc_scr)

    a = jnp.square(jnp.maximum(_mm(h_scr[...], w1_ref[...]), 0.0))
    acc_scr[...] += _mm(a, w2_ref[...])

    @pl.when(j == pl.num_programs(2) - 1)
    def _():
        z = alpha * x1_scr[...] + _mod_part(mod_ref, 5, d) * acc_scr[...]
        y_ref[0] = _layernorm(z, g2_ref[...], b2_ref[...])


def _out_mlp(o, wo, x, mod, per_batch, ln1_g, ln1_b, w1, w2, ln2_g, ln2_b, alpha):
    bsz, t, d = x.shape
    ff = w1.shape[1]
    tm = min(MLP_ROWS, t)
    tf = min(MLP_FUSED_FF_TILE, ff)
    tok = pl.BlockSpec((1, tm, d), lambda b, i, j: (b, i, 0))
    vec = pl.BlockSpec((1, d), lambda b, i, j: (0, 0))
    return pl.pallas_call(
        functools.partial(_out_mlp_kernel, d=d, alpha=alpha),
        grid=(bsz, t // tm, ff // tf),
        in_specs=[tok, pl.BlockSpec((d, d), lambda b, i, j: (0, 0)), tok, _mod_spec(d, per_batch), vec, vec,
                  pl.BlockSpec((d, tf), lambda b, i, j: (0, j)), pl.BlockSpec((tf, d), lambda b, i, j: (j, 0)),
                  vec, vec],
        out_specs=tok,
        out_shape=jax.ShapeDtypeStruct((bsz, t, d), F32),
        scratch_shapes=[pltpu.VMEM((tm, d), F32), pltpu.VMEM((tm, d), MXU_DTYPE), pltpu.VMEM((tm, d), F32)],
        compiler_params=_cparams("parallel", "parallel", "arbitrary"),
        name="out_mlp",
    )(o, wo, x, mod, ln1_g, ln1_b, w1, w2, ln2_g, ln2_b)


def _gqa_qkv_kernel(x_ref, mod_ref, w_ref, gq_ref, gk_ref, cos_ref, sin_ref, q_ref, k_ref, v_ref,
                    *, d, n_q, n_kv):
    h = x_ref[0] * (1.0 + _mod_part(mod_ref, 1, d)) + _mod_part(mod_ref, 0, d)
    acc = _mm(h, w_ref[...])
    cos, sin = cos_ref[...], sin_ref[...]
    hd = ATT_HEAD_DIM
    src = lax.broadcasted_iota(jnp.int32, (hd, hd), 0)
    dst = lax.broadcasted_iota(jnp.int32, (hd, hd), 1)
    first = (dst & (hd // 2 - 1)) < hd // 4
    rot = (jnp.where(first & (src == dst + hd // 4), -1.0, 0.0)
           + jnp.where(jnp.logical_not(first) & (src == dst - hd // 4), 1.0, 0.0))

    def norm_rope(u, g):
        u = u * lax.rsqrt(jnp.mean(u * u, -1, keepdims=True) + LN_EPS) * g
        return u * cos + _mm(u, rot) * sin

    for i in range(n_q):
        q_ref[0, :, i * hd:(i + 1) * hd] = norm_rope(acc[:, i * hd:(i + 1) * hd], gq_ref[...]).astype(q_ref.dtype)
    for i in range(n_kv):
        c0 = (n_q + i) * hd
        k_ref[0, :, i * hd:(i + 1) * hd] = norm_rope(acc[:, c0:c0 + hd], gk_ref[...]).astype(k_ref.dtype)
    ones = jnp.ones((acc.shape[0], hd), v_ref.dtype)
    for i in range(n_kv):
        c0 = (n_q + n_kv + i) * hd
        v_ref[0, :, 2 * i * hd:(2 * i + 1) * hd] = acc[:, c0:c0 + hd].astype(v_ref.dtype)
        v_ref[0, :, (2 * i + 1) * hd:(2 * i + 2) * hd] = ones


def _gqa_qkv(x, mod, per_batch, w, gq, gk, cos, sin):
    bsz, t, d = x.shape
    hd = ATT_HEAD_DIM
    n_kv = ATT_KV_HEADS
    n_q = w.shape[1] // hd - 2 * n_kv
    tm = min(PROJ_ROWS, t)
    tok = lambda b, i: (b, i, 0)
    const = lambda b, i: (0, 0)
    tab = pl.BlockSpec((tm, hd), lambda b, i: (i, 0))
    return pl.pallas_call(
        functools.partial(_gqa_qkv_kernel, d=d, n_q=n_q, n_kv=n_kv),
        grid=(bsz, t // tm),
        in_specs=[pl.BlockSpec((1, tm, d), tok), _mod_spec(d, per_batch), pl.BlockSpec(w.shape, const),
                  pl.BlockSpec((1, hd), const), pl.BlockSpec((1, hd), const), tab, tab],
        out_specs=[pl.BlockSpec((1, tm, n_q * hd), tok), pl.BlockSpec((1, tm, n_kv * hd), tok),
                   pl.BlockSpec((1, tm, 2 * n_kv * hd), tok)],
        out_shape=[jax.ShapeDtypeStruct((bsz, t, n_q * hd), MXU_DTYPE),
                   jax.ShapeDtypeStruct((bsz, t, n_kv * hd), MXU_DTYPE),
                   jax.ShapeDtypeStruct((bsz, t, 2 * n_kv * hd), MXU_DTYPE)],
        compiler_params=_cparams("parallel", "parallel"),
        name="gqa_qkv",
    )(x, mod, w, gq, gk, cos, sin)


def _gqa_att_kernel(*refs, n_sets, groups, tk):
    q_ref, o_ref = refs[0], refs[-1]
    kv = refs[1:-1]
    hd = ATT_HEAD_DIM
    tq = q_ref.shape[1]
    q = jnp.concatenate([q_ref[0, :, g * hd:(g + 1) * hd] for g in range(groups)], axis=0)
    m = acc = None
    for j in range(n_sets):
        k_ref, v_ref = kv[2 * j], kv[2 * j + 1]
        t = k_ref.shape[1]
        for c0 in range(0, t, tk):
            c1 = min(c0 + tk, t)
            s = _mm_nt(q, k_ref[0, c0:c1, :])
            m_c = jnp.max(s, -1, keepdims=True)
            if m is None:
                m = m_c
                acc = _mm(jnp.exp2(s - m), v_ref[0, c0:c1, :])
            else:
                m_new = jnp.maximum(m, m_c)
                acc = acc * jnp.exp2(m - m_new) + _mm(jnp.exp2(s - m_new), v_ref[0, c0:c1, :])
                m = m_new
    o = acc[:, :hd] / acc[:, hd:]
    for g in range(groups):
        o_ref[0, :, g * hd:(g + 1) * hd] = o[g * tq:(g + 1) * tq].astype(o_ref.dtype)


def _gqa_att(q, kv_sets):
    bsz, s, dq = q.shape
    hd = ATT_HEAD_DIM
    n_kv = ATT_KV_HEADS
    groups = dq // hd // n_kv
    tq = min(GQA_Q_ROWS, s)
    in_specs = [pl.BlockSpec((1, tq, groups * hd), lambda b, kh, i: (b, i, kh))]
    args = [q]
    for k, v in kv_sets:
        t = k.shape[1]
        in_specs += [pl.BlockSpec((1, t, hd), lambda b, kh, i: (b, 0, kh)),
                     pl.BlockSpec((1, t, 2 * hd), lambda b, kh, i: (b, 0, kh))]
        args += [k, v]
    return pl.pallas_call(
        functools.partial(_gqa_att_kernel, n_sets=len(kv_sets), groups=groups, tk=GQA_KEY_CHUNK),
        grid=(bsz, n_kv, s // tq),
        in_specs=in_specs,
        out_specs=pl.BlockSpec((1, tq, groups * hd), lambda b, kh, i: (b, i, kh)),
        out_shape=jax.ShapeDtypeStruct((bsz, s, dq), MXU_DTYPE),
        compiler_params=_cparams("parallel", "parallel", "parallel"),
        name="gqa_att",
    )(*args)


def _rope_tables(n_tokens, head_dim):
    t = np.arange(n_tokens)
    row = (t // GRID_W).astype(np.float32)
    col = (t % GRID_W).astype(np.float32)
    half = head_dim // 2
    freqs = jnp.asarray(ROPE_BASE, F32) ** (-jnp.arange(0, half, 2, dtype=F32) / half)
    ang_r = jnp.asarray(row)[:, None] * freqs[None, :]
    ang_c = jnp.asarray(col)[:, None] * freqs[None, :]
    ang = jnp.concatenate([ang_r, ang_r, ang_c, ang_c], axis=-1)
    return jnp.cos(ang), jnp.sin(ang)


def _gqa_layer(x, xc, mod, mod_c, wqkv, wo, q_norm, k_norm, want_ctx):
    s, tc = x.shape[1], xc.shape[1]
    hd = ATT_HEAD_DIM
    gq = (q_norm * (hd ** -0.5 * LOG2E)).reshape(1, hd)
    gk = k_norm.reshape(1, hd)
    cos, sin = _rope_tables(s, hd)
    bsz = x.shape[0]
    one, zero = jnp.ones((bsz * tc, hd), F32), jnp.zeros((bsz * tc, hd), F32)
    q, k, v = _gqa_qkv(x, mod, True, wqkv, gq, gk, cos, sin)
    qc, kc, vc = [_unfold(a, bsz) for a in _gqa_qkv(_fold(xc), mod_c, False, wqkv, gq, gk, one, zero)]
    o = _gqa_att(q, [(k, v), (kc, vc)])
    oc = _gqa_att(qc, [(kc, vc)]) if want_ctx else None
    return o, oc, wo


def _na_qkv_kernel(x_ref, mod_ref, w_ref, q_ref, k_ref, v_ref, *, d, scale):
    h = x_ref[0] * (1.0 + _mod_part(mod_ref, 1, d)) + _mod_part(mod_ref, 0, d)
    acc = _mm(h, w_ref[...])
    q_ref[0] = (acc[:, :d] * scale).astype(q_ref.dtype)
    k_ref[0] = acc[:, d:2 * d].astype(k_ref.dtype)
    ones = jnp.ones((acc.shape[0], LANES), v_ref.dtype)
    for p in range(d // LANES):
        c0 = 2 * d + p * LANES
        v_ref[0, :, 2 * p * LANES:(2 * p + 1) * LANES] = acc[:, c0:c0 + LANES].astype(v_ref.dtype)
        v_ref[0, :, (2 * p + 1) * LANES:(2 * p + 2) * LANES] = ones


def _na_qkv(x, mod, per_batch, w):
    bsz, t, d = x.shape
    tm = min(PROJ_ROWS, t)
    tok = lambda b, i: (b, i, 0)
    out = jax.ShapeDtypeStruct((bsz, t, d), MXU_DTYPE)
    return pl.pallas_call(
        functools.partial(_na_qkv_kernel, d=d, scale=NA_HEAD_DIM ** -0.5 * LOG2E),
        grid=(bsz, t // tm),
        in_specs=[pl.BlockSpec((1, tm, d), tok), _mod_spec(d, per_batch), pl.BlockSpec(w.shape, lambda b, i: (0, 0))],
        out_specs=[pl.BlockSpec((1, tm, d), tok)] * 2 + [pl.BlockSpec((1, tm, 2 * d), tok)],
        out_shape=[out, out, jax.ShapeDtypeStruct((bsz, t, 2 * d), MXU_DTYPE)],
        compiler_params=_cparams("parallel", "parallel"),
        name="na_qkv",
    )(x, mod, w)


def _pair_softmax_att(q, ks, vs, biases):
    lane = lax.broadcasted_iota(jnp.int32, q.shape, 1)
    first = lane < NA_HEAD_DIM
    zero = jnp.zeros_like(q)
    qm = [jnp.where(first, q, zero), jnp.where(first, zero, q)]
    m, acc = [None, None], [None, None]
    for j, (k, v) in enumerate(zip(ks, vs)):
        for h in range(2):
            s = _mm_nt(qm[h], k)
            if biases[h][j] is not None:
                s = s + biases[h][j]
            m_c = jnp.max(s, -1, keepdims=True)
            if m[h] is None:
                m[h] = m_c
                acc[h] = _mm(jnp.exp2(s - m_c), v)
            else:
                m_new = jnp.maximum(m[h], m_c)
                acc[h] = acc[h] * jnp.exp2(m[h] - m_new) + _mm(jnp.exp2(s - m_new), v)
                m[h] = m_new
    outs = [a[:, :LANES] / a[:, LANES:] for a in acc]
    return jnp.where(first, outs[0], outs[1])


def _na_att_kernel(q_ref, k_ref, v_ref, kc_ref, vc_ref, bias_ref, o_ref, *, rows):
    i = pl.program_id(2)
    q_rows = q_ref.shape[1] // GRID_W
    win = NA_WIN_H * GRID_W
    w = GRID_W
    first = lax.broadcasted_iota(jnp.int32, (w, LANES), 1) < NA_HEAD_DIM
    qm, kw, vw, bias = [], [], [], []
    for rq in range(q_rows):
        row_q = i * q_rows + rq
        r0 = jnp.clip(row_q - NA_WIN_H // 2, 0, rows - NA_WIN_H)
        start = pl.multiple_of(r0 * w, w)
        q = q_ref[0, rq * w:(rq + 1) * w, :]
        zero = jnp.zeros_like(q)
        qm.append(jnp.concatenate([jnp.where(first, q, zero), jnp.where(first, zero, q)], axis=0))
        kw.append(k_ref[0, pl.ds(start, win), :])
        vw.append(v_ref[0, pl.ds(start, win), :])
        e = r0 - row_q + NA_WIN_H - 1
        bias.append(jnp.concatenate([bias_ref[0, 0, e], bias_ref[0, 1, e]], axis=0))
    each = lambda fn, *lists: [fn(*args) for args in zip(*lists)]
    s_ctx = _mm_nt(jnp.concatenate(qm, axis=0), kc_ref[0])
    s_ctx = [s_ctx[2 * w * rq:2 * w * (rq + 1)] for rq in range(q_rows)]
    s_loc = each(lambda a, b, c: _mm_nt(a, b) + c, qm, kw, bias)
    m = each(lambda a, b: jnp.maximum(jnp.max(a, -1, keepdims=True), jnp.max(b, -1, keepdims=True)), s_loc, s_ctx)
    acc_ctx = _mm(jnp.concatenate(each(lambda a, mm: jnp.exp2(a - mm), s_ctx, m), axis=0), vc_ref[0])
    acc = each(lambda a, mm, v: _mm(jnp.exp2(a - mm), v), s_loc, m, vw)
    for rq in range(q_rows):
        a = acc[rq] + acc_ctx[2 * w * rq:2 * w * (rq + 1)]
        o = a[:, :LANES] / a[:, LANES:]
        o_ref[0, rq * w:(rq + 1) * w, :] = jnp.where(first, o[:w], o[w:]).astype(o_ref.dtype)


def _na_bias_tables(rpb, scale):
    pad = GRID_W
    rpb_p = jnp.pad(rpb, ((0, 0), (0, 0), (pad, pad)))
    cmat = jnp.stack([rpb_p[:, :, pad + NA_WIN_W - 1 - cq: pad + NA_WIN_W - 1 - cq + GRID_W]
                      for cq in range(GRID_W)], axis=2)
    cq, ck = np.arange(GRID_W)[:, None], np.arange(GRID_W)[None, :]
    c0 = np.clip(cq - NA_WIN_W // 2, 0, GRID_W - NA_WIN_W)
    vcol = (ck >= c0) & (ck < c0 + NA_WIN_W)
    cmat = jnp.where(vcol[None, None], cmat * scale, NEG_BIG)
    return jnp.stack([jnp.concatenate([cmat[:, e + j] for j in range(NA_WIN_H)], axis=-1)
                      for e in range(NA_WIN_H)], axis=1)


def _na_att(q, k, v, kc, vc, bias):
    bsz, s, d = q.shape
    tc = kc.shape[1]
    rows = s // GRID_W
    q_rows = min(NA_Q_ROWS, rows)
    assert rows % q_rows == 0 and rows >= NA_WIN_H
    tq = q_rows * GRID_W
    pairs = d // LANES
    q_spec = pl.BlockSpec((1, tq, LANES), lambda hp, b, i: (b, i, hp))
    whole = lambda t, width: pl.BlockSpec((1, t, width), lambda hp, b, i: (b, 0, hp))
    bias_spec = pl.BlockSpec((1, 2) + bias.shape[1:], lambda hp, b, i: (0, hp, 0, 0, 0))
    return pl.pallas_call(
        functools.partial(_na_att_kernel, rows=rows),
        grid=(pairs, bsz, rows // q_rows),
        in_specs=[q_spec, whole(s, LANES), whole(s, 2 * LANES), whole(tc, LANES), whole(tc, 2 * LANES), bias_spec],
        out_specs=q_spec,
        out_shape=jax.ShapeDtypeStruct((bsz, s, d), MXU_DTYPE),
        compiler_params=_cparams("parallel", "parallel", "parallel"),
        name="na_att",
    )(q, k, v, kc, vc, bias[None])


def _pair_att_kernel(q_ref, k_ref, v_ref, o_ref):
    o_ref[0] = _pair_softmax_att(q_ref[0], [k_ref[0]], [v_ref[0]], [[None], [None]]).astype(o_ref.dtype)


def _pair_att(q, k, v):
    bsz, t, d = q.shape
    spec = pl.BlockSpec((1, t, LANES), lambda b, hp: (b, 0, hp))
    return pl.pallas_call(
        _pair_att_kernel,
        grid=(bsz, d // LANES),
        in_specs=[spec, spec, pl.BlockSpec((1, t, 2 * LANES), lambda b, hp: (b, 0, hp))],
        out_specs=spec,
        out_shape=jax.ShapeDtypeStruct((bsz, t, d), MXU_DTYPE),
        compiler_params=_cparams("parallel", "parallel"),
        name="na_ctx_att",
    )(q, k, v)


def _na_layer(x, xc, mod, mod_c, wqkv, wo, rpb, want_ctx):
    q, k, v = _na_qkv(x, mod, True, wqkv)
    qc, kc, vc = [_unfold(a, x.shape[0]) for a in _na_qkv(_fold(xc), mod_c, False, wqkv)]
    o = _na_att(q, k, v, kc, vc, _na_bias_tables(rpb, LOG2E))
    oc = _pair_att(qc, kc, vc) if want_ctx else None
    return o, oc, wo


def _rw_proj_kernel(x_ref, xp_ref, xn_ref, mod_ref, mu_ref, wr_ref, wk_ref, wv_ref, w1_ref, w2_ref, a1_ref, a2_ref,
                    g1_ref, g2_ref, w0_ref, a0_ref, r_ref, k_ref, v_ref, g_ref, lw_ref, as_ref, *, d):
    t = pl.program_id(1)
    nt = pl.num_programs(1)
    scale = 1.0 + _mod_part(mod_ref, 1, d)
    shift = _mod_part(mod_ref, 0, d)
    h = x_ref[0] * scale + shift
    tm = h.shape[0]
    h_prev = (xp_ref[0, 7:8, :] * scale + shift) * (t > 0).astype(F32)
    h_next = (xn_ref[0, 0:1, :] * scale + shift) * (t < nt - 1).astype(F32)
    row = lax.broadcasted_iota(jnp.int32, h.shape, 0)
    prev = jnp.where(row == 0, h_prev, pltpu.roll(h, 1, 0))
    nxt = jnp.where(row == tm - 1, h_next, pltpu.roll(h, tm - 1, 0))
    xx = 0.5 * (prev + nxt) - h
    mix = lambda j: h + xx * mu_ref[j:j + 1, :]

    r_ref[0] = _mm(mix(0), wr_ref[...]).astype(r_ref.dtype)
    k_ref[0] = _mm(mix(2), wk_ref[...]).astype(k_ref.dtype)
    v_ref[0] = _mm(mix(3), wv_ref[...]).astype(v_ref.dtype)
    g_ref[0] = _mm(_sigmoid(_mm(mix(5), g1_ref[...])), g2_ref[...]).astype(g_ref.dtype)

    tw = jnp.tanh(_mm(mix(1), w1_ref[...]))
    al = _mm(mix(4), a1_ref[...])
    first = lax.broadcasted_iota(jnp.int32, tw.shape, 1) < tw.shape[1] // 2
    zero = jnp.zeros_like(tw)
    for n in range(2):
        pick = lambda u: jnp.where(first, u, zero) if n == 0 else jnp.where(first, zero, u)
        u = w0_ref[n:n + 1, :] + _mm(pick(tw), w2_ref[...])
        lw_ref[n, 0] = -RW_DECAY_FLOOR_RATE * _sigmoid(u)
        as_ref[n, 0] = _sigmoid(a0_ref[n:n + 1, :] + _mm(pick(al), a2_ref[...])).astype(as_ref.dtype)


def _rw_proj(x, mod, per_batch, p):
    bsz, t, d = x.shape
    tm = min(PROJ_ROWS, t)
    n8 = t // 8
    tok = lambda b, i: (b, i, 0)
    dtok = lambda b, i: (0, b, i, 0)
    full = lambda a: pl.BlockSpec(a.shape, lambda b, i: (0,) * a.ndim)
    halo_prev = pl.BlockSpec((1, 8, d), lambda b, i: (b, jnp.maximum(i * (tm // 8) - 1, 0), 0))
    halo_next = pl.BlockSpec((1, 8, d), lambda b, i: (b, jnp.minimum((i + 1) * (tm // 8), n8 - 1), 0))
    weights = [p['mu'], p['wr'], p['wk'], p['wv'], p['w1'], p['w2'], p['a1'], p['a2'], p['g1'], p['g2'],
               p['w0'], p['a0']]
    one = jax.ShapeDtypeStruct((bsz, t, d), MXU_DTYPE)
    two = lambda dtype: jax.ShapeDtypeStruct((2, bsz, t, d), dtype)
    return pl.pallas_call(
        functools.partial(_rw_proj_kernel, d=d),
        grid=(bsz, t // tm),
        in_specs=[pl.BlockSpec((1, tm, d), tok), halo_prev, halo_next, _mod_spec(d, per_batch)]
                 + [full(a) for a in weights],
        out_specs=[pl.BlockSpec((1, tm, d), tok)] * 4 + [pl.BlockSpec((2, 1, tm, d), dtok)] * 2,
        out_shape=[one, one, one, one, two(F32), two(MXU_DTYPE)],
        compiler_params=_cparams("parallel", "parallel"),
        name="rw_proj",
    )(x, x, x, mod, *weights)


def _rw_scan_kernel(r_ref, k_ref, v_ref, lw_ref, as_ref, kk_ref, ka_ref, z0_ref, y_ref, zf_ref, z_scr,
                    *, pairs, sub, reverse):
    c = pl.program_id(1)
    L = RW_CHUNK
    H = RW_HEAD
    assert 2 * L == LANES and 2 * H == LANES

    @pl.when(c == 0)
    def _():
        z_scr[...] = z0_ref[0, 0]

    sign = -1 if reverse else 1
    t_i = lax.broadcasted_iota(jnp.int32, (L, 2 * L), 0)
    s_i = lax.broadcasted_iota(jnp.int32, (L, 2 * L), 1) & (L - 1)
    dt = sign * (s_i - t_i)
    strict_c = dt < 0
    incl_c = dt <= 0
    tri2 = incl_c.astype(F32)
    eye_c = (dt == 0).astype(F32)
    r2 = lax.broadcasted_iota(jnp.int32, (LANES, LANES), 0)
    c2 = lax.broadcasted_iota(jnp.int32, (LANES, LANES), 1)
    eye2 = (r2 == c2).astype(F32)
    same_head = _head_ones()
    first = lax.broadcasted_iota(jnp.int32, (L, LANES), 1) < H
    zero = jnp.zeros((L, LANES), F32)

    head0 = lambda u: jnp.where(first, u, zero)
    head1 = lambda u: jnp.where(first, zero, u)
    bd = lambda u: jnp.concatenate([head0(u), head1(u)], axis=0)

    each = lambda fn, *lists: [fn(*args) for args in zip(*lists)]
    sls = [slice(p * LANES, (p + 1) * LANES) for p in range(pairs)]
    cat0 = lambda *u: jnp.concatenate(u, axis=0)
    cat1 = lambda *u: jnp.concatenate(u, axis=1)

    mm_each = lambda lhs, w: [_mm(a, b) for a, b in zip(lhs, w)]

    def chunk_terms(rw):
        lw = [lw_ref[0, 0, rw, sl] for sl in sls]
        r = [r_ref[0, rw, sl].astype(F32) for sl in sls]
        k = [k_ref[0, rw, sl].astype(F32) for sl in sls]
        v = [v_ref[0, rw, sl].astype(F32) for sl in sls]
        a_s = [as_ref[0, 0, rw, sl].astype(F32) for sl in sls]

        c_in = each(lambda u: _mm(tri2, cat0(*_split(u, 2))), lw)
        c_all = each(lambda u: jnp.sum(u, axis=0, keepdims=True), lw)
        kk = each(lambda u, sl: u * kk_ref[:, sl], k, sls)
        ss = each(lambda u: _mm(u * u, same_head), kk)
        kk = each(lambda u, s: u * lax.rsqrt(jnp.maximum(s, 1e-12)), kk, ss)
        b_v = each(lambda u, a: u * a, kk, a_s)
        k_d = each(lambda u, a, sl: u * (1.0 + (a - 1.0) * ka_ref[:, sl]), k, a_s, sls)
        e_neg = each(lambda ci: jnp.exp(-ci), c_in)
        e_all = each(jnp.exp, c_all)
        a_t = each(lambda u, ci, l: -u * jnp.exp(ci - l), kk, c_in, lw)
        r_t = each(lambda u, ci: u * jnp.exp(ci), r, c_in)
        b_t = each(lambda u, e: u * e, b_v, e_neg)
        k_t = each(lambda u, e: u * e, k_d, e_neg)
        b_h = each(lambda u, e: u * e, b_t, e_all)
        k_h = each(lambda u, e: u * e, k_t, e_all)
        yield None

        x = each(lambda a, rr, b, kt: _mm_nt(cat0(a, rr), cat0(bd(b), bd(kt))), a_t, r_t, b_t, k_t)
        m_ab = each(lambda u: jnp.where(strict_c, u[:L, :2 * L], 0.0), x)
        m_ak = each(lambda u: jnp.where(strict_c, u[:L, 2 * L:], 0.0), x)
        n_rb = each(lambda u: jnp.where(incl_c, u[L:, :2 * L], 0.0), x)
        n_rk = each(lambda u: jnp.where(incl_c, u[L:, 2 * L:], 0.0), x)

        inv = each(lambda m: eye_c + m, m_ab)
        yield None
        pw = mm_each(m_ab, each(bd, m_ab))
        for _ in range(int(np.log2(L)) - 2):
            yield None
            st = mm_each(each(cat0, pw, inv), each(bd, pw))
            pw = each(lambda s: s[:L], st)
            inv = each(lambda b, s: b + s[L:], inv, st)
        yield None
        inv = each(lambda b, s: b + s, inv, mm_each(inv, each(bd, pw)))
        mv = mm_each(each(cat0, m_ak, n_rk), each(bd, v))
        yield None
        tw = each(lambda i, a, m: _mm(i, cat1(bd(a), bd(m[:L]))), inv, a_t, mv)
        p1 = each(lambda u: u[:, :LANES], tw)
        p2 = each(lambda u: u[:, LANES:], tw)
        yield None
        nw = each(lambda m, a, b: _mm(m, cat1(bd(a), bd(b))), n_rb, p1, p2)
        q1 = each(lambda rr, u: rr + u[:, :LANES], r_t, nw)
        q2 = each(lambda u, m: u[:, LANES:] + m[L:], nw, mv)
        gh = each(lambda b, kh, a, c_, u: _mm_tn(cat0(b, kh), cat0(cat1(a, c_), cat1(zero, u))),
                  b_h, k_h, p1, p2, v)
        g_t = each(lambda ea, u: eye2 * ea + same_head * u[:, :LANES], e_all, gh)
        h_t = each(lambda u: same_head * u[:, LANES:], gh)
        yield q1, q2, g_t, h_t

    chunk_rows = [slice(u * L, (u + 1) * L) for u in (range(sub - 1, -1, -1) if reverse else range(sub))]
    gens = [chunk_terms(rw) for rw in chunk_rows]
    terms = [None] * sub
    while any(t is None for t in terms):
        terms = [next(gen) for gen in gens]
    z = [z_scr[p] for p in range(pairs)]
    for rw, (q1, q2, g_t, h_t) in zip(chunk_rows, terms):
        yz = mm_each(each(cat0, q1, g_t), z)
        for p in range(pairs):
            y_ref[0, rw, sls[p]] = (yz[p][:L] + q2[p]).astype(y_ref.dtype)
            z[p] = yz[p][L:] + h_t[p]
    for p in range(pairs):
        z_scr[p] = z[p]

    @pl.when(c == pl.num_programs(1) - 1)
    def _():
        zf_ref[0] = z_scr[...]


def _rw_scan(r, k, v, lw, a_s, kk_w, ka_w, z0, reverse):
    bsz, t, d = r.shape
    sub = min(RW_SCAN_CHUNKS, t // RW_CHUNK)
    rows = sub * RW_CHUNK
    assert t % rows == 0
    n_c = t // rows
    n = int(reverse)
    pairs = d // LANES
    cidx = (lambda c: n_c - 1 - c) if reverse else (lambda c: c)
    tok = pl.BlockSpec((1, rows, d), lambda b, c: (b, cidx(c), 0))
    dtok = pl.BlockSpec((1, 1, rows, d), lambda b, c: (n, b, cidx(c), 0))
    vec = pl.BlockSpec((1, d), lambda b, c: (0, 0))
    return pl.pallas_call(
        functools.partial(_rw_scan_kernel, pairs=pairs, sub=sub, reverse=reverse),
        grid=(bsz, n_c),
        in_specs=[tok, tok, tok, dtok, dtok, vec, vec,
                  pl.BlockSpec((1, 1, pairs, LANES, LANES), lambda b, c: (n, b, 0, 0, 0))],
        out_specs=[tok, pl.BlockSpec((1, pairs, LANES, LANES), lambda b, c: (b, 0, 0, 0))],
        out_shape=[jax.ShapeDtypeStruct((bsz, t, d), MXU_DTYPE),
                   jax.ShapeDtypeStruct((bsz, pairs, LANES, LANES), F32)],
        scratch_shapes=[pltpu.VMEM((pairs, LANES, LANES), F32)],
        compiler_params=_cparams("parallel", "arbitrary"),
        name="rw_scan",
    )(r, k, v, lw, a_s, kk_w, ka_w, z0)


def _rw_out_kernel(yf_ref, yb_ref, r_ref, k_ref, v_ref, g_ref, as_ref, ka_ref, rk_ref, lg_ref, lb_ref, wo_ref, x_ref,
                   mod_ref, g_ln_ref, b_ln_ref, o_ref, u_scr, *, d, alpha):
    width = 2 * LANES
    same_head = _head_ones(width)
    inv_n = 1.0 / RW_HEAD
    for p in range(d // width):
        sl = slice(p * width, (p + 1) * width)
        y = yf_ref[0, :, sl].astype(F32) + yb_ref[0, :, sl].astype(F32)
        mu = _mm_rhs_exact(y, same_head) * inv_n
        dy = y - mu
        var = _mm(dy * dy, same_head) * inv_n
        yn = dy * lax.rsqrt(var + RW_GN_EPS) * lg_ref[:, sl] + lb_ref[:, sl]
        k = k_ref[0, :, sl].astype(F32)
        ka = ka_ref[:, sl]
        k_sum = k * (2.0 + (as_ref[0, 0, :, sl].astype(F32) + as_ref[1, 0, :, sl].astype(F32) - 2.0) * ka)
        bonus = _mm(r_ref[0, :, sl].astype(F32) * k_sum * rk_ref[:, sl], same_head)
        u = (yn + bonus * v_ref[0, :, sl].astype(F32)) * g_ref[0, :, sl].astype(F32)
        u_scr[:, sl] = u.astype(u_scr.dtype)
    z = alpha * x_ref[0] + _mod_part(mod_ref, 2, d) * _mm(u_scr[...], wo_ref[...])
    o_ref[0] = _layernorm(z, g_ln_ref[...], b_ln_ref[...])


def _rw_out(y_f, y_b, r, k, v, g, a_s, p, x, mod, per_batch, ln_g, ln_b, alpha):
    bsz, t, d = x.shape
    tm = min(RW_OUT_ROWS, t)
    tok = pl.BlockSpec((1, tm, d), lambda b, i: (b, i, 0))
    dtok = pl.BlockSpec((2, 1, tm, d), lambda b, i: (0, b, i, 0))
    vec = pl.BlockSpec((1, d), lambda b, i: (0, 0))
    return pl.pallas_call(
        functools.partial(_rw_out_kernel, d=d, alpha=alpha),
        grid=(bsz, t // tm),
        in_specs=[tok, tok, tok, tok, tok, tok, dtok, vec, vec, vec, vec, pl.BlockSpec((d, d), lambda b, i: (0, 0)),
                  tok, _mod_spec(d, per_batch), vec, vec],
        out_specs=tok,
        out_shape=jax.ShapeDtypeStruct((bsz, t, d), F32),
        scratch_shapes=[pltpu.VMEM((tm, d), MXU_DTYPE)],
        compiler_params=_cparams("parallel", "parallel"),
        name="rw_out",
    )(y_f, y_b, r, k, v, g, a_s, p['k_a'], p['r_k'], p['lnx_g'], p['lnx_b'], p['wo'], x, mod, ln_g, ln_b)


def _rw_layer(x, xc, mod, mod_c, p, want_ctx, ln_g, ln_b, alpha):
    bsz, _, d = x.shape
    rc, kc, vc, gc, lwc, asc = _rw_proj(xc, mod_c, False, p)
    r, k, v, g, lw, a_s = _rw_proj(x, mod, True, p)
    z0 = jnp.zeros((2, bsz, d // LANES, LANES, LANES), F32)
    scan = lambda args, z, rev: _rw_scan(*args, p['k_k'], p['k_a'], z, rev)
    yc_f, zc_f = scan((rc, kc, vc, lwc, asc), z0, False)
    yc_b, zc_b = scan((rc, kc, vc, lwc, asc), z0, True)
    zc = jnp.stack([zc_f, zc_b])
    y_f, _ = scan((r, k, v, lw, a_s), zc, False)
    y_b, _ = scan((r, k, v, lw, a_s), zc, True)
    x_new = _rw_out(y_f, y_b, r, k, v, g, a_s, p, x, mod, True, ln_g, ln_b, alpha)
    xc_new = (_rw_out(yc_f, yc_b, rc, kc, vc, gc, asc, p, xc, mod_c, False, ln_g, ln_b, alpha)
              if want_ctx else None)
    return x_new, xc_new


def kernel(x, c, ctx, c_ctx, mod_w, mod_b, post_ln_g, post_ln_b, mlp_w1, mlp_w2, att_wqkv, att_wo, att_q_norm, att_k_norm, na_wqkv, na_wo, na_rpb, rw_mu, rw_wr, rw_wk, rw_wv, rw_wo, rw_w0, rw_w1, rw_w2, rw_a0, rw_a1, rw_a2, rw_g1, rw_g2, rw_k_k, rw_k_a, rw_r_k, rw_lnx_g, rw_lnx_b):
    depth = mod_w.shape[0]
    bsz, _, d = x.shape
    alpha = (2.0 * depth) ** 0.25
    cast = lambda a: a.astype(MXU_DTYPE)

    cond_rows = -(-(bsz + 1) // 8) * 8
    cond = jnp.zeros((cond_rows, d), F32).at[:bsz].set(c).at[bsz].set(c_ctx)
    mods = _modulation(cond, mod_w, mod_b)

    xc = ctx
    for i in range(depth):
        kind, slot = i % N_MIXERS, i // N_MIXERS
        want_ctx = i < depth - 1
        mod = mods[i, :bsz].reshape(bsz, 1, N_MOD * d)
        mod_c = mods[i, bsz:bsz + 1].reshape(1, 1, N_MOD * d)
        g1, b1 = post_ln_g[i, 0:1], post_ln_b[i, 0:1]
        g2, b2 = post_ln_g[i, 1:2], post_ln_b[i, 1:2]
        if kind == 2:
            cat = lambda a: jnp.concatenate([a[0], a[1]], axis=-1)
            p = {'mu': rw_mu[slot], 'wr': cast(rw_wr[slot]), 'wk': cast(rw_wk[slot]), 'wv': cast(rw_wv[slot]),
                 'wo': cast(rw_wo[slot]), 'w0': rw_w0[slot], 'a0': rw_a0[slot],
                 'w1': cast(cat(rw_w1[slot])), 'a1': cast(cat(rw_a1[slot])),
                 'w2': cast(rw_w2[slot].reshape(-1, d)), 'a2': cast(rw_a2[slot].reshape(-1, d)),
                 'g1': cast(rw_g1[slot]), 'g2': cast(rw_g2[slot]),
                 'k_k': rw_k_k[slot].reshape(1, d), 'k_a': rw_k_a[slot].reshape(1, d),
                 'r_k': rw_r_k[slot].reshape(1, d), 'lnx_g': rw_lnx_g[slot].reshape(1, d),
                 'lnx_b': rw_lnx_b[slot].reshape(1, d)}
        w1, w2 = cast(mlp_w1[i]), cast(mlp_w2[i])
        if kind == 2:
            x, xc_new = _rw_layer(x, xc, mod, mod_c, p, want_ctx, g1, b1, alpha)
            x = _mlp(x, mod, True, w1, w2, g2, b2, alpha)
            if want_ctx:
                xc = _unfold(_mlp(_fold(xc_new), mod_c, False, w1, w2, g2, b2, alpha), bsz)
        else:
            if kind == 0:
                o, oc, wo = _gqa_layer(x, xc, mod, mod_c, cast(att_wqkv[slot]), cast(att_wo[slot]),
                                       att_q_norm[slot], att_k_norm[slot], want_ctx)
            else:
                o, oc, wo = _na_layer(x, xc, mod, mod_c, cast(na_wqkv[slot]), cast(na_wo[slot]), na_rpb[slot],
                                      want_ctx)
            x = _out_mlp(o, wo, x, mod, True, g1, b1, w1, w2, g2, b2, alpha)
            if want_ctx:
                xc = _unfold(_out_mlp(_fold(oc), wo, _fold(xc), mod_c, False, g1, b1, w1, w2, g2, b2, alpha), bsz)
    return x
```

```python
import functools

import numpy as np
import jax
import jax.numpy as jnp
from jax import lax
from jax.experimental import pallas as pl
from jax.experimental.pallas import tpu as pltpu

F32 = jnp.float32
MXU_DTYPE = jnp.bfloat16

GRID_W = 64
N_MOD = 6
N_MIXERS = 3
LN_EPS = 1e-6
ATT_HEAD_DIM = 128
ATT_KV_HEADS = 2
ROPE_BASE = 10000.0
GQA_KEY_CHUNK = 512
NA_HEAD_DIM = 64
NA_WIN_H = 8
NA_WIN_W = 16
NA_Q_ROWS = 64
RW_HEAD = 64
RW_GN_EPS = 64e-5
RW_DECAY_FLOOR_RATE = float(np.exp(-0.5))
RW_CHUNK = 64
LANES = 128
NEG_BIG = -1e30
LOG2E = 1.4426950408889634

VMEM_LIMIT = 56 * 1024 * 1024

PROJ_ROWS = 512
MLP_ROWS = 1024
MLP_FF_TILE = 2048
MLP_FUSED_FF_TILE = 1024
GQA_Q_ROWS = 512
RW_OUT_ROWS = 512
RW_SCAN_CHUNKS = 8
MOD_COL_TILES = 4


def _cparams(*sem):
    return pltpu.CompilerParams(dimension_semantics=sem, vmem_limit_bytes=VMEM_LIMIT)


def _mm(a, b):
    return jnp.dot(a.astype(MXU_DTYPE), b.astype(MXU_DTYPE), preferred_element_type=F32)


def _mm_nt(a, b):
    return lax.dot_general(a.astype(MXU_DTYPE), b.astype(MXU_DTYPE), (((1,), (1,)), ((), ())),
                           preferred_element_type=F32)


def _mm_tn(a, b):
    return lax.dot_general(a.astype(MXU_DTYPE), b.astype(MXU_DTYPE), (((0,), (0,)), ((), ())),
                           preferred_element_type=F32)


def _split(a, n):
    parts = []
    for _ in range(n - 1):
        p = a.astype(MXU_DTYPE)
        parts.append(p)
        a = a - p.astype(F32)
    parts.append(a.astype(MXU_DTYPE))
    return parts


def _mm_rhs_exact(a, m01, n=2):
    m01 = m01.astype(MXU_DTYPE)
    return sum(jnp.dot(p, m01, preferred_element_type=F32) for p in _split(a, n))


def _mm_hi(a, b):
    a1, a2 = _split(a, 2)
    b1, b2 = _split(b, 2)
    d = functools.partial(jnp.dot, preferred_element_type=F32)
    return d(a1, b1) + d(a1, b2) + d(a2, b1)


def _sigmoid(x):
    return 1.0 / (1.0 + jnp.exp(-x))


def _layernorm(z, g, b):
    mu = jnp.mean(z, -1, keepdims=True)
    dz = z - mu
    var = jnp.mean(dz * dz, -1, keepdims=True)
    return dz * lax.rsqrt(var + LN_EPS) * g + b


def _mod_part(mod_ref, j, d):
    return mod_ref[0, :, j * d:(j + 1) * d]


def _head_ones(n=LANES):
    r = lax.broadcasted_iota(jnp.int32, (n, n), 0)
    c = lax.broadcasted_iota(jnp.int32, (n, n), 1)
    return ((r // RW_HEAD) == (c // RW_HEAD)).astype(F32)


def _modulation_kernel(c_ref, w_ref, b_ref, o_ref):
    c = c_ref[...]
    o_ref[0] = _mm_hi(c * _sigmoid(c), w_ref[0]) + b_ref[0]


def _modulation(cond, mod_w, mod_b):
    depth, d, n = mod_w.shape
    rows = cond.shape[0]
    tn = n // MOD_COL_TILES
    return pl.pallas_call(
        _modulation_kernel,
        grid=(depth, n // tn),
        in_specs=[pl.BlockSpec((rows, d), lambda i, j: (0, 0)),
                  pl.BlockSpec((1, d, tn), lambda i, j: (i, 0, j)),
                  pl.BlockSpec((1, 1, tn), lambda i, j: (i, 0, j))],
        out_specs=pl.BlockSpec((1, rows, tn), lambda i, j: (i, 0, j)),
        out_shape=jax.ShapeDtypeStruct((depth, rows, n), F32),
        compiler_params=_cparams("parallel", "parallel"),
        name="modulation",
    )(cond, mod_w, mod_b.reshape(depth, 1, n))


def _fold(a):
    return a.reshape(1, -1, a.shape[-1])


def _unfold(a, bsz):
    return a.reshape(bsz, -1, a.shape[-1])


def _mod_spec(d, per_batch):
    if per_batch:
        return pl.BlockSpec((1, 1, N_MOD * d), lambda b, *_: (b, 0, 0))
    return pl.BlockSpec((1, 1, N_MOD * d), lambda b, *_: (0, 0, 0))


def _mlp_kernel(x_ref, mod_ref, w1_ref, w2_ref, g_ref, b_ref, y_ref, h_scr, acc_scr, *, d, alpha):
    j = pl.program_id(2)

    @pl.when(j == 0)
    def _():
        h = x_ref[0] * (1.0 + _mod_part(mod_ref, 4, d)) + _mod_part(mod_ref, 3, d)
        h_scr[...] = h.astype(h_scr.dtype)
        acc_scr[...] = jnp.zeros_like(acc_scr)

    a = jnp.square(jnp.maximum(_mm(h_scr[...], w1_ref[...]), 0.0))
    acc_scr[...] += _mm(a, w2_ref[...])

    @pl.when(j == pl.num_programs(2) - 1)
    def _():
        z = alpha * x_ref[0] + _mod_part(mod_ref, 5, d) * acc_scr[...]
        y_ref[0] = _layernorm(z, g_ref[...], b_ref[...])


def _mlp(x, mod, per_batch, w1, w2, ln_g, ln_b, alpha):
    bsz, t, d = x.shape
    ff = w1.shape[1]
    tm = min(MLP_ROWS, t)
    tf = min(MLP_FF_TILE, ff)
    tok = lambda b, i, j: (b, i, 0)
    const = lambda b, i, j: (0, 0)
    return pl.pallas_call(
        functools.partial(_mlp_kernel, d=d, alpha=alpha),
        grid=(bsz, t // tm, ff // tf),
        in_specs=[pl.BlockSpec((1, tm, d), tok), _mod_spec(d, per_batch),
                  pl.BlockSpec((d, tf), lambda b, i, j: (0, j)), pl.BlockSpec((tf, d), lambda b, i, j: (j, 0)),
                  pl.BlockSpec((1, d), const), pl.BlockSpec((1, d), const)],
        out_specs=pl.BlockSpec((1, tm, d), tok),
        out_shape=jax.ShapeDtypeStruct((bsz, t, d), F32),
        scratch_shapes=[pltpu.VMEM((tm, d), MXU_DTYPE), pltpu.VMEM((tm, d), F32)],
        compiler_params=_cparams("parallel", "parallel", "arbitrary"),
        name="mlp",
    )(x, mod, w1, w2, ln_g, ln_b)


def _out_mlp_kernel(o_ref, wo_ref, x_ref, mod_ref, g1_ref, b1_ref, w1_ref, w2_ref, g2_ref, b2_ref, y_ref,
                    x1_scr, h_scr, acc_scr, *, d, alpha):
    j = pl.program_id(2)

    @pl.when(j == 0)
    def _():
        z = alpha * x_ref[0] + _mod_part(mod_ref, 2, d) * _mm(o_ref[0], wo_ref[...])
        x1 = _layernorm(z, g1_ref[...], b1_ref[...])
        x1_scr[...] = x1
        h_scr[...] = (x1 * (1.0 + _mod_part(mod_ref, 4, d)) + _mod_part(mod_ref, 3, d)).astype(h_scr.dtype)
        acc_scr[...] = jnp.zeros_like(acc_scr)

    a = jnp.square(jnp.maximum(_mm(h_scr[...], w1_ref[...]), 0.0))
    acc_scr[...] += _mm(a, w2_ref[...])

    @pl.when(j == pl.num_programs(2) - 1)
    def _():
        z = alpha * x1_scr[...] + _mod_part(mod_ref, 5, d) * acc_scr[...]
        y_ref[0] = _layernorm(z, g2_ref[...], b2_ref[...])


def _out_mlp(o, wo, x, mod, per_batch, ln1_g, ln1_b, w1, w2, ln2_g, ln2_b, alpha):
    bsz, t, d = x.shape
    ff = w1.shape[1]
    tm = min(MLP_ROWS, t)
    tf = min(MLP_FUSED_FF_TILE, ff)
    tok = pl.BlockSpec((1, tm, d), lambda b, i, j: (b, i, 0))
    vec = pl.BlockSpec((1, d), lambda b, i, j: (0, 0))
    return pl.pallas_call(
        functools.partial(_out_mlp_kernel, d=d, alpha=alpha),
        grid=(bsz, t // tm, ff // tf),
        in_specs=[tok, pl.BlockSpec((d, d), lambda b, i, j: (0, 0)), tok, _mod_spec(d, per_batch), vec, vec,
                  pl.BlockSpec((d, tf), lambda b, i, j: (0, j)), pl.BlockSpec((tf, d), lambda b, i, j: (j, 0)),
                  vec, vec],
        out_specs=tok,
        out_shape=jax.ShapeDtypeStruct((bsz, t, d), F32),
        scratch_shapes=[pltpu.VMEM((tm, d), F32), pltpu.VMEM((tm, d), MXU_DTYPE), pltpu.VMEM((tm, d), F32)],
        compiler_params=_cparams("parallel", "parallel", "arbitrary"),
        name="out_mlp",
    )(o, wo, x, mod, ln1_g, ln1_b, w1, w2, ln2_g, ln2_b)


def _gqa_qkv_kernel(x_ref, mod_ref, w_ref, gq_ref, gk_ref, cos_ref, sin_ref, q_ref, k_ref, v_ref,
                    *, d, n_q, n_kv):
    h = x_ref[0] * (1.0 + _mod_part(mod_ref, 1, d)) + _mod_part(mod_ref, 0, d)
    acc = _mm(h, w_ref[...])
    cos, sin = cos_ref[...], sin_ref[...]
    hd = ATT_HEAD_DIM
    src = lax.broadcasted_iota(jnp.int32, (hd, hd), 0)
    dst = lax.broadcasted_iota(jnp.int32, (hd, hd), 1)
    first = (dst & (hd // 2 - 1)) < hd // 4
    rot = (jnp.where(first & (src == dst + hd // 4), -1.0, 0.0)
           + jnp.where(jnp.logical_not(first) & (src == dst - hd // 4), 1.0, 0.0))

    def norm_rope(u, g):
        u = u * lax.rsqrt(jnp.mean(u * u, -1, keepdims=True) + LN_EPS) * g
        return u * cos + _mm(u, rot) * sin

    for i in range(n_q):
        q_ref[0, :, i * hd:(i + 1) * hd] = norm_rope(acc[:, i * hd:(i + 1) * hd], gq_ref[...]).astype(q_ref.dtype)
    for i in range(n_kv):
        c0 = (n_q + i) * hd
        k_ref[0, :, i * hd:(i + 1) * hd] = norm_rope(acc[:, c0:c0 + hd], gk_ref[...]).astype(k_ref.dtype)
    ones = jnp.ones((acc.shape[0], hd), v_ref.dtype)
    for i in range(n_kv):
        c0 = (n_q + n_kv + i) * hd
        v_ref[0, :, 2 * i * hd:(2 * i + 1) * hd] = acc[:, c0:c0 + hd].astype(v_ref.dtype)
        v_ref[0, :, (2 * i + 1) * hd:(2 * i + 2) * hd] = ones


def _gqa_qkv(x, mod, per_batch, w, gq, gk, cos, sin):
    bsz, t, d = x.shape
    hd = ATT_HEAD_DIM
    n_kv = ATT_KV_HEADS
    n_q = w.shape[1] // hd - 2 * n_kv
    tm = min(PROJ_ROWS, t)
    tok = lambda b, i: (b, i, 0)
    const = lambda b, i: (0, 0)
    tab = pl.BlockSpec((tm, hd), lambda b, i: (i, 0))
    return pl.pallas_call(
        functools.partial(_gqa_qkv_kernel, d=d, n_q=n_q, n_kv=n_kv),
        grid=(bsz, t // tm),
        in_specs=[pl.BlockSpec((1, tm, d), tok), _mod_spec(d, per_batch), pl.BlockSpec(w.shape, const),
                  pl.BlockSpec((1, hd), const), pl.BlockSpec((1, hd), const), tab, tab],
        out_specs=[pl.BlockSpec((1, tm, n_q * hd), tok), pl.BlockSpec((1, tm, n_kv * hd), tok),
                   pl.BlockSpec((1, tm, 2 * n_kv * hd), tok)],
        out_shape=[jax.ShapeDtypeStruct((bsz, t, n_q * hd), MXU_DTYPE),
                   jax.ShapeDtypeStruct((bsz, t, n_kv * hd), MXU_DTYPE),
                   jax.ShapeDtypeStruct((bsz, t, 2 * n_kv * hd), MXU_DTYPE)],
        compiler_params=_cparams("parallel", "parallel"),
        name="gqa_qkv",
    )(x, mod, w, gq, gk, cos, sin)


def _gqa_att_kernel(*refs, n_sets, groups, tk):
    q_ref, o_ref = refs[0], refs[-1]
    kv = refs[1:-1]
    hd = ATT_HEAD_DIM
    tq = q_ref.shape[1]
    q = jnp.concatenate([q_ref[0, :, g * hd:(g + 1) * hd] for g in range(groups)], axis=0)
    m = acc = None
    for j in range(n_sets):
        k_ref, v_ref = kv[2 * j], kv[2 * j + 1]
        t = k_ref.shape[1]
        for c0 in range(0, t, tk):
            c1 = min(c0 + tk, t)
            s = _mm_nt(q, k_ref[0, c0:c1, :])
            m_c = jnp.max(s, -1, keepdims=True)
            if m is None:
                m = m_c
                acc = _mm(jnp.exp2(s - m), v_ref[0, c0:c1, :])
            else:
                m_new = jnp.maximum(m, m_c)
                acc = acc * jnp.exp2(m - m_new) + _mm(jnp.exp2(s - m_new), v_ref[0, c0:c1, :])
                m = m_new
    o = acc[:, :hd] / acc[:, hd:]
    for g in range(groups):
        o_ref[0, :, g * hd:(g + 1) * hd] = o[g * tq:(g + 1) * tq].astype(o_ref.dtype)


def _gqa_att(q, kv_sets):
    bsz, s, dq = q.shape
    hd = ATT_HEAD_DIM
    n_kv = ATT_KV_HEADS
    groups = dq // hd // n_kv
    tq = min(GQA_Q_ROWS, s)
    in_specs = [pl.BlockSpec((1, tq, groups * hd), lambda b, kh, i: (b, i, kh))]
    args = [q]
    for k, v in kv_sets:
        t = k.shape[1]
        in_specs += [pl.BlockSpec((1, t, hd), lambda b, kh, i: (b, 0, kh)),
                     pl.BlockSpec((1, t, 2 * hd), lambda b, kh, i: (b, 0, kh))]
        args += [k, v]
    return pl.pallas_call(
        functools.partial(_gqa_att_kernel, n_sets=len(kv_sets), groups=groups, tk=GQA_KEY_CHUNK),
        grid=(bsz, n_kv, s // tq),
        in_specs=in_specs,
        out_specs=pl.BlockSpec((1, tq, groups * hd), lambda b, kh, i: (b, i, kh)),
        out_shape=jax.ShapeDtypeStruct((bsz, s, dq), MXU_DTYPE),
        compiler_params=_cparams("parallel", "parallel", "parallel"),
        name="gqa_att",
    )(*args)


def _rope_tables(n_tokens, head_dim):
    t = np.arange(n_tokens)
    row = (t // GRID_W).astype(np.float32)
    col = (t % GRID_W).astype(np.float32)
    half = head_dim // 2
    freqs = jnp.asarray(ROPE_BASE, F32) ** (-jnp.arange(0, half, 2, dtype=F32) / half)
    ang_r = jnp.asarray(row)[:, None] * freqs[None, :]
    ang_c = jnp.asarray(col)[:, None] * freqs[None, :]
    ang = jnp.concatenate([ang_r, ang_r, ang_c, ang_c], axis=-1)
    return jnp.cos(ang), jnp.sin(ang)


def _gqa_layer(x, xc, mod, mod_c, wqkv, wo, q_norm, k_norm, want_ctx):
    s, tc = x.shape[1], xc.shape[1]
    hd = ATT_HEAD_DIM
    gq = (q_norm * (hd ** -0.5 * LOG2E)).reshape(1, hd)
    gk = k_norm.reshape(1, hd)
    cos, sin = _rope_tables(s, hd)
    bsz = x.shape[0]
    one, zero = jnp.ones((bsz * tc, hd), F32), jnp.zeros((bsz * tc, hd), F32)
    q, k, v = _gqa_qkv(x, mod, True, wqkv, gq, gk, cos, sin)
    qc, kc, vc = [_unfold(a, bsz) for a in _gqa_qkv(_fold(xc), mod_c, False, wqkv, gq, gk, one, zero)]
    o = _gqa_att(q, [(k, v), (kc, vc)])
    oc = _gqa_att(qc, [(kc, vc)]) if want_ctx else None
    return o, oc, wo


def _na_qkv_kernel(x_ref, mod_ref, w_ref, q_ref, k_ref, v_ref, *, d, scale):
    h = x_ref[0] * (1.0 + _mod_part(mod_ref, 1, d)) + _mod_part(mod_ref, 0, d)
    acc = _mm(h, w_ref[...])
    q_ref[0] = (acc[:, :d] * scale).astype(q_ref.dtype)
    k_ref[0] = acc[:, d:2 * d].astype(k_ref.dtype)
    ones = jnp.ones((acc.shape[0], LANES), v_ref.dtype)
    for p in range(d // LANES):
        c0 = 2 * d + p * LANES
        v_ref[0, :, 2 * p * LANES:(2 * p + 1) * LANES] = acc[:, c0:c0 + LANES].astype(v_ref.dtype)
        v_ref[0, :, (2 * p + 1) * LANES:(2 * p + 2) * LANES] = ones


def _na_qkv(x, mod, per_batch, w):
    bsz, t, d = x.shape
    tm = min(PROJ_ROWS, t)
    tok = lambda b, i: (b, i, 0)
    out = jax.ShapeDtypeStruct((bsz, t, d), MXU_DTYPE)
    return pl.pallas_call(
        functools.partial(_na_qkv_kernel, d=d, scale=NA_HEAD_DIM ** -0.5 * LOG2E),
        grid=(bsz, t // tm),
        in_specs=[pl.BlockSpec((1, tm, d), tok), _mod_spec(d, per_batch), pl.BlockSpec(w.shape, lambda b, i: (0, 0))],
        out_specs=[pl.BlockSpec((1, tm, d), tok)] * 2 + [pl.BlockSpec((1, tm, 2 * d), tok)],
        out_shape=[out, out, jax.ShapeDtypeStruct((bsz, t, 2 * d), MXU_DTYPE)],
        compiler_params=_cparams("parallel", "parallel"),
        name="na_qkv",
    )(x, mod, w)


def _pair_softmax_att(q, ks, vs, biases):
    lane = lax.broadcasted_iota(jnp.int32, q.shape, 1)
    first = lane < NA_HEAD_DIM
    zero = jnp.zeros_like(q)
    qm = [jnp.where(first, q, zero), jnp.where(first, zero, q)]
    m, acc = [None, None], [None, None]
    for j, (k, v) in enumerate(zip(ks, vs)):
        for h in range(2):
            s = _mm_nt(qm[h], k)
            if biases[h][j] is not None:
                s = s + biases[h][j]
            m_c = jnp.max(s, -1, keepdims=True)
            if m[h] is None:
                m[h] = m_c
                acc[h] = _mm(jnp.exp2(s - m_c), v)
            else:
                m_new = jnp.maximum(m[h], m_c)
                acc[h] = acc[h] * jnp.exp2(m[h] - m_new) + _mm(jnp.exp2(s - m_new), v)
                m[h] = m_new
    outs = [a[:, :LANES] / a[:, LANES:] for a in acc]
    return jnp.where(first, outs[0], outs[1])


def _na_att_kernel(q_ref, k_ref, v_ref, kc_ref, vc_ref, bias_ref, o_ref, *, rows):
    i = pl.program_id(2)
    q_rows = q_ref.shape[1] // GRID_W
    win = NA_WIN_H * GRID_W
    w = GRID_W
    first = lax.broadcasted_iota(jnp.int32, (w, LANES), 1) < NA_HEAD_DIM
    qm, kw, vw, bias = [], [], [], []
    for rq in range(q_rows):
        row_q = i * q_rows + rq
        r0 = jnp.clip(row_q - NA_WIN_H // 2, 0, rows - NA_WIN_H)
        start = pl.multiple_of(r0 * w, w)
        q = q_ref[0, rq * w:(rq + 1) * w, :]
        zero = jnp.zeros_like(q)
        qm.append(jnp.concatenate([jnp.where(first, q, zero), jnp.where(first, zero, q)], axis=0))
        kw.append(k_ref[0, pl.ds(start, win), :])
        vw.append(v_ref[0, pl.ds(start, win), :])
        e = r0 - row_q + NA_WIN_H - 1
        bias.append(jnp.concatenate([bias_ref[0, 0, e], bias_ref[0, 1, e]], axis=0))
    each = lambda fn, *lists: [fn(*args) for args in zip(*lists)]
    s_ctx = _mm_nt(jnp.concatenate(qm, axis=0), kc_ref[0])
    s_ctx = [s_ctx[2 * w * rq:2 * w * (rq + 1)] for rq in range(q_rows)]
    s_loc = each(lambda a, b, c: _mm_nt(a, b) + c, qm, kw, bias)
    m = each(lambda a, b: jnp.maximum(jnp.max(a, -1, keepdims=True), jnp.max(b, -1, keepdims=True)), s_loc, s_ctx)
    acc_ctx = _mm(jnp.concatenate(each(lambda a, mm: jnp.exp2(a - mm), s_ctx, m), axis=0), vc_ref[0])
    acc = each(lambda a, mm, v: _mm(jnp.exp2(a - mm), v), s_loc, m, vw)
    for rq in range(q_rows):
        a = acc[rq] + acc_ctx[2 * w * rq:2 * w * (rq + 1)]
        o = a[:, :LANES] / a[:, LANES:]
        o_ref[0, rq * w:(rq + 1) * w, :] = jnp.where(first, o[:w], o[w:]).astype(o_ref.dtype)


def _na_bias_tables(rpb, scale):
    pad = GRID_W
    rpb_p = jnp.pad(rpb, ((0, 0), (0, 0), (pad, pad)))
    cmat = jnp.stack([rpb_p[:, :, pad + NA_WIN_W - 1 - cq: pad + NA_WIN_W - 1 - cq + GRID_W]
                      for cq in range(GRID_W)], axis=2)
    cq, ck = np.arange(GRID_W)[:, None], np.arange(GRID_W)[None, :]
    c0 = np.clip(cq - NA_WIN_W // 2, 0, GRID_W - NA_WIN_W)
    vcol = (ck >= c0) & (ck < c0 + NA_WIN_W)
    cmat = jnp.where(vcol[None, None], cmat * scale, NEG_BIG)
    return jnp.stack([jnp.concatenate([cmat[:, e + j] for j in range(NA_WIN_H)], axis=-1)
                      for e in range(NA_WIN_H)], axis=1)


def _na_att(q, k, v, kc, vc, bias):
    bsz, s, d = q.shape
    tc = kc.shape[1]
    rows = s // GRID_W
    q_rows = min(NA_Q_ROWS, rows)
    assert rows % q_rows == 0 and rows >= NA_WIN_H
    tq = q_rows * GRID_W
    pairs = d // LANES
    q_spec = pl.BlockSpec((1, tq, LANES), lambda hp, b, i: (b, i, hp))
    whole = lambda t, width: pl.BlockSpec((1, t, width), lambda hp, b, i: (b, 0, hp))
    bias_spec = pl.BlockSpec((1, 2) + bias.shape[1:], lambda hp, b, i: (0, hp, 0, 0, 0))
    return pl.pallas_call(
        functools.partial(_na_att_kernel, rows=rows),
        grid=(pairs, bsz, rows // q_rows),
        in_specs=[q_spec, whole(s, LANES), whole(s, 2 * LANES), whole(tc, LANES), whole(tc, 2 * LANES), bias_spec],
        out_specs=q_spec,
        out_shape=jax.ShapeDtypeStruct((bsz, s, d), MXU_DTYPE),
        compiler_params=_cparams("parallel", "parallel", "parallel"),
        name="na_att",
    )(q, k, v, kc, vc, bias[None])


def _pair_att_kernel(q_ref, k_ref, v_ref, o_ref):
    o_ref[0] = _pair_softmax_att(q_ref[0], [k_ref[0]], [v_ref[0]], [[None], [None]]).astype(o_ref.dtype)


def _pair_att(q, k, v):
    bsz, t, d = q.shape
    spec = pl.BlockSpec((1, t, LANES), lambda b, hp: (b, 0, hp))
    return pl.pallas_call(
        _pair_att_kernel,
        grid=(bsz, d // LANES),
        in_specs=[spec, spec, pl.BlockSpec((1, t, 2 * LANES), lambda b, hp: (b, 0, hp))],
        out_specs=spec,
        out_shape=jax.ShapeDtypeStruct((bsz, t, d), MXU_DTYPE),
        compiler_params=_cparams("parallel", "parallel"),
        name="na_ctx_att",
    )(q, k, v)


def _na_layer(x, xc, mod, mod_c, wqkv, wo, rpb, want_ctx):
    q, k, v = _na_qkv(x, mod, True, wqkv)
    qc, kc, vc = [_unfold(a, x.shape[0]) for a in _na_qkv(_fold(xc), mod_c, False, wqkv)]
    o = _na_att(q, k, v, kc, vc, _na_bias_tables(rpb, LOG2E))
    oc = _pair_att(qc, kc, vc) if want_ctx else None
    return o, oc, wo


def _rw_proj_kernel(x_ref, xp_ref, xn_ref, mod_ref, mu_ref, wr_ref, wk_ref, wv_ref, w1_ref, w2_ref, a1_ref, a2_ref,
                    g1_ref, g2_ref, w0_ref, a0_ref, r_ref, k_ref, v_ref, g_ref, lw_ref, as_ref, *, d):
    t = pl.program_id(1)
    nt = pl.num_programs(1)
    scale = 1.0 + _mod_part(mod_ref, 1, d)
    shift = _mod_part(mod_ref, 0, d)
    h = x_ref[0] * scale + shift
    tm = h.shape[0]
    h_prev = (xp_ref[0, 7:8, :] * scale + shift) * (t > 0).astype(F32)
    h_next = (xn_ref[0, 0:1, :] * scale + shift) * (t < nt - 1).astype(F32)
    row = lax.broadcasted_iota(jnp.int32, h.shape, 0)
    prev = jnp.where(row == 0, h_prev, pltpu.roll(h, 1, 0))
    nxt = jnp.where(row == tm - 1, h_next, pltpu.roll(h, tm - 1, 0))
    xx = 0.5 * (prev + nxt) - h
    mix = lambda j: h + xx * mu_ref[j:j + 1, :]

    r_ref[0] = _mm(mix(0), wr_ref[...]).astype(r_ref.dtype)
    k_ref[0] = _mm(mix(2), wk_ref[...]).astype(k_ref.dtype)
    v_ref[0] = _mm(mix(3), wv_ref[...]).astype(v_ref.dtype)
    g_ref[0] = _mm(_sigmoid(_mm(mix(5), g1_ref[...])), g2_ref[...]).astype(g_ref.dtype)

    tw = jnp.tanh(_mm(mix(1), w1_ref[...]))
    al = _mm(mix(4), a1_ref[...])
    first = lax.broadcasted_iota(jnp.int32, tw.shape, 1) < tw.shape[1] // 2
    zero = jnp.zeros_like(tw)
    for n in range(2):
        pick = lambda u: jnp.where(first, u, zero) if n == 0 else jnp.where(first, zero, u)
        u = w0_ref[n:n + 1, :] + _mm(pick(tw), w2_ref[...])
        lw_ref[n, 0] = -RW_DECAY_FLOOR_RATE * _sigmoid(u)
        as_ref[n, 0] = _sigmoid(a0_ref[n:n + 1, :] + _mm(pick(al), a2_ref[...])).astype(as_ref.dtype)


def _rw_proj(x, mod, per_batch, p):
    bsz, t, d = x.shape
    tm = min(PROJ_ROWS, t)
    n8 = t // 8
    tok = lambda b, i: (b, i, 0)
    dtok = lambda b, i: (0, b, i, 0)
    full = lambda a: pl.BlockSpec(a.shape, lambda b, i: (0,) * a.ndim)
    halo_prev = pl.BlockSpec((1, 8, d), lambda b, i: (b, jnp.maximum(i * (tm // 8) - 1, 0), 0))
    halo_next = pl.BlockSpec((1, 8, d), lambda b, i: (b, jnp.minimum((i + 1) * (tm // 8), n8 - 1), 0))
    weights = [p['mu'], p['wr'], p['wk'], p['wv'], p['w1'], p['w2'], p['a1'], p['a2'], p['g1'], p['g2'],
               p['w0'], p['a0']]
    one = jax.ShapeDtypeStruct((bsz, t, d), MXU_DTYPE)
    two = lambda dtype: jax.ShapeDtypeStruct((2, bsz, t, d), dtype)
    return pl.pallas_call(
        functools.partial(_rw_proj_kernel, d=d),
        grid=(bsz, t // tm),
        in_specs=[pl.BlockSpec((1, tm, d), tok), halo_prev, halo_next, _mod_spec(d, per_batch)]
                 + [full(a) for a in weights],
        out_specs=[pl.BlockSpec((1, tm, d), tok)] * 4 + [pl.BlockSpec((2, 1, tm, d), dtok)] * 2,
        out_shape=[one, one, one, one, two(F32), two(MXU_DTYPE)],
        compiler_params=_cparams("parallel", "parallel"),
        name="rw_proj",
    )(x, x, x, mod, *weights)


def _rw_scan_kernel(r_ref, k_ref, v_ref, lw_ref, as_ref, kk_ref, ka_ref, z0_ref, y_ref, zf_ref, z_scr,
                    *, pairs, sub, reverse):
    c = pl.program_id(1)
    L = RW_CHUNK
    H = RW_HEAD
    assert 2 * L == LANES and 2 * H == LANES

    @pl.when(c == 0)
    def _():
        z_scr[...] = z0_ref[0, 0]

    sign = -1 if reverse else 1
    t_i = lax.broadcasted_iota(jnp.int32, (L, 2 * L), 0)
    s_i = lax.broadcasted_iota(jnp.int32, (L, 2 * L), 1) & (L - 1)
    dt = sign * (s_i - t_i)
    strict_c = dt < 0
    incl_c = dt <= 0
    tri2 = incl_c.astype(F32)
    eye_c = (dt == 0).astype(F32)
    r2 = lax.broadcasted_iota(jnp.int32, (LANES, LANES), 0)
    c2 = lax.broadcasted_iota(jnp.int32, (LANES, LANES), 1)
    eye2 = (r2 == c2).astype(F32)
    same_head = _head_ones()
    first = lax.broadcasted_iota(jnp.int32, (L, LANES), 1) < H
    zero = jnp.zeros((L, LANES), F32)

    head0 = lambda u: jnp.where(first, u, zero)
    head1 = lambda u: jnp.where(first, zero, u)
    bd = lambda u: jnp.concatenate([head0(u), head1(u)], axis=0)

    each = lambda fn, *lists: [fn(*args) for args in zip(*lists)]
    sls = [slice(p * LANES, (p + 1) * LANES) for p in range(pairs)]
    cat0 = lambda *u: jnp.concatenate(u, axis=0)
    cat1 = lambda *u: jnp.concatenate(u, axis=1)

    mm_each = lambda lhs, w: [_mm(a, b) for a, b in zip(lhs, w)]

    def chunk_terms(rw):
        lw = [lw_ref[0, 0, rw, sl] for sl in sls]
        r = [r_ref[0, rw, sl].astype(F32) for sl in sls]
        k = [k_ref[0, rw, sl].astype(F32) for sl in sls]
        v = [v_ref[0, rw, sl].astype(F32) for sl in sls]
        a_s = [as_ref[0, 0, rw, sl].astype(F32) for sl in sls]

        c_in = each(lambda u: _mm(tri2, cat0(*_split(u, 2))), lw)
        c_all = each(lambda u: jnp.sum(u, axis=0, keepdims=True), lw)
        kk = each(lambda u, sl: u * kk_ref[:, sl], k, sls)
        ss = each(lambda u: _mm(u * u, same_head), kk)
        kk = each(lambda u, s: u * lax.rsqrt(jnp.maximum(s, 1e-12)), kk, ss)
        b_v = each(lambda u, a: u * a, kk, a_s)
        k_d = each(lambda u, a, sl: u * (1.0 + (a - 1.0) * ka_ref[:, sl]), k, a_s, sls)
        e_neg = each(lambda ci: jnp.exp(-ci), c_in)
        e_all = each(jnp.exp, c_all)
        a_t = each(lambda u, ci, l: -u * jnp.exp(ci - l), kk, c_in, lw)
        r_t = each(lambda u, ci: u * jnp.exp(ci), r, c_in)
        b_t = each(lambda u, e: u * e, b_v, e_neg)
        k_t = each(lambda u, e: u * e, k_d, e_neg)
        b_h = each(lambda u, e: u * e, b_t, e_all)
        k_h = each(lambda u, e: u * e, k_t, e_all)
        yield None

        x = each(lambda a, rr, b, kt: _mm_nt(cat0(a, rr), cat0(bd(b), bd(kt))), a_t, r_t, b_t, k_t)
        m_ab = each(lambda u: jnp.where(strict_c, u[:L, :2 * L], 0.0), x)
        m_ak = each(lambda u: jnp.where(strict_c, u[:L, 2 * L:], 0.0), x)
        n_rb = each(lambda u: jnp.where(incl_c, u[L:, :2 * L], 0.0), x)
        n_rk = each(lambda u: jnp.where(incl_c, u[L:, 2 * L:], 0.0), x)

        inv = each(lambda m: eye_c + m, m_ab)
        yield None
        pw = mm_each(m_ab, each(bd, m_ab))
        for _ in range(int(np.log2(L)) - 2):
            yield None
            st = mm_each(each(cat0, pw, inv), each(bd, pw))
            pw = each(lambda s: s[:L], st)
            inv = each(lambda b, s: b + s[L:], inv, st)
        yield None
        inv = each(lambda b, s: b + s, inv, mm_each(inv, each(bd, pw)))
        mv = mm_each(each(cat0, m_ak, n_rk), each(bd, v))
        yield None
        tw = each(lambda i, a, m: _mm(i, cat1(bd(a), bd(m[:L]))), inv, a_t, mv)
        p1 = each(lambda u: u[:, :LANES], tw)
        p2 = each(lambda u: u[:, LANES:], tw)
        yield None
        nw = each(lambda m, a, b: _mm(m, cat1(bd(a), bd(b))), n_rb, p1, p2)
        q1 = each(lambda rr, u: rr + u[:, :LANES], r_t, nw)
        q2 = each(lambda u, m: u[:, LANES:] + m[L:], nw, mv)
        gh = each(lambda b, kh, a, c_, u: _mm_tn(cat0(b, kh), cat0(cat1(a, c_), cat1(zero, u))),
                  b_h, k_h, p1, p2, v)
        g_t = each(lambda ea, u: eye2 * ea + same_head * u[:, :LANES], e_all, gh)
        h_t = each(lambda u: same_head * u[:, LANES:], gh)
        yield q1, q2, g_t, h_t

    chunk_rows = [slice(u * L, (u + 1) * L) for u in (range(sub - 1, -1, -1) if reverse else range(sub))]
    gens = [chunk_terms(rw) for rw in chunk_rows]
    terms = [None] * sub
    while any(t is None for t in terms):
        terms = [next(gen) for gen in gens]
    z = [z_scr[p] for p in range(pairs)]
    for rw, (q1, q2, g_t, h_t) in zip(chunk_rows, terms):
        yz = mm_each(each(cat0, q1, g_t), z)
        for p in range(pairs):
            y_ref[0, rw, sls[p]] = (yz[p][:L] + q2[p]).astype(y_ref.dtype)
            z[p] = yz[p][L:] + h_t[p]
    for p in range(pairs):
        z_scr[p] = z[p]

    @pl.when(c == pl.num_programs(1) - 1)
    def _():
        zf_ref[0] = z_scr[...]


def _rw_scan(r, k, v, lw, a_s, kk_w, ka_w, z0, reverse):
    bsz, t, d = r.shape
    sub = min(RW_SCAN_CHUNKS, t // RW_CHUNK)
    rows = sub * RW_CHUNK
    assert t % rows == 0
    n_c = t // rows
    n = int(reverse)
    pairs = d // LANES
    cidx = (lambda c: n_c - 1 - c) if reverse else (lambda c: c)
    tok = pl.BlockSpec((1, rows, d), lambda b, c: (b, cidx(c), 0))
    dtok = pl.BlockSpec((1, 1, rows, d), lambda b, c: (n, b, cidx(c), 0))
    vec = pl.BlockSpec((1, d), lambda b, c: (0, 0))
    return pl.pallas_call(
        functools.partial(_rw_scan_kernel, pairs=pairs, sub=sub, reverse=reverse),
        grid=(bsz, n_c),
        in_specs=[tok, tok, tok, dtok, dtok, vec, vec,
                  pl.BlockSpec((1, 1, pairs, LANES, LANES), lambda b, c: (n, b, 0, 0, 0))],
        out_specs=[tok, pl.BlockSpec((1, pairs, LANES, LANES), lambda b, c: (b, 0, 0, 0))],
        out_shape=[jax.ShapeDtypeStruct((bsz, t, d), MXU_DTYPE),
                   jax.ShapeDtypeStruct((bsz, pairs, LANES, LANES), F32)],
        scratch_shapes=[pltpu.VMEM((pairs, LANES, LANES), F32)],
        compiler_params=_cparams("parallel", "arbitrary"),
        name="rw_scan",
    )(r, k, v, lw, a_s, kk_w, ka_w, z0)


def _rw_out_kernel(yf_ref, yb_ref, r_ref, k_ref, v_ref, g_ref, as_ref, ka_ref, rk_ref, lg_ref, lb_ref, wo_ref, x_ref,
                   mod_ref, g_ln_ref, b_ln_ref, o_ref, u_scr, *, d, alpha):
    width = 2 * LANES
    same_head = _head_ones(width)
    inv_n = 1.0 / RW_HEAD
    for p in range(d // width):
        sl = slice(p * width, (p + 1) * width)
        y = yf_ref[0, :, sl].astype(F32) + yb_ref[0, :, sl].astype(F32)
        mu = _mm_rhs_exact(y, same_head) * inv_n
        dy = y - mu
        var = _mm(dy * dy, same_head) * inv_n
        yn = dy * lax.rsqrt(var + RW_GN_EPS) * lg_ref[:, sl] + lb_ref[:, sl]
        k = k_ref[0, :, sl].astype(F32)
        ka = ka_ref[:, sl]
        k_sum = k * (2.0 + (as_ref[0, 0, :, sl].astype(F32) + as_ref[1, 0, :, sl].astype(F32) - 2.0) * ka)
        bonus = _mm(r_ref[0, :, sl].astype(F32) * k_sum * rk_ref[:, sl], same_head)
        u = (yn + bonus * v_ref[0, :, sl].astype(F32)) * g_ref[0, :, sl].astype(F32)
        u_scr[:, sl] = u.astype(u_scr.dtype)
    z = alpha * x_ref[0] + _mod_part(mod_ref, 2, d) * _mm(u_scr[...], wo_ref[...])
    o_ref[0] = _layernorm(z, g_ln_ref[...], b_ln_ref[...])


def _rw_out(y_f, y_b, r, k, v, g, a_s, p, x, mod, per_batch, ln_g, ln_b, alpha):
    bsz, t, d = x.shape
    tm = min(RW_OUT_ROWS, t)
    tok = pl.BlockSpec((1, tm, d), lambda b, i: (b, i, 0))
    dtok = pl.BlockSpec((2, 1, tm, d), lambda b, i: (0, b, i, 0))
    vec = pl.BlockSpec((1, d), lambda b, i: (0, 0))
    return pl.pallas_call(
        functools.partial(_rw_out_kernel, d=d, alpha=alpha),
        grid=(bsz, t // tm),
        in_specs=[tok, tok, tok, tok, tok, tok, dtok, vec, vec, vec, vec, pl.BlockSpec((d, d), lambda b, i: (0, 0)),
                  tok, _mod_spec(d, per_batch), vec, vec],
        out_specs=tok,
        out_shape=jax.ShapeDtypeStruct((bsz, t, d), F32),
        scratch_shapes=[pltpu.VMEM((tm, d), MXU_DTYPE)],
        compiler_params=_cparams("parallel", "parallel"),
        name="rw_out",
    )(y_f, y_b, r, k, v, g, a_s, p['k_a'], p['r_k'], p['lnx_g'], p['lnx_b'], p['wo'], x, mod, ln_g, ln_b)


def _rw_layer(x, xc, mod, mod_c, p, want_ctx, ln_g, ln_b, alpha):
    bsz, _, d = x.shape
    rc, kc, vc, gc, lwc, asc = _rw_proj(xc, mod_c, False, p)
    r, k, v, g, lw, a_s = _rw_proj(x, mod, True, p)
    z0 = jnp.zeros((2, bsz, d // LANES, LANES, LANES), F32)
    scan = lambda args, z, rev: _rw_scan(*args, p['k_k'], p['k_a'], z, rev)
    yc_f, zc_f = scan((rc, kc, vc, lwc, asc), z0, False)
    yc_b, zc_b = scan((rc, kc, vc, lwc, asc), z0, True)
    zc = jnp.stack([zc_f, zc_b])
    y_f, _ = scan((r, k, v, lw, a_s), zc, False)
    y_b, _ = scan((r, k, v, lw, a_s), zc, True)
    x_new = _rw_out(y_f, y_b, r, k, v, g, a_s, p, x, mod, True, ln_g, ln_b, alpha)
    xc_new = (_rw_out(yc_f, yc_b, rc, kc, vc, gc, asc, p, xc, mod_c, False, ln_g, ln_b, alpha)
              if want_ctx else None)
    return x_new, xc_new


def kernel(x, c, ctx, c_ctx, mod_w, mod_b, post_ln_g, post_ln_b, mlp_w1, mlp_w2, att_wqkv, att_wo, att_q_norm, att_k_norm, na_wqkv, na_wo, na_rpb, rw_mu, rw_wr, rw_wk, rw_wv, rw_wo, rw_w0, rw_w1, rw_w2, rw_a0, rw_a1, rw_a2, rw_g1, rw_g2, rw_k_k, rw_k_a, rw_r_k, rw_lnx_g, rw_lnx_b):
    depth = mod_w.shape[0]
    bsz, _, d = x.shape
    alpha = (2.0 * depth) ** 0.25
    cast = lambda a: a.astype(MXU_DTYPE)

    cond_rows = -(-(bsz + 1) // 8) * 8
    cond = jnp.zeros((cond_rows, d), F32).at[:bsz].set(c).at[bsz].set(c_ctx)
    mods = _modulation(cond, mod_w, mod_b)

    xc = ctx
    for i in range(depth):
        kind, slot = i % N_MIXERS, i // N_MIXERS
        want_ctx = i < depth - 1
        mod = mods[i, :bsz].reshape(bsz, 1, N_MOD * d)
        mod_c = mods[i, bsz:bsz + 1].reshape(1, 1, N_MOD * d)
        g1, b1 = post_ln_g[i, 0:1], post_ln_b[i, 0:1]
        g2, b2 = post_ln_g[i, 1:2], post_ln_b[i, 1:2]
        if kind == 2:
            cat = lambda a: jnp.concatenate([a[0], a[1]], axis=-1)
            p = {'mu': rw_mu[slot], 'wr': cast(rw_wr[slot]), 'wk': cast(rw_wk[slot]), 'wv': cast(rw_wv[slot]),
                 'wo': cast(rw_wo[slot]), 'w0': rw_w0[slot], 'a0': rw_a0[slot],
                 'w1': cast(cat(rw_w1[slot])), 'a1': cast(cat(rw_a1[slot])),
                 'w2': cast(rw_w2[slot].reshape(-1, d)), 'a2': cast(rw_a2[slot].reshape(-1, d)),
                 'g1': cast(rw_g1[slot]), 'g2': cast(rw_g2[slot]),
                 'k_k': rw_k_k[slot].reshape(1, d), 'k_a': rw_k_a[slot].reshape(1, d),
                 'r_k': rw_r_k[slot].reshape(1, d), 'lnx_g': rw_lnx_g[slot].reshape(1, d),
                 'lnx_b': rw_lnx_b[slot].reshape(1, d)}
        w1, w2 = cast(mlp_w1[i]), cast(mlp_w2[i])
        if kind == 2:
            x, xc_new = _rw_layer(x, xc, mod, mod_c, p, want_ctx, g1, b1, alpha)
            x = _mlp(x, mod, True, w1, w2, g2, b2, alpha)
            if want_ctx:
                xc = _unfold(_mlp(_fold(xc_new), mod_c, False, w1, w2, g2, b2, alpha), bsz)
        else:
            if kind == 0:
                o, oc, wo = _gqa_layer(x, xc, mod, mod_c, cast(att_wqkv[slot]), cast(att_wo[slot]),
                                       att_q_norm[slot], att_k_norm[slot], want_ctx)
            else:
                o, oc, wo = _na_layer(x, xc, mod, mod_c, cast(na_wqkv[slot]), cast(na_wo[slot]), na_rpb[slot],
                                      want_ctx)
            x = _out_mlp(o, wo, x, mod, True, g1, b1, w1, w2, g2, b2, alpha)
            if want_ctx:
                xc = _unfold(_out_mlp(_fold(oc), wo, _fold(xc), mod_c, False, g1, b1, w1, w2, g2, b2, alpha), bsz)
    return x
```

```python
import functools

import numpy as np
import jax
import jax.numpy as jnp
from jax import lax
from jax.experimental import pallas as pl
from jax.experimental.pallas import tpu as pltpu

F32 = jnp.float32
MXU_DTYPE = jnp.bfloat16

GRID_W = 64
N_MOD = 6
N_MIXERS = 3
LN_EPS = 1e-6
ATT_HEAD_DIM = 128
ATT_KV_HEADS = 2
ROPE_BASE = 10000.0
GQA_KEY_CHUNK = 512
NA_HEAD_DIM = 64
NA_WIN_H = 8
NA_WIN_W = 16
NA_Q_ROWS = 64
RW_HEAD = 64
RW_GN_EPS = 64e-5
RW_DECAY_FLOOR_RATE = float(np.exp(-0.5))
RW_CHUNK = 64
LANES = 128
NEG_BIG = -1e30
LOG2E = 1.4426950408889634

VMEM_LIMIT = 56 * 1024 * 1024

PROJ_ROWS = 512
MLP_ROWS = 1024
MLP_FF_TILE = 2048
MLP_FUSED_FF_TILE = 1024
GQA_Q_ROWS = 512
RW_OUT_ROWS = 512
RW_SCAN_CHUNKS = 8
MOD_COL_TILES = 4


def _cparams(*sem):
    return pltpu.CompilerParams(dimension_semantics=sem, vmem_limit_bytes=VMEM_LIMIT)


def _mm(a, b):
    return jnp.dot(a.astype(MXU_DTYPE), b.astype(MXU_DTYPE), preferred_element_type=F32)


def _mm_nt(a, b):
    return lax.dot_general(a.astype(MXU_DTYPE), b.astype(MXU_DTYPE), (((1,), (1,)), ((), ())),
                           preferred_element_type=F32)


def _mm_tn(a, b):
    return lax.dot_general(a.astype(MXU_DTYPE), b.astype(MXU_DTYPE), (((0,), (0,)), ((), ())),
                           preferred_element_type=F32)


def _split(a, n):
    parts = []
    for _ in range(n - 1):
        p = a.astype(MXU_DTYPE)
        parts.append(p)
        a = a - p.astype(F32)
    parts.append(a.astype(MXU_DTYPE))
    return parts


def _mm_hi(a, b):
    a1, a2 = _split(a, 2)
    b1, b2 = _split(b, 2)
    d = functools.partial(jnp.dot, preferred_element_type=F32)
    return d(a1, b1) + d(a1, b2) + d(a2, b1)


def _sigmoid(x):
    return 1.0 / (1.0 + jnp.exp(-x))


def _layernorm(z, g, b):
    mu = jnp.mean(z, -1, keepdims=True)
    dz = z - mu
    var = jnp.mean(dz * dz, -1, keepdims=True)
    return dz * lax.rsqrt(var + LN_EPS) * g + b


def _mod_part(mod_ref, j, d):
    return mod_ref[0, :, j * d:(j + 1) * d]


def _head_ones(n=LANES):
    r = lax.broadcasted_iota(jnp.int32, (n, n), 0)
    c = lax.broadcasted_iota(jnp.int32, (n, n), 1)
    return ((r // RW_HEAD) == (c // RW_HEAD)).astype(F32)


def _modulation_kernel(c_ref, w_ref, b_ref, o_ref):
    c = c_ref[...]
    o_ref[0] = _mm_hi(c * _sigmoid(c), w_ref[0]) + b_ref[0]


def _modulation(cond, mod_w, mod_b):
    depth, d, n = mod_w.shape
    rows = cond.shape[0]
    tn = n // MOD_COL_TILES
    return pl.pallas_call(
        _modulation_kernel,
        grid=(depth, n // tn),
        in_specs=[pl.BlockSpec((rows, d), lambda i, j: (0, 0)),
                  pl.BlockSpec((1, d, tn), lambda i, j: (i, 0, j)),
                  pl.BlockSpec((1, 1, tn), lambda i, j: (i, 0, j))],
        out_specs=pl.BlockSpec((1, rows, tn), lambda i, j: (i, 0, j)),
        out_shape=jax.ShapeDtypeStruct((depth, rows, n), F32),
        compiler_params=_cparams("parallel", "parallel"),
        name="modulation",
    )(cond, mod_w, mod_b.reshape(depth, 1, n))


def _fold(a):
    return a.reshape(1, -1, a.shape[-1])


def _unfold(a, bsz):
    return a.reshape(bsz, -1, a.shape[-1])


def _mod_spec(d, per_batch):
    if per_batch:
        return pl.BlockSpec((1, 1, N_MOD * d), lambda b, *_: (b, 0, 0))
    return pl.BlockSpec((1, 1, N_MOD * d), lambda b, *_: (0, 0, 0))


def _mlp_kernel(x_ref, mod_ref, w1_ref, w2_ref, g_ref, b_ref, y_ref, h_scr, acc_scr, *, d, alpha):
    j = pl.program_id(2)

    @pl.when(j == 0)
    def _():
        h = x_ref[0] * (1.0 + _mod_part(mod_ref, 4, d)) + _mod_part(mod_ref, 3, d)
        h_scr[...] = h.astype(h_scr.dtype)
        acc_scr[...] = jnp.zeros_like(acc_scr)

    a = jnp.square(jnp.maximum(_mm(h_scr[...], w1_ref[...]), 0.0))
    acc_scr[...] += _mm(a, w2_ref[...])

    @pl.when(j == pl.num_programs(2) - 1)
    def _():
        z = alpha * x_ref[0] + _mod_part(mod_ref, 5, d) * acc_scr[...]
        y_ref[0] = _layernorm(z, g_ref[...], b_ref[...])


def _mlp(x, mod, per_batch, w1, w2, ln_g, ln_b, alpha):
    bsz, t, d = x.shape
    ff = w1.shape[1]
    tm = min(MLP_ROWS, t)
    tf = min(MLP_FF_TILE, ff)
    tok = lambda b, i, j: (b, i, 0)
    const = lambda b, i, j: (0, 0)
    return pl.pallas_call(
        functools.partial(_mlp_kernel, d=d, alpha=alpha),
        grid=(bsz, t // tm, ff // tf),
        in_specs=[pl.BlockSpec((1, tm, d), tok), _mod_spec(d, per_batch),
                  pl.BlockSpec((d, tf), lambda b, i, j: (0, j)), pl.BlockSpec((tf, d), lambda b, i, j: (j, 0)),
                  pl.BlockSpec((1, d), const), pl.BlockSpec((1, d), const)],
        out_specs=pl.BlockSpec((1, tm, d), tok),
        out_shape=jax.ShapeDtypeStruct((bsz, t, d), F32),
        scratch_shapes=[pltpu.VMEM((tm, d), MXU_DTYPE), pltpu.VMEM((tm, d), F32)],
        compiler_params=_cparams("parallel", "parallel", "arbitrary"),
        name="mlp",
    )(x, mod, w1, w2, ln_g, ln_b)


def _out_mlp_kernel(o_ref, wo_ref, x_ref, mod_ref, g1_ref, b1_ref, w1_ref, w2_ref, g2_ref, b2_ref, y_ref,
                    x1_scr, h_scr, acc_scr, *, d, alpha):
    j = pl.program_id(2)

    @pl.when(j == 0)
    def _():
        z = alpha * x_ref[0] + _mod_part(mod_ref, 2, d) * _mm(o_ref[0], wo_ref[...])
        x1 = _layernorm(z, g1_ref[...], b1_ref[...])
        x1_scr[...] = x1
        h_scr[...] = (x1 * (1.0 + _mod_part(mod_ref, 4, d)) + _mod_part(mod_ref, 3, d)).astype(h_scr.dtype)
        acc_scr[...] = jnp.zeros_like(acc_scr)

    a = jnp.square(jnp.maximum(_mm(h_scr[...], w1_ref[...]), 0.0))
    acc_scr[...] += _mm(a, w2_ref[...])

    @pl.when(j == pl.num_programs(2) - 1)
    def _():
        z = alpha * x1_scr[...] + _mod_part(mod_ref, 5, d) * acc_scr[...]
        y_ref[0] = _layernorm(z, g2_ref[...], b2_ref[...])


def _out_mlp(o, wo, x, mod, per_batch, ln1_g, ln1_b, w1, w2, ln2_g, ln2_b, alpha):
    bsz, t, d = x.shape
    ff = w1.shape[1]
    tm = min(MLP_ROWS, t)
    tf = min(MLP_FUSED_FF_TILE, ff)
    tok = pl.BlockSpec((1, tm, d), lambda b, i, j: (b, i, 0))
    vec = pl.BlockSpec((1, d), lambda b, i, j: (0, 0))
    return pl.pallas_call(
        functools.partial(_out_mlp_kernel, d=d, alpha=alpha),
        grid=(bsz, t // tm, ff // tf),
        in_specs=[tok, pl.BlockSpec((d, d), lambda b, i, j: (0, 0)), tok, _mod_spec(d, per_batch), vec, vec,
                  pl.BlockSpec((d, tf), lambda b, i, j: (0, j)), pl.BlockSpec((tf, d), lambda b, i, j: (j, 0)),
                  vec, vec],
        out_specs=tok,
        out_shape=jax.ShapeDtypeStruct((bsz, t, d), F32),
        scratch_shapes=[pltpu.VMEM((tm, d), F32), pltpu.VMEM((tm, d), MXU_DTYPE), pltpu.VMEM((tm, d), F32)],
        compiler_params=_cparams("parallel", "parallel", "arbitrary"),
        name="out_mlp",
    )(o, wo, x, mod, ln1_g, ln1_b, w1, w2, ln2_g, ln2_b)


def _gqa_qkv_kernel(x_ref, mod_ref, w_ref, gq_ref, gk_ref, cos_ref, sin_ref, q_ref, k_ref, v_ref,
                    *, d, n_q, n_kv):
    h = x_ref[0] * (1.0 + _mod_part(mod_ref, 1, d)) + _mod_part(mod_ref, 0, d)
    acc = _mm(h, w_ref[...])
    cos, sin = cos_ref[...], sin_ref[...]
    hd = ATT_HEAD_DIM
    src = lax.broadcasted_iota(jnp.int32, (hd, hd), 0)
    dst = lax.broadcasted_iota(jnp.int32, (hd, hd), 1)
    first = (dst & (hd // 2 - 1)) < hd // 4
    rot = (jnp.where(first & (src == dst + hd // 4), -1.0, 0.0)
           + jnp.where(jnp.logical_not(first) & (src == dst - hd // 4), 1.0, 0.0))

    def norm_rope(u, g):
        u = u * lax.rsqrt(jnp.mean(u * u, -1, keepdims=True) + LN_EPS) * g
        return u * cos + _mm(u, rot) * sin

    for i in range(n_q):
        q_ref[0, :, i * hd:(i + 1) * hd] = norm_rope(acc[:, i * hd:(i + 1) * hd], gq_ref[...]).astype(q_ref.dtype)
    for i in range(n_kv):
        c0 = (n_q + i) * hd
        k_ref[0, :, i * hd:(i + 1) * hd] = norm_rope(acc[:, c0:c0 + hd], gk_ref[...]).astype(k_ref.dtype)
    ones = jnp.ones((acc.shape[0], hd), v_ref.dtype)
    for i in range(n_kv):
        c0 = (n_q + n_kv + i) * hd
        v_ref[0, :, 2 * i * hd:(2 * i + 1) * hd] = acc[:, c0:c0 + hd].astype(v_ref.dtype)
        v_ref[0, :, (2 * i + 1) * hd:(2 * i + 2) * hd] = ones


def _gqa_qkv(x, mod, per_batch, w, gq, gk, cos, sin):
    bsz, t, d = x.shape
    hd = ATT_HEAD_DIM
    n_kv = ATT_KV_HEADS
    n_q = w.shape[1] // hd - 2 * n_kv
    tm = min(PROJ_ROWS, t)
    tok = lambda b, i: (b, i, 0)
    const = lambda b, i: (0, 0)
    tab = pl.BlockSpec((tm, hd), lambda b, i: (i, 0))
    return pl.pallas_call(
        functools.partial(_gqa_qkv_kernel, d=d, n_q=n_q, n_kv=n_kv),
        grid=(bsz, t // tm),
        in_specs=[pl.BlockSpec((1, tm, d), tok), _mod_spec(d, per_batch), pl.BlockSpec(w.shape, const),
                  pl.BlockSpec((1, hd), const), pl.BlockSpec((1, hd), const), tab, tab],
        out_specs=[pl.BlockSpec((1, tm, n_q * hd), tok), pl.BlockSpec((1, tm, n_kv * hd), tok),
                   pl.BlockSpec((1, tm, 2 * n_kv * hd), tok)],
        out_shape=[jax.ShapeDtypeStruct((bsz, t, n_q * hd), MXU_DTYPE),
                   jax.ShapeDtypeStruct((bsz, t, n_kv * hd), MXU_DTYPE),
                   jax.ShapeDtypeStruct((bsz, t, 2 * n_kv * hd), MXU_DTYPE)],
        compiler_params=_cparams("parallel", "parallel"),
        name="gqa_qkv",
    )(x, mod, w, gq, gk, cos, sin)


def _gqa_att_kernel(*refs, n_sets, groups, tk):
    q_ref, o_ref = refs[0], refs[-1]
    kv = refs[1:-1]
    hd = ATT_HEAD_DIM
    tq = q_ref.shape[1]
    q = jnp.concatenate([q_ref[0, :, g * hd:(g + 1) * hd] for g in range(groups)], axis=0)
    m = acc = None
    for j in range(n_sets):
        k_ref, v_ref = kv[2 * j], kv[2 * j + 1]
        t = k_ref.shape[1]
        for c0 in range(0, t, tk):
            c1 = min(c0 + tk, t)
            s = _mm_nt(q, k_ref[0, c0:c1, :])
            m_c = jnp.max(s, -1, keepdims=True)
            if m is None:
                m = m_c
                acc = _mm(jnp.exp2(s - m), v_ref[0, c0:c1, :])
            else:
                m_new = jnp.maximum(m, m_c)
                acc = acc * jnp.exp2(m - m_new) + _mm(jnp.exp2(s - m_new), v_ref[0, c0:c1, :])
                m = m_new
    o = acc[:, :hd] / acc[:, hd:]
    for g in range(groups):
        o_ref[0, :, g * hd:(g + 1) * hd] = o[g * tq:(g + 1) * tq].astype(o_ref.dtype)


def _gqa_att(q, kv_sets):
    bsz, s, dq = q.shape
    hd = ATT_HEAD_DIM
    n_kv = ATT_KV_HEADS
    groups = dq // hd // n_kv
    tq = min(GQA_Q_ROWS, s)
    in_specs = [pl.BlockSpec((1, tq, groups * hd), lambda b, kh, i: (b, i, kh))]
    args = [q]
    for k, v in kv_sets:
        t = k.shape[1]
        in_specs += [pl.BlockSpec((1, t, hd), lambda b, kh, i: (b, 0, kh)),
                     pl.BlockSpec((1, t, 2 * hd), lambda b, kh, i: (b, 0, kh))]
        args += [k, v]
    return pl.pallas_call(
        functools.partial(_gqa_att_kernel, n_sets=len(kv_sets), groups=groups, tk=GQA_KEY_CHUNK),
        grid=(bsz, n_kv, s // tq),
        in_specs=in_specs,
        out_specs=pl.BlockSpec((1, tq, groups * hd), lambda b, kh, i: (b, i, kh)),
        out_shape=jax.ShapeDtypeStruct((bsz, s, dq), MXU_DTYPE),
        compiler_params=_cparams("parallel", "parallel", "parallel"),
        name="gqa_att",
    )(*args)


def _rope_tables(n_tokens, head_dim):
    t = np.arange(n_tokens)
    row = (t // GRID_W).astype(np.float32)
    col = (t % GRID_W).astype(np.float32)
    half = head_dim // 2
    freqs = jnp.asarray(ROPE_BASE, F32) ** (-jnp.arange(0, half, 2, dtype=F32) / half)
    ang_r = jnp.asarray(row)[:, None] * freqs[None, :]
    ang_c = jnp.asarray(col)[:, None] * freqs[None, :]
    ang = jnp.concatenate([ang_r, ang_r, ang_c, ang_c], axis=-1)
    return jnp.cos(ang), jnp.sin(ang)


def _gqa_layer(x, xc, mod, mod_c, wqkv, wo, q_norm, k_norm, want_ctx):
    s, tc = x.shape[1], xc.shape[1]
    hd = ATT_HEAD_DIM
    gq = (q_norm * (hd ** -0.5 * LOG2E)).reshape(1, hd)
    gk = k_norm.reshape(1, hd)
    cos, sin = _rope_tables(s, hd)
    bsz = x.shape[0]
    one, zero = jnp.ones((bsz * tc, hd), F32), jnp.zeros((bsz * tc, hd), F32)
    q, k, v = _gqa_qkv(x, mod, True, wqkv, gq, gk, cos, sin)
    qc, kc, vc = [_unfold(a, bsz) for a in _gqa_qkv(_fold(xc), mod_c, False, wqkv, gq, gk, one, zero)]
    o = _gqa_att(q, [(k, v), (kc, vc)])
    oc = _gqa_att(qc, [(kc, vc)]) if want_ctx else None
    return o, oc, wo


def _na_qkv_kernel(x_ref, mod_ref, w_ref, q_ref, k_ref, v_ref, *, d, scale):
    h = x_ref[0] * (1.0 + _mod_part(mod_ref, 1, d)) + _mod_part(mod_ref, 0, d)
    acc = _mm(h, w_ref[...])
    q_ref[0] = (acc[:, :d] * scale).astype(q_ref.dtype)
    k_ref[0] = acc[:, d:2 * d].astype(k_ref.dtype)
    ones = jnp.ones((acc.shape[0], LANES), v_ref.dtype)
    for p in range(d // LANES):
        c0 = 2 * d + p * LANES
        v_ref[0, :, 2 * p * LANES:(2 * p + 1) * LANES] = acc[:, c0:c0 + LANES].astype(v_ref.dtype)
        v_ref[0, :, (2 * p + 1) * LANES:(2 * p + 2) * LANES] = ones


def _na_qkv(x, mod, per_batch, w):
    bsz, t, d = x.shape
    tm = min(PROJ_ROWS, t)
    tok = lambda b, i: (b, i, 0)
    out = jax.ShapeDtypeStruct((bsz, t, d), MXU_DTYPE)
    return pl.pallas_call(
        functools.partial(_na_qkv_kernel, d=d, scale=NA_HEAD_DIM ** -0.5 * LOG2E),
        grid=(bsz, t // tm),
        in_specs=[pl.BlockSpec((1, tm, d), tok), _mod_spec(d, per_batch), pl.BlockSpec(w.shape, lambda b, i: (0, 0))],
        out_specs=[pl.BlockSpec((1, tm, d), tok)] * 2 + [pl.BlockSpec((1, tm, 2 * d), tok)],
        out_shape=[out, out, jax.ShapeDtypeStruct((bsz, t, 2 * d), MXU_DTYPE)],
        compiler_params=_cparams("parallel", "parallel"),
        name="na_qkv",
    )(x, mod, w)


def _pair_softmax_att(q, ks, vs, biases):
    lane = lax.broadcasted_iota(jnp.int32, q.shape, 1)
    first = lane < NA_HEAD_DIM
    zero = jnp.zeros_like(q)
    qm = [jnp.where(first, q, zero), jnp.where(first, zero, q)]
    m, acc = [None, None], [None, None]
    for j, (k, v) in enumerate(zip(ks, vs)):
        for h in range(2):
            s = _mm_nt(qm[h], k)
            if biases[h][j] is not None:
                s = s + biases[h][j]
            m_c = jnp.max(s, -1, keepdims=True)
            if m[h] is None:
                m[h] = m_c
                acc[h] = _mm(jnp.exp2(s - m_c), v)
            else:
                m_new = jnp.maximum(m[h], m_c)
                acc[h] = acc[h] * jnp.exp2(m[h] - m_new) + _mm(jnp.exp2(s - m_new), v)
                m[h] = m_new
    outs = [a[:, :LANES] / a[:, LANES:] for a in acc]
    return jnp.where(first, outs[0], outs[1])


def _na_att_kernel(q_ref, k_ref, v_ref, kc_ref, vc_ref, bias_ref, o_ref, *, rows):
    i = pl.program_id(2)
    q_rows = q_ref.shape[1] // GRID_W
    win = NA_WIN_H * GRID_W
    w = GRID_W
    first = lax.broadcasted_iota(jnp.int32, (w, LANES), 1) < NA_HEAD_DIM
    qm, kw, vw, bias = [], [], [], []
    for rq in range(q_rows):
        row_q = i * q_rows + rq
        r0 = jnp.clip(row_q - NA_WIN_H // 2, 0, rows - NA_WIN_H)
        start = pl.multiple_of(r0 * w, w)
        q = q_ref[0, rq * w:(rq + 1) * w, :]
        zero = jnp.zeros_like(q)
        qm.append(jnp.concatenate([jnp.where(first, q, zero), jnp.where(first, zero, q)], axis=0))
        kw.append(k_ref[0, pl.ds(start, win), :])
        vw.append(v_ref[0, pl.ds(start, win), :])
        e = r0 - row_q + NA_WIN_H - 1
        bias.append(jnp.concatenate([bias_ref[0, 0, e], bias_ref[0, 1, e]], axis=0))
    each = lambda fn, *lists: [fn(*args) for args in zip(*lists)]
    s_ctx = _mm_nt(jnp.concatenate(qm, axis=0), kc_ref[0])
    s_ctx = [s_ctx[2 * w * rq:2 * w * (rq + 1)] for rq in range(q_rows)]
    s_loc = each(lambda a, b, c: _mm_nt(a, b) + c, qm, kw, bias)
    m = each(lambda a, b: jnp.maximum(jnp.max(a, -1, keepdims=True), jnp.max(b, -1, keepdims=True)), s_loc, s_ctx)
    acc_ctx = _mm(jnp.concatenate(each(lambda a, mm: jnp.exp2(a - mm), s_ctx, m), axis=0), vc_ref[0])
    acc = each(lambda a, mm, v: _mm(jnp.exp2(a - mm), v), s_loc, m, vw)
    for rq in range(q_rows):
        a = acc[rq] + acc_ctx[2 * w * rq:2 * w * (rq + 1)]
        o = a[:, :LANES] / a[:, LANES:]
        o_ref[0, rq * w:(rq + 1) * w, :] = jnp.where(first, o[:w], o[w:]).astype(o_ref.dtype)


def _na_bias_tables(rpb, scale):
    pad = GRID_W
    rpb_p = jnp.pad(rpb, ((0, 0), (0, 0), (pad, pad)))
    cmat = jnp.stack([rpb_p[:, :, pad + NA_WIN_W - 1 - cq: pad + NA_WIN_W - 1 - cq + GRID_W]
                      for cq in range(GRID_W)], axis=2)
    cq, ck = np.arange(GRID_W)[:, None], np.arange(GRID_W)[None, :]
    c0 = np.clip(cq - NA_WIN_W // 2, 0, GRID_W - NA_WIN_W)
    vcol = (ck >= c0) & (ck < c0 + NA_WIN_W)
    cmat = jnp.where(vcol[None, None], cmat * scale, NEG_BIG)
    return jnp.stack([jnp.concatenate([cmat[:, e + j] for j in range(NA_WIN_H)], axis=-1)
                      for e in range(NA_WIN_H)], axis=1)


def _na_att(q, k, v, kc, vc, bias):
    bsz, s, d = q.shape
    tc = kc.shape[1]
    rows = s // GRID_W
    q_rows = min(NA_Q_ROWS, rows)
    assert rows % q_rows == 0 and rows >= NA_WIN_H
    tq = q_rows * GRID_W
    pairs = d // LANES
    q_spec = pl.BlockSpec((1, tq, LANES), lambda hp, b, i: (b, i, hp))
    whole = lambda t, width: pl.BlockSpec((1, t, width), lambda hp, b, i: (b, 0, hp))
    bias_spec = pl.BlockSpec((1, 2) + bias.shape[1:], lambda hp, b, i: (0, hp, 0, 0, 0))
    return pl.pallas_call(
        functools.partial(_na_att_kernel, rows=rows),
        grid=(pairs, bsz, rows // q_rows),
        in_specs=[q_spec, whole(s, LANES), whole(s, 2 * LANES), whole(tc, LANES), whole(tc, 2 * LANES), bias_spec],
        out_specs=q_spec,
        out_shape=jax.ShapeDtypeStruct((bsz, s, d), MXU_DTYPE),
        compiler_params=_cparams("parallel", "parallel", "parallel"),
        name="na_att",
    )(q, k, v, kc, vc, bias[None])


def _pair_att_kernel(q_ref, k_ref, v_ref, o_ref):
    for p in range(q_ref.shape[2] // LANES):
        sl = slice(p * LANES, (p + 1) * LANES)
        o_ref[0, :, sl] = _pair_softmax_att(q_ref[0, :, sl], [k_ref[0, :, sl]],
                                            [v_ref[0, :, 2 * p * LANES:2 * (p + 1) * LANES]],
                                            [[None], [None]]).astype(o_ref.dtype)


def _pair_att(q, k, v):
    bsz, t, d = q.shape
    spec = pl.BlockSpec((1, t, d), lambda b: (b, 0, 0))
    return pl.pallas_call(
        _pair_att_kernel,
        grid=(bsz,),
        in_specs=[spec, spec, pl.BlockSpec((1, t, 2 * d), lambda b: (b, 0, 0))],
        out_specs=spec,
        out_shape=jax.ShapeDtypeStruct((bsz, t, d), MXU_DTYPE),
        compiler_params=_cparams("parallel"),
        name="na_ctx_att",
    )(q, k, v)


def _na_layer(x, xc, mod, mod_c, wqkv, wo, rpb, want_ctx):
    q, k, v = _na_qkv(x, mod, True, wqkv)
    qc, kc, vc = [_unfold(a, x.shape[0]) for a in _na_qkv(_fold(xc), mod_c, False, wqkv)]
    o = _na_att(q, k, v, kc, vc, _na_bias_tables(rpb, LOG2E))
    oc = _pair_att(qc, kc, vc) if want_ctx else None
    return o, oc, wo


def _rw_proj_kernel(x_ref, xp_ref, xn_ref, mod_ref, mu_ref, wr_ref, wk_ref, wv_ref, w1_ref, w2_ref, a1_ref, a2_ref,
                    g1_ref, g2_ref, w0_ref, a0_ref, r_ref, k_ref, v_ref, g_ref, lw_ref, as_ref, *, d):
    t = pl.program_id(1)
    nt = pl.num_programs(1)
    scale = 1.0 + _mod_part(mod_ref, 1, d)
    shift = _mod_part(mod_ref, 0, d)
    h = x_ref[0] * scale + shift
    tm = h.shape[0]
    h_prev = (xp_ref[0, 7:8, :] * scale + shift) * (t > 0).astype(F32)
    h_next = (xn_ref[0, 0:1, :] * scale + shift) * (t < nt - 1).astype(F32)
    row = lax.broadcasted_iota(jnp.int32, h.shape, 0)
    prev = jnp.where(row == 0, h_prev, pltpu.roll(h, 1, 0))
    nxt = jnp.where(row == tm - 1, h_next, pltpu.roll(h, tm - 1, 0))
    xx = 0.5 * (prev + nxt) - h
    mix = lambda j: h + xx * mu_ref[j:j + 1, :]

    r_ref[0] = _mm(mix(0), wr_ref[...]).astype(r_ref.dtype)
    k_ref[0] = _mm(mix(2), wk_ref[...]).astype(k_ref.dtype)
    v_ref[0] = _mm(mix(3), wv_ref[...]).astype(v_ref.dtype)
    g_ref[0] = _mm(_sigmoid(_mm(mix(5), g1_ref[...])), g2_ref[...]).astype(g_ref.dtype)

    tw = jnp.tanh(_mm(mix(1), w1_ref[...]))
    al = _mm(mix(4), a1_ref[...])
    first = lax.broadcasted_iota(jnp.int32, tw.shape, 1) < tw.shape[1] // 2
    zero = jnp.zeros_like(tw)
    for n in range(2):
        pick = lambda u: jnp.where(first, u, zero) if n == 0 else jnp.where(first, zero, u)
        u = w0_ref[n:n + 1, :] + _mm(pick(tw), w2_ref[...])
        lw_ref[n, 0] = -RW_DECAY_FLOOR_RATE * _sigmoid(u)
        as_ref[n, 0] = _sigmoid(a0_ref[n:n + 1, :] + _mm(pick(al), a2_ref[...])).astype(as_ref.dtype)


def _rw_proj(x, mod, per_batch, p):
    bsz, t, d = x.shape
    tm = min(PROJ_ROWS, t)
    n8 = t // 8
    tok = lambda b, i: (b, i, 0)
    dtok = lambda b, i: (0, b, i, 0)
    full = lambda a: pl.BlockSpec(a.shape, lambda b, i: (0,) * a.ndim)
    halo_prev = pl.BlockSpec((1, 8, d), lambda b, i: (b, jnp.maximum(i * (tm // 8) - 1, 0), 0))
    halo_next = pl.BlockSpec((1, 8, d), lambda b, i: (b, jnp.minimum((i + 1) * (tm // 8), n8 - 1), 0))
    weights = [p['mu'], p['wr'], p['wk'], p['wv'], p['w1'], p['w2'], p['a1'], p['a2'], p['g1'], p['g2'],
               p['w0'], p['a0']]
    one = jax.ShapeDtypeStruct((bsz, t, d), MXU_DTYPE)
    two = lambda dtype: jax.ShapeDtypeStruct((2, bsz, t, d), dtype)
    return pl.pallas_call(
        functools.partial(_rw_proj_kernel, d=d),
        grid=(bsz, t // tm),
        in_specs=[pl.BlockSpec((1, tm, d), tok), halo_prev, halo_next, _mod_spec(d, per_batch)]
                 + [full(a) for a in weights],
        out_specs=[pl.BlockSpec((1, tm, d), tok)] * 4 + [pl.BlockSpec((2, 1, tm, d), dtok)] * 2,
        out_shape=[one, one, one, one, two(F32), two(MXU_DTYPE)],
        compiler_params=_cparams("parallel", "parallel"),
        name="rw_proj",
    )(x, x, x, mod, *weights)


def _rw_scan_kernel(r_ref, k_ref, v_ref, lw_ref, as_ref, kk_ref, ka_ref, z0_ref, y_ref, zf_ref, z_scr,
                    *, pairs, sub, reverse):
    c = pl.program_id(1)
    L = RW_CHUNK
    H = RW_HEAD
    assert 2 * L == LANES and 2 * H == LANES

    @pl.when(c == 0)
    def _():
        z_scr[...] = z0_ref[0, 0]

    sign = -1 if reverse else 1
    t_i = lax.broadcasted_iota(jnp.int32, (L, 2 * L), 0)
    s_i = lax.broadcasted_iota(jnp.int32, (L, 2 * L), 1) & (L - 1)
    dt = sign * (s_i - t_i)
    strict_c = dt < 0
    incl_c = dt <= 0
    tri2 = incl_c.astype(F32)
    eye_c = (dt == 0).astype(F32)
    r2 = lax.broadcasted_iota(jnp.int32, (LANES, LANES), 0)
    c2 = lax.broadcasted_iota(jnp.int32, (LANES, LANES), 1)
    eye2 = (r2 == c2).astype(F32)
    same_head = _head_ones()
    first = lax.broadcasted_iota(jnp.int32, (L, LANES), 1) < H
    zero = jnp.zeros((L, LANES), F32)

    head0 = lambda u: jnp.where(first, u, zero)
    head1 = lambda u: jnp.where(first, zero, u)
    bd = lambda u: jnp.concatenate([head0(u), head1(u)], axis=0)

    each = lambda fn, *lists: [fn(*args) for args in zip(*lists)]
    sls = [slice(p * LANES, (p + 1) * LANES) for p in range(pairs)]
    cat0 = lambda *u: jnp.concatenate(u, axis=0)
    cat1 = lambda *u: jnp.concatenate(u, axis=1)

    mm_each = lambda lhs, w: [_mm(a, b) for a, b in zip(lhs, w)]

    def chunk_terms(rw):
        lw = [lw_ref[0, 0, rw, sl] for sl in sls]
        r = [r_ref[0, rw, sl].astype(F32) for sl in sls]
        k = [k_ref[0, rw, sl].astype(F32) for sl in sls]
        v = [v_ref[0, rw, sl].astype(F32) for sl in sls]
        a_s = [as_ref[0, 0, rw, sl].astype(F32) for sl in sls]

        c_in = each(lambda u: _mm(tri2, cat0(*_split(u, 2))), lw)
        c_all = each(lambda u: jnp.sum(u, axis=0, keepdims=True), lw)
        kk = each(lambda u, sl: u * kk_ref[:, sl], k, sls)
        ss = each(lambda u: _mm(u * u, same_head), kk)
        kk = each(lambda u, s: u * lax.rsqrt(jnp.maximum(s, 1e-12)), kk, ss)
        b_v = each(lambda u, a: u * a, kk, a_s)
        k_d = each(lambda u, a, sl: u * (1.0 + (a - 1.0) * ka_ref[:, sl]), k, a_s, sls)
        e_neg = each(lambda ci: jnp.exp(-ci), c_in)
        e_all = each(jnp.exp, c_all)
        a_t = each(lambda u, ci, l: -u * jnp.exp(ci - l), kk, c_in, lw)
        r_t = each(lambda u, ci: u * jnp.exp(ci), r, c_in)
        b_t = each(lambda u, e: u * e, b_v, e_neg)
        k_t = each(lambda u, e: u * e, k_d, e_neg)
        b_h = each(lambda u, e: u * e, b_t, e_all)
        k_h = each(lambda u, e: u * e, k_t, e_all)
        yield None

        x = each(lambda a, rr, b, kt: _mm_nt(cat0(a, rr), cat0(bd(b), bd(kt))), a_t, r_t, b_t, k_t)
        m_ab = each(lambda u: jnp.where(strict_c, u[:L, :2 * L], 0.0), x)
        m_ak = each(lambda u: jnp.where(strict_c, u[:L, 2 * L:], 0.0), x)
        n_rb = each(lambda u: jnp.where(incl_c, u[L:, :2 * L], 0.0), x)
        n_rk = each(lambda u: jnp.where(incl_c, u[L:, 2 * L:], 0.0), x)

        inv = each(lambda m: eye_c + m, m_ab)
        yield None
        pw = mm_each(m_ab, each(bd, m_ab))
        for _ in range(int(np.log2(L)) - 2):
            yield None
            st = mm_each(each(cat0, pw, inv), each(bd, pw))
            pw = each(lambda s: s[:L], st)
            inv = each(lambda b, s: b + s[L:], inv, st)
        yield None
        inv = each(lambda b, s: b + s, inv, mm_each(inv, each(bd, pw)))
        mv = mm_each(each(cat0, m_ak, n_rk), each(bd, v))
        yield None
        tw = each(lambda i, a, m: _mm(i, cat1(bd(a), bd(m[:L]))), inv, a_t, mv)
        p1 = each(lambda u: u[:, :LANES], tw)
        p2 = each(lambda u: u[:, LANES:], tw)
        yield None
        nw = each(lambda m, a, b: _mm(m, cat1(bd(a), bd(b))), n_rb, p1, p2)
        q1 = each(lambda rr, u: rr + u[:, :LANES], r_t, nw)
        q2 = each(lambda u, m: u[:, LANES:] + m[L:], nw, mv)
        gh = each(lambda b, kh, a, c_, u: _mm_tn(cat0(b, kh), cat0(cat1(a, c_), cat1(zero, u))),
                  b_h, k_h, p1, p2, v)
        g_t = each(lambda ea, u: eye2 * ea + same_head * u[:, :LANES], e_all, gh)
        h_t = each(lambda u: same_head * u[:, LANES:], gh)
        yield q1, q2, g_t, h_t

    chunk_rows = [slice(u * L, (u + 1) * L) for u in (range(sub - 1, -1, -1) if reverse else range(sub))]
    gens = [chunk_terms(rw) for rw in chunk_rows]
    terms = [None] * sub
    while any(t is None for t in terms):
        terms = [next(gen) for gen in gens]
    z = [z_scr[p] for p in range(pairs)]
    for rw, (q1, q2, g_t, h_t) in zip(chunk_rows, terms):
        yz = mm_each(each(cat0, q1, g_t), z)
        for p in range(pairs):
            y_ref[0, rw, sls[p]] = (yz[p][:L] + q2[p]).astype(y_ref.dtype)
            z[p] = yz[p][L:] + h_t[p]
    for p in range(pairs):
        z_scr[p] = z[p]

    @pl.when(c == pl.num_programs(1) - 1)
    def _():
        zf_ref[0] = z_scr[...]


def _rw_scan(r, k, v, lw, a_s, kk_w, ka_w, z0, reverse):
    bsz, t, d = r.shape
    sub = min(RW_SCAN_CHUNKS, t // RW_CHUNK)
    rows = sub * RW_CHUNK
    assert t % rows == 0
    n_c = t // rows
    n = int(reverse)
    pairs = d // LANES
    cidx = (lambda c: n_c - 1 - c) if reverse else (lambda c: c)
    tok = pl.BlockSpec((1, rows, d), lambda b, c: (b, cidx(c), 0))
    dtok = pl.BlockSpec((1, 1, rows, d), lambda b, c: (n, b, cidx(c), 0))
    vec = pl.BlockSpec((1, d), lambda b, c: (0, 0))
    return pl.pallas_call(
        functools.partial(_rw_scan_kernel, pairs=pairs, sub=sub, reverse=reverse),
        grid=(bsz, n_c),
        in_specs=[tok, tok, tok, dtok, dtok, vec, vec,
                  pl.BlockSpec((1, 1, pairs, LANES, LANES), lambda b, c: (n, b, 0, 0, 0))],
        out_specs=[tok, pl.BlockSpec((1, pairs, LANES, LANES), lambda b, c: (b, 0, 0, 0))],
        out_shape=[jax.ShapeDtypeStruct((bsz, t, d), MXU_DTYPE),
                   jax.ShapeDtypeStruct((bsz, pairs, LANES, LANES), F32)],
        scratch_shapes=[pltpu.VMEM((pairs, LANES, LANES), F32)],
        compiler_params=_cparams("parallel", "arbitrary"),
        name="rw_scan",
    )(r, k, v, lw, a_s, kk_w, ka_w, z0)


def _rw_out_kernel(yf_ref, yb_ref, r_ref, k_ref, v_ref, g_ref, as_ref, ka_ref, rk_ref, lg_ref, lb_ref, wo_ref, x_ref,
                   mod_ref, g_ln_ref, b_ln_ref, o_ref, u_scr, *, d, alpha):
    width = 2 * LANES
    same_head = _head_ones(width)
    inv_n = 1.0 / RW_HEAD
    for p in range(d // width):
        sl = slice(p * width, (p + 1) * width)
        y = yf_ref[0, :, sl].astype(F32) + yb_ref[0, :, sl].astype(F32)
        mu = _mm(y, same_head) * inv_n
        dy = y - mu
        var = _mm(dy * dy, same_head) * inv_n
        yn = dy * lax.rsqrt(var + RW_GN_EPS) * lg_ref[:, sl] + lb_ref[:, sl]
        k = k_ref[0, :, sl].astype(F32)
        ka = ka_ref[:, sl]
        k_sum = k * (2.0 + (as_ref[0, 0, :, sl].astype(F32) + as_ref[1, 0, :, sl].astype(F32) - 2.0) * ka)
        bonus = _mm(r_ref[0, :, sl].astype(F32) * k_sum * rk_ref[:, sl], same_head)
        u = (yn + bonus * v_ref[0, :, sl].astype(F32)) * g_ref[0, :, sl].astype(F32)
        u_scr[:, sl] = u.astype(u_scr.dtype)
    z = alpha * x_ref[0] + _mod_part(mod_ref, 2, d) * _mm(u_scr[...], wo_ref[...])
    o_ref[0] = _layernorm(z, g_ln_ref[...], b_ln_ref[...])


def _rw_out(y_f, y_b, r, k, v, g, a_s, p, x, mod, per_batch, ln_g, ln_b, alpha):
    bsz, t, d = x.shape
    tm = min(RW_OUT_ROWS, t)
    tok = pl.BlockSpec((1, tm, d), lambda b, i: (b, i, 0))
    dtok = pl.BlockSpec((2, 1, tm, d), lambda b, i: (0, b, i, 0))
    vec = pl.BlockSpec((1, d), lambda b, i: (0, 0))
    return pl.pallas_call(
        functools.partial(_rw_out_kernel, d=d, alpha=alpha),
        grid=(bsz, t // tm),
        in_specs=[tok, tok, tok, tok, tok, tok, dtok, vec, vec, vec, vec, pl.BlockSpec((d, d), lambda b, i: (0, 0)),
                  tok, _mod_spec(d, per_batch), vec, vec],
        out_specs=tok,
        out_shape=jax.ShapeDtypeStruct((bsz, t, d), F32),
        scratch_shapes=[pltpu.VMEM((tm, d), MXU_DTYPE)],
        compiler_params=_cparams("parallel", "parallel"),
        name="rw_out",
    )(y_f, y_b, r, k, v, g, a_s, p['k_a'], p['r_k'], p['lnx_g'], p['lnx_b'], p['wo'], x, mod, ln_g, ln_b)


def _rw_layer(x, xc, mod, mod_c, p, want_ctx, ln_g, ln_b, alpha):
    bsz, _, d = x.shape
    rc, kc, vc, gc, lwc, asc = _rw_proj(xc, mod_c, False, p)
    r, k, v, g, lw, a_s = _rw_proj(x, mod, True, p)
    z0 = jnp.zeros((2, bsz, d // LANES, LANES, LANES), F32)
    scan = lambda args, z, rev: _rw_scan(*args, p['k_k'], p['k_a'], z, rev)
    yc_f, zc_f = scan((rc, kc, vc, lwc, asc), z0, False)
    yc_b, zc_b = scan((rc, kc, vc, lwc, asc), z0, True)
    zc = jnp.stack([zc_f, zc_b])
    y_f, _ = scan((r, k, v, lw, a_s), zc, False)
    y_b, _ = scan((r, k, v, lw, a_s), zc, True)
    x_new = _rw_out(y_f, y_b, r, k, v, g, a_s, p, x, mod, True, ln_g, ln_b, alpha)
    xc_new = (_rw_out(yc_f, yc_b, rc, kc, vc, gc, asc, p, xc, mod_c, False, ln_g, ln_b, alpha)
              if want_ctx else None)
    return x_new, xc_new


def kernel(x, c, ctx, c_ctx, mod_w, mod_b, post_ln_g, post_ln_b, mlp_w1, mlp_w2, att_wqkv, att_wo, att_q_norm, att_k_norm, na_wqkv, na_wo, na_rpb, rw_mu, rw_wr, rw_wk, rw_wv, rw_wo, rw_w0, rw_w1, rw_w2, rw_a0, rw_a1, rw_a2, rw_g1, rw_g2, rw_k_k, rw_k_a, rw_r_k, rw_lnx_g, rw_lnx_b):
    depth = mod_w.shape[0]
    bsz, _, d = x.shape
    alpha = (2.0 * depth) ** 0.25
    cast = lambda a: a.astype(MXU_DTYPE)

    cond_rows = -(-(bsz + 1) // 8) * 8
    cond = jnp.zeros((cond_rows, d), F32).at[:bsz].set(c).at[bsz].set(c_ctx)
    mods = _modulation(cond, mod_w, mod_b)

    xc = ctx
    for i in range(depth):
        kind, slot = i % N_MIXERS, i // N_MIXERS
        want_ctx = i < depth - 1
        mod = mods[i, :bsz].reshape(bsz, 1, N_MOD * d)
        mod_c = mods[i, bsz:bsz + 1].reshape(1, 1, N_MOD * d)
        g1, b1 = post_ln_g[i, 0:1], post_ln_b[i, 0:1]
        g2, b2 = post_ln_g[i, 1:2], post_ln_b[i, 1:2]
        if kind == 2:
            cat = lambda a: jnp.concatenate([a[0], a[1]], axis=-1)
            p = {'mu': rw_mu[slot], 'wr': cast(rw_wr[slot]), 'wk': cast(rw_wk[slot]), 'wv': cast(rw_wv[slot]),
                 'wo': cast(rw_wo[slot]), 'w0': rw_w0[slot], 'a0': rw_a0[slot],
                 'w1': cast(cat(rw_w1[slot])), 'a1': cast(cat(rw_a1[slot])),
                 'w2': cast(rw_w2[slot].reshape(-1, d)), 'a2': cast(rw_a2[slot].reshape(-1, d)),
                 'g1': cast(rw_g1[slot]), 'g2': cast(rw_g2[slot]),
                 'k_k': rw_k_k[slot].reshape(1, d), 'k_a': rw_k_a[slot].reshape(1, d),
                 'r_k': rw_r_k[slot].reshape(1, d), 'lnx_g': rw_lnx_g[slot].reshape(1, d),
                 'lnx_b': rw_lnx_b[slot].reshape(1, d)}
        w1, w2 = cast(mlp_w1[i]), cast(mlp_w2[i])
        if kind == 2:
            x, xc_new = _rw_layer(x, xc, mod, mod_c, p, want_ctx, g1, b1, alpha)
            x = _mlp(x, mod, True, w1, w2, g2, b2, alpha)
            if want_ctx:
                xc = _unfold(_mlp(_fold(xc_new), mod_c, False, w1, w2, g2, b2, alpha), bsz)
        else:
            if kind == 0:
                o, oc, wo = _gqa_layer(x, xc, mod, mod_c, cast(att_wqkv[slot]), cast(att_wo[slot]),
                                       att_q_norm[slot], att_k_norm[slot], want_ctx)
            else:
                o, oc, wo = _na_layer(x, xc, mod, mod_c, cast(na_wqkv[slot]), cast(na_wo[slot]), na_rpb[slot],
                                      want_ctx)
            x = _out_mlp(o, wo, x, mod, True, g1, b1, w1, w2, g2, b2, alpha)
            if want_ctx:
                xc = _unfold(_out_mlp(_fold(oc), wo, _fold(xc), mod_c, False, g1, b1, w1, w2, g2, b2, alpha), bsz)
    return x
```

```python
import functools

import numpy as np
import jax
import jax.numpy as jnp
from jax import lax
from jax.experimental import pallas as pl
from jax.experimental.pallas import tpu as pltpu

F32 = jnp.float32
MXU_DTYPE = jnp.bfloat16

GRID_W = 64
N_MOD = 6
N_MIXERS = 3
LN_EPS = 1e-6
ATT_HEAD_DIM = 128
ATT_KV_HEADS = 2
ROPE_BASE = 10000.0
GQA_KEY_CHUNK = 512
NA_HEAD_DIM = 64
NA_WIN_H = 8
NA_WIN_W = 16
NA_Q_ROWS = 64
RW_HEAD = 64
RW_GN_EPS = 64e-5
RW_DECAY_FLOOR_RATE = float(np.exp(-0.5))
RW_CHUNK = 64
LANES = 128
NEG_BIG = -1e30
LOG2E = 1.4426950408889634

VMEM_LIMIT = 56 * 1024 * 1024

PROJ_ROWS = 512
MLP_ROWS = 1024
MLP_FF_TILE = 2048
MLP_FUSED_FF_TILE = 1024
GQA_Q_ROWS = 512
RW_OUT_ROWS = 512
RW_SCAN_CHUNKS = 8
MOD_COL_TILES = 4


def _cparams(*sem):
    return pltpu.CompilerParams(dimension_semantics=sem, vmem_limit_bytes=VMEM_LIMIT)


def _mm(a, b):
    return jnp.dot(a.astype(MXU_DTYPE), b.astype(MXU_DTYPE), preferred_element_type=F32)


def _mm_nt(a, b):
    return lax.dot_general(a.astype(MXU_DTYPE), b.astype(MXU_DTYPE), (((1,), (1,)), ((), ())),
                           preferred_element_type=F32)


def _mm_tn(a, b):
    return lax.dot_general(a.astype(MXU_DTYPE), b.astype(MXU_DTYPE), (((0,), (0,)), ((), ())),
                           preferred_element_type=F32)


def _split(a, n):
    parts = []
    for _ in range(n - 1):
        p = a.astype(MXU_DTYPE)
        parts.append(p)
        a = a - p.astype(F32)
    parts.append(a.astype(MXU_DTYPE))
    return parts


def _mm_hi(a, b):
    a1, a2 = _split(a, 2)
    b1, b2 = _split(b, 2)
    d = functools.partial(jnp.dot, preferred_element_type=F32)
    return d(a1, b1) + d(a1, b2) + d(a2, b1)


def _sigmoid(x):
    return 1.0 / (1.0 + jnp.exp(-x))


def _layernorm(z, g, b):
    mu = jnp.mean(z, -1, keepdims=True)
    dz = z - mu
    var = jnp.mean(dz * dz, -1, keepdims=True)
    return dz * lax.rsqrt(var + LN_EPS) * g + b


def _mod_part(mod_ref, j, d):
    return mod_ref[0, :, j * d:(j + 1) * d]


def _head_ones(n=LANES):
    r = lax.broadcasted_iota(jnp.int32, (n, n), 0)
    c = lax.broadcasted_iota(jnp.int32, (n, n), 1)
    return ((r // RW_HEAD) == (c // RW_HEAD)).astype(F32)


def _modulation_kernel(c_ref, w_ref, b_ref, o_ref):
    c = c_ref[...]
    o_ref[0] = _mm_hi(c * _sigmoid(c), w_ref[0]) + b_ref[0]


def _modulation(cond, mod_w, mod_b):
    depth, d, n = mod_w.shape
    rows = cond.shape[0]
    tn = n // MOD_COL_TILES
    return pl.pallas_call(
        _modulation_kernel,
        grid=(depth, n // tn),
        in_specs=[pl.BlockSpec((rows, d), lambda i, j: (0, 0)),
                  pl.BlockSpec((1, d, tn), lambda i, j: (i, 0, j)),
                  pl.BlockSpec((1, 1, tn), lambda i, j: (i, 0, j))],
        out_specs=pl.BlockSpec((1, rows, tn), lambda i, j: (i, 0, j)),
        out_shape=jax.ShapeDtypeStruct((depth, rows, n), F32),
        compiler_params=_cparams("parallel", "parallel"),
        name="modulation",
    )(cond, mod_w, mod_b.reshape(depth, 1, n))


def _fold(a):
    return a.reshape(1, -1, a.shape[-1])


def _unfold(a, bsz):
    return a.reshape(bsz, -1, a.shape[-1])


def _mod_spec(d, per_batch):
    if per_batch:
        return pl.BlockSpec((1, 1, N_MOD * d), lambda b, *_: (b, 0, 0))
    return pl.BlockSpec((1, 1, N_MOD * d), lambda b, *_: (0, 0, 0))


def _mlp_kernel(x_ref, mod_ref, w1_ref, w2_ref, g_ref, b_ref, y_ref, h_scr, acc_scr, *, d, alpha):
    j = pl.program_id(2)

    @pl.when(j == 0)
    def _():
        h = x_ref[0] * (1.0 + _mod_part(mod_ref, 4, d)) + _mod_part(mod_ref, 3, d)
        h_scr[...] = h.astype(h_scr.dtype)
        acc_scr[...] = jnp.zeros_like(acc_scr)

    a = jnp.square(jnp.maximum(_mm(h_scr[...], w1_ref[...]), 0.0))
    acc_scr[...] += _mm(a, w2_ref[...])

    @pl.when(j == pl.num_programs(2) - 1)
    def _():
        z = alpha * x_ref[0] + _mod_part(mod_ref, 5, d) * acc_scr[...]
        y_ref[0] = _layernorm(z, g_ref[...], b_ref[...])


def _mlp(x, mod, per_batch, w1, w2, ln_g, ln_b, alpha):
    bsz, t, d = x.shape
    ff = w1.shape[1]
    tm = min(MLP_ROWS, t)
    tf = min(MLP_FF_TILE, ff)
    tok = lambda b, i, j: (b, i, 0)
    const = lambda b, i, j: (0, 0)
    return pl.pallas_call(
        functools.partial(_mlp_kernel, d=d, alpha=alpha),
        grid=(bsz, t // tm, ff // tf),
        in_specs=[pl.BlockSpec((1, tm, d), tok), _mod_spec(d, per_batch),
                  pl.BlockSpec((d, tf), lambda b, i, j: (0, j)), pl.BlockSpec((tf, d), lambda b, i, j: (j, 0)),
                  pl.BlockSpec((1, d), const), pl.BlockSpec((1, d), const)],
        out_specs=pl.BlockSpec((1, tm, d), tok),
        out_shape=jax.ShapeDtypeStruct((bsz, t, d), F32),
        scratch_shapes=[pltpu.VMEM((tm, d), MXU_DTYPE), pltpu.VMEM((tm, d), F32)],
        compiler_params=_cparams("parallel", "parallel", "arbitrary"),
        name="mlp",
    )(x, mod, w1, w2, ln_g, ln_b)


def _out_mlp_kernel(o_ref, wo_ref, x_ref, mod_ref, g1_ref, b1_ref, w1_ref, w2_ref, g2_ref, b2_ref, y_ref,
                    x1_scr, h_scr, acc_scr, *, d, alpha):
    j = pl.program_id(2)

    @pl.when(j == 0)
    def _():
        z = alpha * x_ref[0] + _mod_part(mod_ref, 2, d) * _mm(o_ref[0], wo_ref[...])
        x1 = _layernorm(z, g1_ref[...], b1_ref[...])
        x1_scr[...] = x1
        h_scr[...] = (x1 * (1.0 + _mod_part(mod_ref, 4, d)) + _mod_part(mod_ref, 3, d)).astype(h_scr.dtype)
        acc_scr[...] = jnp.zeros_like(acc_scr)

    a = jnp.square(jnp.maximum(_mm(h_scr[...], w1_ref[...]), 0.0))
    acc_scr[...] += _mm(a, w2_ref[...])

    @pl.when(j == pl.num_programs(2) - 1)
    def _():
        z = alpha * x1_scr[...] + _mod_part(mod_ref, 5, d) * acc_scr[...]
        y_ref[0] = _layernorm(z, g2_ref[...], b2_ref[...])


def _out_mlp(o, wo, x, mod, per_batch, ln1_g, ln1_b, w1, w2, ln2_g, ln2_b, alpha):
    bsz, t, d = x.shape
    ff = w1.shape[1]
    tm = min(MLP_ROWS, t)
    tf = min(MLP_FUSED_FF_TILE, ff)
    tok = pl.BlockSpec((1, tm, d), lambda b, i, j: (b, i, 0))
    vec = pl.BlockSpec((1, d), lambda b, i, j: (0, 0))
    return pl.pallas_call(
        functools.partial(_out_mlp_kernel, d=d, alpha=alpha),
        grid=(bsz, t // tm, ff // tf),
        in_specs=[tok, pl.BlockSpec((d, d), lambda b, i, j: (0, 0)), tok, _mod_spec(d, per_batch), vec, vec,
                  pl.BlockSpec((d, tf), lambda b, i, j: (0, j)), pl.BlockSpec((tf, d), lambda b, i, j: (j, 0)),
                  vec, vec],
        out_specs=tok,
        out_shape=jax.ShapeDtypeStruct((bsz, t, d), F32),
        scratch_shapes=[pltpu.VMEM((tm, d), F32), pltpu.VMEM((tm, d), MXU_DTYPE), pltpu.VMEM((tm, d), F32)],
        compiler_params=_cparams("parallel", "parallel", "arbitrary"),
        name="out_mlp",
    )(o, wo, x, mod, ln1_g, ln1_b, w1, w2, ln2_g, ln2_b)


def _gqa_qkv_kernel(x_ref, mod_ref, w_ref, gq_ref, gk_ref, cos_ref, sin_ref, q_ref, k_ref, v_ref,
                    *, d, n_q, n_kv):
    h = x_ref[0] * (1.0 + _mod_part(mod_ref, 1, d)) + _mod_part(mod_ref, 0, d)
    acc = _mm(h, w_ref[...])
    cos, sin = cos_ref[...], sin_ref[...]
    hd = ATT_HEAD_DIM
    src = lax.broadcasted_iota(jnp.int32, (hd, hd), 0)
    dst = lax.broadcasted_iota(jnp.int32, (hd, hd), 1)
    first = (dst & (hd // 2 - 1)) < hd // 4
    rot = (jnp.where(first & (src == dst + hd // 4), -1.0, 0.0)
           + jnp.where(jnp.logical_not(first) & (src == dst - hd // 4), 1.0, 0.0))

    def norm_rope(u, g):
        u = u * lax.rsqrt(jnp.mean(u * u, -1, keepdims=True) + LN_EPS) * g
        return u * cos + _mm(u, rot) * sin

    for i in range(n_q):
        q_ref[0, :, i * hd:(i + 1) * hd] = norm_rope(acc[:, i * hd:(i + 1) * hd], gq_ref[...]).astype(q_ref.dtype)
    for i in range(n_kv):
        c0 = (n_q + i) * hd
        k_ref[0, :, i * hd:(i + 1) * hd] = norm_rope(acc[:, c0:c0 + hd], gk_ref[...]).astype(k_ref.dtype)
    ones = jnp.ones((acc.shape[0], hd), v_ref.dtype)
    for i in range(n_kv):
        c0 = (n_q + n_kv + i) * hd
        v_ref[0, :, 2 * i * hd:(2 * i + 1) * hd] = acc[:, c0:c0 + hd].astype(v_ref.dtype)
        v_ref[0, :, (2 * i + 1) * hd:(2 * i + 2) * hd] = ones


def _gqa_qkv(x, mod, per_batch, w, gq, gk, cos, sin):
    bsz, t, d = x.shape
    hd = ATT_HEAD_DIM
    n_kv = ATT_KV_HEADS
    n_q = w.shape[1] // hd - 2 * n_kv
    tm = min(PROJ_ROWS, t)
    tok = lambda b, i: (b, i, 0)
    const = lambda b, i: (0, 0)
    tab = pl.BlockSpec((tm, hd), lambda b, i: (i, 0))
    return pl.pallas_call(
        functools.partial(_gqa_qkv_kernel, d=d, n_q=n_q, n_kv=n_kv),
        grid=(bsz, t // tm),
        in_specs=[pl.BlockSpec((1, tm, d), tok), _mod_spec(d, per_batch), pl.BlockSpec(w.shape, const),
                  pl.BlockSpec((1, hd), const), pl.BlockSpec((1, hd), const), tab, tab],
        out_specs=[pl.BlockSpec((1, tm, n_q * hd), tok), pl.BlockSpec((1, tm, n_kv * hd), tok),
                   pl.BlockSpec((1, tm, 2 * n_kv * hd), tok)],
        out_shape=[jax.ShapeDtypeStruct((bsz, t, n_q * hd), MXU_DTYPE),
                   jax.ShapeDtypeStruct((bsz, t, n_kv * hd), MXU_DTYPE),
                   jax.ShapeDtypeStruct((bsz, t, 2 * n_kv * hd), MXU_DTYPE)],
        compiler_params=_cparams("parallel", "parallel"),
        name="gqa_qkv",
    )(x, mod, w, gq, gk, cos, sin)


def _gqa_att_kernel(*refs, n_sets, groups, tk):
    q_ref, o_ref = refs[0], refs[-1]
    kv = refs[1:-1]
    hd = ATT_HEAD_DIM
    tq = q_ref.shape[1]
    q = jnp.concatenate([q_ref[0, :, g * hd:(g + 1) * hd] for g in range(groups)], axis=0)
    m = acc = None
    for j in range(n_sets):
        k_ref, v_ref = kv[2 * j], kv[2 * j + 1]
        t = k_ref.shape[1]
        for c0 in range(0, t, tk):
            c1 = min(c0 + tk, t)
            s = _mm_nt(q, k_ref[0, c0:c1, :])
            m_c = jnp.max(s, -1, keepdims=True)
            if m is None:
                m = m_c
                acc = _mm(jnp.exp2(s - m), v_ref[0, c0:c1, :])
            else:
                m_new = jnp.maximum(m, m_c)
                acc = acc * jnp.exp2(m - m_new) + _mm(jnp.exp2(s - m_new), v_ref[0, c0:c1, :])
                m = m_new
    o = acc[:, :hd] / acc[:, hd:]
    for g in range(groups):
        o_ref[0, :, g * hd:(g + 1) * hd] = o[g * tq:(g + 1) * tq].astype(o_ref.dtype)


def _gqa_att(q, kv_sets):
    bsz, s, dq = q.shape
    hd = ATT_HEAD_DIM
    n_kv = ATT_KV_HEADS
    groups = dq // hd // n_kv
    tq = min(GQA_Q_ROWS, s)
    in_specs = [pl.BlockSpec((1, tq, groups * hd), lambda b, kh, i: (b, i, kh))]
    args = [q]
    for k, v in kv_sets:
        t = k.shape[1]
        in_specs += [pl.BlockSpec((1, t, hd), lambda b, kh, i: (b, 0, kh)),
                     pl.BlockSpec((1, t, 2 * hd), lambda b, kh, i: (b, 0, kh))]
        args += [k, v]
    return pl.pallas_call(
        functools.partial(_gqa_att_kernel, n_sets=len(kv_sets), groups=groups, tk=GQA_KEY_CHUNK),
        grid=(bsz, n_kv, s // tq),
        in_specs=in_specs,
        out_specs=pl.BlockSpec((1, tq, groups * hd), lambda b, kh, i: (b, i, kh)),
        out_shape=jax.ShapeDtypeStruct((bsz, s, dq), MXU_DTYPE),
        compiler_params=_cparams("parallel", "parallel", "parallel"),
        name="gqa_att",
    )(*args)


def _rope_tables(n_tokens, head_dim):
    t = np.arange(n_tokens)
    row = (t // GRID_W).astype(np.float32)
    col = (t % GRID_W).astype(np.float32)
    half = head_dim // 2
    freqs = jnp.asarray(ROPE_BASE, F32) ** (-jnp.arange(0, half, 2, dtype=F32) / half)
    ang_r = jnp.asarray(row)[:, None] * freqs[None, :]
    ang_c = jnp.asarray(col)[:, None] * freqs[None, :]
    ang = jnp.concatenate([ang_r, ang_r, ang_c, ang_c], axis=-1)
    return jnp.cos(ang), jnp.sin(ang)


def _gqa_layer(x, xc, mod, mod_c, wqkv, wo, q_norm, k_norm, want_ctx):
    s, tc = x.shape[1], xc.shape[1]
    hd = ATT_HEAD_DIM
    gq = (q_norm * (hd ** -0.5 * LOG2E)).reshape(1, hd)
    gk = k_norm.reshape(1, hd)
    cos, sin = _rope_tables(s, hd)
    bsz = x.shape[0]
    one, zero = jnp.ones((bsz * tc, hd), F32), jnp.zeros((bsz * tc, hd), F32)
    q, k, v = _gqa_qkv(x, mod, True, wqkv, gq, gk, cos, sin)
    qc, kc, vc = [_unfold(a, bsz) for a in _gqa_qkv(_fold(xc), mod_c, False, wqkv, gq, gk, one, zero)]
    o = _gqa_att(q, [(k, v), (kc, vc)])
    oc = _gqa_att(qc, [(kc, vc)]) if want_ctx else None
    return o, oc, wo


def _na_qkv_kernel(x_ref, mod_ref, w_ref, q_ref, k_ref, v_ref, *, d, scale):
    h = x_ref[0] * (1.0 + _mod_part(mod_ref, 1, d)) + _mod_part(mod_ref, 0, d)
    acc = _mm(h, w_ref[...])
    q_ref[0] = (acc[:, :d] * scale).astype(q_ref.dtype)
    k_ref[0] = acc[:, d:2 * d].astype(k_ref.dtype)
    ones = jnp.ones((acc.shape[0], LANES), v_ref.dtype)
    for p in range(d // LANES):
        c0 = 2 * d + p * LANES
        v_ref[0, :, 2 * p * LANES:(2 * p + 1) * LANES] = acc[:, c0:c0 + LANES].astype(v_ref.dtype)
        v_ref[0, :, (2 * p + 1) * LANES:(2 * p + 2) * LANES] = ones


def _na_qkv(x, mod, per_batch, w):
    bsz, t, d = x.shape
    tm = min(PROJ_ROWS, t)
    tok = lambda b, i: (b, i, 0)
    out = jax.ShapeDtypeStruct((bsz, t, d), MXU_DTYPE)
    return pl.pallas_call(
        functools.partial(_na_qkv_kernel, d=d, scale=NA_HEAD_DIM ** -0.5 * LOG2E),
        grid=(bsz, t // tm),
        in_specs=[pl.BlockSpec((1, tm, d), tok), _mod_spec(d, per_batch), pl.BlockSpec(w.shape, lambda b, i: (0, 0))],
        out_specs=[pl.BlockSpec((1, tm, d), tok)] * 2 + [pl.BlockSpec((1, tm, 2 * d), tok)],
        out_shape=[out, out, jax.ShapeDtypeStruct((bsz, t, 2 * d), MXU_DTYPE)],
        compiler_params=_cparams("parallel", "parallel"),
        name="na_qkv",
    )(x, mod, w)


def _pair_softmax_att(q, ks, vs, biases):
    lane = lax.broadcasted_iota(jnp.int32, q.shape, 1)
    first = lane < NA_HEAD_DIM
    zero = jnp.zeros_like(q)
    qm = [jnp.where(first, q, zero), jnp.where(first, zero, q)]
    m, acc = [None, None], [None, None]
    for j, (k, v) in enumerate(zip(ks, vs)):
        for h in range(2):
            s = _mm_nt(qm[h], k)
            if biases[h][j] is not None:
                s = s + biases[h][j]
            m_c = jnp.max(s, -1, keepdims=True)
            if m[h] is None:
                m[h] = m_c
                acc[h] = _mm(jnp.exp2(s - m_c), v)
            else:
                m_new = jnp.maximum(m[h], m_c)
                acc[h] = acc[h] * jnp.exp2(m[h] - m_new) + _mm(jnp.exp2(s - m_new), v)
                m[h] = m_new
    outs = [a[:, :LANES] / a[:, LANES:] for a in acc]
    return jnp.where(first, outs[0], outs[1])


def _na_att_kernel(q_ref, k_ref, v_ref, kc_ref, vc_ref, bias_ref, o_ref, *, rows):
    i = pl.program_id(2)
    q_rows = q_ref.shape[1] // GRID_W
    win = NA_WIN_H * GRID_W
    w = GRID_W
    first = lax.broadcasted_iota(jnp.int32, (w, LANES), 1) < NA_HEAD_DIM
    qm, kw, vw, bias = [], [], [], []
    for rq in range(q_rows):
        row_q = i * q_rows + rq
        r0 = jnp.clip(row_q - NA_WIN_H // 2, 0, rows - NA_WIN_H)
        start = pl.multiple_of(r0 * w, w)
        q = q_ref[0, rq * w:(rq + 1) * w, :]
        zero = jnp.zeros_like(q)
        qm.append(jnp.concatenate([jnp.where(first, q, zero), jnp.where(first, zero, q)], axis=0))
        kw.append(k_ref[0, pl.ds(start, win), :])
        vw.append(v_ref[0, pl.ds(start, win), :])
        e = r0 - row_q + NA_WIN_H - 1
        bias.append(jnp.concatenate([bias_ref[0, 0, e], bias_ref[0, 1, e]], axis=0))
    each = lambda fn, *lists: [fn(*args) for args in zip(*lists)]
    s_ctx = _mm_nt(jnp.concatenate(qm, axis=0), kc_ref[0])
    s_ctx = [s_ctx[2 * w * rq:2 * w * (rq + 1)] for rq in range(q_rows)]
    s_loc = each(lambda a, b, c: _mm_nt(a, b) + c, qm, kw, bias)
    m = each(lambda a, b: jnp.maximum(jnp.max(a, -1, keepdims=True), jnp.max(b, -1, keepdims=True)), s_loc, s_ctx)
    acc_ctx = _mm(jnp.concatenate(each(lambda a, mm: jnp.exp2(a - mm), s_ctx, m), axis=0), vc_ref[0])
    acc = each(lambda a, mm, v: _mm(jnp.exp2(a - mm), v), s_loc, m, vw)
    for rq in range(q_rows):
        a = acc[rq] + acc_ctx[2 * w * rq:2 * w * (rq + 1)]
        o = a[:, :LANES] / a[:, LANES:]
        o_ref[0, rq * w:(rq + 1) * w, :] = jnp.where(first, o[:w], o[w:]).astype(o_ref.dtype)


def _na_bias_tables(rpb, scale):
    pad = GRID_W
    rpb_p = jnp.pad(rpb, ((0, 0), (0, 0), (pad, pad)))
    cmat = jnp.stack([rpb_p[:, :, pad + NA_WIN_W - 1 - cq: pad + NA_WIN_W - 1 - cq + GRID_W]
                      for cq in range(GRID_W)], axis=2)
    cq, ck = np.arange(GRID_W)[:, None], np.arange(GRID_W)[None, :]
    c0 = np.clip(cq - NA_WIN_W // 2, 0, GRID_W - NA_WIN_W)
    vcol = (ck >= c0) & (ck < c0 + NA_WIN_W)
    cmat = jnp.where(vcol[None, None], cmat * scale, NEG_BIG)
    return jnp.stack([jnp.concatenate([cmat[:, e + j] for j in range(NA_WIN_H)], axis=-1)
                      for e in range(NA_WIN_H)], axis=1)


def _na_att(q, k, v, kc, vc, bias):
    bsz, s, d = q.shape
    tc = kc.shape[1]
    rows = s // GRID_W
    q_rows = min(NA_Q_ROWS, rows)
    assert rows % q_rows == 0 and rows >= NA_WIN_H
    tq = q_rows * GRID_W
    pairs = d // LANES
    q_spec = pl.BlockSpec((1, tq, LANES), lambda hp, b, i: (b, i, hp))
    whole = lambda t, width: pl.BlockSpec((1, t, width), lambda hp, b, i: (b, 0, hp))
    bias_spec = pl.BlockSpec((1, 2) + bias.shape[1:], lambda hp, b, i: (0, hp, 0, 0, 0))
    return pl.pallas_call(
        functools.partial(_na_att_kernel, rows=rows),
        grid=(pairs, bsz, rows // q_rows),
        in_specs=[q_spec, whole(s, LANES), whole(s, 2 * LANES), whole(tc, LANES), whole(tc, 2 * LANES), bias_spec],
        out_specs=q_spec,
        out_shape=jax.ShapeDtypeStruct((bsz, s, d), MXU_DTYPE),
        compiler_params=_cparams("parallel", "parallel", "parallel"),
        name="na_att",
    )(q, k, v, kc, vc, bias[None])


def _pair_att_kernel(q_ref, k_ref, v_ref, o_ref):
    for p in range(q_ref.shape[2] // LANES):
        sl = slice(p * LANES, (p + 1) * LANES)
        o_ref[0, :, sl] = _pair_softmax_att(q_ref[0, :, sl], [k_ref[0, :, sl]],
                                            [v_ref[0, :, 2 * p * LANES:2 * (p + 1) * LANES]],
                                            [[None], [None]]).astype(o_ref.dtype)


def _pair_att(q, k, v):
    bsz, t, d = q.shape
    spec = pl.BlockSpec((1, t, d), lambda b: (b, 0, 0))
    return pl.pallas_call(
        _pair_att_kernel,
        grid=(bsz,),
        in_specs=[spec, spec, pl.BlockSpec((1, t, 2 * d), lambda b: (b, 0, 0))],
        out_specs=spec,
        out_shape=jax.ShapeDtypeStruct((bsz, t, d), MXU_DTYPE),
        compiler_params=_cparams("parallel"),
        name="na_ctx_att",
    )(q, k, v)


def _na_layer(x, xc, mod, mod_c, wqkv, wo, rpb, want_ctx):
    q, k, v = _na_qkv(x, mod, True, wqkv)
    qc, kc, vc = [_unfold(a, x.shape[0]) for a in _na_qkv(_fold(xc), mod_c, False, wqkv)]
    o = _na_att(q, k, v, kc, vc, _na_bias_tables(rpb, LOG2E))
    oc = _pair_att(qc, kc, vc) if want_ctx else None
    return o, oc, wo


def _rw_proj_kernel(x_ref, xp_ref, xn_ref, mod_ref, mu_ref, wr_ref, wk_ref, wv_ref, w1_ref, w2_ref, a1_ref, a2_ref,
                    g1_ref, g2_ref, w0_ref, a0_ref, r_ref, k_ref, v_ref, g_ref, lw_ref, as_ref, *, d):
    t = pl.program_id(1)
    nt = pl.num_programs(1)
    scale = 1.0 + _mod_part(mod_ref, 1, d)
    shift = _mod_part(mod_ref, 0, d)
    h = x_ref[0] * scale + shift
    tm = h.shape[0]
    h_prev = (xp_ref[0, 7:8, :] * scale + shift) * (t > 0).astype(F32)
    h_next = (xn_ref[0, 0:1, :] * scale + shift) * (t < nt - 1).astype(F32)
    row = lax.broadcasted_iota(jnp.int32, h.shape, 0)
    prev = jnp.where(row == 0, h_prev, pltpu.roll(h, 1, 0))
    nxt = jnp.where(row == tm - 1, h_next, pltpu.roll(h, tm - 1, 0))
    xx = 0.5 * (prev + nxt) - h
    mix = lambda j: h + xx * mu_ref[j:j + 1, :]

    r_ref[0] = _mm(mix(0), wr_ref[...]).astype(r_ref.dtype)
    k_ref[0] = _mm(mix(2), wk_ref[...]).astype(k_ref.dtype)
    v_ref[0] = _mm(mix(3), wv_ref[...]).astype(v_ref.dtype)
    g_ref[0] = _mm(_sigmoid(_mm(mix(5), g1_ref[...])), g2_ref[...]).astype(g_ref.dtype)

    tw = jnp.tanh(_mm(mix(1), w1_ref[...]))
    al = _mm(mix(4), a1_ref[...])
    first = lax.broadcasted_iota(jnp.int32, tw.shape, 1) < tw.shape[1] // 2
    zero = jnp.zeros_like(tw)
    for n in range(2):
        pick = lambda u: jnp.where(first, u, zero) if n == 0 else jnp.where(first, zero, u)
        u = w0_ref[n:n + 1, :] + _mm(pick(tw), w2_ref[...])
        lw_ref[n, 0] = -RW_DECAY_FLOOR_RATE * _sigmoid(u)
        as_ref[n, 0] = _sigmoid(a0_ref[n:n + 1, :] + _mm(pick(al), a2_ref[...])).astype(as_ref.dtype)


def _rw_proj(x, mod, per_batch, p):
    bsz, t, d = x.shape
    tm = min(PROJ_ROWS, t)
    n8 = t // 8
    tok = lambda b, i: (b, i, 0)
    dtok = lambda b, i: (0, b, i, 0)
    full = lambda a: pl.BlockSpec(a.shape, lambda b, i: (0,) * a.ndim)
    halo_prev = pl.BlockSpec((1, 8, d), lambda b, i: (b, jnp.maximum(i * (tm // 8) - 1, 0), 0))
    halo_next = pl.BlockSpec((1, 8, d), lambda b, i: (b, jnp.minimum((i + 1) * (tm // 8), n8 - 1), 0))
    weights = [p['mu'], p['wr'], p['wk'], p['wv'], p['w1'], p['w2'], p['a1'], p['a2'], p['g1'], p['g2'],
               p['w0'], p['a0']]
    one = jax.ShapeDtypeStruct((bsz, t, d), MXU_DTYPE)
    two = lambda dtype: jax.ShapeDtypeStruct((2, bsz, t, d), dtype)
    return pl.pallas_call(
        functools.partial(_rw_proj_kernel, d=d),
        grid=(bsz, t // tm),
        in_specs=[pl.BlockSpec((1, tm, d), tok), halo_prev, halo_next, _mod_spec(d, per_batch)]
                 + [full(a) for a in weights],
        out_specs=[pl.BlockSpec((1, tm, d), tok)] * 4 + [pl.BlockSpec((2, 1, tm, d), dtok)] * 2,
        out_shape=[one, one, one, one, two(F32), two(MXU_DTYPE)],
        compiler_params=_cparams("parallel", "parallel"),
        name="rw_proj",
    )(x, x, x, mod, *weights)


def _rw_scan_kernel(r_ref, k_ref, v_ref, lw_ref, as_ref, kk_ref, ka_ref, z0_ref, y_ref, zf_ref, z_scr,
                    *, pairs, sub, reverse):
    c = pl.program_id(1)
    L = RW_CHUNK
    H = RW_HEAD
    assert 2 * L == LANES and 2 * H == LANES

    @pl.when(c == 0)
    def _():
        z_scr[...] = z0_ref[0]

    sign = -1 if reverse else 1
    t_i = lax.broadcasted_iota(jnp.int32, (L, 2 * L), 0)
    s_i = lax.broadcasted_iota(jnp.int32, (L, 2 * L), 1) & (L - 1)
    dt = sign * (s_i - t_i)
    strict_c = dt < 0
    incl_c = dt <= 0
    tri2 = incl_c.astype(F32)
    eye_c = (dt == 0).astype(F32)
    r2 = lax.broadcasted_iota(jnp.int32, (LANES, LANES), 0)
    c2 = lax.broadcasted_iota(jnp.int32, (LANES, LANES), 1)
    eye2 = (r2 == c2).astype(F32)
    same_head = _head_ones()
    first = lax.broadcasted_iota(jnp.int32, (L, LANES), 1) < H
    zero = jnp.zeros((L, LANES), F32)

    head0 = lambda u: jnp.where(first, u, zero)
    head1 = lambda u: jnp.where(first, zero, u)
    bd = lambda u: jnp.concatenate([head0(u), head1(u)], axis=0)

    each = lambda fn, *lists: [fn(*args) for args in zip(*lists)]
    sls = [slice(p * LANES, (p + 1) * LANES) for p in range(pairs)]
    cat0 = lambda *u: jnp.concatenate(u, axis=0)
    cat1 = lambda *u: jnp.concatenate(u, axis=1)

    mm_each = lambda lhs, w: [_mm(a, b) for a, b in zip(lhs, w)]

    def chunk_terms(rw):
        lw = [lw_ref[0, 0, rw, sl] for sl in sls]
        r = [r_ref[0, rw, sl].astype(F32) for sl in sls]
        k = [k_ref[0, rw, sl].astype(F32) for sl in sls]
        v = [v_ref[0, rw, sl].astype(F32) for sl in sls]
        a_s = [as_ref[0, 0, rw, sl].astype(F32) for sl in sls]

        c_in = each(lambda u: _mm(tri2, cat0(*_split(u, 2))), lw)
        c_all = each(lambda u: jnp.sum(u, axis=0, keepdims=True), lw)
        kk = each(lambda u, sl: u * kk_ref[:, sl], k, sls)
        ss = each(lambda u: _mm(u * u, same_head), kk)
        kk = each(lambda u, s: u * lax.rsqrt(jnp.maximum(s, 1e-12)), kk, ss)
        b_v = each(lambda u, a: u * a, kk, a_s)
        k_d = each(lambda u, a, sl: u * (1.0 + (a - 1.0) * ka_ref[:, sl]), k, a_s, sls)
        e_neg = each(lambda ci: jnp.exp(-ci), c_in)
        e_all = each(jnp.exp, c_all)
        a_t = each(lambda u, ci, l: -u * jnp.exp(ci - l), kk, c_in, lw)
        r_t = each(lambda u, ci: u * jnp.exp(ci), r, c_in)
        b_t = each(lambda u, e: u * e, b_v, e_neg)
        k_t = each(lambda u, e: u * e, k_d, e_neg)
        b_h = each(lambda u, e: u * e, b_t, e_all)
        k_h = each(lambda u, e: u * e, k_t, e_all)
        yield None

        x = each(lambda a, rr, b, kt: _mm_nt(cat0(a, rr), cat0(bd(b), bd(kt))), a_t, r_t, b_t, k_t)
        m_ab = each(lambda u: jnp.where(strict_c, u[:L, :2 * L], 0.0), x)
        m_ak = each(lambda u: jnp.where(strict_c, u[:L, 2 * L:], 0.0), x)
        n_rb = each(lambda u: jnp.where(incl_c, u[L:, :2 * L], 0.0), x)
        n_rk = each(lambda u: jnp.where(incl_c, u[L:, 2 * L:], 0.0), x)

        inv = each(lambda m: eye_c + m, m_ab)
        yield None
        pw = mm_each(m_ab, each(bd, m_ab))
        for _ in range(int(np.log2(L)) - 2):
            yield None
            st = mm_each(each(cat0, pw, inv), each(bd, pw))
            pw = each(lambda s: s[:L], st)
            inv = each(lambda b, s: b + s[L:], inv, st)
        yield None
        inv = each(lambda b, s: b + s, inv, mm_each(inv, each(bd, pw)))
        mv = mm_each(each(cat0, m_ak, n_rk), each(bd, v))
        yield None
        tw = each(lambda i, a, m: _mm(i, cat1(bd(a), bd(m[:L]))), inv, a_t, mv)
        p1 = each(lambda u: u[:, :LANES], tw)
        p2 = each(lambda u: u[:, LANES:], tw)
        yield None
        nw = each(lambda m, a, b: _mm(m, cat1(bd(a), bd(b))), n_rb, p1, p2)
        q1 = each(lambda rr, u: rr + u[:, :LANES], r_t, nw)
        q2 = each(lambda u, m: u[:, LANES:] + m[L:], nw, mv)
        gh = each(lambda b, kh, a, c_, u: _mm_tn(cat0(b, kh), cat0(cat1(a, c_), cat1(zero, u))),
                  b_h, k_h, p1, p2, v)
        g_t = each(lambda ea, u: eye2 * ea + same_head * u[:, :LANES], e_all, gh)
        h_t = each(lambda u: same_head * u[:, LANES:], gh)
        yield q1, q2, g_t, h_t

    chunk_rows = [slice(u * L, (u + 1) * L) for u in (range(sub - 1, -1, -1) if reverse else range(sub))]
    gens = [chunk_terms(rw) for rw in chunk_rows]
    terms = [None] * sub
    while any(t is None for t in terms):
        terms = [next(gen) for gen in gens]
    z = [z_scr[p] for p in range(pairs)]
    for rw, (q1, q2, g_t, h_t) in zip(chunk_rows, terms):
        yz = mm_each(each(cat0, q1, g_t), z)
        for p in range(pairs):
            y_ref[0, rw, sls[p]] = (yz[p][:L] + q2[p]).astype(y_ref.dtype)
            z[p] = yz[p][L:] + h_t[p]
    for p in range(pairs):
        z_scr[p] = z[p]

    @pl.when(c == pl.num_programs(1) - 1)
    def _():
        zf_ref[0] = z_scr[...]


def _rw_scan(r, k, v, lw, a_s, kk_w, ka_w, z0, reverse):
    bsz, t, d = r.shape
    sub = min(RW_SCAN_CHUNKS, t // RW_CHUNK)
    rows = sub * RW_CHUNK
    assert t % rows == 0
    n_c = t // rows
    n = int(reverse)
    pairs = d // LANES
    cidx = (lambda c: n_c - 1 - c) if reverse else (lambda c: c)
    tok = pl.BlockSpec((1, rows, d), lambda b, c: (b, cidx(c), 0))
    dtok = pl.BlockSpec((1, 1, rows, d), lambda b, c: (n, b, cidx(c), 0))
    vec = pl.BlockSpec((1, d), lambda b, c: (0, 0))
    return pl.pallas_call(
        functools.partial(_rw_scan_kernel, pairs=pairs, sub=sub, reverse=reverse),
        grid=(bsz, n_c),
        in_specs=[tok, tok, tok, dtok, dtok, vec, vec,
                  pl.BlockSpec((1, pairs, LANES, LANES), lambda b, c: (b, 0, 0, 0))],
        out_specs=[tok, pl.BlockSpec((1, pairs, LANES, LANES), lambda b, c: (b, 0, 0, 0))],
        out_shape=[jax.ShapeDtypeStruct((bsz, t, d), MXU_DTYPE),
                   jax.ShapeDtypeStruct((bsz, pairs, LANES, LANES), F32)],
        scratch_shapes=[pltpu.VMEM((pairs, LANES, LANES), F32)],
        compiler_params=_cparams("parallel", "arbitrary"),
        name="rw_scan",
    )(r, k, v, lw, a_s, kk_w, ka_w, z0)


def _rw_out_kernel(yf_ref, yb_ref, r_ref, k_ref, v_ref, g_ref, as_ref, ka_ref, rk_ref, lg_ref, lb_ref, wo_ref, x_ref,
                   mod_ref, g_ln_ref, b_ln_ref, o_ref, u_scr, *, d, alpha):
    width = 2 * LANES
    same_head = _head_ones(width)
    inv_n = 1.0 / RW_HEAD
    for p in range(d // width):
        sl = slice(p * width, (p + 1) * width)
        y = yf_ref[0, :, sl].astype(F32) + yb_ref[0, :, sl].astype(F32)
        mu = _mm(y, same_head) * inv_n
        dy = y - mu
        var = _mm(dy * dy, same_head) * inv_n
        yn = dy * lax.rsqrt(var + RW_GN_EPS) * lg_ref[:, sl] + lb_ref[:, sl]
        k = k_ref[0, :, sl].astype(F32)
        ka = ka_ref[:, sl]
        k_sum = k * (2.0 + (as_ref[0, 0, :, sl].astype(F32) + as_ref[1, 0, :, sl].astype(F32) - 2.0) * ka)
        bonus = _mm(r_ref[0, :, sl].astype(F32) * k_sum * rk_ref[:, sl], same_head)
        u = (yn + bonus * v_ref[0, :, sl].astype(F32)) * g_ref[0, :, sl].astype(F32)
        u_scr[:, sl] = u.astype(u_scr.dtype)
    z = alpha * x_ref[0] + _mod_part(mod_ref, 2, d) * _mm(u_scr[...], wo_ref[...])
    o_ref[0] = _layernorm(z, g_ln_ref[...], b_ln_ref[...])


def _rw_out(y_f, y_b, r, k, v, g, a_s, p, x, mod, per_batch, ln_g, ln_b, alpha):
    bsz, t, d = x.shape
    tm = min(RW_OUT_ROWS, t)
    tok = pl.BlockSpec((1, tm, d), lambda b, i: (b, i, 0))
    dtok = pl.BlockSpec((2, 1, tm, d), lambda b, i: (0, b, i, 0))
    vec = pl.BlockSpec((1, d), lambda b, i: (0, 0))
    return pl.pallas_call(
        functools.partial(_rw_out_kernel, d=d, alpha=alpha),
        grid=(bsz, t // tm),
        in_specs=[tok, tok, tok, tok, tok, tok, dtok, vec, vec, vec, vec, pl.BlockSpec((d, d), lambda b, i: (0, 0)),
                  tok, _mod_spec(d, per_batch), vec, vec],
        out_specs=tok,
        out_shape=jax.ShapeDtypeStruct((bsz, t, d), F32),
        scratch_shapes=[pltpu.VMEM((tm, d), MXU_DTYPE)],
        compiler_params=_cparams("parallel", "parallel"),
        name="rw_out",
    )(y_f, y_b, r, k, v, g, a_s, p['k_a'], p['r_k'], p['lnx_g'], p['lnx_b'], p['wo'], x, mod, ln_g, ln_b)


def _rw_layer(x, xc, mod, mod_c, p, want_ctx, ln_g, ln_b, alpha):
    bsz, _, d = x.shape
    rc, kc, vc, gc, lwc, asc = _rw_proj(xc, mod_c, False, p)
    r, k, v, g, lw, a_s = _rw_proj(x, mod, True, p)
    z0 = jnp.zeros((bsz, d // LANES, LANES, LANES), F32)
    scan = lambda args, z, rev: _rw_scan(*args, p['k_k'], p['k_a'], z, rev)
    yc_f, zc_f = scan((rc, kc, vc, lwc, asc), z0, False)
    yc_b, zc_b = scan((rc, kc, vc, lwc, asc), z0, True)
    y_f, _ = scan((r, k, v, lw, a_s), zc_f, False)
    y_b, _ = scan((r, k, v, lw, a_s), zc_b, True)
    x_new = _rw_out(y_f, y_b, r, k, v, g, a_s, p, x, mod, True, ln_g, ln_b, alpha)
    xc_new = (_rw_out(yc_f, yc_b, rc, kc, vc, gc, asc, p, xc, mod_c, False, ln_g, ln_b, alpha)
              if want_ctx else None)
    return x_new, xc_new


def kernel(x, c, ctx, c_ctx, mod_w, mod_b, post_ln_g, post_ln_b, mlp_w1, mlp_w2, att_wqkv, att_wo, att_q_norm, att_k_norm, na_wqkv, na_wo, na_rpb, rw_mu, rw_wr, rw_wk, rw_wv, rw_wo, rw_w0, rw_w1, rw_w2, rw_a0, rw_a1, rw_a2, rw_g1, rw_g2, rw_k_k, rw_k_a, rw_r_k, rw_lnx_g, rw_lnx_b):
    depth = mod_w.shape[0]
    bsz, _, d = x.shape
    alpha = (2.0 * depth) ** 0.25
    cast = lambda a: a.astype(MXU_DTYPE)

    cond_rows = -(-(bsz + 1) // 8) * 8
    cond = jnp.zeros((cond_rows, d), F32).at[:bsz].set(c).at[bsz].set(c_ctx)
    mods = _modulation(cond, mod_w, mod_b)

    xc = ctx
    for i in range(depth):
        kind, slot = i % N_MIXERS, i // N_MIXERS
        want_ctx = i < depth - 1
        mod = mods[i, :bsz].reshape(bsz, 1, N_MOD * d)
        mod_c = mods[i, bsz:bsz + 1].reshape(1, 1, N_MOD * d)
        g1, b1 = post_ln_g[i, 0:1], post_ln_b[i, 0:1]
        g2, b2 = post_ln_g[i, 1:2], post_ln_b[i, 1:2]
        if kind == 2:
            cat = lambda a: jnp.concatenate([a[0], a[1]], axis=-1)
            p = {'mu': rw_mu[slot], 'wr': cast(rw_wr[slot]), 'wk': cast(rw_wk[slot]), 'wv': cast(rw_wv[slot]),
                 'wo': cast(rw_wo[slot]), 'w0': rw_w0[slot], 'a0': rw_a0[slot],
                 'w1': cast(cat(rw_w1[slot])), 'a1': cast(cat(rw_a1[slot])),
                 'w2': cast(rw_w2[slot].reshape(-1, d)), 'a2': cast(rw_a2[slot].reshape(-1, d)),
                 'g1': cast(rw_g1[slot]), 'g2': cast(rw_g2[slot]),
                 'k_k': rw_k_k[slot].reshape(1, d), 'k_a': rw_k_a[slot].reshape(1, d),
                 'r_k': rw_r_k[slot].reshape(1, d), 'lnx_g': rw_lnx_g[slot].reshape(1, d),
                 'lnx_b': rw_lnx_b[slot].reshape(1, d)}
        w1, w2 = cast(mlp_w1[i]), cast(mlp_w2[i])
        if kind == 2:
            x, xc_new = _rw_layer(x, xc, mod, mod_c, p, want_ctx, g1, b1, alpha)
            x = _mlp(x, mod, True, w1, w2, g2, b2, alpha)
            if want_ctx:
                xc = _unfold(_mlp(_fold(xc_new), mod_c, False, w1, w2, g2, b2, alpha), bsz)
        else:
            if kind == 0:
                o, oc, wo = _gqa_layer(x, xc, mod, mod_c, cast(att_wqkv[slot]), cast(att_wo[slot]),
                                       att_q_norm[slot], att_k_norm[slot], want_ctx)
            else:
                o, oc, wo = _na_layer(x, xc, mod, mod_c, cast(na_wqkv[slot]), cast(na_wo[slot]), na_rpb[slot],
                                      want_ctx)
            x = _out_mlp(o, wo, x, mod, True, g1, b1, w1, w2, g2, b2, alpha)
            if want_ctx:
                xc = _unfold(_out_mlp(_fold(oc), wo, _fold(xc), mod_c, False, g1, b1, w1, w2, g2, b2, alpha), bsz)
    return x
```

```python
import functools

import numpy as np
import jax
import jax.numpy as jnp
from jax import lax
from jax.experimental import pallas as pl
from jax.experimental.pallas import tpu as pltpu

F32 = jnp.float32
MXU_DTYPE = jnp.bfloat16

GRID_W = 64
N_MOD = 6
N_MIXERS = 3
LN_EPS = 1e-6
ATT_HEAD_DIM = 128
ATT_KV_HEADS = 2
ROPE_BASE = 10000.0
GQA_KEY_CHUNK = 512
NA_HEAD_DIM = 64
NA_WIN_H = 8
NA_WIN_W = 16
NA_Q_ROWS = 64
RW_HEAD = 64
RW_GN_EPS = 64e-5
RW_DECAY_FLOOR_RATE = float(np.exp(-0.5))
RW_CHUNK = 64
LANES = 128
NEG_BIG = -1e30
LOG2E = 1.4426950408889634

VMEM_LIMIT = 56 * 1024 * 1024
OUT_MLP_VMEM_LIMIT = 60 * 1024 * 1024

PROJ_ROWS = 512
MLP_ROWS = 1024
MLP_FF_TILE = 2048
MLP_FUSED_FF_TILE = 2048
GQA_Q_ROWS = 512
RW_OUT_ROWS = 512
RW_SCAN_CHUNKS = 8
MOD_COL_TILES = 4


def _cparams(*sem):
    return pltpu.CompilerParams(dimension_semantics=sem, vmem_limit_bytes=VMEM_LIMIT)


def _mm(a, b):
    return jnp.dot(a.astype(MXU_DTYPE), b.astype(MXU_DTYPE), preferred_element_type=F32)


def _mm_nt(a, b):
    return lax.dot_general(a.astype(MXU_DTYPE), b.astype(MXU_DTYPE), (((1,), (1,)), ((), ())),
                           preferred_element_type=F32)


def _mm_tn(a, b):
    return lax.dot_general(a.astype(MXU_DTYPE), b.astype(MXU_DTYPE), (((0,), (0,)), ((), ())),
                           preferred_element_type=F32)


def _split(a, n):
    parts = []
    for _ in range(n - 1):
        p = a.astype(MXU_DTYPE)
        parts.append(p)
        a = a - p.astype(F32)
    parts.append(a.astype(MXU_DTYPE))
    return parts


def _mm_hi(a, b):
    a1, a2 = _split(a, 2)
    b1, b2 = _split(b, 2)
    d = functools.partial(jnp.dot, preferred_element_type=F32)
    return d(a1, b1) + d(a1, b2) + d(a2, b1)


def _sigmoid(x):
    return 1.0 / (1.0 + jnp.exp(-x))


def _layernorm(z, g, b):
    mu = jnp.mean(z, -1, keepdims=True)
    dz = z - mu
    var = jnp.mean(dz * dz, -1, keepdims=True)
    return dz * lax.rsqrt(var + LN_EPS) * g + b


def _mod_part(mod_ref, j, d):
    return mod_ref[0, :, j * d:(j + 1) * d]


def _head_ones(n=LANES):
    r = lax.broadcasted_iota(jnp.int32, (n, n), 0)
    c = lax.broadcasted_iota(jnp.int32, (n, n), 1)
    return ((r // RW_HEAD) == (c // RW_HEAD)).astype(F32)


def _modulation_kernel(c_ref, w_ref, b_ref, o_ref):
    c = c_ref[...]
    o_ref[0] = _mm_hi(c * _sigmoid(c), w_ref[0]) + b_ref[0]


def _modulation(cond, mod_w, mod_b):
    depth, d, n = mod_w.shape
    rows = cond.shape[0]
    tn = n // MOD_COL_TILES
    return pl.pallas_call(
        _modulation_kernel,
        grid=(depth, n // tn),
        in_specs=[pl.BlockSpec((rows, d), lambda i, j: (0, 0)),
                  pl.BlockSpec((1, d, tn), lambda i, j: (i, 0, j)),
                  pl.BlockSpec((1, 1, tn), lambda i, j: (i, 0, j))],
        out_specs=pl.BlockSpec((1, rows, tn), lambda i, j: (i, 0, j)),
        out_shape=jax.ShapeDtypeStruct((depth, rows, n), F32),
        compiler_params=_cparams("parallel", "parallel"),
        name="modulation",
    )(cond, mod_w, mod_b.reshape(depth, 1, n))


def _fold(a):
    return a.reshape(1, -1, a.shape[-1])


def _unfold(a, bsz):
    return a.reshape(bsz, -1, a.shape[-1])


def _mod_spec(d, per_batch):
    if per_batch:
        return pl.BlockSpec((1, 1, N_MOD * d), lambda b, *_: (b, 0, 0))
    return pl.BlockSpec((1, 1, N_MOD * d), lambda b, *_: (0, 0, 0))


def _mlp_kernel(x_ref, mod_ref, w1_ref, w2_ref, g_ref, b_ref, y_ref, h_scr, acc_scr, *, d, alpha):
    j = pl.program_id(2)

    @pl.when(j == 0)
    def _():
        h = x_ref[0] * (1.0 + _mod_part(mod_ref, 4, d)) + _mod_part(mod_ref, 3, d)
        h_scr[...] = h.astype(h_scr.dtype)
        acc_scr[...] = jnp.zeros_like(acc_scr)

    a = jnp.square(jnp.maximum(_mm(h_scr[...], w1_ref[...]), 0.0))
    acc_scr[...] += _mm(a, w2_ref[...])

    @pl.when(j == pl.num_programs(2) - 1)
    def _():
        z = alpha * x_ref[0] + _mod_part(mod_ref, 5, d) * acc_scr[...]
        y_ref[0] = _layernorm(z, g_ref[...], b_ref[...])


def _mlp(x, mod, per_batch, w1, w2, ln_g, ln_b, alpha):
    bsz, t, d = x.shape
    ff = w1.shape[1]
    tm = min(MLP_ROWS, t)
    tf = min(MLP_FF_TILE, ff)
    tok = lambda b, i, j: (b, i, 0)
    const = lambda b, i, j: (0, 0)
    return pl.pallas_call(
        functools.partial(_mlp_kernel, d=d, alpha=alpha),
        grid=(bsz, t // tm, ff // tf),
        in_specs=[pl.BlockSpec((1, tm, d), tok), _mod_spec(d, per_batch),
                  pl.BlockSpec((d, tf), lambda b, i, j: (0, j)), pl.BlockSpec((tf, d), lambda b, i, j: (j, 0)),
                  pl.BlockSpec((1, d), const), pl.BlockSpec((1, d), const)],
        out_specs=pl.BlockSpec((1, tm, d), tok),
        out_shape=jax.ShapeDtypeStruct((bsz, t, d), F32),
        scratch_shapes=[pltpu.VMEM((tm, d), MXU_DTYPE), pltpu.VMEM((tm, d), F32)],
        compiler_params=_cparams("parallel", "parallel", "arbitrary"),
        name="mlp",
    )(x, mod, w1, w2, ln_g, ln_b)


def _out_mlp_kernel(o_ref, wo_ref, x_ref, mod_ref, g1_ref, b1_ref, w1_ref, w2_ref, g2_ref, b2_ref, y_ref,
                    x1_scr, h_scr, acc_scr, *, d, alpha):
    j = pl.program_id(2)

    @pl.when(j == 0)
    def _():
        z = alpha * x_ref[0] + _mod_part(mod_ref, 2, d) * _mm(o_ref[0], wo_ref[...])
        x1 = _layernorm(z, g1_ref[...], b1_ref[...])
        x1_scr[...] = x1
        h_scr[...] = (x1 * (1.0 + _mod_part(mod_ref, 4, d)) + _mod_part(mod_ref, 3, d)).astype(h_scr.dtype)
        acc_scr[...] = jnp.zeros_like(acc_scr)

    a = jnp.square(jnp.maximum(_mm(h_scr[...], w1_ref[...]), 0.0))
    acc_scr[...] += _mm(a, w2_ref[...])

    @pl.when(j == pl.num_programs(2) - 1)
    def _():
        z = alpha * x1_scr[...] + _mod_part(mod_ref, 5, d) * acc_scr[...]
        y_ref[0] = _layernorm(z, g2_ref[...], b2_ref[...])


def _out_mlp(o, wo, x, mod, per_batch, ln1_g, ln1_b, w1, w2, ln2_g, ln2_b, alpha):
    bsz, t, d = x.shape
    ff = w1.shape[1]
    tm = min(MLP_ROWS, t)
    tf = min(MLP_FUSED_FF_TILE, ff)
    tok = pl.BlockSpec((1, tm, d), lambda b, i, j: (b, i, 0))
    vec = pl.BlockSpec((1, d), lambda b, i, j: (0, 0))
    return pl.pallas_call(
        functools.partial(_out_mlp_kernel, d=d, alpha=alpha),
        grid=(bsz, t // tm, ff // tf),
        in_specs=[tok, pl.BlockSpec((d, d), lambda b, i, j: (0, 0), pipeline_mode=pl.Buffered(1)), tok,
                  _mod_spec(d, per_batch), vec, vec,
                  pl.BlockSpec((d, tf), lambda b, i, j: (0, j)), pl.BlockSpec((tf, d), lambda b, i, j: (j, 0)),
                  vec, vec],
        out_specs=tok,
        out_shape=jax.ShapeDtypeStruct((bsz, t, d), F32),
        scratch_shapes=[pltpu.VMEM((tm, d), F32), pltpu.VMEM((tm, d), MXU_DTYPE), pltpu.VMEM((tm, d), F32)],
        compiler_params=pltpu.CompilerParams(dimension_semantics=("parallel", "parallel", "arbitrary"),
                                             vmem_limit_bytes=OUT_MLP_VMEM_LIMIT),
        name="out_mlp",
    )(o, wo, x, mod, ln1_g, ln1_b, w1, w2, ln2_g, ln2_b)


def _gqa_qkv_kernel(x_ref, mod_ref, w_ref, gq_ref, gk_ref, cos_ref, sin_ref, q_ref, k_ref, v_ref,
                    *, d, n_q, n_kv):
    h = x_ref[0] * (1.0 + _mod_part(mod_ref, 1, d)) + _mod_part(mod_ref, 0, d)
    acc = _mm(h, w_ref[...])
    cos, sin = cos_ref[...], sin_ref[...]
    hd = ATT_HEAD_DIM
    src = lax.broadcasted_iota(jnp.int32, (hd, hd), 0)
    dst = lax.broadcasted_iota(jnp.int32, (hd, hd), 1)
    first = (dst & (hd // 2 - 1)) < hd // 4
    rot = (jnp.where(first & (src == dst + hd // 4), -1.0, 0.0)
           + jnp.where(jnp.logical_not(first) & (src == dst - hd // 4), 1.0, 0.0))

    def norm_rope(u, g):
        u = u * lax.rsqrt(jnp.mean(u * u, -1, keepdims=True) + LN_EPS) * g
        return u * cos + _mm(u, rot) * sin

    for i in range(n_q):
        q_ref[0, :, i * hd:(i + 1) * hd] = norm_rope(acc[:, i * hd:(i + 1) * hd], gq_ref[...]).astype(q_ref.dtype)
    for i in range(n_kv):
        c0 = (n_q + i) * hd
        k_ref[0, :, i * hd:(i + 1) * hd] = norm_rope(acc[:, c0:c0 + hd], gk_ref[...]).astype(k_ref.dtype)
    ones = jnp.ones((acc.shape[0], hd), v_ref.dtype)
    for i in range(n_kv):
        c0 = (n_q + n_kv + i) * hd
        v_ref[0, :, 2 * i * hd:(2 * i + 1) * hd] = acc[:, c0:c0 + hd].astype(v_ref.dtype)
        v_ref[0, :, (2 * i + 1) * hd:(2 * i + 2) * hd] = ones


def _gqa_qkv(x, mod, per_batch, w, gq, gk, cos, sin):
    bsz, t, d = x.shape
    hd = ATT_HEAD_DIM
    n_kv = ATT_KV_HEADS
    n_q = w.shape[1] // hd - 2 * n_kv
    tm = min(PROJ_ROWS, t)
    tok = lambda b, i: (b, i, 0)
    const = lambda b, i: (0, 0)
    tab = pl.BlockSpec((tm, hd), lambda b, i: (i, 0))
    return pl.pallas_call(
        functools.partial(_gqa_qkv_kernel, d=d, n_q=n_q, n_kv=n_kv),
        grid=(bsz, t // tm),
        in_specs=[pl.BlockSpec((1, tm, d), tok), _mod_spec(d, per_batch), pl.BlockSpec(w.shape, const),
                  pl.BlockSpec((1, hd), const), pl.BlockSpec((1, hd), const), tab, tab],
        out_specs=[pl.BlockSpec((1, tm, n_q * hd), tok), pl.BlockSpec((1, tm, n_kv * hd), tok),
                   pl.BlockSpec((1, tm, 2 * n_kv * hd), tok)],
        out_shape=[jax.ShapeDtypeStruct((bsz, t, n_q * hd), MXU_DTYPE),
                   jax.ShapeDtypeStruct((bsz, t, n_kv * hd), MXU_DTYPE),
                   jax.ShapeDtypeStruct((bsz, t, 2 * n_kv * hd), MXU_DTYPE)],
        compiler_params=_cparams("parallel", "parallel"),
        name="gqa_qkv",
    )(x, mod, w, gq, gk, cos, sin)


def _gqa_att_kernel(*refs, n_sets, groups, tk):
    q_ref, o_ref = refs[0], refs[-1]
    kv = refs[1:-1]
    hd = ATT_HEAD_DIM
    tq = q_ref.shape[1]
    q = jnp.concatenate([q_ref[0, :, g * hd:(g + 1) * hd] for g in range(groups)], axis=0)
    m = acc = None
    for j in range(n_sets):
        k_ref, v_ref = kv[2 * j], kv[2 * j + 1]
        t = k_ref.shape[1]
        for c0 in range(0, t, tk):
            c1 = min(c0 + tk, t)
            s = _mm_nt(q, k_ref[0, c0:c1, :])
            m_c = jnp.max(s, -1, keepdims=True)
            if m is None:
                m = m_c
                acc = _mm(jnp.exp2(s - m), v_ref[0, c0:c1, :])
            else:
                m_new = jnp.maximum(m, m_c)
                acc = acc * jnp.exp2(m - m_new) + _mm(jnp.exp2(s - m_new), v_ref[0, c0:c1, :])
                m = m_new
    o = acc[:, :hd] / acc[:, hd:]
    for g in range(groups):
        o_ref[0, :, g * hd:(g + 1) * hd] = o[g * tq:(g + 1) * tq].astype(o_ref.dtype)


def _gqa_att(q, kv_sets):
    bsz, s, dq = q.shape
    hd = ATT_HEAD_DIM
    n_kv = ATT_KV_HEADS
    groups = dq // hd // n_kv
    tq = min(GQA_Q_ROWS, s)
    in_specs = [pl.BlockSpec((1, tq, groups * hd), lambda b, kh, i: (b, i, kh))]
    args = [q]
    for k, v in kv_sets:
        t = k.shape[1]
        in_specs += [pl.BlockSpec((1, t, hd), lambda b, kh, i: (b, 0, kh)),
                     pl.BlockSpec((1, t, 2 * hd), lambda b, kh, i: (b, 0, kh))]
        args += [k, v]
    return pl.pallas_call(
        functools.partial(_gqa_att_kernel, n_sets=len(kv_sets), groups=groups, tk=GQA_KEY_CHUNK),
        grid=(bsz, n_kv, s // tq),
        in_specs=in_specs,
        out_specs=pl.BlockSpec((1, tq, groups * hd), lambda b, kh, i: (b, i, kh)),
        out_shape=jax.ShapeDtypeStruct((bsz, s, dq), MXU_DTYPE),
        compiler_params=_cparams("parallel", "parallel", "parallel"),
        name="gqa_att",
    )(*args)


def _rope_tables(n_tokens, head_dim):
    t = np.arange(n_tokens)
    row = (t // GRID_W).astype(np.float32)
    col = (t % GRID_W).astype(np.float32)
    half = head_dim // 2
    freqs = jnp.asarray(ROPE_BASE, F32) ** (-jnp.arange(0, half, 2, dtype=F32) / half)
    ang_r = jnp.asarray(row)[:, None] * freqs[None, :]
    ang_c = jnp.asarray(col)[:, None] * freqs[None, :]
    ang = jnp.concatenate([ang_r, ang_r, ang_c, ang_c], axis=-1)
    return jnp.cos(ang), jnp.sin(ang)


def _gqa_layer(x, xc, mod, mod_c, wqkv, wo, q_norm, k_norm, want_ctx):
    s, tc = x.shape[1], xc.shape[1]
    hd = ATT_HEAD_DIM
    gq = (q_norm * (hd ** -0.5 * LOG2E)).reshape(1, hd)
    gk = k_norm.reshape(1, hd)
    cos, sin = _rope_tables(s, hd)
    bsz = x.shape[0]
    one, zero = jnp.ones((bsz * tc, hd), F32), jnp.zeros((bsz * tc, hd), F32)
    q, k, v = _gqa_qkv(x, mod, True, wqkv, gq, gk, cos, sin)
    qc, kc, vc = [_unfold(a, bsz) for a in _gqa_qkv(_fold(xc), mod_c, False, wqkv, gq, gk, one, zero)]
    o = _gqa_att(q, [(k, v), (kc, vc)])
    oc = _gqa_att(qc, [(kc, vc)]) if want_ctx else None
    return o, oc, wo


def _na_qkv_kernel(x_ref, mod_ref, w_ref, q_ref, k_ref, v_ref, *, d, scale):
    h = x_ref[0] * (1.0 + _mod_part(mod_ref, 1, d)) + _mod_part(mod_ref, 0, d)
    acc = _mm(h, w_ref[...])
    q_ref[0] = (acc[:, :d] * scale).astype(q_ref.dtype)
    k_ref[0] = acc[:, d:2 * d].astype(k_ref.dtype)
    ones = jnp.ones((acc.shape[0], LANES), v_ref.dtype)
    for p in range(d // LANES):
        c0 = 2 * d + p * LANES
        v_ref[0, :, 2 * p * LANES:(2 * p + 1) * LANES] = acc[:, c0:c0 + LANES].astype(v_ref.dtype)
        v_ref[0, :, (2 * p + 1) * LANES:(2 * p + 2) * LANES] = ones


def _na_qkv(x, mod, per_batch, w):
    bsz, t, d = x.shape
    tm = min(PROJ_ROWS, t)
    tok = lambda b, i: (b, i, 0)
    out = jax.ShapeDtypeStruct((bsz, t, d), MXU_DTYPE)
    return pl.pallas_call(
        functools.partial(_na_qkv_kernel, d=d, scale=NA_HEAD_DIM ** -0.5 * LOG2E),
        grid=(bsz, t // tm),
        in_specs=[pl.BlockSpec((1, tm, d), tok), _mod_spec(d, per_batch), pl.BlockSpec(w.shape, lambda b, i: (0, 0))],
        out_specs=[pl.BlockSpec((1, tm, d), tok)] * 2 + [pl.BlockSpec((1, tm, 2 * d), tok)],
        out_shape=[out, out, jax.ShapeDtypeStruct((bsz, t, 2 * d), MXU_DTYPE)],
        compiler_params=_cparams("parallel", "parallel"),
        name="na_qkv",
    )(x, mod, w)


def _pair_softmax_att(q, ks, vs, biases):
    lane = lax.broadcasted_iota(jnp.int32, q.shape, 1)
    first = lane < NA_HEAD_DIM
    zero = jnp.zeros_like(q)
    qm = [jnp.where(first, q, zero), jnp.where(first, zero, q)]
    m, acc = [None, None], [None, None]
    for j, (k, v) in enumerate(zip(ks, vs)):
        for h in range(2):
            s = _mm_nt(qm[h], k)
            if biases[h][j] is not None:
                s = s + biases[h][j]
            m_c = jnp.max(s, -1, keepdims=True)
            if m[h] is None:
                m[h] = m_c
                acc[h] = _mm(jnp.exp2(s - m_c), v)
            else:
                m_new = jnp.maximum(m[h], m_c)
                acc[h] = acc[h] * jnp.exp2(m[h] - m_new) + _mm(jnp.exp2(s - m_new), v)
                m[h] = m_new
    outs = [a[:, :LANES] / a[:, LANES:] for a in acc]
    return jnp.where(first, outs[0], outs[1])


def _na_att_kernel(q_ref, k_ref, v_ref, kc_ref, vc_ref, bias_ref, o_ref, *, rows):
    i = pl.program_id(2)
    q_rows = q_ref.shape[1] // GRID_W
    win = NA_WIN_H * GRID_W
    w = GRID_W
    first = lax.broadcasted_iota(jnp.int32, (w, LANES), 1) < NA_HEAD_DIM
    qm, kw, vw, bias = [], [], [], []
    for rq in range(q_rows):
        row_q = i * q_rows + rq
        r0 = jnp.clip(row_q - NA_WIN_H // 2, 0, rows - NA_WIN_H)
        start = pl.multiple_of(r0 * w, w)
        q = q_ref[0, rq * w:(rq + 1) * w, :]
        zero = jnp.zeros_like(q)
        qm.append(jnp.concatenate([jnp.where(first, q, zero), jnp.where(first, zero, q)], axis=0))
        kw.append(k_ref[0, pl.ds(start, win), :])
        vw.append(v_ref[0, pl.ds(start, win), :])
        e = r0 - row_q + NA_WIN_H - 1
        bias.append(jnp.concatenate([bias_ref[0, 0, e], bias_ref[0, 1, e]], axis=0))
    each = lambda fn, *lists: [fn(*args) for args in zip(*lists)]
    s_ctx = _mm_nt(jnp.concatenate(qm, axis=0), kc_ref[0])
    s_ctx = [s_ctx[2 * w * rq:2 * w * (rq + 1)] for rq in range(q_rows)]
    s_loc = each(lambda a, b, c: _mm_nt(a, b) + c, qm, kw, bias)
    m = each(lambda a, b: jnp.maximum(jnp.max(a, -1, keepdims=True), jnp.max(b, -1, keepdims=True)), s_loc, s_ctx)
    acc_ctx = _mm(jnp.concatenate(each(lambda a, mm: jnp.exp2(a - mm), s_ctx, m), axis=0), vc_ref[0])
    acc = each(lambda a, mm, v: _mm(jnp.exp2(a - mm), v), s_loc, m, vw)
    for rq in range(q_rows):
        a = acc[rq] + acc_ctx[2 * w * rq:2 * w * (rq + 1)]
        o = a[:, :LANES] / a[:, LANES:]
        o_ref[0, rq * w:(rq + 1) * w, :] = jnp.where(first, o[:w], o[w:]).astype(o_ref.dtype)


def _na_bias_tables(rpb, scale):
    pad = GRID_W
    rpb_p = jnp.pad(rpb, ((0, 0), (0, 0), (pad, pad)))
    cmat = jnp.stack([rpb_p[:, :, pad + NA_WIN_W - 1 - cq: pad + NA_WIN_W - 1 - cq + GRID_W]
                      for cq in range(GRID_W)], axis=2)
    cq, ck = np.arange(GRID_W)[:, None], np.arange(GRID_W)[None, :]
    c0 = np.clip(cq - NA_WIN_W // 2, 0, GRID_W - NA_WIN_W)
    vcol = (ck >= c0) & (ck < c0 + NA_WIN_W)
    cmat = jnp.where(vcol[None, None], cmat * scale, NEG_BIG)
    return jnp.stack([jnp.concatenate([cmat[:, e + j] for j in range(NA_WIN_H)], axis=-1)
                      for e in range(NA_WIN_H)], axis=1)


def _na_att(q, k, v, kc, vc, bias):
    bsz, s, d = q.shape
    tc = kc.shape[1]
    rows = s // GRID_W
    q_rows = min(NA_Q_ROWS, rows)
    assert rows % q_rows == 0 and rows >= NA_WIN_H
    tq = q_rows * GRID_W
    pairs = d // LANES
    q_spec = pl.BlockSpec((1, tq, LANES), lambda hp, b, i: (b, i, hp))
    whole = lambda t, width: pl.BlockSpec((1, t, width), lambda hp, b, i: (b, 0, hp))
    bias_spec = pl.BlockSpec((1, 2) + bias.shape[1:], lambda hp, b, i: (0, hp, 0, 0, 0))
    return pl.pallas_call(
        functools.partial(_na_att_kernel, rows=rows),
        grid=(pairs, bsz, rows // q_rows),
        in_specs=[q_spec, whole(s, LANES), whole(s, 2 * LANES), whole(tc, LANES), whole(tc, 2 * LANES), bias_spec],
        out_specs=q_spec,
        out_shape=jax.ShapeDtypeStruct((bsz, s, d), MXU_DTYPE),
        compiler_params=_cparams("parallel", "parallel", "parallel"),
        name="na_att",
    )(q, k, v, kc, vc, bias[None])


def _pair_att_kernel(q_ref, k_ref, v_ref, o_ref):
    for p in range(q_ref.shape[2] // LANES):
        sl = slice(p * LANES, (p + 1) * LANES)
        o_ref[0, :, sl] = _pair_softmax_att(q_ref[0, :, sl], [k_ref[0, :, sl]],
                                            [v_ref[0, :, 2 * p * LANES:2 * (p + 1) * LANES]],
                                            [[None], [None]]).astype(o_ref.dtype)


def _pair_att(q, k, v):
    bsz, t, d = q.shape
    spec = pl.BlockSpec((1, t, d), lambda b: (b, 0, 0))
    return pl.pallas_call(
        _pair_att_kernel,
        grid=(bsz,),
        in_specs=[spec, spec, pl.BlockSpec((1, t, 2 * d), lambda b: (b, 0, 0))],
        out_specs=spec,
        out_shape=jax.ShapeDtypeStruct((bsz, t, d), MXU_DTYPE),
        compiler_params=_cparams("parallel"),
        name="na_ctx_att",
    )(q, k, v)


def _na_layer(x, xc, mod, mod_c, wqkv, wo, rpb, want_ctx):
    q, k, v = _na_qkv(x, mod, True, wqkv)
    qc, kc, vc = [_unfold(a, x.shape[0]) for a in _na_qkv(_fold(xc), mod_c, False, wqkv)]
    o = _na_att(q, k, v, kc, vc, _na_bias_tables(rpb, LOG2E))
    oc = _pair_att(qc, kc, vc) if want_ctx else None
    return o, oc, wo


def _rw_proj_kernel(x_ref, xp_ref, xn_ref, mod_ref, mu_ref, wr_ref, wk_ref, wv_ref, w1_ref, w2_ref, a1_ref, a2_ref,
                    g1_ref, g2_ref, w0_ref, a0_ref, r_ref, k_ref, v_ref, g_ref, lw_ref, as_ref, *, d):
    t = pl.program_id(1)
    nt = pl.num_programs(1)
    scale = 1.0 + _mod_part(mod_ref, 1, d)
    shift = _mod_part(mod_ref, 0, d)
    h = x_ref[0] * scale + shift
    tm = h.shape[0]
    h_prev = (xp_ref[0, 7:8, :] * scale + shift) * (t > 0).astype(F32)
    h_next = (xn_ref[0, 0:1, :] * scale + shift) * (t < nt - 1).astype(F32)
    row = lax.broadcasted_iota(jnp.int32, h.shape, 0)
    prev = jnp.where(row == 0, h_prev, pltpu.roll(h, 1, 0))
    nxt = jnp.where(row == tm - 1, h_next, pltpu.roll(h, tm - 1, 0))
    xx = 0.5 * (prev + nxt) - h
    mix = lambda j: h + xx * mu_ref[j:j + 1, :]

    r_ref[0] = _mm(mix(0), wr_ref[...]).astype(r_ref.dtype)
    k_ref[0] = _mm(mix(2), wk_ref[...]).astype(k_ref.dtype)
    v_ref[0] = _mm(mix(3), wv_ref[...]).astype(v_ref.dtype)
    g_ref[0] = _mm(_sigmoid(_mm(mix(5), g1_ref[...])), g2_ref[...]).astype(g_ref.dtype)

    tw = jnp.tanh(_mm(mix(1), w1_ref[...]))
    al = _mm(mix(4), a1_ref[...])
    first = lax.broadcasted_iota(jnp.int32, tw.shape, 1) < tw.shape[1] // 2
    zero = jnp.zeros_like(tw)
    for n in range(2):
        pick = lambda u: jnp.where(first, u, zero) if n == 0 else jnp.where(first, zero, u)
        u = w0_ref[n:n + 1, :] + _mm(pick(tw), w2_ref[...])
        lw_ref[n, 0] = -RW_DECAY_FLOOR_RATE * _sigmoid(u)
        as_ref[n, 0] = _sigmoid(a0_ref[n:n + 1, :] + _mm(pick(al), a2_ref[...])).astype(as_ref.dtype)


def _rw_proj(x, mod, per_batch, p):
    bsz, t, d = x.shape
    tm = min(PROJ_ROWS, t)
    n8 = t // 8
    tok = lambda b, i: (b, i, 0)
    dtok = lambda b, i: (0, b, i, 0)
    full = lambda a: pl.BlockSpec(a.shape, lambda b, i: (0,) * a.ndim)
    halo_prev = pl.BlockSpec((1, 8, d), lambda b, i: (b, jnp.maximum(i * (tm // 8) - 1, 0), 0))
    halo_next = pl.BlockSpec((1, 8, d), lambda b, i: (b, jnp.minimum((i + 1) * (tm // 8), n8 - 1), 0))
    weights = [p['mu'], p['wr'], p['wk'], p['wv'], p['w1'], p['w2'], p['a1'], p['a2'], p['g1'], p['g2'],
               p['w0'], p['a0']]
    one = jax.ShapeDtypeStruct((bsz, t, d), MXU_DTYPE)
    two = lambda dtype: jax.ShapeDtypeStruct((2, bsz, t, d), dtype)
    return pl.pallas_call(
        functools.partial(_rw_proj_kernel, d=d),
        grid=(bsz, t // tm),
        in_specs=[pl.BlockSpec((1, tm, d), tok), halo_prev, halo_next, _mod_spec(d, per_batch)]
                 + [full(a) for a in weights],
        out_specs=[pl.BlockSpec((1, tm, d), tok)] * 4 + [pl.BlockSpec((2, 1, tm, d), dtok)] * 2,
        out_shape=[one, one, one, one, two(F32), two(MXU_DTYPE)],
        compiler_params=_cparams("parallel", "parallel"),
        name="rw_proj",
    )(x, x, x, mod, *weights)


def _rw_scan_kernel(r_ref, k_ref, v_ref, lw_ref, as_ref, kk_ref, ka_ref, z0_ref, y_ref, zf_ref, z_scr,
                    *, pairs, sub, reverse):
    c = pl.program_id(1)
    L = RW_CHUNK
    H = RW_HEAD
    assert 2 * L == LANES and 2 * H == LANES

    @pl.when(c == 0)
    def _():
        z_scr[...] = z0_ref[0]

    sign = -1 if reverse else 1
    t_i = lax.broadcasted_iota(jnp.int32, (L, 2 * L), 0)
    s_i = lax.broadcasted_iota(jnp.int32, (L, 2 * L), 1) & (L - 1)
    dt = sign * (s_i - t_i)
    strict_c = dt < 0
    incl_c = dt <= 0
    tri2 = incl_c.astype(F32)
    eye_c = (dt == 0).astype(F32)
    r2 = lax.broadcasted_iota(jnp.int32, (LANES, LANES), 0)
    c2 = lax.broadcasted_iota(jnp.int32, (LANES, LANES), 1)
    eye2 = (r2 == c2).astype(F32)
    same_head = _head_ones()
    first = lax.broadcasted_iota(jnp.int32, (L, LANES), 1) < H
    zero = jnp.zeros((L, LANES), F32)

    head0 = lambda u: jnp.where(first, u, zero)
    head1 = lambda u: jnp.where(first, zero, u)
    bd = lambda u: jnp.concatenate([head0(u), head1(u)], axis=0)

    each = lambda fn, *lists: [fn(*args) for args in zip(*lists)]
    sls = [slice(p * LANES, (p + 1) * LANES) for p in range(pairs)]
    cat0 = lambda *u: jnp.concatenate(u, axis=0)
    cat1 = lambda *u: jnp.concatenate(u, axis=1)

    mm_each = lambda lhs, w: [_mm(a, b) for a, b in zip(lhs, w)]

    def chunk_terms(rw):
        lw = [lw_ref[0, 0, rw, sl] for sl in sls]
        r = [r_ref[0, rw, sl].astype(F32) for sl in sls]
        k = [k_ref[0, rw, sl].astype(F32) for sl in sls]
        v = [v_ref[0, rw, sl].astype(F32) for sl in sls]
        a_s = [as_ref[0, 0, rw, sl].astype(F32) for sl in sls]

        c_in = each(lambda u: _mm(tri2, cat0(*_split(u, 2))), lw)
        c_all = each(lambda u: jnp.sum(u, axis=0, keepdims=True), lw)
        kk = each(lambda u, sl: u * kk_ref[:, sl], k, sls)
        ss = each(lambda u: _mm(u * u, same_head), kk)
        kk = each(lambda u, s: u * lax.rsqrt(jnp.maximum(s, 1e-12)), kk, ss)
        b_v = each(lambda u, a: u * a, kk, a_s)
        k_d = each(lambda u, a, sl: u * (1.0 + (a - 1.0) * ka_ref[:, sl]), k, a_s, sls)
        e_neg = each(lambda ci: jnp.exp(-ci), c_in)
        e_all = each(jnp.exp, c_all)
        a_t = each(lambda u, ci, l: -u * jnp.exp(ci - l), kk, c_in, lw)
        r_t = each(lambda u, ci: u * jnp.exp(ci), r, c_in)
        b_t = each(lambda u, e: u * e, b_v, e_neg)
        k_t = each(lambda u, e: u * e, k_d, e_neg)
        b_h = each(lambda u, e: u * e, b_t, e_all)
        k_h = each(lambda u, e: u * e, k_t, e_all)
        yield None

        x = each(lambda a, rr, b, kt: _mm_nt(cat0(a, rr), cat0(bd(b), bd(kt))), a_t, r_t, b_t, k_t)
        m_ab = each(lambda u: jnp.where(strict_c, u[:L, :2 * L], 0.0), x)
        m_ak = each(lambda u: jnp.where(strict_c, u[:L, 2 * L:], 0.0), x)
        n_rb = each(lambda u: jnp.where(incl_c, u[L:, :2 * L], 0.0), x)
        n_rk = each(lambda u: jnp.where(incl_c, u[L:, 2 * L:], 0.0), x)

        inv = each(lambda m: eye_c + m, m_ab)
        yield None
        pw = mm_each(m_ab, each(bd, m_ab))
        for _ in range(int(np.log2(L)) - 2):
            yield None
            st = mm_each(each(cat0, pw, inv), each(bd, pw))
            pw = each(lambda s: s[:L], st)
            inv = each(lambda b, s: b + s[L:], inv, st)
        yield None
        inv = each(lambda b, s: b + s, inv, mm_each(inv, each(bd, pw)))
        mv = mm_each(each(cat0, m_ak, n_rk), each(bd, v))
        yield None
        tw = each(lambda i, a, m: _mm(i, cat1(bd(a), bd(m[:L]))), inv, a_t, mv)
        p1 = each(lambda u: u[:, :LANES], tw)
        p2 = each(lambda u: u[:, LANES:], tw)
        yield None
        nw = each(lambda m, a, b: _mm(m, cat1(bd(a), bd(b))), n_rb, p1, p2)
        q1 = each(lambda rr, u: rr + u[:, :LANES], r_t, nw)
        q2 = each(lambda u, m: u[:, LANES:] + m[L:], nw, mv)
        gh = each(lambda b, kh, a, c_, u: _mm_tn(cat0(b, kh), cat0(cat1(a, c_), cat1(zero, u))),
                  b_h, k_h, p1, p2, v)
        g_t = each(lambda ea, u: eye2 * ea + same_head * u[:, :LANES], e_all, gh)
        h_t = each(lambda u: same_head * u[:, LANES:], gh)
        yield q1, q2, g_t, h_t

    chunk_rows = [slice(u * L, (u + 1) * L) for u in (range(sub - 1, -1, -1) if reverse else range(sub))]
    gens = [chunk_terms(rw) for rw in chunk_rows]
    terms = [None] * sub
    while any(t is None for t in terms):
        terms = [next(gen) for gen in gens]
    z = [z_scr[p] for p in range(pairs)]
    for rw, (q1, q2, g_t, h_t) in zip(chunk_rows, terms):
        yz = mm_each(each(cat0, q1, g_t), z)
        for p in range(pairs):
            y_ref[0, rw, sls[p]] = (yz[p][:L] + q2[p]).astype(y_ref.dtype)
            z[p] = yz[p][L:] + h_t[p]
    for p in range(pairs):
        z_scr[p] = z[p]

    @pl.when(c == pl.num_programs(1) - 1)
    def _():
        zf_ref[0] = z_scr[...]


def _rw_scan(r, k, v, lw, a_s, kk_w, ka_w, z0, reverse):
    bsz, t, d = r.shape
    sub = min(RW_SCAN_CHUNKS, t // RW_CHUNK)
    rows = sub * RW_CHUNK
    assert t % rows == 0
    n_c = t // rows
    n = int(reverse)
    pairs = d // LANES
    cidx = (lambda c: n_c - 1 - c) if reverse else (lambda c: c)
    tok = pl.BlockSpec((1, rows, d), lambda b, c: (b, cidx(c), 0))
    dtok = pl.BlockSpec((1, 1, rows, d), lambda b, c: (n, b, cidx(c), 0))
    vec = pl.BlockSpec((1, d), lambda b, c: (0, 0))
    return pl.pallas_call(
        functools.partial(_rw_scan_kernel, pairs=pairs, sub=sub, reverse=reverse),
        grid=(bsz, n_c),
        in_specs=[tok, tok, tok, dtok, dtok, vec, vec,
                  pl.BlockSpec((1, pairs, LANES, LANES), lambda b, c: (b, 0, 0, 0))],
        out_specs=[tok, pl.BlockSpec((1, pairs, LANES, LANES), lambda b, c: (b, 0, 0, 0))],
        out_shape=[jax.ShapeDtypeStruct((bsz, t, d), MXU_DTYPE),
                   jax.ShapeDtypeStruct((bsz, pairs, LANES, LANES), F32)],
        scratch_shapes=[pltpu.VMEM((pairs, LANES, LANES), F32)],
        compiler_params=_cparams("parallel", "arbitrary"),
        name="rw_scan",
    )(r, k, v, lw, a_s, kk_w, ka_w, z0)


def _rw_out_kernel(yf_ref, yb_ref, r_ref, k_ref, v_ref, g_ref, as_ref, ka_ref, rk_ref, lg_ref, lb_ref, wo_ref, x_ref,
                   mod_ref, g_ln_ref, b_ln_ref, o_ref, u_scr, *, d, alpha):
    width = 2 * LANES
    same_head = _head_ones(width)
    inv_n = 1.0 / RW_HEAD
    for p in range(d // width):
        sl = slice(p * width, (p + 1) * width)
        y = yf_ref[0, :, sl].astype(F32) + yb_ref[0, :, sl].astype(F32)
        mu = _mm(y, same_head) * inv_n
        dy = y - mu
        var = _mm(dy * dy, same_head) * inv_n
        yn = dy * lax.rsqrt(var + RW_GN_EPS) * lg_ref[:, sl] + lb_ref[:, sl]
        k = k_ref[0, :, sl].astype(F32)
        ka = ka_ref[:, sl]
        k_sum = k * (2.0 + (as_ref[0, 0, :, sl].astype(F32) + as_ref[1, 0, :, sl].astype(F32) - 2.0) * ka)
        bonus = _mm(r_ref[0, :, sl].astype(F32) * k_sum * rk_ref[:, sl], same_head)
        u = (yn + bonus * v_ref[0, :, sl].astype(F32)) * g_ref[0, :, sl].astype(F32)
        u_scr[:, sl] = u.astype(u_scr.dtype)
    z = alpha * x_ref[0] + _mod_part(mod_ref, 2, d) * _mm(u_scr[...], wo_ref[...])
    o_ref[0] = _layernorm(z, g_ln_ref[...], b_ln_ref[...])


def _rw_out(y_f, y_b, r, k, v, g, a_s, p, x, mod, per_batch, ln_g, ln_b, alpha):
    bsz, t, d = x.shape
    tm = min(RW_OUT_ROWS, t)
    tok = pl.BlockSpec((1, tm, d), lambda b, i: (b, i, 0))
    dtok = pl.BlockSpec((2, 1, tm, d), lambda b, i: (0, b, i, 0))
    vec = pl.BlockSpec((1, d), lambda b, i: (0, 0))
    return pl.pallas_call(
        functools.partial(_rw_out_kernel, d=d, alpha=alpha),
        grid=(bsz, t // tm),
        in_specs=[tok, tok, tok, tok, tok, tok, dtok, vec, vec, vec, vec, pl.BlockSpec((d, d), lambda b, i: (0, 0)),
                  tok, _mod_spec(d, per_batch), vec, vec],
        out_specs=tok,
        out_shape=jax.ShapeDtypeStruct((bsz, t, d), F32),
        scratch_shapes=[pltpu.VMEM((tm, d), MXU_DTYPE)],
        compiler_params=_cparams("parallel", "parallel"),
        name="rw_out",
    )(y_f, y_b, r, k, v, g, a_s, p['k_a'], p['r_k'], p['lnx_g'], p['lnx_b'], p['wo'], x, mod, ln_g, ln_b)


def _rw_layer(x, xc, mod, mod_c, p, want_ctx, ln_g, ln_b, alpha):
    bsz, _, d = x.shape
    rc, kc, vc, gc, lwc, asc = _rw_proj(xc, mod_c, False, p)
    r, k, v, g, lw, a_s = _rw_proj(x, mod, True, p)
    z0 = jnp.zeros((bsz, d // LANES, LANES, LANES), F32)
    scan = lambda args, z, rev: _rw_scan(*args, p['k_k'], p['k_a'], z, rev)
    yc_f, zc_f = scan((rc, kc, vc, lwc, asc), z0, False)
    yc_b, zc_b = scan((rc, kc, vc, lwc, asc), z0, True)
    y_f, _ = scan((r, k, v, lw, a_s), zc_f, False)
    y_b, _ = scan((r, k, v, lw, a_s), zc_b, True)
    x_new = _rw_out(y_f, y_b, r, k, v, g, a_s, p, x, mod, True, ln_g, ln_b, alpha)
    xc_new = (_rw_out(yc_f, yc_b, rc, kc, vc, gc, asc, p, xc, mod_c, False, ln_g, ln_b, alpha)
              if want_ctx else None)
    return x_new, xc_new


def kernel(x, c, ctx, c_ctx, mod_w, mod_b, post_ln_g, post_ln_b, mlp_w1, mlp_w2, att_wqkv, att_wo, att_q_norm, att_k_norm, na_wqkv, na_wo, na_rpb, rw_mu, rw_wr, rw_wk, rw_wv, rw_wo, rw_w0, rw_w1, rw_w2, rw_a0, rw_a1, rw_a2, rw_g1, rw_g2, rw_k_k, rw_k_a, rw_r_k, rw_lnx_g, rw_lnx_b):
    depth = mod_w.shape[0]
    bsz, _, d = x.shape
    alpha = (2.0 * depth) ** 0.25
    cast = lambda a: a.astype(MXU_DTYPE)

    cond_rows = -(-(bsz + 1) // 8) * 8
    cond = jnp.zeros((cond_rows, d), F32).at[:bsz].set(c).at[bsz].set(c_ctx)
    mods = _modulation(cond, mod_w, mod_b)

    xc = ctx
    for i in range(depth):
        kind, slot = i % N_MIXERS, i // N_MIXERS
        want_ctx = i < depth - 1
        mod = mods[i, :bsz].reshape(bsz, 1, N_MOD * d)
        mod_c = mods[i, bsz:bsz + 1].reshape(1, 1, N_MOD * d)
        g1, b1 = post_ln_g[i, 0:1], post_ln_b[i, 0:1]
        g2, b2 = post_ln_g[i, 1:2], post_ln_b[i, 1:2]
        if kind == 2:
            cat = lambda a: jnp.concatenate([a[0], a[1]], axis=-1)
            p = {'mu': rw_mu[slot], 'wr': cast(rw_wr[slot]), 'wk': cast(rw_wk[slot]), 'wv': cast(rw_wv[slot]),
                 'wo': cast(rw_wo[slot]), 'w0': rw_w0[slot], 'a0': rw_a0[slot],
                 'w1': cast(cat(rw_w1[slot])), 'a1': cast(cat(rw_a1[slot])),
                 'w2': cast(rw_w2[slot].reshape(-1, d)), 'a2': cast(rw_a2[slot].reshape(-1, d)),
                 'g1': cast(rw_g1[slot]), 'g2': cast(rw_g2[slot]),
                 'k_k': rw_k_k[slot].reshape(1, d), 'k_a': rw_k_a[slot].reshape(1, d),
                 'r_k': rw_r_k[slot].reshape(1, d), 'lnx_g': rw_lnx_g[slot].reshape(1, d),
                 'lnx_b': rw_lnx_b[slot].reshape(1, d)}
        w1, w2 = cast(mlp_w1[i]), cast(mlp_w2[i])
        if kind == 2:
            x, xc_new = _rw_layer(x, xc, mod, mod_c, p, want_ctx, g1, b1, alpha)
            x = _mlp(x, mod, True, w1, w2, g2, b2, alpha)
            if want_ctx:
                xc = _unfold(_mlp(_fold(xc_new), mod_c, False, w1, w2, g2, b2, alpha), bsz)
        else:
            if kind == 0:
                o, oc, wo = _gqa_layer(x, xc, mod, mod_c, cast(att_wqkv[slot]), cast(att_wo[slot]),
                                       att_q_norm[slot], att_k_norm[slot], want_ctx)
            else:
                o, oc, wo = _na_layer(x, xc, mod, mod_c, cast(na_wqkv[slot]), cast(na_wo[slot]), na_rpb[slot],
                                      want_ctx)
            x = _out_mlp(o, wo, x, mod, True, g1, b1, w1, w2, g2, b2, alpha)
            if want_ctx:
                xc = _unfold(_out_mlp(_fold(oc), wo, _fold(xc), mod_c, False, g1, b1, w1, w2, g2, b2, alpha), bsz)
    return x
```
